```python
import math
import jax, jax.numpy as jnp
from jax import lax
import numpy as np

D_MODEL = 1024
BATCH = 8
SEQ = 8192
DEPTH = 4

D_MIX = D_MODEL
GM_HEADS = 4
GM_HEAD_DIM = D_MODEL // 16
GM_WIDTH = GM_HEADS * GM_HEAD_DIM
GM_CHUNK = 128
MLA_HEADS = 8
MLA_NOPE = 64
MLA_ROPE = 32
MLA_V = 64
MLA_WIDTH = MLA_HEADS * MLA_V
Q_LORA = 256
KV_LORA = 128
ROPE_BASE = 10000.0
Q_BLOCK = 128
SSM_GROUPS = 16
SSM_GROUP_CH = 16
SSM_WIDTH = SSM_GROUPS * SSM_GROUP_CH
SSM_STATE = 64
DT_MIN = 1e-3
DT_MAX = 1e-1
IN_COLS = 2 * GM_WIDTH + Q_LORA + KV_LORA + MLA_ROPE + SSM_WIDTH
D_FF = 2816
ALPHA = (2 * DEPTH) ** 0.25
BETA = (8 * DEPTH) ** -0.25
LN_EPS = 1e-5
RMS_EPS = 1e-6
NEG_BIG = -1e30

kernel_name = "hybrid_gmlp_mla_s5_macaron_deepnorm"


def layer_norm(x, g, b):
    xf = x.astype(jnp.float32)
    mu = jnp.mean(xf, axis=-1, keepdims=True)
    var = jnp.mean(jnp.square(xf - mu), axis=-1, keepdims=True)
    y = (xf - mu) * lax.rsqrt(var + LN_EPS) * g.astype(jnp.float32) + b.astype(jnp.float32)
    return y.astype(x.dtype)


def rms_norm(x, g):
    xf = x.astype(jnp.float32)
    y = xf * lax.rsqrt(jnp.mean(jnp.square(xf), axis=-1, keepdims=True) + RMS_EPS) * g.astype(jnp.float32)
    return y.astype(x.dtype)


def rms_only(x):
    xf = x.astype(jnp.float32)
    return (xf * lax.rsqrt(jnp.mean(jnp.square(xf), axis=-1, keepdims=True) + RMS_EPS)).astype(x.dtype)


def swiglu(x, w_gate, w_up, w_down):
    return (jax.nn.silu(x @ w_gate) * (x @ w_up)) @ w_down


def rope(x, cos, sin):
    half = x.shape[-1] // 2
    x1, x2 = x[..., :half], x[..., half:]
    cos = cos.astype(x.dtype)
    sin = sin.astype(x.dtype)
    return jnp.concatenate([x1 * cos - x2 * sin, x2 * cos + x1 * sin], axis=-1)


def gmlp_mixer(u, v, norm_g, ws, bs):
    b, s, _ = u.shape
    u = jax.nn.gelu(u)
    v = jax.nn.gelu(v).reshape(b, s, GM_HEADS, GM_HEAD_DIM)
    vf = v.astype(jnp.float32)
    mu = jnp.mean(vf, axis=-1, keepdims=True)
    var = jnp.mean(jnp.square(vf - mu), axis=-1, keepdims=True)
    v = ((vf - mu) * lax.rsqrt(var + LN_EPS) * norm_g.reshape(GM_HEADS, GM_HEAD_DIM).astype(jnp.float32)).astype(u.dtype)
    v = v.reshape(b, s // GM_CHUNK, GM_CHUNK, GM_HEADS, GM_HEAD_DIM)
    mask = jnp.tril(jnp.ones((GM_CHUNK, GM_CHUNK), dtype=ws.dtype))
    w_causal = ws * mask[None]
    z = jnp.einsum("bnchd,htc->bnthd", v, w_causal) + bs.T[:, :, None]
    return u * z.reshape(b, s, GM_WIDTH)


def mla_mixer(c_q, c_kv, k_rope_in, cos, sin, q_norm_g, w_uq, kv_norm_g, w_ukv):
    b, s, _ = c_q.shape
    q = (rms_norm(c_q, q_norm_g) @ w_uq).reshape(b, s, MLA_HEADS, MLA_NOPE + MLA_ROPE)
    q_nope = q[..., :MLA_NOPE]
    q_rope = rope(q[..., MLA_NOPE:], cos[:, :, None, :], sin[:, :, None, :])
    kv = (rms_norm(c_kv, kv_norm_g) @ w_ukv).reshape(b, s, MLA_HEADS, MLA_NOPE + MLA_V)
    k_nope = kv[..., :MLA_NOPE]
    v = kv[..., MLA_NOPE:]
    k_rope = rope(k_rope_in, cos, sin)
    nblk = s // Q_BLOCK
    qn_b = q_nope.reshape(b, nblk, Q_BLOCK, MLA_HEADS, MLA_NOPE).transpose(1, 0, 2, 3, 4)
    qr_b = q_rope.reshape(b, nblk, Q_BLOCK, MLA_HEADS, MLA_ROPE).transpose(1, 0, 2, 3, 4)
    scale = (MLA_NOPE + MLA_ROPE) ** -0.5
    kpos = jnp.arange(s)

    def block(args):
        qn, qr, i = args
        sc = jnp.einsum("bqhd,bkhd->bhqk", qn, k_nope) + jnp.einsum("bqhr,bkr->bhqk", qr, k_rope)
        sc = sc.astype(jnp.float32) * scale
        qpos = i * Q_BLOCK + jnp.arange(Q_BLOCK)
        causal = kpos[None, :] <= qpos[:, None]
        sc = jnp.where(causal[None, None], sc, NEG_BIG)
        p = jax.nn.softmax(sc, axis=-1).astype(v.dtype)
        return jnp.einsum("bhqk,bkhd->bqhd", p, v)

    o = lax.map(block, (qn_b, qr_b, jnp.arange(nblk)))
    return o.transpose(1, 0, 2, 3, 4).reshape(b, s, MLA_WIDTH)


def _complex_scan_combine(e1, e2):
    a1r, a1i, b1r, b1i = e1
    a2r, a2i, b2r, b2i = e2
    ar = a1r * a2r - a1i * a2i
    ai = a1r * a2i + a1i * a2r
    br = a2r * b1r - a2i * b1i + b2r
    bi = a2r * b1i + a2i * b1r + b2i
    return (ar, ai, br, bi)


def s5_mixer(u, a_re, a_im, b_re, b_im, c_re, c_im, d_skip, log_dt, glu_w, glu_b):
    bsz, s, _ = u.shape
    f32 = jnp.float32
    uf = u.astype(f32).reshape(bsz, s, SSM_GROUPS, SSM_GROUP_CH)
    ar, ai = a_re.astype(f32), a_im.astype(f32)
    dt = jnp.exp(log_dt.astype(f32))[:, None]
    mag = jnp.exp(ar * dt)
    abr = mag * jnp.cos(ai * dt)
    abi = mag * jnp.sin(ai * dt)
    den = ar * ar + ai * ai
    cr = ((abr - 1.0) * ar + abi * ai) / den
    ci = (abi * ar - (abr - 1.0) * ai) / den
    br, bi = b_re.astype(f32), b_im.astype(f32)
    bbr = cr[..., None] * br - ci[..., None] * bi
    bbi = cr[..., None] * bi + ci[..., None] * br
    bu_re = jnp.einsum("bsgc,gpc->bsgp", uf, bbr)
    bu_im = jnp.einsum("bsgc,gpc->bsgp", uf, bbi)
    a_seq_re = jnp.broadcast_to(abr[None, None], (1, s, SSM_GROUPS, SSM_STATE))
    a_seq_im = jnp.broadcast_to(abi[None, None], (1, s, SSM_GROUPS, SSM_STATE))
    _, _, h_re, h_im = lax.associative_scan(_complex_scan_combine, (a_seq_re, a_seq_im, bu_re, bu_im), axis=1)
    y = (jnp.einsum("bsgp,gcp->bsgc", h_re, c_re.astype(f32))
         - jnp.einsum("bsgp,gcp->bsgc", h_im, c_im.astype(f32))
         + d_skip.astype(f32) * uf)
    y = jax.nn.gelu(y).reshape(bsz, s, SSM_WIDTH)
    y = y * jax.nn.sigmoid(y @ glu_w.astype(f32) + glu_b.astype(f32))
    return y.astype(u.dtype)


def _fwd_setup_inputs(seed: int = 0) -> dict:
    key = jax.random.key(seed)
    ks = jax.random.split(key, 32)
    f32 = jnp.float32
    L, D, F = DEPTH, D_MODEL, D_FF

    def nrm(k, shape, scale):
        return jax.random.normal(k, shape, f32) * scale

    x = jax.random.normal(ks[0], (BATCH, SEQ, D), f32)
    offset = jax.random.randint(ks[1], (BATCH, 1), 0, 1024, dtype=jnp.int32)
    positions = (offset + jnp.arange(SEQ, dtype=jnp.int32)[None, :]).astype(jnp.int32)
    ln_g = 1.0 + nrm(ks[2], (L, 3, D), 0.01)
    ln_b = nrm(ks[3], (L, 3, D), 0.01)
    ffn1_w_gate = nrm(ks[4], (L, D, F), D ** -0.5)
    ffn1_w_up = nrm(ks[5], (L, D, F), D ** -0.5)
    ffn1_w_down = nrm(ks[6], (L, F, D), BETA * F ** -0.5)
    w_in = nrm(ks[7], (L, D, IN_COLS), D ** -0.5)
    gmlp_norm_g = 1.0 + nrm(ks[8], (L, GM_WIDTH), 0.01)
    gmlp_ws = nrm(ks[9], (L, GM_HEADS, GM_CHUNK, GM_CHUNK), 0.5 * GM_CHUNK ** -0.5)
    gmlp_bs = 1.0 + nrm(ks[10], (L, GM_HEADS, GM_CHUNK), 0.01)
    mla_q_norm_g = 1.0 + nrm(ks[11], (L, Q_LORA), 0.01)
    mla_w_uq = nrm(ks[12], (L, Q_LORA, MLA_HEADS * (MLA_NOPE + MLA_ROPE)), Q_LORA ** -0.5)
    mla_kv_norm_g = 1.0 + nrm(ks[13], (L, KV_LORA), 0.01)
    mla_w_ukv = nrm(ks[14], (L, KV_LORA, MLA_HEADS * (MLA_NOPE + MLA_V)), KV_LORA ** -0.5)
    ssm_a_re = -0.5 + nrm(ks[15], (L, SSM_GROUPS, SSM_STATE), 0.01)
    ssm_a_im = (math.pi * jnp.arange(SSM_STATE, dtype=f32))[None, None, :] + nrm(ks[16], (L, SSM_GROUPS, SSM_STATE), 0.01)
    ssm_b_re = nrm(ks[17], (L, SSM_GROUPS, SSM_STATE, SSM_GROUP_CH), (2 * SSM_GROUP_CH) ** -0.5)
    ssm_b_im = nrm(ks[18], (L, SSM_GROUPS, SSM_STATE, SSM_GROUP_CH), (2 * SSM_GROUP_CH) ** -0.5)
    ssm_c_re = nrm(ks[19], (L, SSM_GROUPS, SSM_GROUP_CH, SSM_STATE), (2 * SSM_STATE) ** -0.5)
    ssm_c_im = nrm(ks[20], (L, SSM_GROUPS, SSM_GROUP_CH, SSM_STATE), (2 * SSM_STATE) ** -0.5)
    ssm_d = nrm(ks[21], (L, SSM_GROUPS, SSM_GROUP_CH), 1.0)
    ssm_log_dt = jax.random.uniform(ks[22], (L, SSM_GROUPS), f32, math.log(DT_MIN), math.log(DT_MAX))
    ssm_glu_w = nrm(ks[23], (L, SSM_WIDTH, SSM_WIDTH), SSM_WIDTH ** -0.5)
    ssm_glu_b = nrm(ks[24], (L, SSM_WIDTH), 0.01)
    mix_norm_g = 1.0 + nrm(ks[25], (L, D_MIX), 0.01)
    w_out = nrm(ks[26], (L, D_MIX, D), BETA * D_MIX ** -0.5)
    ffn2_w_gate = nrm(ks[27], (L, D, F), D ** -0.5)
    ffn2_w_up = nrm(ks[28], (L, D, F), D ** -0.5)
    ffn2_w_down = nrm(ks[29], (L, F, D), BETA * F ** -0.5)
    return {
        "x": x, "positions": positions, "ln_g": ln_g, "ln_b": ln_b,
        "ffn1_w_gate": ffn1_w_gate, "ffn1_w_up": ffn1_w_up, "ffn1_w_down": ffn1_w_down,
        "w_in": w_in,
        "gmlp_norm_g": gmlp_norm_g, "gmlp_ws": gmlp_ws, "gmlp_bs": gmlp_bs,
        "mla_q_norm_g": mla_q_norm_g, "mla_w_uq": mla_w_uq, "mla_kv_norm_g": mla_kv_norm_g, "mla_w_ukv": mla_w_ukv,
        "ssm_a_re": ssm_a_re, "ssm_a_im": ssm_a_im, "ssm_b_re": ssm_b_re, "ssm_b_im": ssm_b_im,
        "ssm_c_re": ssm_c_re, "ssm_c_im": ssm_c_im, "ssm_d": ssm_d, "ssm_log_dt": ssm_log_dt,
        "ssm_glu_w": ssm_glu_w, "ssm_glu_b": ssm_glu_b,
        "mix_norm_g": mix_norm_g, "w_out": w_out,
        "ffn2_w_gate": ffn2_w_gate, "ffn2_w_up": ffn2_w_up, "ffn2_w_down": ffn2_w_down,
    }


def _fwd_reference(x, positions, ln_g, ln_b, ffn1_w_gate, ffn1_w_up, ffn1_w_down, w_in,
              gmlp_norm_g, gmlp_ws, gmlp_bs, mla_q_norm_g, mla_w_uq, mla_kv_norm_g, mla_w_ukv,
              ssm_a_re, ssm_a_im, ssm_b_re, ssm_b_im, ssm_c_re, ssm_c_im, ssm_d, ssm_log_dt,
              ssm_glu_w, ssm_glu_b, mix_norm_g, w_out, ffn2_w_gate, ffn2_w_up, ffn2_w_down):
    inv_freq = 1.0 / (ROPE_BASE ** (jnp.arange(0, MLA_ROPE, 2, dtype=jnp.float32) / MLA_ROPE))
    ang = positions.astype(jnp.float32)[..., None] * inv_freq
    cos, sin = jnp.cos(ang), jnp.sin(ang)
    o1 = 2 * GM_WIDTH
    o2 = o1 + Q_LORA
    o3 = o2 + KV_LORA
    o4 = o3 + MLA_ROPE
    for l in range(DEPTH):
        x = layer_norm(ALPHA * x + 0.5 * swiglu(x, ffn1_w_gate[l], ffn1_w_up[l], ffn1_w_down[l]), ln_g[l, 0], ln_b[l, 0])
        h = x @ w_in[l]
        y_a = gmlp_mixer(h[..., :GM_WIDTH], h[..., GM_WIDTH:o1], gmlp_norm_g[l], gmlp_ws[l], gmlp_bs[l])
        y_b = mla_mixer(h[..., o1:o2], h[..., o2:o3], h[..., o3:o4], cos, sin,
                        mla_q_norm_g[l], mla_w_uq[l], mla_kv_norm_g[l], mla_w_ukv[l])
        y_c = s5_mixer(h[..., o4:], ssm_a_re[l], ssm_a_im[l], ssm_b_re[l], ssm_b_im[l],
                       ssm_c_re[l], ssm_c_im[l], ssm_d[l], ssm_log_dt[l], ssm_glu_w[l], ssm_glu_b[l])
        y = jnp.concatenate([rms_only(y_a), rms_only(y_b), rms_only(y_c)], axis=-1) * mix_norm_g[l]
        x = layer_norm(ALPHA * x + y @ w_out[l], ln_g[l, 1], ln_b[l, 1])
        x = layer_norm(ALPHA * x + 0.5 * swiglu(x, ffn2_w_gate[l], ffn2_w_up[l], ffn2_w_down[l]), ln_g[l, 2], ln_b[l, 2])
    return x


import jax as _jax
import jax.numpy as _jnp

TWIN_FORMAT = 'train_step'
FWD_PARAMS = ['x', 'positions', 'ln_g', 'ln_b', 'ffn1_w_gate', 'ffn1_w_up', 'ffn1_w_down', 'w_in', 'gmlp_norm_g', 'gmlp_ws', 'gmlp_bs', 'mla_q_norm_g', 'mla_w_uq', 'mla_kv_norm_g', 'mla_w_ukv', 'ssm_a_re', 'ssm_a_im', 'ssm_b_re', 'ssm_b_im', 'ssm_c_re', 'ssm_c_im', 'ssm_d', 'ssm_log_dt', 'ssm_glu_w', 'ssm_glu_b', 'mix_norm_g', 'w_out', 'ffn2_w_gate', 'ffn2_w_up', 'ffn2_w_down']
TWIN_WEIGHTS = ['ln_g', 'ln_b', 'ffn1_w_gate', 'ffn1_w_up', 'ffn1_w_down', 'w_in', 'gmlp_norm_g', 'gmlp_ws', 'gmlp_bs', 'mla_q_norm_g', 'mla_w_uq', 'mla_kv_norm_g', 'mla_w_ukv', 'ssm_a_re', 'ssm_a_im', 'ssm_b_re', 'ssm_b_im', 'ssm_c_re', 'ssm_c_im', 'ssm_d', 'ssm_log_dt', 'ssm_glu_w', 'ssm_glu_b', 'mix_norm_g', 'w_out', 'ffn2_w_gate', 'ffn2_w_up', 'ffn2_w_down']
TWIN_DIFF_INPUT = 'x'
TWIN_INPUTS = ['x', 'positions', 'ln_g', 'ln_b', 'ffn1_w_gate', 'ffn1_w_up', 'ffn1_w_down', 'w_in', 'gmlp_norm_g', 'gmlp_ws', 'gmlp_bs', 'mla_q_norm_g', 'mla_w_uq', 'mla_kv_norm_g', 'mla_w_ukv', 'ssm_a_re', 'ssm_a_im', 'ssm_b_re', 'ssm_b_im', 'ssm_c_re', 'ssm_c_im', 'ssm_d', 'ssm_log_dt', 'ssm_glu_w', 'ssm_glu_b', 'mix_norm_g', 'w_out', 'ffn2_w_gate', 'ffn2_w_up', 'ffn2_w_down', 'loss_target', 'm_ln_g', 'm_ln_b', 'm_ffn1_w_gate', 'm_ffn1_w_up', 'm_ffn1_w_down', 'm_w_in', 'm_gmlp_norm_g', 'm_gmlp_ws', 'm_gmlp_bs', 'm_mla_q_norm_g', 'm_mla_w_uq', 'm_mla_kv_norm_g', 'm_mla_w_ukv', 'm_ssm_a_re', 'm_ssm_a_im', 'm_ssm_b_re', 'm_ssm_b_im', 'm_ssm_c_re', 'm_ssm_c_im', 'm_ssm_d', 'm_ssm_log_dt', 'm_ssm_glu_w', 'm_ssm_glu_b', 'm_mix_norm_g', 'm_w_out', 'm_ffn2_w_gate', 'm_ffn2_w_up', 'm_ffn2_w_down', 'v_ln_g', 'v_ln_b', 'v_ffn1_w_gate', 'v_ffn1_w_up', 'v_ffn1_w_down', 'v_w_in', 'v_gmlp_norm_g', 'v_gmlp_ws', 'v_gmlp_bs', 'v_mla_q_norm_g', 'v_mla_w_uq', 'v_mla_kv_norm_g', 'v_mla_w_ukv', 'v_ssm_a_re', 'v_ssm_a_im', 'v_ssm_b_re', 'v_ssm_b_im', 'v_ssm_c_re', 'v_ssm_c_im', 'v_ssm_d', 'v_ssm_log_dt', 'v_ssm_glu_w', 'v_ssm_glu_b', 'v_mix_norm_g', 'v_w_out', 'v_ffn2_w_gate', 'v_ffn2_w_up', 'v_ffn2_w_down']
TWIN_OUTPUTS = ['loss', 'grad_x', 'grad_ln_g', 'grad_ln_b', 'grad_ffn1_w_gate', 'grad_ffn1_w_up', 'grad_ffn1_w_down', 'grad_w_in', 'grad_gmlp_norm_g', 'grad_gmlp_ws', 'grad_gmlp_bs', 'grad_mla_q_norm_g', 'grad_mla_w_uq', 'grad_mla_kv_norm_g', 'grad_mla_w_ukv', 'grad_ssm_a_re', 'grad_ssm_a_im', 'grad_ssm_b_re', 'grad_ssm_b_im', 'grad_ssm_c_re', 'grad_ssm_c_im', 'grad_ssm_d', 'grad_ssm_log_dt', 'grad_ssm_glu_w', 'grad_ssm_glu_b', 'grad_mix_norm_g', 'grad_w_out', 'grad_ffn2_w_gate', 'grad_ffn2_w_up', 'grad_ffn2_w_down', 'delta_ln_g', 'delta_ln_b', 'delta_ffn1_w_gate', 'delta_ffn1_w_up', 'delta_ffn1_w_down', 'delta_w_in', 'delta_gmlp_norm_g', 'delta_gmlp_ws', 'delta_gmlp_bs', 'delta_mla_q_norm_g', 'delta_mla_w_uq', 'delta_mla_kv_norm_g', 'delta_mla_w_ukv', 'delta_ssm_a_re', 'delta_ssm_a_im', 'delta_ssm_b_re', 'delta_ssm_b_im', 'delta_ssm_c_re', 'delta_ssm_c_im', 'delta_ssm_d', 'delta_ssm_log_dt', 'delta_ssm_glu_w', 'delta_ssm_glu_b', 'delta_mix_norm_g', 'delta_w_out', 'delta_ffn2_w_gate', 'delta_ffn2_w_up', 'delta_ffn2_w_down', 'new_m_ln_g', 'new_m_ln_b', 'new_m_ffn1_w_gate', 'new_m_ffn1_w_up', 'new_m_ffn1_w_down', 'new_m_w_in', 'new_m_gmlp_norm_g', 'new_m_gmlp_ws', 'new_m_gmlp_bs', 'new_m_mla_q_norm_g', 'new_m_mla_w_uq', 'new_m_mla_kv_norm_g', 'new_m_mla_w_ukv', 'new_m_ssm_a_re', 'new_m_ssm_a_im', 'new_m_ssm_b_re', 'new_m_ssm_b_im', 'new_m_ssm_c_re', 'new_m_ssm_c_im', 'new_m_ssm_d', 'new_m_ssm_log_dt', 'new_m_ssm_glu_w', 'new_m_ssm_glu_b', 'new_m_mix_norm_g', 'new_m_w_out', 'new_m_ffn2_w_gate', 'new_m_ffn2_w_up', 'new_m_ffn2_w_down', 'new_v_ln_g', 'new_v_ln_b', 'new_v_ffn1_w_gate', 'new_v_ffn1_w_up', 'new_v_ffn1_w_down', 'new_v_w_in', 'new_v_gmlp_norm_g', 'new_v_gmlp_ws', 'new_v_gmlp_bs', 'new_v_mla_q_norm_g', 'new_v_mla_w_uq', 'new_v_mla_kv_norm_g', 'new_v_mla_w_ukv', 'new_v_ssm_a_re', 'new_v_ssm_a_im', 'new_v_ssm_b_re', 'new_v_ssm_b_im', 'new_v_ssm_c_re', 'new_v_ssm_c_im', 'new_v_ssm_d', 'new_v_ssm_log_dt', 'new_v_ssm_glu_w', 'new_v_ssm_glu_b', 'new_v_mix_norm_g', 'new_v_w_out', 'new_v_ffn2_w_gate', 'new_v_ffn2_w_up', 'new_v_ffn2_w_down']
TWIN_LEAF_KINDS = {'loss': 'loss', 'grad_x': 'grad_x', 'grad_ln_g': 'grad_w', 'grad_ln_b': 'grad_w', 'grad_ffn1_w_gate': 'grad_w', 'grad_ffn1_w_up': 'grad_w', 'grad_ffn1_w_down': 'grad_w', 'grad_w_in': 'grad_w', 'grad_gmlp_norm_g': 'grad_w', 'grad_gmlp_ws': 'grad_w', 'grad_gmlp_bs': 'grad_w', 'grad_mla_q_norm_g': 'grad_w', 'grad_mla_w_uq': 'grad_w', 'grad_mla_kv_norm_g': 'grad_w', 'grad_mla_w_ukv': 'grad_w', 'grad_ssm_a_re': 'grad_w', 'grad_ssm_a_im': 'grad_w', 'grad_ssm_b_re': 'grad_w', 'grad_ssm_b_im': 'grad_w', 'grad_ssm_c_re': 'grad_w', 'grad_ssm_c_im': 'grad_w', 'grad_ssm_d': 'grad_w', 'grad_ssm_log_dt': 'grad_w', 'grad_ssm_glu_w': 'grad_w', 'grad_ssm_glu_b': 'grad_w', 'grad_mix_norm_g': 'grad_w', 'grad_w_out': 'grad_w', 'grad_ffn2_w_gate': 'grad_w', 'grad_ffn2_w_up': 'grad_w', 'grad_ffn2_w_down': 'grad_w', 'delta_ln_g': 'delta_w', 'delta_ln_b': 'delta_w', 'delta_ffn1_w_gate': 'delta_w', 'delta_ffn1_w_up': 'delta_w', 'delta_ffn1_w_down': 'delta_w', 'delta_w_in': 'delta_w', 'delta_gmlp_norm_g': 'delta_w', 'delta_gmlp_ws': 'delta_w', 'delta_gmlp_bs': 'delta_w', 'delta_mla_q_norm_g': 'delta_w', 'delta_mla_w_uq': 'delta_w', 'delta_mla_kv_norm_g': 'delta_w', 'delta_mla_w_ukv': 'delta_w', 'delta_ssm_a_re': 'delta_w', 'delta_ssm_a_im': 'delta_w', 'delta_ssm_b_re': 'delta_w', 'delta_ssm_b_im': 'delta_w', 'delta_ssm_c_re': 'delta_w', 'delta_ssm_c_im': 'delta_w', 'delta_ssm_d': 'delta_w', 'delta_ssm_log_dt': 'delta_w', 'delta_ssm_glu_w': 'delta_w', 'delta_ssm_glu_b': 'delta_w', 'delta_mix_norm_g': 'delta_w', 'delta_w_out': 'delta_w', 'delta_ffn2_w_gate': 'delta_w', 'delta_ffn2_w_up': 'delta_w', 'delta_ffn2_w_down': 'delta_w', 'new_m_ln_g': 'new_m', 'new_m_ln_b': 'new_m', 'new_m_ffn1_w_gate': 'new_m', 'new_m_ffn1_w_up': 'new_m', 'new_m_ffn1_w_down': 'new_m', 'new_m_w_in': 'new_m', 'new_m_gmlp_norm_g': 'new_m', 'new_m_gmlp_ws': 'new_m', 'new_m_gmlp_bs': 'new_m', 'new_m_mla_q_norm_g': 'new_m', 'new_m_mla_w_uq': 'new_m', 'new_m_mla_kv_norm_g': 'new_m', 'new_m_mla_w_ukv': 'new_m', 'new_m_ssm_a_re': 'new_m', 'new_m_ssm_a_im': 'new_m', 'new_m_ssm_b_re': 'new_m', 'new_m_ssm_b_im': 'new_m', 'new_m_ssm_c_re': 'new_m', 'new_m_ssm_c_im': 'new_m', 'new_m_ssm_d': 'new_m', 'new_m_ssm_log_dt': 'new_m', 'new_m_ssm_glu_w': 'new_m', 'new_m_ssm_glu_b': 'new_m', 'new_m_mix_norm_g': 'new_m', 'new_m_w_out': 'new_m', 'new_m_ffn2_w_gate': 'new_m', 'new_m_ffn2_w_up': 'new_m', 'new_m_ffn2_w_down': 'new_m', 'new_v_ln_g': 'new_v', 'new_v_ln_b': 'new_v', 'new_v_ffn1_w_gate': 'new_v', 'new_v_ffn1_w_up': 'new_v', 'new_v_ffn1_w_down': 'new_v', 'new_v_w_in': 'new_v', 'new_v_gmlp_norm_g': 'new_v', 'new_v_gmlp_ws': 'new_v', 'new_v_gmlp_bs': 'new_v', 'new_v_mla_q_norm_g': 'new_v', 'new_v_mla_w_uq': 'new_v', 'new_v_mla_kv_norm_g': 'new_v', 'new_v_mla_w_ukv': 'new_v', 'new_v_ssm_a_re': 'new_v', 'new_v_ssm_a_im': 'new_v', 'new_v_ssm_b_re': 'new_v', 'new_v_ssm_b_im': 'new_v', 'new_v_ssm_c_re': 'new_v', 'new_v_ssm_c_im': 'new_v', 'new_v_ssm_d': 'new_v', 'new_v_ssm_log_dt': 'new_v', 'new_v_ssm_glu_w': 'new_v', 'new_v_ssm_glu_b': 'new_v', 'new_v_mix_norm_g': 'new_v', 'new_v_w_out': 'new_v', 'new_v_ffn2_w_gate': 'new_v', 'new_v_ffn2_w_up': 'new_v', 'new_v_ffn2_w_down': 'new_v'}


def _forward(args):
    return _fwd_reference(*[args[k] for k in FWD_PARAMS])


def _output_shape():
    out = _jax.eval_shape(lambda: _forward(_fwd_setup_inputs(0)))
    return out.shape, out.dtype

N_MICROBATCH = 1
ADAM_LR = 0.001
ADAM_B1 = 0.9
ADAM_B2 = 0.999
ADAM_EPS = 1e-08
ADAM_WD = 0.01
ADAM_STEP = 10
PER_EXAMPLE_BATCH_AXIS = {'x': 0, 'positions': 0, 'loss_target': 0}
SHARED_INPUTS = []
_WEIGHT_DTYPES = {'ln_g': _jnp.float32, 'ln_b': _jnp.float32, 'ffn1_w_gate': _jnp.float32, 'ffn1_w_up': _jnp.float32, 'ffn1_w_down': _jnp.float32, 'w_in': _jnp.float32, 'gmlp_norm_g': _jnp.float32, 'gmlp_ws': _jnp.float32, 'gmlp_bs': _jnp.float32, 'mla_q_norm_g': _jnp.float32, 'mla_w_uq': _jnp.float32, 'mla_kv_norm_g': _jnp.float32, 'mla_w_ukv': _jnp.float32, 'ssm_a_re': _jnp.float32, 'ssm_a_im': _jnp.float32, 'ssm_b_re': _jnp.float32, 'ssm_b_im': _jnp.float32, 'ssm_c_re': _jnp.float32, 'ssm_c_im': _jnp.float32, 'ssm_d': _jnp.float32, 'ssm_log_dt': _jnp.float32, 'ssm_glu_w': _jnp.float32, 'ssm_glu_b': _jnp.float32, 'mix_norm_g': _jnp.float32, 'w_out': _jnp.float32, 'ffn2_w_gate': _jnp.float32, 'ffn2_w_up': _jnp.float32, 'ffn2_w_down': _jnp.float32}
MOMENT_SCALE = {'ln_g': 1.854045e+01, 'ln_b': 4.800474e+00, 'ffn1_w_gate': 1.195226e-02, 'ffn1_w_up': 1.159317e-02, 'ffn1_w_down': 4.567587e-02, 'w_in': 7.402477e-02, 'gmlp_norm_g': 2.015017e-02, 'gmlp_ws': 2.831001e-02, 'gmlp_bs': 3.642823e-02, 'mla_q_norm_g': 6.343132e-02, 'mla_w_uq': 3.713211e-02, 'mla_kv_norm_g': 2.305460e-01, 'mla_w_ukv': 6.481587e-02, 'ssm_a_re': 4.977173e-03, 'ssm_a_im': 4.617332e-03, 'ssm_b_re': 2.613238e-03, 'ssm_b_im': 2.567093e-03, 'ssm_c_re': 4.772822e-03, 'ssm_c_im': 4.893545e-03, 'ssm_d': 9.653589e-02, 'ssm_log_dt': 3.427299e+00, 'ssm_glu_w': 1.840801e-02, 'ssm_glu_b': 3.531150e-02, 'mix_norm_g': 8.149001e-02, 'w_out': 2.040257e-01, 'ffn2_w_gate': 1.175484e-02, 'ffn2_w_up': 1.139434e-02, 'ffn2_w_down': 4.490249e-02}


def _to_microbatches(a, axis):
    t = _jnp.moveaxis(a, axis, 0)
    t = t.reshape((N_MICROBATCH, t.shape[0] // N_MICROBATCH) + t.shape[1:])
    return _jnp.moveaxis(t, 1, axis + 1)


def setup_inputs(seed: int = 0) -> dict:
    inp = _fwd_setup_inputs(seed)
    key = _jax.random.fold_in(_jax.random.key(seed), 7919)
    shape, _ = _output_shape()
    out = dict(inp)
    out["loss_target"] = _jax.random.normal(_jax.random.fold_in(key, 0), shape, _jnp.float32)
    for i, name in enumerate(TWIN_WEIGHTS):
        w = inp[name].astype(_jnp.float32)
        if MOMENT_SCALE is None:
            s = _jnp.sqrt(_jnp.mean(_jnp.square(w)) + 1e-30)
        else:
            s = MOMENT_SCALE[name]
        km, kv = _jax.random.split(_jax.random.fold_in(key, i + 1))
        out[name] = w
        out["m_" + name] = s * _jax.random.normal(km, w.shape, _jnp.float32)
        out["v_" + name] = (s * s) * _jax.random.uniform(kv, w.shape, _jnp.float32, 0.5, 1.5)
    if N_MICROBATCH > 1:
        for name, axis in PER_EXAMPLE_BATCH_AXIS.items():
            out[name] = _to_microbatches(out[name], axis)
    return {'x': out['x'], 'positions': out['positions'], 'ln_g': out['ln_g'], 'ln_b': out['ln_b'], 'ffn1_w_gate': out['ffn1_w_gate'], 'ffn1_w_up': out['ffn1_w_up'], 'ffn1_w_down': out['ffn1_w_down'], 'w_in': out['w_in'], 'gmlp_norm_g': out['gmlp_norm_g'], 'gmlp_ws': out['gmlp_ws'], 'gmlp_bs': out['gmlp_bs'], 'mla_q_norm_g': out['mla_q_norm_g'], 'mla_w_uq': out['mla_w_uq'], 'mla_kv_norm_g': out['mla_kv_norm_g'], 'mla_w_ukv': out['mla_w_ukv'], 'ssm_a_re': out['ssm_a_re'], 'ssm_a_im': out['ssm_a_im'], 'ssm_b_re': out['ssm_b_re'], 'ssm_b_im': out['ssm_b_im'], 'ssm_c_re': out['ssm_c_re'], 'ssm_c_im': out['ssm_c_im'], 'ssm_d': out['ssm_d'], 'ssm_log_dt': out['ssm_log_dt'], 'ssm_glu_w': out['ssm_glu_w'], 'ssm_glu_b': out['ssm_glu_b'], 'mix_norm_g': out['mix_norm_g'], 'w_out': out['w_out'], 'ffn2_w_gate': out['ffn2_w_gate'], 'ffn2_w_up': out['ffn2_w_up'], 'ffn2_w_down': out['ffn2_w_down'], 'loss_target': out['loss_target'], 'm_ln_g': out['m_ln_g'], 'm_ln_b': out['m_ln_b'], 'm_ffn1_w_gate': out['m_ffn1_w_gate'], 'm_ffn1_w_up': out['m_ffn1_w_up'], 'm_ffn1_w_down': out['m_ffn1_w_down'], 'm_w_in': out['m_w_in'], 'm_gmlp_norm_g': out['m_gmlp_norm_g'], 'm_gmlp_ws': out['m_gmlp_ws'], 'm_gmlp_bs': out['m_gmlp_bs'], 'm_mla_q_norm_g': out['m_mla_q_norm_g'], 'm_mla_w_uq': out['m_mla_w_uq'], 'm_mla_kv_norm_g': out['m_mla_kv_norm_g'], 'm_mla_w_ukv': out['m_mla_w_ukv'], 'm_ssm_a_re': out['m_ssm_a_re'], 'm_ssm_a_im': out['m_ssm_a_im'], 'm_ssm_b_re': out['m_ssm_b_re'], 'm_ssm_b_im': out['m_ssm_b_im'], 'm_ssm_c_re': out['m_ssm_c_re'], 'm_ssm_c_im': out['m_ssm_c_im'], 'm_ssm_d': out['m_ssm_d'], 'm_ssm_log_dt': out['m_ssm_log_dt'], 'm_ssm_glu_w': out['m_ssm_glu_w'], 'm_ssm_glu_b': out['m_ssm_glu_b'], 'm_mix_norm_g': out['m_mix_norm_g'], 'm_w_out': out['m_w_out'], 'm_ffn2_w_gate': out['m_ffn2_w_gate'], 'm_ffn2_w_up': out['m_ffn2_w_up'], 'm_ffn2_w_down': out['m_ffn2_w_down'], 'v_ln_g': out['v_ln_g'], 'v_ln_b': out['v_ln_b'], 'v_ffn1_w_gate': out['v_ffn1_w_gate'], 'v_ffn1_w_up': out['v_ffn1_w_up'], 'v_ffn1_w_down': out['v_ffn1_w_down'], 'v_w_in': out['v_w_in'], 'v_gmlp_norm_g': out['v_gmlp_norm_g'], 'v_gmlp_ws': out['v_gmlp_ws'], 'v_gmlp_bs': out['v_gmlp_bs'], 'v_mla_q_norm_g': out['v_mla_q_norm_g'], 'v_mla_w_uq': out['v_mla_w_uq'], 'v_mla_kv_norm_g': out['v_mla_kv_norm_g'], 'v_mla_w_ukv': out['v_mla_w_ukv'], 'v_ssm_a_re': out['v_ssm_a_re'], 'v_ssm_a_im': out['v_ssm_a_im'], 'v_ssm_b_re': out['v_ssm_b_re'], 'v_ssm_b_im': out['v_ssm_b_im'], 'v_ssm_c_re': out['v_ssm_c_re'], 'v_ssm_c_im': out['v_ssm_c_im'], 'v_ssm_d': out['v_ssm_d'], 'v_ssm_log_dt': out['v_ssm_log_dt'], 'v_ssm_glu_w': out['v_ssm_glu_w'], 'v_ssm_glu_b': out['v_ssm_glu_b'], 'v_mix_norm_g': out['v_mix_norm_g'], 'v_w_out': out['v_w_out'], 'v_ffn2_w_gate': out['v_ffn2_w_gate'], 'v_ffn2_w_up': out['v_ffn2_w_up'], 'v_ffn2_w_down': out['v_ffn2_w_down']}


def _loss(weights, diff, rest, loss_target):
    with _jax.named_scope("forward"):
        args = {**rest, TWIN_DIFF_INPUT: diff, **{k: w.astype(_WEIGHT_DTYPES[k]) for k, w in weights.items()}}
        y = _forward(args)
    with _jax.named_scope("loss_head"):
        err = _jnp.square(y.astype(_jnp.float32) - loss_target)
        return 0.5 * _jnp.sum(_jnp.mean(err, axis=-1)) if err.ndim else 0.5 * err


def _adamw(w, g, m, v):
    m = ADAM_B1 * m + (1.0 - ADAM_B1) * g
    v = ADAM_B2 * v + (1.0 - ADAM_B2) * _jnp.square(g)
    m_hat = m / (1.0 - ADAM_B1 ** ADAM_STEP)
    v_hat = v / (1.0 - ADAM_B2 ** ADAM_STEP)
    delta = -ADAM_LR * (m_hat / (_jnp.sqrt(v_hat) + ADAM_EPS) + ADAM_WD * w)
    return delta, m, v


def reference(x, positions, ln_g, ln_b, ffn1_w_gate, ffn1_w_up, ffn1_w_down, w_in, gmlp_norm_g, gmlp_ws, gmlp_bs, mla_q_norm_g, mla_w_uq, mla_kv_norm_g, mla_w_ukv, ssm_a_re, ssm_a_im, ssm_b_re, ssm_b_im, ssm_c_re, ssm_c_im, ssm_d, ssm_log_dt, ssm_glu_w, ssm_glu_b, mix_norm_g, w_out, ffn2_w_gate, ffn2_w_up, ffn2_w_down, loss_target, m_ln_g, m_ln_b, m_ffn1_w_gate, m_ffn1_w_up, m_ffn1_w_down, m_w_in, m_gmlp_norm_g, m_gmlp_ws, m_gmlp_bs, m_mla_q_norm_g, m_mla_w_uq, m_mla_kv_norm_g, m_mla_w_ukv, m_ssm_a_re, m_ssm_a_im, m_ssm_b_re, m_ssm_b_im, m_ssm_c_re, m_ssm_c_im, m_ssm_d, m_ssm_log_dt, m_ssm_glu_w, m_ssm_glu_b, m_mix_norm_g, m_w_out, m_ffn2_w_gate, m_ffn2_w_up, m_ffn2_w_down, v_ln_g, v_ln_b, v_ffn1_w_gate, v_ffn1_w_up, v_ffn1_w_down, v_w_in, v_gmlp_norm_g, v_gmlp_ws, v_gmlp_bs, v_mla_q_norm_g, v_mla_w_uq, v_mla_kv_norm_g, v_mla_w_ukv, v_ssm_a_re, v_ssm_a_im, v_ssm_b_re, v_ssm_b_im, v_ssm_c_re, v_ssm_c_im, v_ssm_d, v_ssm_log_dt, v_ssm_glu_w, v_ssm_glu_b, v_mix_norm_g, v_w_out, v_ffn2_w_gate, v_ffn2_w_up, v_ffn2_w_down):
    given = dict(x=x, positions=positions, ln_g=ln_g, ln_b=ln_b, ffn1_w_gate=ffn1_w_gate, ffn1_w_up=ffn1_w_up, ffn1_w_down=ffn1_w_down, w_in=w_in, gmlp_norm_g=gmlp_norm_g, gmlp_ws=gmlp_ws, gmlp_bs=gmlp_bs, mla_q_norm_g=mla_q_norm_g, mla_w_uq=mla_w_uq, mla_kv_norm_g=mla_kv_norm_g, mla_w_ukv=mla_w_ukv, ssm_a_re=ssm_a_re, ssm_a_im=ssm_a_im, ssm_b_re=ssm_b_re, ssm_b_im=ssm_b_im, ssm_c_re=ssm_c_re, ssm_c_im=ssm_c_im, ssm_d=ssm_d, ssm_log_dt=ssm_log_dt, ssm_glu_w=ssm_glu_w, ssm_glu_b=ssm_glu_b, mix_norm_g=mix_norm_g, w_out=w_out, ffn2_w_gate=ffn2_w_gate, ffn2_w_up=ffn2_w_up, ffn2_w_down=ffn2_w_down, loss_target=loss_target, m_ln_g=m_ln_g, m_ln_b=m_ln_b, m_ffn1_w_gate=m_ffn1_w_gate, m_ffn1_w_up=m_ffn1_w_up, m_ffn1_w_down=m_ffn1_w_down, m_w_in=m_w_in, m_gmlp_norm_g=m_gmlp_norm_g, m_gmlp_ws=m_gmlp_ws, m_gmlp_bs=m_gmlp_bs, m_mla_q_norm_g=m_mla_q_norm_g, m_mla_w_uq=m_mla_w_uq, m_mla_kv_norm_g=m_mla_kv_norm_g, m_mla_w_ukv=m_mla_w_ukv, m_ssm_a_re=m_ssm_a_re, m_ssm_a_im=m_ssm_a_im, m_ssm_b_re=m_ssm_b_re, m_ssm_b_im=m_ssm_b_im, m_ssm_c_re=m_ssm_c_re, m_ssm_c_im=m_ssm_c_im, m_ssm_d=m_ssm_d, m_ssm_log_dt=m_ssm_log_dt, m_ssm_glu_w=m_ssm_glu_w, m_ssm_glu_b=m_ssm_glu_b, m_mix_norm_g=m_mix_norm_g, m_w_out=m_w_out, m_ffn2_w_gate=m_ffn2_w_gate, m_ffn2_w_up=m_ffn2_w_up, m_ffn2_w_down=m_ffn2_w_down, v_ln_g=v_ln_g, v_ln_b=v_ln_b, v_ffn1_w_gate=v_ffn1_w_gate, v_ffn1_w_up=v_ffn1_w_up, v_ffn1_w_down=v_ffn1_w_down, v_w_in=v_w_in, v_gmlp_norm_g=v_gmlp_norm_g, v_gmlp_ws=v_gmlp_ws, v_gmlp_bs=v_gmlp_bs, v_mla_q_norm_g=v_mla_q_norm_g, v_mla_w_uq=v_mla_w_uq, v_mla_kv_norm_g=v_mla_kv_norm_g, v_mla_w_ukv=v_mla_w_ukv, v_ssm_a_re=v_ssm_a_re, v_ssm_a_im=v_ssm_a_im, v_ssm_b_re=v_ssm_b_re, v_ssm_b_im=v_ssm_b_im, v_ssm_c_re=v_ssm_c_re, v_ssm_c_im=v_ssm_c_im, v_ssm_d=v_ssm_d, v_ssm_log_dt=v_ssm_log_dt, v_ssm_glu_w=v_ssm_glu_w, v_ssm_glu_b=v_ssm_glu_b, v_mix_norm_g=v_mix_norm_g, v_w_out=v_w_out, v_ffn2_w_gate=v_ffn2_w_gate, v_ffn2_w_up=v_ffn2_w_up, v_ffn2_w_down=v_ffn2_w_down)
    weights = {n: given[n] for n in TWIN_WEIGHTS}
    shared = {n: given[n] for n in SHARED_INPUTS}
    per_example = {n: given[n] for n in ['x', 'positions']}
    grad_fn = _jax.value_and_grad(_loss, argnums=(0, 1))

    def one_microbatch(ex, loss_target):
        ex = dict(ex)
        diff = ex.pop(TWIN_DIFF_INPUT)
        return grad_fn(weights, diff, {**shared, **ex}, loss_target)

    if N_MICROBATCH == 1:
        loss, (grad_w, grad_x) = one_microbatch(per_example, given["loss_target"])
    else:
        def body(carry, xs):
            loss_sum, grad_sum = carry
            l_k, (gw_k, gx_k) = one_microbatch(xs[0], xs[1])
            with _jax.named_scope("update"):
                return (loss_sum + l_k, _jax.tree.map(_jnp.add, grad_sum, gw_k)), gx_k

        init = (_jnp.zeros((), _jnp.float32), _jax.tree.map(_jnp.zeros_like, weights))
        (loss, grad_w), grad_x = _jax.lax.scan(body, init, (per_example, given["loss_target"]))
    with _jax.named_scope("update"):
        delta_w, new_m, new_v = {}, {}, {}
        for n in TWIN_WEIGHTS:
            delta_w[n], new_m[n], new_v[n] = _adamw(weights[n], grad_w[n], given["m_" + n], given["v_" + n])
    return (loss, grad_x, *[grad_w[n] for n in TWIN_WEIGHTS], *[delta_w[n] for n in TWIN_WEIGHTS],
            *[new_m[n] for n in TWIN_WEIGHTS], *[new_v[n] for n in TWIN_WEIGHTS])
```

```python
import functools
import math

import jax
import jax.numpy as jnp
from jax import lax
from jax.experimental import pallas as pl
from jax.experimental.pallas import tpu as pltpu

f32 = jnp.float32
bf16 = jnp.bfloat16

D_MODEL = 1024
DEPTH = 4
D_FF = 2816
GM_HEADS, GM_HEAD_DIM, GM_WIDTH, GM_CHUNK = 4, 64, 256, 128
MLA_HEADS, MLA_NOPE, MLA_ROPE, MLA_V = 8, 64, 32, 64
ROPE_HALF = MLA_ROPE // 2
Q_LORA, KV_LORA = 256, 128
ROPE_BASE = 10000.0
SSM_GROUPS, SSM_GROUP_CH, SSM_WIDTH, SSM_STATE = 16, 16, 256, 64
N_STATE = SSM_GROUPS * SSM_STATE
ALPHA = (2 * DEPTH) ** 0.25
LN_EPS = 1e-5
RMS_EPS = 1e-6
NEG_BIG = -1e30
ATT_SCALE = (MLA_NOPE + MLA_ROPE) ** -0.5
ADAM_LR, ADAM_B1, ADAM_B2, ADAM_EPS, ADAM_WD, ADAM_STEP = 0.001, 0.9, 0.999, 1e-08, 0.01, 10

N_DEV = 8
LANES = 128
VMEM_LIMIT = 48 * 1024 * 1024
MESH = pl.DeviceIdType.MESH

H_UG, H_VG, H_CQ, H_US, H_CKV, H_K1, H_K2, H_COLS = 0, 256, 512, 768, 1024, 1152, 1280, 1408

W_NAMES = ['ln_g', 'ln_b', 'ffn1_w_gate', 'ffn1_w_up', 'ffn1_w_down', 'w_in', 'gmlp_norm_g', 'gmlp_ws', 'gmlp_bs',
           'mla_q_norm_g', 'mla_w_uq', 'mla_kv_norm_g', 'mla_w_ukv', 'ssm_a_re', 'ssm_a_im', 'ssm_b_re', 'ssm_b_im',
           'ssm_c_re', 'ssm_c_im', 'ssm_d', 'ssm_log_dt', 'ssm_glu_w', 'ssm_glu_b', 'mix_norm_g', 'w_out',
           'ffn2_w_gate', 'ffn2_w_up', 'ffn2_w_down']
BIG = {'ffn1_w_gate': 1, 'ffn1_w_up': 1, 'ffn1_w_down': 0, 'w_in': 1, 'mla_w_uq': 1, 'mla_w_ukv': 1,
       'ssm_glu_w': 0, 'w_out': 0, 'ffn2_w_gate': 1, 'ffn2_w_up': 1, 'ffn2_w_down': 0}
BIG_SHAPE = {'ffn1_w_gate': (D_MODEL, D_FF), 'ffn1_w_up': (D_MODEL, D_FF), 'ffn1_w_down': (D_FF, D_MODEL),
             'w_in': (D_MODEL, 1184), 'mla_w_uq': (Q_LORA, 768), 'mla_w_ukv': (KV_LORA, 1024),
             'ssm_glu_w': (SSM_WIDTH, SSM_WIDTH), 'w_out': (D_MODEL, D_MODEL),
             'ffn2_w_gate': (D_MODEL, D_FF), 'ffn2_w_up': (D_MODEL, D_FF), 'ffn2_w_down': (D_FF, D_MODEL)}
LN_NAMES = ['ln_g', 'ln_b']
REPL = [n for n in W_NAMES if n not in BIG and n not in LN_NAMES]


def _pick(n, cands):
    for c in cands:
        if n % c == 0:
            return c
    return n


def _params(sem):
    return pltpu.CompilerParams(dimension_semantics=sem, vmem_limit_bytes=VMEM_LIMIT)


def mm(a, b, *, out_dtype=f32, add=None, add_scale=1.0, a_col0=0, k_dim=None, name):
    M = a.shape[0]
    K, N = b.shape
    tm = _pick(M, (512, 256, 128, 64, 32, 16, 8))
    tn = _pick(N, (512, 384, 256, 128))
    tk = K if K <= 2048 else _pick(K, (1024, 512, 256, 128))
    nk = K // tk
    assert a_col0 % tk == 0 and a_col0 + K <= a.shape[1]
    kb0 = a_col0 // tk
    has_add = add is not None

    def body(*refs):
        if has_add:
            a_ref, b_ref, add_ref, o_ref, acc = refs
        else:
            a_ref, b_ref, o_ref, acc = refs
        k = pl.program_id(2)
        part = jnp.dot(a_ref[...].astype(bf16), b_ref[...].astype(bf16), preferred_element_type=f32)

        def finish(total):
            if has_add:
                total = total + add_scale * add_ref[...]
            o_ref[...] = total.astype(o_ref.dtype)

        if nk == 1:
            finish(part)
        else:
            @pl.when(k == 0)
            def _():
                acc[...] = part

            @pl.when(k > 0)
            def _():
                acc[...] += part

            @pl.when(k == nk - 1)
            def _():
                finish(acc[...])

    in_specs = [pl.BlockSpec((tm, tk), lambda i, j, k: (i, kb0 + k)), pl.BlockSpec((tk, tn), lambda i, j, k: (k, j))]
    ops = [a, b]
    if has_add:
        in_specs.append(pl.BlockSpec((tm, tn), lambda i, j, k: (i, j)))
        ops.append(add)
    return pl.pallas_call(
        body, grid=(M // tm, N // tn, nk), in_specs=in_specs,
        out_specs=pl.BlockSpec((tm, tn), lambda i, j, k: (i, j)),
        out_shape=jax.ShapeDtypeStruct((M, N), out_dtype),
        scratch_shapes=[pltpu.VMEM((tm, tn) if nk > 1 else (8, LANES), f32)],
        compiler_params=_params(("parallel", "parallel", "arbitrary")), name=name)(*ops)


def mm_tn(a, b, *, a_col0=0, m_dim=None, name):
    K = a.shape[0]
    M = a.shape[1] if m_dim is None else m_dim
    N = b.shape[1]
    tm = _pick(M, (1024, 768, 512, 384, 256, 128))
    tn = _pick(N, (1024, 768, 512, 384, 256, 128))
    tk = _pick(K, (512, 256, 128, 64, 32, 16))
    nk = K // tk
    assert a_col0 % tm == 0
    mb0 = a_col0 // tm

    def body(a_ref, b_ref, o_ref, acc):
        k = pl.program_id(2)
        part = lax.dot_general(a_ref[...].astype(bf16), b_ref[...].astype(bf16), (((0,), (0,)), ((), ())),
                               preferred_element_type=f32)

        @pl.when(k == 0)
        def _():
            acc[...] = part

        @pl.when(k > 0)
        def _():
            acc[...] += part

        @pl.when(k == nk - 1)
        def _():
            o_ref[...] = acc[...]

    return pl.pallas_call(
        body, grid=(M // tm, N // tn, nk),
        in_specs=[pl.BlockSpec((tk, tm), lambda i, j, k: (k, mb0 + i)), pl.BlockSpec((tk, tn), lambda i, j, k: (k, j))],
        out_specs=pl.BlockSpec((tm, tn), lambda i, j, k: (i, j)),
        out_shape=jax.ShapeDtypeStruct((M, N), f32),
        scratch_shapes=[pltpu.VMEM((tm, tn), f32)],
        compiler_params=_params(("parallel", "parallel", "arbitrary")), name=name)(a, b)


def rowwise(fn, rows, pars, out_rows, out_accs, tm, name):
    first = rows[0]
    if isinstance(first, tuple):
        R = first[0].shape[1] if first[1] == 'lead' else first[0].shape[0]
    else:
        R = first.shape[0]
    assert R % tm == 0, (R, tm, name)
    n_rows, n_pars, n_or, n_oa = len(rows), len(pars), len(out_rows), len(out_accs)

    in_specs, ops = [], []
    for r in rows:
        if isinstance(r, tuple) and r[1] == 'lead':
            arr, _, kk = r
            in_specs.append(pl.BlockSpec((None, tm, arr.shape[2]), lambda i, kk=kk: (kk, i, 0)))
        elif isinstance(r, tuple):
            arr, c0, w = r
            assert c0 % w == 0
            in_specs.append(pl.BlockSpec((tm, w), lambda i, cb=c0 // w: (i, cb)))
        else:
            arr = r
            in_specs.append(pl.BlockSpec((tm, arr.shape[1]), lambda i: (i, 0)))
        ops.append(arr)
    for p in pars:
        in_specs.append(pl.BlockSpec(p.shape, lambda i, nd=p.ndim: (0,) * nd))
        ops.append(p)
    out_specs = [pl.BlockSpec((tm, w), lambda i: (i, 0)) for (w, _) in out_rows]
    out_specs += [pl.BlockSpec(s, lambda i, nd=len(s): (0,) * nd) for s in out_accs]
    out_shape = [jax.ShapeDtypeStruct((R, w), dt) for (w, dt) in out_rows]
    out_shape += [jax.ShapeDtypeStruct(s, f32) for s in out_accs]

    def body(*refs):
        ins = [r[...] for r in refs[:n_rows + n_pars]]
        o_refs = refs[n_rows + n_pars:]
        res = fn(*ins)
        if not isinstance(res, (tuple, list)):
            res = (res,)
        assert len(res) == n_or + n_oa, (len(res), n_or, n_oa, name)
        for o, v in zip(o_refs[:n_or], res[:n_or]):
            o[...] = v.astype(o.dtype)
        if n_oa:
            i = pl.program_id(0)

            @pl.when(i == 0)
            def _():
                for o, v in zip(o_refs[n_or:], res[n_or:]):
                    o[...] = v.astype(f32)

            @pl.when(i > 0)
            def _():
                for o, v in zip(o_refs[n_or:], res[n_or:]):
                    o[...] += v.astype(f32)

    return pl.pallas_call(
        body, grid=(R // tm,), in_specs=in_specs, out_specs=out_specs, out_shape=out_shape,
        compiler_params=_params(("arbitrary",)), name=name)(*ops)


def whole(fn, ins, out_shapes, name):
    def body(*refs):
        res = fn(*[r[...] for r in refs[:len(ins)]])
        for o, v in zip(refs[len(ins):], res):
            o[...] = v

    return pl.pallas_call(body, out_shape=[jax.ShapeDtypeStruct(s, f32) for s in out_shapes], name=name)(*ins)


@jax.custom_vjp
def _bdot(a, b):
    return jnp.dot(a.astype(bf16), b.astype(bf16), preferred_element_type=f32)


def _bdot_fwd(a, b):
    return _bdot(a, b), (a, b)


def _bdot_bwd(res, g):
    a, b = res
    gb = g.astype(bf16)
    da = lax.dot_general(gb, b.astype(bf16), (((1,), (1,)), ((), ())), preferred_element_type=f32)
    db = lax.dot_general(a.astype(bf16), gb, (((0,), (0,)), ((), ())), preferred_element_type=f32)
    return da, db


_bdot.defvjp(_bdot_fwd, _bdot_bwd)


def _ln(z, g, b):
    mu = jnp.mean(z, axis=-1, keepdims=True)
    var = jnp.mean(jnp.square(z - mu), axis=-1, keepdims=True)
    return (z - mu) * lax.rsqrt(var + LN_EPS) * g + b


def _rms_only(x):
    return x * lax.rsqrt(jnp.mean(jnp.square(x), axis=-1, keepdims=True) + RMS_EPS)


def _swiglu(a, b):
    return jax.nn.silu(a) * b


def _gmlp(hu, hv, ng, ws, bsb):
    u = jax.nn.gelu(hu)
    v = jax.nn.gelu(hv)
    lane = lax.broadcasted_iota(jnp.int32, (1, GM_WIDTH), 1)
    masks = [((lane >= GM_HEAD_DIM * h) & (lane < GM_HEAD_DIM * (h + 1))).astype(f32) for h in range(GM_HEADS)]
    mu = jnp.zeros_like(v)
    for m in masks:
        mu = mu + m * (jnp.sum(v * m, axis=-1, keepdims=True) / GM_HEAD_DIM)
    d = v - mu
    var = jnp.zeros_like(v)
    for m in masks:
        var = var + m * (jnp.sum(d * d * m, axis=-1, keepdims=True) / GM_HEAD_DIM)
    vn = d * lax.rsqrt(var + LN_EPS) * ng
    r = lax.broadcasted_iota(jnp.int32, (GM_CHUNK, GM_CHUNK), 0)
    c = lax.broadcasted_iota(jnp.int32, (GM_CHUNK, GM_CHUNK), 1)
    tril = (c <= r).astype(f32)
    z = bsb
    for h, m in enumerate(masks):
        z = z + _bdot(ws[h] * tril, vn * m)
    return u * z


def _mla_prep(cq, ckv, qg, kvg):
    return _rms_only(cq) * qg, _rms_only(ckv) * kvg


def _rope(q1, q2, k1, k2, cos, sin):
    return q1 * cos - q2 * sin, q2 * cos + q1 * sin, k1 * cos - k2 * sin, k2 * cos + k1 * sin


def _mix_post(ya, ob, y1, us, dskip, gluw, glub, gmix):
    y = jax.nn.gelu(y1 + dskip * us)
    yc = y * jax.nn.sigmoid(_bdot(y, gluw) + glub)
    return jnp.concatenate([_rms_only(ya), _rms_only(ob), _rms_only(yc)], axis=1) * gmix


def _ssm_prep(ar, ai, ldt, brT, biT):
    dt = jnp.exp(ldt)
    mag = jnp.exp(ar * dt)
    abr = mag * jnp.cos(ai * dt)
    abi = mag * jnp.sin(ai * dt)
    den = ar * ar + ai * ai
    cr = ((abr - 1.0) * ar + abi * ai) / den
    ci = (abi * ar - (abr - 1.0) * ai) / den
    return abr, abi, cr * brT - ci * biT, cr * biT + ci * brT


def _att_tile(S):
    return _pick(S, (512, 256, 128))


def _scores(q, k, i, j, T):
    s = lax.dot_general(q, k, (((1,), (1,)), ((), ())), preferred_element_type=f32) * ATT_SCALE
    rows = i * T + lax.broadcasted_iota(jnp.int32, (T, T), 0)
    cols = j * T + lax.broadcasted_iota(jnp.int32, (T, T), 1)
    return s, cols <= rows


def flash_fwd(q, k, v):
    Hh, S, _ = q.shape
    T = _att_tile(S)
    n = S // T

    def body(q_ref, k_ref, v_ref, o_ref, lse_ref, m_sc, l_sc, acc_sc):
        i, j = pl.program_id(1), pl.program_id(2)

        @pl.when(j == 0)
        def _():
            m_sc[...] = jnp.full_like(m_sc, NEG_BIG)
            l_sc[...] = jnp.zeros_like(l_sc)
            acc_sc[...] = jnp.zeros_like(acc_sc)

        @pl.when(j <= i)
        def _():
            s, keep = _scores(q_ref[0], k_ref[0], i, j, T)
            s = jnp.where(keep, s, NEG_BIG)
            m_old = m_sc[...]
            m_new = jnp.maximum(m_old, jnp.max(s, axis=-1, keepdims=True))
            alpha = jnp.exp(m_old - m_new)
            p = jnp.exp(s - m_new)
            l_sc[...] = alpha * l_sc[...] + jnp.sum(p, axis=-1, keepdims=True)
            acc_sc[...] = alpha * acc_sc[...] + jnp.dot(p.astype(bf16), v_ref[0], preferred_element_type=f32)
            m_sc[...] = m_new

        @pl.when(j == i)
        def _():
            o_ref[0] = acc_sc[...] / l_sc[...]
            lse_ref[0] = m_sc[...] + jnp.log(l_sc[...])

    kv_map = lambda h, i, j: (h, jnp.minimum(i, j), 0)
    return pl.pallas_call(
        body, grid=(Hh, n, n),
        in_specs=[pl.BlockSpec((1, T, LANES), lambda h, i, j: (h, i, 0)), pl.BlockSpec((1, T, LANES), kv_map),
                  pl.BlockSpec((1, T, MLA_V), kv_map)],
        out_specs=[pl.BlockSpec((1, T, MLA_V), lambda h, i, j: (h, i, 0)), pl.BlockSpec((1, T, 1), lambda h, i, j: (h, i, 0))],
        out_shape=[jax.ShapeDtypeStruct((Hh, S, MLA_V), f32), jax.ShapeDtypeStruct((Hh, S, 1), f32)],
        scratch_shapes=[pltpu.VMEM((T, 1), f32), pltpu.VMEM((T, 1), f32), pltpu.VMEM((T, MLA_V), f32)],
        compiler_params=_params(("parallel", "parallel", "arbitrary")), name="flash_fwd")(q, k, v)


def _att_grads(q, k, v, do, lse, delta, i, j, T):
    s, keep = _scores(q, k, i, j, T)
    p = jnp.where(keep, jnp.exp(s - lse), 0.0)
    dp = lax.dot_general(do, v, (((1,), (1,)), ((), ())), preferred_element_type=f32)
    ds = p * (dp - delta) * ATT_SCALE
    return p.astype(bf16), ds.astype(bf16)


def flash_dq(q, k, v, do, lse, delta):
    Hh, S, _ = q.shape
    T = _att_tile(S)
    n = S // T

    def body(q_ref, k_ref, v_ref, do_ref, lse_ref, dl_ref, dq_ref, acc):
        i, j = pl.program_id(1), pl.program_id(2)

        @pl.when(j == 0)
        def _():
            acc[...] = jnp.zeros_like(acc)

        @pl.when(j <= i)
        def _():
            _, ds = _att_grads(q_ref[0], k_ref[0], v_ref[0], do_ref[0], lse_ref[0], dl_ref[0], i, j, T)
            acc[...] += jnp.dot(ds, k_ref[0], preferred_element_type=f32)

        @pl.when(j == i)
        def _():
            dq_ref[0] = acc[...]

    q_map = lambda h, i, j: (h, i, 0)
    kv_map = lambda h, i, j: (h, jnp.minimum(i, j), 0)
    return pl.pallas_call(
        body, grid=(Hh, n, n),
        in_specs=[pl.BlockSpec((1, T, LANES), q_map), pl.BlockSpec((1, T, LANES), kv_map), pl.BlockSpec((1, T, MLA_V), kv_map),
                  pl.BlockSpec((1, T, MLA_V), q_map), pl.BlockSpec((1, T, 1), q_map), pl.BlockSpec((1, T, 1), q_map)],
        out_specs=pl.BlockSpec((1, T, LANES), q_map),
        out_shape=jax.ShapeDtypeStruct((Hh, S, LANES), f32),
        scratch_shapes=[pltpu.VMEM((T, LANES), f32)],
        compiler_params=_params(("parallel", "parallel", "arbitrary")), name="flash_dq")(q, k, v, do, lse, delta)


def flash_dkv(q, k, v, do, lse, delta):
    Hh, S, _ = q.shape
    T = _att_tile(S)
    n = S // T

    def body(q_ref, k_ref, v_ref, do_ref, lse_ref, dl_ref, dk_ref, dv_ref, dk_acc, dv_acc):
        j, i = pl.program_id(1), pl.program_id(2)

        @pl.when(i == 0)
        def _():
            dk_acc[...] = jnp.zeros_like(dk_acc)
            dv_acc[...] = jnp.zeros_like(dv_acc)

        @pl.when(i >= j)
        def _():
            p, ds = _att_grads(q_ref[0], k_ref[0], v_ref[0], do_ref[0], lse_ref[0], dl_ref[0], i, j, T)
            tn = (((0,), (0,)), ((), ()))
            dv_acc[...] += lax.dot_general(p, do_ref[0], tn, preferred_element_type=f32)
            dk_acc[...] += lax.dot_general(ds, q_ref[0], tn, preferred_element_type=f32)

        @pl.when(i == n - 1)
        def _():
            dk_ref[0] = dk_acc[...]
            dv_ref[0] = dv_acc[...]

    q_map = lambda h, j, i: (h, jnp.maximum(i, j), 0)
    kv_map = lambda h, j, i: (h, j, 0)
    return pl.pallas_call(
        body, grid=(Hh, n, n),
        in_specs=[pl.BlockSpec((1, T, LANES), q_map), pl.BlockSpec((1, T, LANES), kv_map), pl.BlockSpec((1, T, MLA_V), kv_map),
                  pl.BlockSpec((1, T, MLA_V), q_map), pl.BlockSpec((1, T, 1), q_map), pl.BlockSpec((1, T, 1), q_map)],
        out_specs=[pl.BlockSpec((1, T, LANES), kv_map), pl.BlockSpec((1, T, MLA_V), kv_map)],
        out_shape=[jax.ShapeDtypeStruct((Hh, S, LANES), f32), jax.ShapeDtypeStruct((Hh, S, MLA_V), f32)],
        scratch_shapes=[pltpu.VMEM((T, LANES), f32), pltpu.VMEM((T, MLA_V), f32)],
        compiler_params=_params(("parallel", "parallel", "arbitrary")), name="flash_dkv")(q, k, v, do, lse, delta)


def _scan_tile(S):
    return _pick(S, (256, 128, 64, 32, 16, 8))


def scan_fwd(bu3, a16):
    S = bu3.shape[0]
    ts = _scan_tile(S)

    def body(bu_ref, a_ref, o_ref, h_sc):
        @pl.when(pl.program_id(0) == 0)
        def _():
            h_sc[...] = jnp.zeros_like(h_sc)

        ar, ai = a_ref[0:8, :], a_ref[8:16, :]

        def step(t, carry):
            hr, hi = carry
            nr = ar * hr - ai * hi + bu_ref[t, 0:8, :]
            ni = ar * hi + ai * hr + bu_ref[t, 8:16, :]
            o_ref[t, 0:8, :] = nr
            o_ref[t, 8:16, :] = ni
            return nr, ni

        hr, hi = lax.fori_loop(0, ts, step, (h_sc[0:8, :], h_sc[8:16, :]), unroll=8)
        h_sc[0:8, :] = hr
        h_sc[8:16, :] = hi

    blk = pl.BlockSpec((ts, 16, LANES), lambda i: (i, 0, 0))
    return pl.pallas_call(
        body, grid=(S // ts,), in_specs=[blk, pl.BlockSpec((16, LANES), lambda i: (0, 0))], out_specs=blk,
        out_shape=jax.ShapeDtypeStruct(bu3.shape, f32), scratch_shapes=[pltpu.VMEM((16, LANES), f32)],
        compiler_params=_params(("arbitrary",)), name="scan_fwd")(bu3, a16)


def scan_bwd(g3, h3, a16):
    S = g3.shape[0]
    ts = _scan_tile(S)
    nb = S // ts

    def body(g_ref, h_ref, a_ref, o_ref, da_ref, lam_sc, da_sc):
        @pl.when(pl.program_id(0) == 0)
        def _():
            lam_sc[...] = jnp.zeros_like(lam_sc)
            da_sc[...] = jnp.zeros_like(da_sc)

        ar, ai = a_ref[0:8, :], a_ref[8:16, :]

        def step(kk, carry):
            lr, li, dar, dai = carry
            t = ts - 1 - kk
            hr, hi = h_ref[t, 0:8, :], h_ref[t, 8:16, :]
            dar = dar + lr * hr + li * hi
            dai = dai + li * hr - lr * hi
            nlr = ar * lr + ai * li + g_ref[t, 0:8, :]
            nli = ar * li - ai * lr + g_ref[t, 8:16, :]
            o_ref[t, 0:8, :] = nlr
            o_ref[t, 8:16, :] = nli
            return nlr, nli, dar, dai

        lr, li, dar, dai = lax.fori_loop(
            0, ts, step, (lam_sc[0:8, :], lam_sc[8:16, :], da_sc[0:8, :], da_sc[8:16, :]), unroll=8)
        lam_sc[0:8, :] = lr
        lam_sc[8:16, :] = li
        da_sc[0:8, :] = dar
        da_sc[8:16, :] = dai
        da_ref[0:8, :] = dar
        da_ref[8:16, :] = dai

    blk = pl.BlockSpec((ts, 16, LANES), lambda i: (nb - 1 - i, 0, 0))
    small = pl.BlockSpec((16, LANES), lambda i: (0, 0))
    return pl.pallas_call(
        body, grid=(nb,), in_specs=[blk, blk, small], out_specs=[blk, small],
        out_shape=[jax.ShapeDtypeStruct(g3.shape, f32), jax.ShapeDtypeStruct((16, LANES), f32)],
        scratch_shapes=[pltpu.VMEM((16, LANES), f32), pltpu.VMEM((16, LANES), f32)],
        compiler_params=_params(("arbitrary",)), name="scan_bwd")(g3, h3, a16)


HBM_SPEC = pl.BlockSpec(memory_space=pltpu.HBM)


def _my_id():
    return 4 * lax.axis_index("x") + 2 * lax.axis_index("y") + lax.axis_index("c")


def all_gather(xs, name):
    n = len(xs)

    def body(*refs):
        x_refs, o_refs = refs[:n], refs[n:2 * n]
        send_sems, recv_sems, local_sems = refs[2 * n:]
        x, y, c = lax.axis_index("x"), lax.axis_index("y"), lax.axis_index("c")
        me, sibling = (x, y, c), (x, y, 1 - c)
        chips = [(1 - x, y), (x, 1 - y), (1 - x, 1 - y)]

        def slot(o, p):
            return o.at[4 * p[0] + 2 * p[1] + p[2]]

        def copy(a, k, block, to, src=None):
            o = o_refs[a]
            return pltpu.make_async_remote_copy(
                src_ref=slot(o, block) if src is None else src, dst_ref=slot(o, block),
                send_sem=send_sems.at[7 * a + k], recv_sem=recv_sems.at[7 * a + k], device_id=to, device_id_type=MESH)

        own, sends = [], []
        for a in range(n):
            mine = pltpu.make_async_copy(x_refs[a], slot(o_refs[a], me), local_sems.at[a])
            mine.start()
            own.append(mine)
            first = [copy(a, 0, me, sibling, src=x_refs[a])]
            first += [copy(a, 1 + j, me, (*chip, c), src=x_refs[a]) for j, chip in enumerate(chips)]
            for cp in first:
                cp.start()
            sends += first
        for a in range(n):
            for j, chip in enumerate(chips):
                copy(a, 1 + j, (*chip, c), me).wait_recv()
                fwd = copy(a, 4 + j, (*chip, c), sibling)
                fwd.start()
                sends.append(fwd)
        for a in range(n):
            copy(a, 0, sibling, me).wait_recv()
            for j, chip in enumerate(chips):
                copy(a, 4 + j, (*chip, 1 - c), me).wait_recv()
        for cp in sends:
            cp.wait_send()
        for cp in own:
            cp.wait()

    return pl.pallas_call(
        body, out_shape=[jax.ShapeDtypeStruct((N_DEV,) + v.shape, v.dtype) for v in xs],
        in_specs=[HBM_SPEC] * n, out_specs=[HBM_SPEC] * n,
        scratch_shapes=[pltpu.SemaphoreType.DMA((7 * n,)), pltpu.SemaphoreType.DMA((7 * n,)), pltpu.SemaphoreType.DMA((n,))],
        name=name)(*xs)


def all_to_all(xs, name):
    n = len(xs)

    def body(*refs):
        x_refs, o_refs = refs[:n], refs[n:2 * n]
        send_sems, recv_sems, local_sems = refs[2 * n:]
        x, y, c = lax.axis_index("x"), lax.axis_index("y"), lax.axis_index("c")
        my = 4 * x + 2 * y + c
        waits = []
        for a in range(n):
            mine = pltpu.make_async_copy(x_refs[a].at[my], o_refs[a].at[my], local_sems.at[a])
            mine.start()
            for k in range(1, N_DEV):
                px = 1 - x if k & 4 else x
                py = 1 - y if k & 2 else y
                pc = 1 - c if k & 1 else c
                pid = 4 * px + 2 * py + pc
                out = pltpu.make_async_remote_copy(
                    src_ref=x_refs[a].at[pid], dst_ref=o_refs[a].at[my],
                    send_sem=send_sems.at[7 * a + k - 1], recv_sem=recv_sems.at[7 * a + k - 1],
                    device_id=(px, py, pc), device_id_type=MESH)
                out.start()
                back = pltpu.make_async_remote_copy(
                    src_ref=x_refs[a].at[pid], dst_ref=o_refs[a].at[pid],
                    send_sem=send_sems.at[7 * a + k - 1], recv_sem=recv_sems.at[7 * a + k - 1],
                    device_id=(px, py, pc), device_id_type=MESH)
                waits.append((out, back))
            waits.append((mine, None))
        for out, back in waits:
            if back is None:
                out.wait()
            else:
                out.wait_send()
                back.wait_recv()

    return pl.pallas_call(
        body, out_shape=[jax.ShapeDtypeStruct(v.shape, v.dtype) for v in xs],
        in_specs=[HBM_SPEC] * n, out_specs=[HBM_SPEC] * n,
        scratch_shapes=[pltpu.SemaphoreType.DMA((7 * n,)), pltpu.SemaphoreType.DMA((7 * n,)), pltpu.SemaphoreType.DMA((n,))],
        name=name)(*xs)


def sum_slots(g8, name):
    R, C = g8.shape[1:]

    def fn(*tiles):
        tot = tiles[0].astype(f32)
        for t in tiles[1:]:
            tot = tot + t.astype(f32)
        return tot

    return rowwise(fn, [(g8, 'lead', k) for k in range(N_DEV)], [], [(C, f32)], [], _pick(R, (256, 128, 64, 32, 16, 8)), name)[0]


PACK_W = 1024


def _pad_rows(flat, mult):
    n = flat.shape[0]
    tot = -(-n // (PACK_W * mult)) * PACK_W * mult
    return jnp.pad(flat, (0, tot - n)).reshape(tot // PACK_W, PACK_W)


def _shard_shape(name):
    r, c = BIG_SHAPE[name]
    return (r // N_DEV, c) if BIG[name] == 0 else (r, c // N_DEV)


def _split_for_devices(name, full):
    r, c = BIG_SHAPE[name]
    if BIG[name] == 0:
        return full.reshape(N_DEV, (r // N_DEV) * c)
    return full.reshape(r, N_DEV, c // N_DEV).transpose(1, 0, 2).reshape(N_DEV, r * (c // N_DEV))


def _join_from_devices(name, parts):
    r, c = BIG_SHAPE[name]
    if BIG[name] == 0:
        return parts.reshape(r, c)
    return parts.reshape(N_DEV, r, c // N_DEV).transpose(1, 0, 2).reshape(r, c)


def _row_tile(S, want):
    return _pick(S, tuple(t for t in (512, 256, 128, 64, 32, 16) if t <= want))


def resid_ln(x, f, g, b, scale, name):
    D = x.shape[1]

    def fn(x, f, g, b):
        z = ALPHA * x + scale * f
        xo = _ln(z, g, b)
        return z, xo, xo

    return rowwise(fn, [x, f], [g, b], [(D, f32), (D, f32), (D, bf16)], [], _row_tile(x.shape[0], 512), name)


def ln_bwd(z, g, b, dxo, scale, name):
    D = z.shape[1]

    def fn(z, dxo, g, b):
        _, vjp = jax.vjp(_ln, z, g, b)
        dz, dg, db = vjp(dxo)
        return dz, scale * dz, dg, db

    return rowwise(fn, [z, dxo], [g, b], [(D, f32), (D, bf16)], [(1, D), (1, D)], _row_tile(z.shape[0], 256), name)


def ffn_fwd(x, xb, w, g, b):
    S = x.shape[0]
    ab = mm(xb, w['gu'], name="ffn_up")
    h = rowwise(_swiglu, [(ab, 0, D_FF), (ab, D_FF, D_FF)], [], [(D_FF, bf16)], [], _row_tile(S, 256), "ffn_act")[0]
    f = mm(h, w['d'], name="ffn_down")
    z, xo, xob = resid_ln(x, f, g, b, 0.5, "ffn_ln")
    return xo, xob, (xb, ab, z)


def ffn_bwd(dxo, res, w, g, b):
    xb, ab, z = res
    S = z.shape[0]
    dz, dzs, dg, db = ln_bwd(z, g, b, dxo, 0.5, "ffn_ln_bwd")
    dh = mm(dzs, w['dT'], name="ffn_down_dx")

    def fn(a, bb, dh):
        h, vjp = jax.vjp(_swiglu, a, bb)
        da, dbb = vjp(dh)
        return jnp.concatenate([da, dbb], axis=1), h

    dab, h = rowwise(fn, [(ab, 0, D_FF), (ab, D_FF, D_FF), dh], [], [(2 * D_FF, bf16), (D_FF, bf16)], [],
                     _row_tile(S, 128), "ffn_act_bwd")
    dwd = mm_tn(h, dzs, name="ffn_down_dw")
    dwgu = mm_tn(xb, dab, name="ffn_up_dw")
    dx = mm(dab, w['guT'], add=dz, add_scale=ALPHA, name="ffn_up_dx")
    return dx, dwgu[:, :D_FF], dwgu[:, D_FF:], dwd, dg, db


def _heads_first(a, width):
    return a.reshape(a.shape[0], MLA_HEADS, width)


def mixer_fwd(x, xb, w, g, b, cos8, sin8):
    S = x.shape[0]
    H = mm(xb, w['in'], name="mix_in")
    ya = rowwise(_gmlp, [(H, H_UG, 256), (H, H_VG, 256)], [w['gm_ng'], w['gm_ws'], w['gm_bsb']], [(GM_WIDTH, f32)], [],
                 GM_CHUNK, "gmlp")[0]
    cqn, ckvn = rowwise(_mla_prep, [(H, H_CQ, Q_LORA), (H, H_CKV, KV_LORA)], [w['qg'], w['kvg']],
                        [(Q_LORA, bf16), (KV_LORA, bf16)], [], _row_tile(S, 512), "mla_prep")
    qraw = mm(cqn, w['uq'], name="mla_uq")
    kv = mm(ckvn, w['ukv'], name="mla_ukv")
    q1, q2, k1, k2 = rowwise(_rope, [(qraw, 512, LANES), (qraw, 640, LANES), (H, H_K1, LANES), (H, H_K2, LANES), cos8, sin8],
                             [], [(LANES, f32)] * 4, [], _row_tile(S, 512), "rope")
    zpad = jnp.zeros((S, MLA_HEADS, LANES - MLA_NOPE - MLA_ROPE), f32)
    qp = jnp.concatenate([_heads_first(qraw[:, :512], 64), _heads_first(q1, ROPE_HALF), _heads_first(q2, ROPE_HALF), zpad], axis=2)
    k1b = jnp.broadcast_to(k1[:, None, :ROPE_HALF], (S, MLA_HEADS, ROPE_HALF))
    k2b = jnp.broadcast_to(k2[:, None, :ROPE_HALF], (S, MLA_HEADS, ROPE_HALF))
    kp = jnp.concatenate([_heads_first(kv[:, :512], 64), k1b, k2b, zpad], axis=2)
    qp = qp.transpose(1, 0, 2).astype(bf16)
    kp = kp.transpose(1, 0, 2).astype(bf16)
    vp = _heads_first(kv[:, 512:], 64).transpose(1, 0, 2).astype(bf16)
    o, lse = flash_fwd(qp, kp, vp)
    ob = o.transpose(1, 0, 2).reshape(S, MLA_HEADS * MLA_V)
    bu = mm(H, w['ssm_wb'], a_col0=H_US, name="ssm_bu")
    hs3 = scan_fwd(bu.reshape(S, 16, LANES), w['ssm_a16'])
    hs = hs3.reshape(S, 2 * N_STATE)
    y1 = mm(hs, w['ssm_wc'], name="ssm_c")
    y = rowwise(_mix_post, [ya, ob, y1, (H, H_US, SSM_WIDTH)], [w['ssm_d'], w['glu_w'], w['glu_b'], w['gmix']],
                [(D_MODEL, bf16)], [], _row_tile(S, 256), "mix_post")[0]
    f = mm(y, w['out'], name="mix_out")
    z, xo, xob = resid_ln(x, f, g, b, 1.0, "mix_ln")
    return xo, xob, (xb, H, cqn, ckvn, qp, kp, vp, o, lse, hs3, ya, ob, y1, y, z)


def mixer_bwd(dxo, res, w, g, b, cos8, sin8):
    xb, H, cqn, ckvn, qp, kp, vp, o, lse, hs3, ya, ob, y1, y, z = res
    S = z.shape[0]
    gr = {}
    dz, dzs, gr['ln_g'], gr['ln_b'] = ln_bwd(z, g, b, dxo, 1.0, "mix_ln_bwd")
    gr['w_out'] = mm_tn(y, dzs, name="mix_out_dw")
    dy = mm(dzs, w['outT'], name="mix_out_dx")

    def post_bwd(ya, ob, y1, us, dy, dskip, gluw, glub, gmix):
        _, vjp = jax.vjp(_mix_post, ya, ob, y1, us, dskip, gluw, glub, gmix)
        return vjp(dy)

    dya, dob, dy1, dus_skip, gr['ssm_d'], gr['ssm_glu_w'], gr['ssm_glu_b'], gr['mix_norm_g'] = rowwise(
        post_bwd, [ya, ob, y1, (H, H_US, SSM_WIDTH), dy], [w['ssm_d'], w['glu_w'], w['glu_b'], w['gmix']],
        [(GM_WIDTH, f32), (MLA_HEADS * MLA_V, f32), (SSM_WIDTH, f32), (SSM_WIDTH, f32)],
        [(1, SSM_WIDTH), (SSM_WIDTH, SSM_WIDTH), (1, SSM_WIDTH), (1, D_MODEL)], _row_tile(S, 128), "mix_post_bwd")

    hs = hs3.reshape(S, 2 * N_STATE)
    gr['ssm_wc'] = mm_tn(hs, dy1, name="ssm_c_dw")
    dhs = mm(dy1, w['ssm_wcT'], name="ssm_c_dx")
    dbu3, gr['ssm_a16'] = scan_bwd(dhs.reshape(S, 16, LANES), hs3, w['ssm_a16'])
    dbu = dbu3.reshape(S, 2 * N_STATE)
    gr['ssm_wb'] = mm_tn(H, dbu, a_col0=H_US, m_dim=SSM_WIDTH, name="ssm_bu_dw")
    dus = mm(dbu, w['ssm_wbT'], add=dus_skip, add_scale=1.0, name="ssm_bu_dx")

    do = _heads_first(dob, MLA_V).transpose(1, 0, 2)
    delta = rowwise(lambda a, c: jnp.sum(a * c, axis=-1, keepdims=True),
                    [do.reshape(MLA_HEADS * S, MLA_V), o.reshape(MLA_HEADS * S, MLA_V)], [], [(1, f32)], [],
                    _row_tile(MLA_HEADS * S, 512), "att_delta")[0].reshape(MLA_HEADS, S, 1)
    dob16 = do.astype(bf16)
    dqp = flash_dq(qp, kp, vp, dob16, lse, delta).transpose(1, 0, 2)
    dkp, dvp = flash_dkv(qp, kp, vp, dob16, lse, delta)
    dkp = dkp.transpose(1, 0, 2)
    dv = dvp.transpose(1, 0, 2).reshape(S, MLA_HEADS * MLA_V)
    lane_pad = ((0, 0), (0, LANES - ROPE_HALF))
    dq1r = dqp[:, :, 64:80].reshape(S, LANES)
    dq2r = dqp[:, :, 80:96].reshape(S, LANES)
    dk1r = jnp.pad(jnp.sum(dkp[:, :, 64:80], axis=1), lane_pad)
    dk2r = jnp.pad(jnp.sum(dkp[:, :, 80:96], axis=1), lane_pad)

    def rope_bwd(d1, d2, d3, d4, cos, sin):
        return d1 * cos + d2 * sin, d2 * cos - d1 * sin, d3 * cos + d4 * sin, d4 * cos - d3 * sin

    dq1, dq2, dk1, dk2 = rowwise(rope_bwd, [dq1r, dq2r, dk1r, dk2r, cos8, sin8], [], [(LANES, f32)] * 4, [],
                                 _row_tile(S, 512), "rope_bwd")
    dqraw = jnp.concatenate([dqp[:, :, :64].reshape(S, 512), dq1, dq2], axis=1).astype(bf16)
    dkv = jnp.concatenate([dkp[:, :, :64].reshape(S, 512), dv], axis=1).astype(bf16)
    gr['uq'] = mm_tn(cqn, dqraw, name="mla_uq_dw")
    dcqn = mm(dqraw, w['uqT'], name="mla_uq_dx")
    gr['ukv'] = mm_tn(ckvn, dkv, name="mla_ukv_dw")
    dckvn = mm(dkv, w['ukvT'], name="mla_ukv_dx")

    def prep_bwd(cq, ckv, d1, d2, qg, kvg):
        _, vjp = jax.vjp(_mla_prep, cq, ckv, qg, kvg)
        return vjp((d1, d2))

    dcq, dckv, gr['mla_q_norm_g'], gr['mla_kv_norm_g'] = rowwise(
        prep_bwd, [(H, H_CQ, Q_LORA), (H, H_CKV, KV_LORA), dcqn, dckvn], [w['qg'], w['kvg']],
        [(Q_LORA, f32), (KV_LORA, f32)], [(1, Q_LORA), (1, KV_LORA)], _row_tile(S, 256), "mla_prep_bwd")

    def gmlp_bwd(hu, hv, dya, ng, ws, bsb):
        _, vjp = jax.vjp(_gmlp, hu, hv, ng, ws, bsb)
        return vjp(dya)

    dhu, dhv, gr['gmlp_norm_g'], gr['gmlp_ws'], gr['gm_bsb'] = rowwise(
        gmlp_bwd, [(H, H_UG, 256), (H, H_VG, 256), dya], [w['gm_ng'], w['gm_ws'], w['gm_bsb']],
        [(GM_WIDTH, f32), (GM_WIDTH, f32)], [(1, GM_WIDTH), (GM_HEADS, GM_CHUNK, GM_CHUNK), (GM_CHUNK, GM_WIDTH)],
        GM_CHUNK, "gmlp_bwd")

    dH = jnp.concatenate([dhu, dhv, dcq, dus, dckv, dk1, dk2], axis=1).astype(bf16)
    gr['in'] = mm_tn(xb, dH, name="mix_in_dw")
    dx = mm(dH, w['inT'], add=dz, add_scale=ALPHA, name="mix_in_dx")
    return dx, gr


def _block_diag(blocks):
    G, a, b = blocks.shape
    eye = jnp.eye(G, dtype=blocks.dtype)
    return (eye[:, None, :, None] * blocks[:, :, None, :]).reshape(G * a, G * b)


def _diag_blocks(mat, G):
    a, b = mat.shape[0] // G, mat.shape[1] // G
    m4 = mat.reshape(G, a, G, b)
    idx = jnp.arange(G)
    return m4[idx, :, idx, :]


def prep_layer(W, rep, l):
    w = {}
    for f in ('ffn1', 'ffn2'):
        gu = jnp.concatenate([W[f + '_w_gate'], W[f + '_w_up']], axis=1)
        w[f] = {'gu': gu, 'guT': gu.T, 'd': W[f + '_w_down'], 'dT': W[f + '_w_down'].T}
    wi = W['w_in']
    z112 = jnp.zeros((D_MODEL, LANES - ROPE_HALF), wi.dtype)
    w['in'] = jnp.concatenate([wi[:, :768], wi[:, 928:1184], wi[:, 768:896], wi[:, 896:912], z112, wi[:, 912:928], z112], axis=1)
    w['inT'] = w['in'].T
    uq = W['mla_w_uq'].reshape(Q_LORA, MLA_HEADS, MLA_NOPE + MLA_ROPE)
    w['uq'] = jnp.concatenate([uq[:, :, :64].reshape(Q_LORA, 512), uq[:, :, 64:80].reshape(Q_LORA, LANES),
                               uq[:, :, 80:96].reshape(Q_LORA, LANES)], axis=1)
    w['uqT'] = w['uq'].T
    ukv = W['mla_w_ukv'].reshape(KV_LORA, MLA_HEADS, MLA_NOPE + MLA_V)
    w['ukv'] = jnp.concatenate([ukv[:, :, :64].reshape(KV_LORA, 512), ukv[:, :, 64:].reshape(KV_LORA, 512)], axis=1)
    w['ukvT'] = w['ukv'].T
    w['out'] = W['w_out']
    w['outT'] = W['w_out'].T
    w['glu_w'] = W['ssm_glu_w']
    w['gm_ng'] = rep['gmlp_norm_g'][l].reshape(1, GM_WIDTH)
    w['gm_ws'] = rep['gmlp_ws'][l]
    w['gm_bsb'] = jnp.repeat(rep['gmlp_bs'][l].T, GM_HEAD_DIM, axis=1)
    w['qg'] = rep['mla_q_norm_g'][l].reshape(1, Q_LORA)
    w['kvg'] = rep['mla_kv_norm_g'][l].reshape(1, KV_LORA)
    w['ssm_d'] = rep['ssm_d'][l].reshape(1, SSM_WIDTH)
    w['glu_b'] = rep['ssm_glu_b'][l].reshape(1, SSM_WIDTH)
    w['gmix'] = rep['mix_norm_g'][l].reshape(1, D_MODEL)
    ar = rep['ssm_a_re'][l].reshape(1, N_STATE)
    ai = rep['ssm_a_im'][l].reshape(1, N_STATE)
    ldt = jnp.repeat(rep['ssm_log_dt'][l], SSM_STATE).reshape(1, N_STATE)
    brT = rep['ssm_b_re'][l].transpose(2, 0, 1).reshape(SSM_GROUP_CH, N_STATE)
    biT = rep['ssm_b_im'][l].transpose(2, 0, 1).reshape(SSM_GROUP_CH, N_STATE)
    w['ssm_prep_in'] = (ar, ai, ldt, brT, biT)
    abr, abi, bbrT, bbiT = whole(_ssm_prep, w['ssm_prep_in'], [(1, N_STATE)] * 2 + [(SSM_GROUP_CH, N_STATE)] * 2, "ssm_prep")
    w['ssm_a16'] = jnp.concatenate([abr.reshape(8, LANES), abi.reshape(8, LANES)], axis=0)

    def to_gcp(t):
        return t.reshape(SSM_GROUP_CH, SSM_GROUPS, SSM_STATE).transpose(1, 0, 2)

    w['ssm_wb'] = jnp.concatenate([_block_diag(to_gcp(bbrT)), _block_diag(to_gcp(bbiT))], axis=1).astype(bf16)
    w['ssm_wbT'] = w['ssm_wb'].T
    cre = rep['ssm_c_re'][l].transpose(0, 2, 1)
    cim = rep['ssm_c_im'][l].transpose(0, 2, 1)
    w['ssm_wc'] = jnp.concatenate([_block_diag(cre), -_block_diag(cim)], axis=0).astype(bf16)
    w['ssm_wcT'] = w['ssm_wc'].T
    return w


def unprep_grads(gr, w):
    out = {}
    for k in ('ln_g', 'ln_b', 'w_out', 'mla_q_norm_g', 'mla_kv_norm_g', 'ssm_glu_w', 'gmlp_ws'):
        out[k] = gr[k]
    out['gmlp_norm_g'] = gr['gmlp_norm_g'].reshape(GM_WIDTH)
    out['mla_q_norm_g'] = gr['mla_q_norm_g'].reshape(Q_LORA)
    out['mla_kv_norm_g'] = gr['mla_kv_norm_g'].reshape(KV_LORA)
    out['ssm_d'] = gr['ssm_d'].reshape(SSM_GROUPS, SSM_GROUP_CH)
    out['ssm_glu_b'] = gr['ssm_glu_b'].reshape(SSM_WIDTH)
    out['mix_norm_g'] = gr['mix_norm_g'].reshape(D_MODEL)
    out['gmlp_bs'] = gr['gm_bsb'].reshape(GM_CHUNK, GM_HEADS, GM_HEAD_DIM).sum(axis=-1).T
    d = gr['in']
    out['w_in'] = jnp.concatenate([d[:, :768], d[:, H_CKV:H_CKV + KV_LORA], d[:, H_K1:H_K1 + ROPE_HALF],
                                   d[:, H_K2:H_K2 + ROPE_HALF], d[:, H_US:H_US + SSM_WIDTH]], axis=1)
    d = gr['uq']
    out['mla_w_uq'] = jnp.concatenate([d[:, :512].reshape(Q_LORA, MLA_HEADS, 64), d[:, 512:640].reshape(Q_LORA, MLA_HEADS, ROPE_HALF),
                                       d[:, 640:768].reshape(Q_LORA, MLA_HEADS, ROPE_HALF)], axis=2).reshape(Q_LORA, 768)
    d = gr['ukv']
    out['mla_w_ukv'] = jnp.concatenate([d[:, :512].reshape(KV_LORA, MLA_HEADS, 64), d[:, 512:].reshape(KV_LORA, MLA_HEADS, 64)],
                                       axis=2).reshape(KV_LORA, 1024)
    dwc = gr['ssm_wc']
    out['ssm_c_re'] = _diag_blocks(dwc[:N_STATE], SSM_GROUPS).transpose(0, 2, 1)
    out['ssm_c_im'] = -_diag_blocks(dwc[N_STATE:], SSM_GROUPS).transpose(0, 2, 1)
    dwb = gr['ssm_wb']

    def from_blocks(m):
        return _diag_blocks(m, SSM_GROUPS).transpose(1, 0, 2).reshape(SSM_GROUP_CH, N_STATE)

    dbbrT, dbbiT = from_blocks(dwb[:, :N_STATE]), from_blocks(dwb[:, N_STATE:])
    da16 = gr['ssm_a16']
    dabr, dabi = da16[0:8].reshape(1, N_STATE), da16[8:16].reshape(1, N_STATE)

    def prep_bwd(ar, ai, ldt, brT, biT, d1, d2, d3, d4):
        _, vjp = jax.vjp(_ssm_prep, ar, ai, ldt, brT, biT)
        return vjp((d1, d2, d3, d4))

    dar, dai, dldt, dbrT, dbiT = whole(prep_bwd, w['ssm_prep_in'] + (dabr, dabi, dbbrT, dbbiT),
                                       [(1, N_STATE)] * 3 + [(SSM_GROUP_CH, N_STATE)] * 2, "ssm_prep_bwd")
    out['ssm_a_re'] = dar.reshape(SSM_GROUPS, SSM_STATE)
    out['ssm_a_im'] = dai.reshape(SSM_GROUPS, SSM_STATE)
    out['ssm_log_dt'] = dldt.reshape(SSM_GROUPS, SSM_STATE).sum(axis=-1)
    out['ssm_b_re'] = dbrT.reshape(SSM_GROUP_CH, SSM_GROUPS, SSM_STATE).transpose(1, 2, 0)
    out['ssm_b_im'] = dbiT.reshape(SSM_GROUP_CH, SSM_GROUPS, SSM_STATE).transpose(1, 2, 0)
    return out


def adamw(w, g, m, v, name):
    R, C = w.shape

    def fn(w, g, m, v):
        m = ADAM_B1 * m + (1.0 - ADAM_B1) * g
        v = ADAM_B2 * v + (1.0 - ADAM_B2) * jnp.square(g)
        m_hat = m / (1.0 - ADAM_B1 ** ADAM_STEP)
        v_hat = v / (1.0 - ADAM_B2 ** ADAM_STEP)
        delta = -ADAM_LR * (m_hat / (jnp.sqrt(v_hat) + ADAM_EPS) + ADAM_WD * w)
        return delta, m, v

    return rowwise(fn, [w, g, m, v], [], [(C, f32)] * 3, [], _pick(R, (256, 128, 64, 32, 16, 8)), name)


def kernel(x, positions, ln_g, ln_b, ffn1_w_gate, ffn1_w_up, ffn1_w_down, w_in, gmlp_norm_g, gmlp_ws, gmlp_bs, mla_q_norm_g, mla_w_uq, mla_kv_norm_g, mla_w_ukv, ssm_a_re, ssm_a_im, ssm_b_re, ssm_b_im, ssm_c_re, ssm_c_im, ssm_d, ssm_log_dt, ssm_glu_w, ssm_glu_b, mix_norm_g, w_out, ffn2_w_gate, ffn2_w_up, ffn2_w_down, loss_target, m_ln_g, m_ln_b, m_ffn1_w_gate, m_ffn1_w_up, m_ffn1_w_down, m_w_in, m_gmlp_norm_g, m_gmlp_ws, m_gmlp_bs, m_mla_q_norm_g, m_mla_w_uq, m_mla_kv_norm_g, m_mla_w_ukv, m_ssm_a_re, m_ssm_a_im, m_ssm_b_re, m_ssm_b_im, m_ssm_c_re, m_ssm_c_im, m_ssm_d, m_ssm_log_dt, m_ssm_glu_w, m_ssm_glu_b, m_mix_norm_g, m_w_out, m_ffn2_w_gate, m_ffn2_w_up, m_ffn2_w_down, v_ln_g, v_ln_b, v_ffn1_w_gate, v_ffn1_w_up, v_ffn1_w_down, v_w_in, v_gmlp_norm_g, v_gmlp_ws, v_gmlp_bs, v_mla_q_norm_g, v_mla_w_uq, v_mla_kv_norm_g, v_mla_w_ukv, v_ssm_a_re, v_ssm_a_im, v_ssm_b_re, v_ssm_b_im, v_ssm_c_re, v_ssm_c_im, v_ssm_d, v_ssm_log_dt, v_ssm_glu_w, v_ssm_glu_b, v_mix_norm_g, v_w_out, v_ffn2_w_gate, v_ffn2_w_up, v_ffn2_w_down):
    Wp = dict(zip(W_NAMES, (ln_g, ln_b, ffn1_w_gate, ffn1_w_up, ffn1_w_down, w_in, gmlp_norm_g, gmlp_ws, gmlp_bs, mla_q_norm_g, mla_w_uq, mla_kv_norm_g, mla_w_ukv, ssm_a_re, ssm_a_im, ssm_b_re, ssm_b_im, ssm_c_re, ssm_c_im, ssm_d, ssm_log_dt, ssm_glu_w, ssm_glu_b, mix_norm_g, w_out, ffn2_w_gate, ffn2_w_up, ffn2_w_down)))
    Mp = dict(zip(W_NAMES, (m_ln_g, m_ln_b, m_ffn1_w_gate, m_ffn1_w_up, m_ffn1_w_down, m_w_in, m_gmlp_norm_g, m_gmlp_ws, m_gmlp_bs, m_mla_q_norm_g, m_mla_w_uq, m_mla_kv_norm_g, m_mla_w_ukv, m_ssm_a_re, m_ssm_a_im, m_ssm_b_re, m_ssm_b_im, m_ssm_c_re, m_ssm_c_im, m_ssm_d, m_ssm_log_dt, m_ssm_glu_w, m_ssm_glu_b, m_mix_norm_g, m_w_out, m_ffn2_w_gate, m_ffn2_w_up, m_ffn2_w_down)))
    Vp = dict(zip(W_NAMES, (v_ln_g, v_ln_b, v_ffn1_w_gate, v_ffn1_w_up, v_ffn1_w_down, v_w_in, v_gmlp_norm_g, v_gmlp_ws, v_gmlp_bs, v_mla_q_norm_g, v_mla_w_uq, v_mla_kv_norm_g, v_mla_w_ukv, v_ssm_a_re, v_ssm_a_im, v_ssm_b_re, v_ssm_b_im, v_ssm_c_re, v_ssm_c_im, v_ssm_d, v_ssm_log_dt, v_ssm_glu_w, v_ssm_glu_b, v_mix_norm_g, v_w_out, v_ffn2_w_gate, v_ffn2_w_up, v_ffn2_w_down)))
    S = x.shape[1]
    my = _my_id()

    big_flat = jnp.concatenate([Wp[n][l].reshape(-1) for l in range(DEPTH) for n in BIG]).astype(bf16)
    ln_flat = jnp.concatenate([Wp[n].reshape(-1) for n in LN_NAMES])
    big_all, ln_all = all_gather([_pad_rows(big_flat, 16), _pad_rows(ln_flat, 8)], "gather_weights")
    big_all = big_all.reshape(N_DEV, -1)
    ln_all = ln_all.reshape(N_DEV, -1)
    lnsz = DEPTH * 3 * (D_MODEL // N_DEV)
    ln_full = {}
    for t, n in enumerate(LN_NAMES):
        sh = ln_all[:, t * lnsz:(t + 1) * lnsz].reshape(N_DEV, DEPTH, 3, D_MODEL // N_DEV)
        ln_full[n] = sh.transpose(1, 2, 0, 3).reshape(DEPTH, 3, 1, D_MODEL)
    Wl, off = [], 0
    for l in range(DEPTH):
        d = {}
        for n in BIG:
            r, c = _shard_shape(n)
            d[n] = _join_from_devices(n, big_all[:, off:off + r * c])
            off += r * c
        Wl.append(d)
    n_big = off
    rep = {n: Wp[n] for n in REPL}

    inv_freq = 1.0 / (ROPE_BASE ** (jnp.arange(0, MLA_ROPE, 2, dtype=f32) / MLA_ROPE))
    ang = positions.astype(f32).reshape(S, 1) * inv_freq[None, :]
    cos8 = jnp.tile(jnp.cos(ang), (1, MLA_HEADS))
    sin8 = jnp.tile(jnp.sin(ang), (1, MLA_HEADS))

    xs = x.reshape(S, D_MODEL)
    xb = xs.astype(bf16)
    ws, saved = [], []
    for l in range(DEPTH):
        w = prep_layer(Wl[l], rep, l)
        lg, lb = ln_full['ln_g'][l], ln_full['ln_b'][l]
        xs, xb, r1 = ffn_fwd(xs, xb, w['ffn1'], lg[0], lb[0])
        xs, xb, r2 = mixer_fwd(xs, xb, w, lg[1], lb[1], cos8, sin8)
        xs, xb, r3 = ffn_fwd(xs, xb, w['ffn2'], lg[2], lb[2])
        ws.append(w)
        saved.append((r1, r2, r3))

    def loss_fn(y, t):
        d = y - t
        part = jnp.sum(jnp.mean(jnp.square(d), axis=-1, keepdims=True), axis=0, keepdims=True)
        return d * (1.0 / D_MODEL), 0.5 * part

    dx, loss_part = rowwise(loss_fn, [xs, loss_target.reshape(S, D_MODEL)], [], [(D_MODEL, f32)], [(1, 1)],
                            _row_tile(S, 512), "loss")
    loss = lax.psum(loss_part[0, 0], ("x", "y", "c"))

    grads = [None] * DEPTH
    for l in reversed(range(DEPTH)):
        w = ws[l]
        lg, lb = ln_full['ln_g'][l], ln_full['ln_b'][l]
        r1, r2, r3 = saved[l]
        dx, g2g, g2u, g2d, dg2, db2 = ffn_bwd(dx, r3, w['ffn2'], lg[2], lb[2])
        dx, gm = mixer_bwd(dx, r2, w, lg[1], lb[1], cos8, sin8)
        dx, g1g, g1u, g1d, dg0, db0 = ffn_bwd(dx, r1, w['ffn1'], lg[0], lb[0])
        g = unprep_grads(gm, w)
        g.update({'ffn1_w_gate': g1g, 'ffn1_w_up': g1u, 'ffn1_w_down': g1d,
                  'ffn2_w_gate': g2g, 'ffn2_w_up': g2u, 'ffn2_w_down': g2d})
        g['ln_g'] = jnp.concatenate([dg0, g['ln_g'], dg2], axis=0)
        g['ln_b'] = jnp.concatenate([db0, g['ln_b'], db2], axis=0)
        grads[l] = g
    grad_x = dx.reshape(1, S, D_MODEL)

    gpack = jnp.concatenate([_split_for_devices(n, grads[l][n]).astype(bf16) for l in range(DEPTH) for n in BIG], axis=1)
    gpack = jnp.pad(gpack, ((0, 0), (0, -n_big % (16 * PACK_W)))).reshape(N_DEV, -1, PACK_W)
    gbig = sum_slots(all_to_all([gpack], "scatter_grads")[0], "sum_big").reshape(-1)
    small_names = LN_NAMES + REPL
    spack = jnp.concatenate([jnp.stack([grads[l][n] for l in range(DEPTH)]).reshape(-1) for n in small_names])
    n_small = spack.shape[0]
    gsmall = sum_slots(all_gather([_pad_rows(spack, 8)], "gather_small")[0], "sum_small").reshape(-1)

    G, off = {}, 0
    big_parts = {n: [] for n in BIG}
    for l in range(DEPTH):
        for n in BIG:
            r, c = _shard_shape(n)
            big_parts[n].append(gbig[off:off + r * c].reshape(r, c))
            off += r * c
    for n in BIG:
        G[n] = jnp.stack(big_parts[n])
    off = 0
    for n in small_names:
        shp = (DEPTH, 3, D_MODEL) if n in LN_NAMES else Wp[n].shape
        sz = math.prod(shp)
        G[n] = gsmall[off:off + sz].reshape(shp)
        off += sz
    for n in LN_NAMES:
        G[n] = lax.dynamic_slice_in_dim(G[n], my * (D_MODEL // N_DEV), D_MODEL // N_DEV, axis=2)

    delta, new_m, new_v = {}, {}, {}
    for n in BIG:
        shp = Wp[n].shape
        two = (shp[0] * shp[1], shp[2])
        d_, m_, v_ = adamw(Wp[n].reshape(two), G[n].reshape(two), Mp[n].reshape(two), Vp[n].reshape(two), "adamw_" + n)
        delta[n], new_m[n], new_v[n] = d_.reshape(shp), m_.reshape(shp), v_.reshape(shp)

    def pack_small(src):
        return _pad_rows(jnp.concatenate([src[n].reshape(-1) for n in small_names]), 8)

    d_, m_, v_ = adamw(pack_small(Wp), pack_small(G), pack_small(Mp), pack_small(Vp), "adamw_small")
    d_, m_, v_ = d_.reshape(-1), m_.reshape(-1), v_.reshape(-1)
    off = 0
    for n in small_names:
        shp = Wp[n].shape
        sz = math.prod(shp)
        delta[n], new_m[n], new_v[n] = (t[off:off + sz].reshape(shp) for t in (d_, m_, v_))
        off += sz

    return (loss, grad_x, *[G[n] for n in W_NAMES], *[delta[n] for n in W_NAMES],
            *[new_m[n] for n in W_NAMES], *[new_v[n] for n in W_NAMES])
```

```python
import functools
import math

import jax
import jax.numpy as jnp
from jax import lax
from jax.experimental import pallas as pl
from jax.experimental.pallas import tpu as pltpu

f32 = jnp.float32
bf16 = jnp.bfloat16

D_MODEL = 1024
DEPTH = 4
D_FF = 2816
GM_HEADS, GM_HEAD_DIM, GM_WIDTH, GM_CHUNK = 4, 64, 256, 128
MLA_HEADS, MLA_NOPE, MLA_ROPE, MLA_V = 8, 64, 32, 64
ROPE_HALF = MLA_ROPE // 2
Q_LORA, KV_LORA = 256, 128
ROPE_BASE = 10000.0
SSM_GROUPS, SSM_GROUP_CH, SSM_WIDTH, SSM_STATE = 16, 16, 256, 64
N_STATE = SSM_GROUPS * SSM_STATE
ALPHA = (2 * DEPTH) ** 0.25
LN_EPS = 1e-5
RMS_EPS = 1e-6
NEG_BIG = -1e30
ATT_SCALE = (MLA_NOPE + MLA_ROPE) ** -0.5
ADAM_LR, ADAM_B1, ADAM_B2, ADAM_EPS, ADAM_WD, ADAM_STEP = 0.001, 0.9, 0.999, 1e-08, 0.01, 10

N_DEV = 8
LANES = 128
VMEM_LIMIT = 48 * 1024 * 1024
MESH = pl.DeviceIdType.MESH

H_UG, H_VG, H_CQ, H_US, H_CKV, H_K1, H_K2, H_COLS = 0, 256, 512, 768, 1024, 1152, 1280, 1408

W_NAMES = ['ln_g', 'ln_b', 'ffn1_w_gate', 'ffn1_w_up', 'ffn1_w_down', 'w_in', 'gmlp_norm_g', 'gmlp_ws', 'gmlp_bs',
           'mla_q_norm_g', 'mla_w_uq', 'mla_kv_norm_g', 'mla_w_ukv', 'ssm_a_re', 'ssm_a_im', 'ssm_b_re', 'ssm_b_im',
           'ssm_c_re', 'ssm_c_im', 'ssm_d', 'ssm_log_dt', 'ssm_glu_w', 'ssm_glu_b', 'mix_norm_g', 'w_out',
           'ffn2_w_gate', 'ffn2_w_up', 'ffn2_w_down']
BIG = {'ffn1_w_gate': 1, 'ffn1_w_up': 1, 'ffn1_w_down': 0, 'w_in': 1, 'mla_w_uq': 1, 'mla_w_ukv': 1,
       'ssm_glu_w': 0, 'w_out': 0, 'ffn2_w_gate': 1, 'ffn2_w_up': 1, 'ffn2_w_down': 0}
BIG_SHAPE = {'ffn1_w_gate': (D_MODEL, D_FF), 'ffn1_w_up': (D_MODEL, D_FF), 'ffn1_w_down': (D_FF, D_MODEL),
             'w_in': (D_MODEL, 1184), 'mla_w_uq': (Q_LORA, 768), 'mla_w_ukv': (KV_LORA, 1024),
             'ssm_glu_w': (SSM_WIDTH, SSM_WIDTH), 'w_out': (D_MODEL, D_MODEL),
             'ffn2_w_gate': (D_MODEL, D_FF), 'ffn2_w_up': (D_MODEL, D_FF), 'ffn2_w_down': (D_FF, D_MODEL)}
LN_NAMES = ['ln_g', 'ln_b']
REPL = [n for n in W_NAMES if n not in BIG and n not in LN_NAMES]


def _pick(n, cands):
    for c in cands:
        if n % c == 0:
            return c
    return n


def _params(sem):
    return pltpu.CompilerParams(dimension_semantics=sem, vmem_limit_bytes=VMEM_LIMIT)


def mm(a, b, *, out_dtype=f32, add=None, add_scale=1.0, a_col0=0, name):
    M = a.shape[0]
    K, N = b.shape
    tm = _pick(M, (512, 256, 128, 64, 32, 16, 8))
    tn = _pick(N, (512, 384, 256, 128))
    tk = K if K <= 2 * D_FF else _pick(K, (1024, 512, 256, 128))
    nk = K // tk
    assert a_col0 % tk == 0 and a_col0 + K <= a.shape[1]
    kb0 = a_col0 // tk
    has_add = add is not None

    def body(*refs):
        if has_add:
            a_ref, b_ref, add_ref, o_ref, acc = refs
        else:
            a_ref, b_ref, o_ref, acc = refs
        k = pl.program_id(2)
        part = jnp.dot(a_ref[...].astype(bf16), b_ref[...].astype(bf16), preferred_element_type=f32)

        def finish(total):
            if has_add:
                total = total + add_scale * add_ref[...]
            o_ref[...] = total.astype(o_ref.dtype)

        if nk == 1:
            finish(part)
        else:
            @pl.when(k == 0)
            def _():
                acc[...] = part

            @pl.when(k > 0)
            def _():
                acc[...] += part

            @pl.when(k == nk - 1)
            def _():
                finish(acc[...])

    in_specs = [pl.BlockSpec((tm, tk), lambda i, j, k: (i, kb0 + k)), pl.BlockSpec((tk, tn), lambda i, j, k: (k, j))]
    ops = [a, b]
    if has_add:
        in_specs.append(pl.BlockSpec((tm, tn), lambda i, j, k: (i, j)))
        ops.append(add)
    return pl.pallas_call(
        body, grid=(M // tm, N // tn, nk), in_specs=in_specs,
        out_specs=pl.BlockSpec((tm, tn), lambda i, j, k: (i, j)),
        out_shape=jax.ShapeDtypeStruct((M, N), out_dtype),
        scratch_shapes=[pltpu.VMEM((tm, tn) if nk > 1 else (8, LANES), f32)],
        compiler_params=_params(("parallel", "parallel", "arbitrary")), name=name)(*ops)


def mm_tn(a, b, *, a_col0=0, m_dim=None, name):
    K = a.shape[0]
    M = a.shape[1] if m_dim is None else m_dim
    N = b.shape[1]
    tm = _pick(M, (1024, 768, 512, 384, 256, 128))
    tn = _pick(N, (1024, 768, 512, 384, 256, 128))
    tk = _pick(K, (2048, 1024, 512, 256, 128, 64, 32, 16))
    nk = K // tk
    assert a_col0 % tm == 0
    mb0 = a_col0 // tm

    def body(a_ref, b_ref, o_ref, acc):
        k = pl.program_id(2)
        part = lax.dot_general(a_ref[...].astype(bf16), b_ref[...].astype(bf16), (((0,), (0,)), ((), ())),
                               preferred_element_type=f32)

        @pl.when(k == 0)
        def _():
            acc[...] = part

        @pl.when(k > 0)
        def _():
            acc[...] += part

        @pl.when(k == nk - 1)
        def _():
            o_ref[...] = acc[...]

    return pl.pallas_call(
        body, grid=(M // tm, N // tn, nk),
        in_specs=[pl.BlockSpec((tk, tm), lambda i, j, k: (k, mb0 + i)), pl.BlockSpec((tk, tn), lambda i, j, k: (k, j))],
        out_specs=pl.BlockSpec((tm, tn), lambda i, j, k: (i, j)),
        out_shape=jax.ShapeDtypeStruct((M, N), f32),
        scratch_shapes=[pltpu.VMEM((tm, tn), f32)],
        compiler_params=_params(("parallel", "parallel", "arbitrary")), name=name)(a, b)


def rowwise(fn, rows, pars, out_rows, out_accs, tm, name):
    first = rows[0]
    if isinstance(first, tuple):
        R = first[0].shape[1] if first[1] == 'lead' else first[0].shape[0]
    else:
        R = first.shape[0]
    assert R % tm == 0, (R, tm, name)
    n_rows, n_pars, n_or, n_oa = len(rows), len(pars), len(out_rows), len(out_accs)

    in_specs, ops = [], []
    for r in rows:
        if isinstance(r, tuple) and r[1] == 'lead':
            arr, _, kk = r
            in_specs.append(pl.BlockSpec((None, tm, arr.shape[2]), lambda i, kk=kk: (kk, i, 0)))
        elif isinstance(r, tuple):
            arr, c0, w = r
            assert c0 % w == 0
            in_specs.append(pl.BlockSpec((tm, w), lambda i, cb=c0 // w: (i, cb)))
        else:
            arr = r
            in_specs.append(pl.BlockSpec((tm, arr.shape[1]), lambda i: (i, 0)))
        ops.append(arr)
    for p in pars:
        in_specs.append(pl.BlockSpec(p.shape, lambda i, nd=p.ndim: (0,) * nd))
        ops.append(p)
    out_specs = [pl.BlockSpec((tm, w), lambda i: (i, 0)) for (w, _) in out_rows]
    out_specs += [pl.BlockSpec(s, lambda i, nd=len(s): (0,) * nd) for s in out_accs]
    out_shape = [jax.ShapeDtypeStruct((R, w), dt) for (w, dt) in out_rows]
    out_shape += [jax.ShapeDtypeStruct(s, f32) for s in out_accs]

    def body(*refs):
        ins = [r[...] for r in refs[:n_rows + n_pars]]
        o_refs = refs[n_rows + n_pars:]
        res = fn(*ins)
        if not isinstance(res, (tuple, list)):
            res = (res,)
        assert len(res) == n_or + n_oa, (len(res), n_or, n_oa, name)
        for o, v in zip(o_refs[:n_or], res[:n_or]):
            o[...] = v.astype(o.dtype)
        if n_oa:
            i = pl.program_id(0)

            @pl.when(i == 0)
            def _():
                for o, v in zip(o_refs[n_or:], res[n_or:]):
                    o[...] = v.astype(f32)

            @pl.when(i > 0)
            def _():
                for o, v in zip(o_refs[n_or:], res[n_or:]):
                    o[...] += v.astype(f32)

    return pl.pallas_call(
        body, grid=(R // tm,), in_specs=in_specs, out_specs=out_specs, out_shape=out_shape,
        compiler_params=_params(("arbitrary",)), name=name)(*ops)


def whole(fn, ins, out_shapes, name):
    def body(*refs):
        res = fn(*[r[...] for r in refs[:len(ins)]])
        for o, v in zip(refs[len(ins):], res):
            o[...] = v

    return pl.pallas_call(body, out_shape=[jax.ShapeDtypeStruct(s, f32) for s in out_shapes], name=name)(*ins)


@jax.custom_vjp
def _bdot(a, b):
    return jnp.dot(a.astype(bf16), b.astype(bf16), preferred_element_type=f32)


def _bdot_fwd(a, b):
    return _bdot(a, b), (a, b)


def _bdot_bwd(res, g):
    a, b = res
    gb = g.astype(bf16)
    da = lax.dot_general(gb, b.astype(bf16), (((1,), (1,)), ((), ())), preferred_element_type=f32)
    db = lax.dot_general(a.astype(bf16), gb, (((0,), (0,)), ((), ())), preferred_element_type=f32)
    return da, db


_bdot.defvjp(_bdot_fwd, _bdot_bwd)


def _ln(z, g, b):
    mu = jnp.mean(z, axis=-1, keepdims=True)
    var = jnp.mean(jnp.square(z - mu), axis=-1, keepdims=True)
    return (z - mu) * lax.rsqrt(var + LN_EPS) * g + b


def _rms_only(x):
    return x * lax.rsqrt(jnp.mean(jnp.square(x), axis=-1, keepdims=True) + RMS_EPS)


def _swiglu(a, b):
    return jax.nn.silu(a) * b


def _gmlp(hu, hv, ng, ws, bsb):
    u = jax.nn.gelu(hu)
    v = jax.nn.gelu(hv)
    lane = lax.broadcasted_iota(jnp.int32, (1, GM_WIDTH), 1)
    masks = [((lane >= GM_HEAD_DIM * h) & (lane < GM_HEAD_DIM * (h + 1))).astype(f32) for h in range(GM_HEADS)]
    mu = jnp.zeros_like(v)
    for m in masks:
        mu = mu + m * (jnp.sum(v * m, axis=-1, keepdims=True) / GM_HEAD_DIM)
    d = v - mu
    var = jnp.zeros_like(v)
    for m in masks:
        var = var + m * (jnp.sum(d * d * m, axis=-1, keepdims=True) / GM_HEAD_DIM)
    vn = d * lax.rsqrt(var + LN_EPS) * ng
    r = lax.broadcasted_iota(jnp.int32, (GM_CHUNK, GM_CHUNK), 0)
    c = lax.broadcasted_iota(jnp.int32, (GM_CHUNK, GM_CHUNK), 1)
    tril = (c <= r).astype(f32)
    z = bsb
    for h, m in enumerate(masks):
        z = z + _bdot(ws[h] * tril, vn * m)
    return u * z


def _mla_prep(cq, ckv, qg, kvg):
    return _rms_only(cq) * qg, _rms_only(ckv) * kvg


def _rope(q1, q2, k1, k2, cos, sin):
    return q1 * cos - q2 * sin, q2 * cos + q1 * sin, k1 * cos - k2 * sin, k2 * cos + k1 * sin


def _mix_post(ya, ob, y1, us, dskip, gluw, glub, gmix):
    y = jax.nn.gelu(y1 + dskip * us)
    yc = y * jax.nn.sigmoid(_bdot(y, gluw) + glub)
    return jnp.concatenate([_rms_only(ya), _rms_only(ob), _rms_only(yc)], axis=1) * gmix


def _ssm_prep(ar, ai, ldt, brT, biT):
    dt = jnp.exp(ldt)
    mag = jnp.exp(ar * dt)
    abr = mag * jnp.cos(ai * dt)
    abi = mag * jnp.sin(ai * dt)
    den = ar * ar + ai * ai
    cr = ((abr - 1.0) * ar + abi * ai) / den
    ci = (abi * ar - (abr - 1.0) * ai) / den
    return abr, abi, cr * brT - ci * biT, cr * biT + ci * brT


def _att_tile(S):
    return _pick(S, (512, 256, 128))


def _nt(a, b):
    return lax.dot_general(a, b, (((1,), (1,)), ((), ())), preferred_element_type=f32)


def _diag_keep(T):
    krow = lax.broadcasted_iota(jnp.int32, (T, T), 0)
    qcol = lax.broadcasted_iota(jnp.int32, (T, T), 1)
    return qcol >= krow


def flash_fwd(q, k, vT):
    Hh, S, _ = q.shape
    T = _att_tile(S)

    def body(q_ref, k_ref, vT_ref, o_ref, lse_ref, m_sc, l_sc, acc_sc):
        i = pl.program_id(1)
        qi = q_ref[0]
        m_sc[...] = jnp.full_like(m_sc, NEG_BIG)
        l_sc[...] = jnp.zeros_like(l_sc)
        acc_sc[...] = jnp.zeros_like(acc_sc)

        def block(j, diagonal):
            rows = pl.ds(pl.multiple_of(j * T, T), T)
            sT = _nt(k_ref[0, rows, :], qi) * ATT_SCALE
            if diagonal:
                sT = jnp.where(_diag_keep(T), sT, NEG_BIG)
            m_old = m_sc[...]
            m_new = jnp.maximum(m_old, jnp.max(sT, axis=0, keepdims=True))
            alpha = jnp.exp(m_old - m_new)
            pT = jnp.exp(sT - m_new)
            l_sc[...] = alpha * l_sc[...] + jnp.sum(pT, axis=0, keepdims=True)
            acc_sc[...] = alpha * acc_sc[...] + jnp.dot(vT_ref[0, :, rows], pT.astype(bf16), preferred_element_type=f32)
            m_sc[...] = m_new

        def loop_body(j, c):
            block(j, False)
            return c

        lax.fori_loop(0, i, loop_body, 0)
        block(i, True)
        o_ref[0] = acc_sc[...] / l_sc[...]
        lse_ref[0] = m_sc[...] + jnp.log(l_sc[...])

    return pl.pallas_call(
        body, grid=(Hh, S // T),
        in_specs=[pl.BlockSpec((1, T, LANES), lambda h, i: (h, i, 0)), pl.BlockSpec((1, S, LANES), lambda h, i: (h, 0, 0)),
                  pl.BlockSpec((1, MLA_V, S), lambda h, i: (h, 0, 0))],
        out_specs=[pl.BlockSpec((1, MLA_V, T), lambda h, i: (h, 0, i)), pl.BlockSpec((1, 1, T), lambda h, i: (h, 0, i))],
        out_shape=[jax.ShapeDtypeStruct((Hh, MLA_V, S), f32), jax.ShapeDtypeStruct((Hh, 1, S), f32)],
        scratch_shapes=[pltpu.VMEM((1, T), f32), pltpu.VMEM((1, T), f32), pltpu.VMEM((MLA_V, T), f32)],
        compiler_params=_params(("parallel", "arbitrary")), name="flash_fwd")(q, k, vT)


def flash_bwd(q, k, kT, v, do, lse, delta):
    Hh, S, _ = q.shape
    T = _att_tile(S)

    def body(q_ref, do_ref, lse_ref, dl_ref, k_ref, kT_ref, v_ref, dq_ref, dk_ref, dv_ref, dq_sc):
        i = pl.program_id(1)

        @pl.when(i == 0)
        def _():
            dk_ref[...] = jnp.zeros_like(dk_ref)
            dv_ref[...] = jnp.zeros_like(dv_ref)

        qi, doi = q_ref[0], do_ref[0]
        lse_i, dl_i = lse_ref[0], dl_ref[0]
        dq_sc[...] = jnp.zeros_like(dq_sc)

        def block(j, diagonal):
            rows = pl.ds(pl.multiple_of(j * T, T), T)
            sT = _nt(k_ref[0, rows, :], qi) * ATT_SCALE
            pT = jnp.exp(sT - lse_i)
            if diagonal:
                pT = jnp.where(_diag_keep(T), pT, 0.0)
            dpT = _nt(v_ref[0, rows, :], doi)
            dsT = (pT * (dpT - dl_i) * ATT_SCALE).astype(bf16)
            dv_ref[0, rows, :] += jnp.dot(pT.astype(bf16), doi, preferred_element_type=f32)
            dk_ref[0, rows, :] += jnp.dot(dsT, qi, preferred_element_type=f32)
            dq_sc[...] += jnp.dot(kT_ref[0, :, rows], dsT, preferred_element_type=f32)

        def loop_body(j, c):
            block(j, False)
            return c

        lax.fori_loop(0, i, loop_body, 0)
        block(i, True)
        dq_ref[0] = dq_sc[...]

    tile = lambda w: pl.BlockSpec((1, T, w), lambda h, i: (h, i, 0))
    row = pl.BlockSpec((1, 1, T), lambda h, i: (h, 0, i))
    full = lambda w: pl.BlockSpec((1, S, w), lambda h, i: (h, 0, 0))
    return pl.pallas_call(
        body, grid=(Hh, S // T),
        in_specs=[tile(LANES), tile(MLA_V), row, row, full(LANES), pl.BlockSpec((1, LANES, S), lambda h, i: (h, 0, 0)), full(MLA_V)],
        out_specs=[pl.BlockSpec((1, LANES, T), lambda h, i: (h, 0, i)), full(LANES), full(MLA_V)],
        out_shape=[jax.ShapeDtypeStruct((Hh, LANES, S), f32), jax.ShapeDtypeStruct((Hh, S, LANES), f32),
                   jax.ShapeDtypeStruct((Hh, S, MLA_V), f32)],
        scratch_shapes=[pltpu.VMEM((LANES, T), f32)],
        compiler_params=_params(("parallel", "arbitrary")), name="flash_bwd")(q, do, lse, delta, k, kT, v)


def _scan_tile(S):
    return _pick(S, (256, 128, 64, 32, 16, 8))


def scan_fwd(bu3, a16):
    S = bu3.shape[0]
    ts = _scan_tile(S)

    def body(bu_ref, a_ref, o_ref, h_sc):
        @pl.when(pl.program_id(0) == 0)
        def _():
            h_sc[...] = jnp.zeros_like(h_sc)

        ar, ai = a_ref[0:8, :], a_ref[8:16, :]

        def step(t, carry):
            hr, hi = carry
            nr = ar * hr - ai * hi + bu_ref[t, 0:8, :]
            ni = ar * hi + ai * hr + bu_ref[t, 8:16, :]
            o_ref[t, 0:8, :] = nr
            o_ref[t, 8:16, :] = ni
            return nr, ni

        hr, hi = lax.fori_loop(0, ts, step, (h_sc[0:8, :], h_sc[8:16, :]), unroll=8)
        h_sc[0:8, :] = hr
        h_sc[8:16, :] = hi

    blk = pl.BlockSpec((ts, 16, LANES), lambda i: (i, 0, 0))
    return pl.pallas_call(
        body, grid=(S // ts,), in_specs=[blk, pl.BlockSpec((16, LANES), lambda i: (0, 0))], out_specs=blk,
        out_shape=jax.ShapeDtypeStruct(bu3.shape, f32), scratch_shapes=[pltpu.VMEM((16, LANES), f32)],
        compiler_params=_params(("arbitrary",)), name="scan_fwd")(bu3, a16)


def scan_bwd(g3, h3, a16):
    S = g3.shape[0]
    ts = _scan_tile(S)
    nb = S // ts

    def body(g_ref, h_ref, a_ref, o_ref, da_ref, lam_sc, da_sc):
        @pl.when(pl.program_id(0) == 0)
        def _():
            lam_sc[...] = jnp.zeros_like(lam_sc)
            da_sc[...] = jnp.zeros_like(da_sc)

        ar, ai = a_ref[0:8, :], a_ref[8:16, :]

        def step(kk, carry):
            lr, li, dar, dai = carry
            t = ts - 1 - kk
            hr, hi = h_ref[t, 0:8, :], h_ref[t, 8:16, :]
            dar = dar + lr * hr + li * hi
            dai = dai + li * hr - lr * hi
            nlr = ar * lr + ai * li + g_ref[t, 0:8, :]
            nli = ar * li - ai * lr + g_ref[t, 8:16, :]
            o_ref[t, 0:8, :] = nlr
            o_ref[t, 8:16, :] = nli
            return nlr, nli, dar, dai

        lr, li, dar, dai = lax.fori_loop(
            0, ts, step, (lam_sc[0:8, :], lam_sc[8:16, :], da_sc[0:8, :], da_sc[8:16, :]), unroll=8)
        lam_sc[0:8, :] = lr
        lam_sc[8:16, :] = li
        da_sc[0:8, :] = dar
        da_sc[8:16, :] = dai
        da_ref[0:8, :] = dar
        da_ref[8:16, :] = dai

    blk = pl.BlockSpec((ts, 16, LANES), lambda i: (nb - 1 - i, 0, 0))
    small = pl.BlockSpec((16, LANES), lambda i: (0, 0))
    return pl.pallas_call(
        body, grid=(nb,), in_specs=[blk, blk, small], out_specs=[blk, small],
        out_shape=[jax.ShapeDtypeStruct(g3.shape, f32), jax.ShapeDtypeStruct((16, LANES), f32)],
        scratch_shapes=[pltpu.VMEM((16, LANES), f32), pltpu.VMEM((16, LANES), f32)],
        compiler_params=_params(("arbitrary",)), name="scan_bwd")(g3, h3, a16)


HBM_SPEC = pl.BlockSpec(memory_space=pltpu.HBM)


def _my_id():
    return 4 * lax.axis_index("x") + 2 * lax.axis_index("y") + lax.axis_index("c")


def all_gather(xs, name):
    n = len(xs)

    def body(*refs):
        x_refs, o_refs = refs[:n], refs[n:2 * n]
        send_sems, recv_sems, local_sems = refs[2 * n:]
        x, y, c = lax.axis_index("x"), lax.axis_index("y"), lax.axis_index("c")
        me, sibling = (x, y, c), (x, y, 1 - c)
        chips = [(1 - x, y), (x, 1 - y), (1 - x, 1 - y)]

        def slot(o, p):
            return o.at[4 * p[0] + 2 * p[1] + p[2]]

        def copy(a, k, block, to, src=None):
            o = o_refs[a]
            return pltpu.make_async_remote_copy(
                src_ref=slot(o, block) if src is None else src, dst_ref=slot(o, block),
                send_sem=send_sems.at[7 * a + k], recv_sem=recv_sems.at[7 * a + k], device_id=to, device_id_type=MESH)

        own, sends = [], []
        for a in range(n):
            mine = pltpu.make_async_copy(x_refs[a], slot(o_refs[a], me), local_sems.at[a])
            mine.start()
            own.append(mine)
            first = [copy(a, 0, me, sibling, src=x_refs[a])]
            first += [copy(a, 1 + j, me, (*chip, c), src=x_refs[a]) for j, chip in enumerate(chips)]
            for cp in first:
                cp.start()
            sends += first
        for a in range(n):
            for j, chip in enumerate(chips):
                copy(a, 1 + j, (*chip, c), me).wait_recv()
                fwd = copy(a, 4 + j, (*chip, c), sibling)
                fwd.start()
                sends.append(fwd)
        for a in range(n):
            copy(a, 0, sibling, me).wait_recv()
            for j, chip in enumerate(chips):
                copy(a, 4 + j, (*chip, 1 - c), me).wait_recv()
        for cp in sends:
            cp.wait_send()
        for cp in own:
            cp.wait()

    return pl.pallas_call(
        body, out_shape=[jax.ShapeDtypeStruct((N_DEV,) + v.shape, v.dtype) for v in xs],
        in_specs=[HBM_SPEC] * n, out_specs=[HBM_SPEC] * n,
        scratch_shapes=[pltpu.SemaphoreType.DMA((7 * n,)), pltpu.SemaphoreType.DMA((7 * n,)), pltpu.SemaphoreType.DMA((n,))],
        name=name)(*xs)


def all_to_all(xs, name):
    n = len(xs)

    def body(*refs):
        x_refs, o_refs = refs[:n], refs[n:2 * n]
        send_sems, recv_sems, local_sems = refs[2 * n:]
        x, y, c = lax.axis_index("x"), lax.axis_index("y"), lax.axis_index("c")
        my = 4 * x + 2 * y + c
        waits = []
        for a in range(n):
            mine = pltpu.make_async_copy(x_refs[a].at[my], o_refs[a].at[my], local_sems.at[a])
            mine.start()
            for k in range(1, N_DEV):
                px = 1 - x if k & 4 else x
                py = 1 - y if k & 2 else y
                pc = 1 - c if k & 1 else c
                pid = 4 * px + 2 * py + pc
                out = pltpu.make_async_remote_copy(
                    src_ref=x_refs[a].at[pid], dst_ref=o_refs[a].at[my],
                    send_sem=send_sems.at[7 * a + k - 1], recv_sem=recv_sems.at[7 * a + k - 1],
                    device_id=(px, py, pc), device_id_type=MESH)
                out.start()
                back = pltpu.make_async_remote_copy(
                    src_ref=x_refs[a].at[pid], dst_ref=o_refs[a].at[pid],
                    send_sem=send_sems.at[7 * a + k - 1], recv_sem=recv_sems.at[7 * a + k - 1],
                    device_id=(px, py, pc), device_id_type=MESH)
                waits.append((out, back))
            waits.append((mine, None))
        for out, back in waits:
            if back is None:
                out.wait()
            else:
                out.wait_send()
                back.wait_recv()

    return pl.pallas_call(
        body, out_shape=[jax.ShapeDtypeStruct(v.shape, v.dtype) for v in xs],
        in_specs=[HBM_SPEC] * n, out_specs=[HBM_SPEC] * n,
        scratch_shapes=[pltpu.SemaphoreType.DMA((7 * n,)), pltpu.SemaphoreType.DMA((7 * n,)), pltpu.SemaphoreType.DMA((n,))],
        name=name)(*xs)


def sum_slots(g8, name):
    R, C = g8.shape[1:]

    def fn(*tiles):
        tot = tiles[0].astype(f32)
        for t in tiles[1:]:
            tot = tot + t.astype(f32)
        return tot

    return rowwise(fn, [(g8, 'lead', k) for k in range(N_DEV)], [], [(C, f32)], [], _pick(R, (256, 128, 64, 32, 16, 8)), name)[0]


PACK_W = 1024


def _pad_rows(flat, mult):
    n = flat.shape[0]
    tot = -(-n // (PACK_W * mult)) * PACK_W * mult
    return jnp.pad(flat, (0, tot - n)).reshape(tot // PACK_W, PACK_W)


def _shard_shape(name):
    r, c = BIG_SHAPE[name]
    return (r // N_DEV, c) if BIG[name] == 0 else (r, c // N_DEV)


def _split_for_devices(name, full):
    r, c = BIG_SHAPE[name]
    if BIG[name] == 0:
        return full.reshape(N_DEV, (r // N_DEV) * c)
    return full.reshape(r, N_DEV, c // N_DEV).transpose(1, 0, 2).reshape(N_DEV, r * (c // N_DEV))


def _join_from_devices(name, parts):
    r, c = BIG_SHAPE[name]
    if BIG[name] == 0:
        return parts.reshape(r, c)
    return parts.reshape(N_DEV, r, c // N_DEV).transpose(1, 0, 2).reshape(r, c)


def _row_tile(S, want):
    return _pick(S, tuple(t for t in (512, 256, 128, 64, 32, 16) if t <= want))


def resid_ln(x, f, g, b, scale, name):
    D = x.shape[1]

    def fn(x, f, g, b):
        z = ALPHA * x + scale * f
        xo = _ln(z, g, b)
        return z, xo, xo

    return rowwise(fn, [x, f], [g, b], [(D, f32), (D, f32), (D, bf16)], [], _row_tile(x.shape[0], 512), name)


def ln_bwd(z, g, b, dxo, scale, name):
    D = z.shape[1]

    def fn(z, dxo, g, b):
        _, vjp = jax.vjp(_ln, z, g, b)
        dz, dg, db = vjp(dxo)
        return dz, scale * dz, dg, db

    return rowwise(fn, [z, dxo], [g, b], [(D, f32), (D, bf16)], [(1, D), (1, D)], _row_tile(z.shape[0], 256), name)


FF_TILE = 256


def _interleave_gate_up(wg, wu):
    d = wg.shape[0]
    t = jnp.stack([wg.reshape(d, D_FF // FF_TILE, FF_TILE), wu.reshape(d, D_FF // FF_TILE, FF_TILE)], axis=2)
    return t.reshape(d, 2 * D_FF)


def _split_gate_up(w):
    t = w.reshape(w.shape[0], D_FF // FF_TILE, 2, FF_TILE)
    return t[:, :, 0, :].reshape(w.shape[0], D_FF), t[:, :, 1, :].reshape(w.shape[0], D_FF)


def ffn_up_act(xb, wgu):
    M, K = xb.shape
    tm = _pick(M, (512, 256, 128, 64, 32, 16))

    def body(x_ref, w_ref, ab_ref, h_ref):
        ab = jnp.dot(x_ref[...], w_ref[...], preferred_element_type=f32)
        ab_ref[...] = ab
        h_ref[...] = _swiglu(ab[:, :FF_TILE], ab[:, FF_TILE:]).astype(bf16)

    return pl.pallas_call(
        body, grid=(M // tm, D_FF // FF_TILE),
        in_specs=[pl.BlockSpec((tm, K), lambda i, j: (i, 0)), pl.BlockSpec((K, 2 * FF_TILE), lambda i, j: (0, j))],
        out_specs=[pl.BlockSpec((tm, 2 * FF_TILE), lambda i, j: (i, j)), pl.BlockSpec((tm, FF_TILE), lambda i, j: (i, j))],
        out_shape=[jax.ShapeDtypeStruct((M, 2 * D_FF), f32), jax.ShapeDtypeStruct((M, D_FF), bf16)],
        compiler_params=_params(("parallel", "parallel")), name="ffn_up_act")(xb, wgu)


def ffn_down_dx_act(dzs, wdT, ab):
    M, K = dzs.shape
    tm = _pick(M, (512, 256, 128, 64, 32, 16))

    def body(dz_ref, w_ref, ab_ref, dab_ref, h_ref):
        dh = jnp.dot(dz_ref[...], w_ref[...], preferred_element_type=f32)
        ab = ab_ref[...]
        h, vjp = jax.vjp(_swiglu, ab[:, :FF_TILE], ab[:, FF_TILE:])
        da, db = vjp(dh)
        dab_ref[...] = jnp.concatenate([da, db], axis=1).astype(bf16)
        h_ref[...] = h.astype(bf16)

    pair = pl.BlockSpec((tm, 2 * FF_TILE), lambda i, j: (i, j))
    return pl.pallas_call(
        body, grid=(M // tm, D_FF // FF_TILE),
        in_specs=[pl.BlockSpec((tm, K), lambda i, j: (i, 0)), pl.BlockSpec((K, FF_TILE), lambda i, j: (0, j)), pair],
        out_specs=[pair, pl.BlockSpec((tm, FF_TILE), lambda i, j: (i, j))],
        out_shape=[jax.ShapeDtypeStruct((M, 2 * D_FF), bf16), jax.ShapeDtypeStruct((M, D_FF), bf16)],
        compiler_params=_params(("parallel", "parallel")), name="ffn_down_dx_act")(dzs, wdT, ab)


def ffn_fwd(x, xb, w, g, b):
    ab, h = ffn_up_act(xb, w['gu'])
    f = mm(h, w['d'], name="ffn_down")
    z, xo, xob = resid_ln(x, f, g, b, 0.5, "ffn_ln")
    return xo, xob, (xb, ab, z)


def ffn_bwd(dxo, res, w, g, b):
    xb, ab, z = res
    dz, dzs, dg, db = ln_bwd(z, g, b, dxo, 0.5, "ffn_ln_bwd")
    dab, h = ffn_down_dx_act(dzs, w['dT'], ab)
    dwd = mm_tn(h, dzs, name="ffn_down_dw")
    dwg, dwu = _split_gate_up(mm_tn(xb, dab, name="ffn_up_dw"))
    dx = mm(dab, w['guT'], add=dz, add_scale=ALPHA, name="ffn_up_dx")
    return dx, dwg, dwu, dwd, dg, db


def _heads_first(a, width):
    return a.reshape(a.shape[0], MLA_HEADS, width)


def mixer_fwd(x, xb, w, g, b, cos8, sin8):
    S = x.shape[0]
    H = mm(xb, w['in'], name="mix_in")
    ya = rowwise(_gmlp, [(H, H_UG, 256), (H, H_VG, 256)], [w['gm_ng'], w['gm_ws'], w['gm_bsb']], [(GM_WIDTH, f32)], [],
                 GM_CHUNK, "gmlp")[0]
    cqn, ckvn = rowwise(_mla_prep, [(H, H_CQ, Q_LORA), (H, H_CKV, KV_LORA)], [w['qg'], w['kvg']],
                        [(Q_LORA, bf16), (KV_LORA, bf16)], [], _row_tile(S, 512), "mla_prep")
    qraw = mm(cqn, w['uq'], name="mla_uq")
    kv = mm(ckvn, w['ukv'], name="mla_ukv")
    q1, q2, k1, k2 = rowwise(_rope, [(qraw, 512, LANES), (qraw, 640, LANES), (H, H_K1, LANES), (H, H_K2, LANES), cos8, sin8],
                             [], [(LANES, f32)] * 4, [], _row_tile(S, 512), "rope")
    zpad = jnp.zeros((S, MLA_HEADS, LANES - MLA_NOPE - MLA_ROPE), f32)
    qp = jnp.concatenate([_heads_first(qraw[:, :512], 64), _heads_first(q1, ROPE_HALF), _heads_first(q2, ROPE_HALF), zpad], axis=2)
    k1b = jnp.broadcast_to(k1[:, None, :ROPE_HALF], (S, MLA_HEADS, ROPE_HALF))
    k2b = jnp.broadcast_to(k2[:, None, :ROPE_HALF], (S, MLA_HEADS, ROPE_HALF))
    kp = jnp.concatenate([_heads_first(kv[:, :512], 64), k1b, k2b, zpad], axis=2)
    qp = qp.transpose(1, 0, 2).astype(bf16)
    kp = kp.transpose(1, 0, 2).astype(bf16)
    v3 = _heads_first(kv[:, 512:], 64).astype(bf16)
    vp = v3.transpose(1, 0, 2)
    oT, lse = flash_fwd(qp, kp, v3.transpose(1, 2, 0))
    o = oT.transpose(0, 2, 1)
    ob = oT.transpose(2, 0, 1).reshape(S, MLA_HEADS * MLA_V)
    bu = mm(H, w['ssm_wb'], a_col0=H_US, name="ssm_bu")
    hs3 = scan_fwd(bu.reshape(S, 16, LANES), w['ssm_a16'])
    hs = hs3.reshape(S, 2 * N_STATE)
    y1 = mm(hs, w['ssm_wc'], name="ssm_c")
    y = rowwise(_mix_post, [ya, ob, y1, (H, H_US, SSM_WIDTH)], [w['ssm_d'], w['glu_w'], w['glu_b'], w['gmix']],
                [(D_MODEL, bf16)], [], _row_tile(S, 256), "mix_post")[0]
    f = mm(y, w['out'], name="mix_out")
    z, xo, xob = resid_ln(x, f, g, b, 1.0, "mix_ln")
    return xo, xob, (xb, H, cqn, ckvn, qp, kp, vp, o, lse, hs3, ya, ob, y1, y, z)


def mixer_bwd(dxo, res, w, g, b, cos8, sin8):
    xb, H, cqn, ckvn, qp, kp, vp, o, lse, hs3, ya, ob, y1, y, z = res
    S = z.shape[0]
    gr = {}
    dz, dzs, gr['ln_g'], gr['ln_b'] = ln_bwd(z, g, b, dxo, 1.0, "mix_ln_bwd")
    gr['w_out'] = mm_tn(y, dzs, name="mix_out_dw")
    dy = mm(dzs, w['outT'], name="mix_out_dx")

    def post_bwd(ya, ob, y1, us, dy, dskip, gluw, glub, gmix):
        _, vjp = jax.vjp(_mix_post, ya, ob, y1, us, dskip, gluw, glub, gmix)
        return vjp(dy)

    dya, dob, dy1, dus_skip, gr['ssm_d'], gr['ssm_glu_w'], gr['ssm_glu_b'], gr['mix_norm_g'] = rowwise(
        post_bwd, [ya, ob, y1, (H, H_US, SSM_WIDTH), dy], [w['ssm_d'], w['glu_w'], w['glu_b'], w['gmix']],
        [(GM_WIDTH, f32), (MLA_HEADS * MLA_V, f32), (SSM_WIDTH, f32), (SSM_WIDTH, f32)],
        [(1, SSM_WIDTH), (SSM_WIDTH, SSM_WIDTH), (1, SSM_WIDTH), (1, D_MODEL)], _row_tile(S, 128), "mix_post_bwd")

    hs = hs3.reshape(S, 2 * N_STATE)
    gr['ssm_wc'] = mm_tn(hs, dy1, name="ssm_c_dw")
    dhs = mm(dy1, w['ssm_wcT'], name="ssm_c_dx")
    dbu3, gr['ssm_a16'] = scan_bwd(dhs.reshape(S, 16, LANES), hs3, w['ssm_a16'])
    dbu = dbu3.reshape(S, 2 * N_STATE)
    gr['ssm_wb'] = mm_tn(H, dbu, a_col0=H_US, m_dim=SSM_WIDTH, name="ssm_bu_dw")
    dus = mm(dbu, w['ssm_wbT'], add=dus_skip, add_scale=1.0, name="ssm_bu_dx")

    do = _heads_first(dob, MLA_V).transpose(1, 0, 2)
    delta = rowwise(lambda a, c: jnp.sum(a * c, axis=-1, keepdims=True),
                    [do.reshape(MLA_HEADS * S, MLA_V), o.reshape(MLA_HEADS * S, MLA_V)], [], [(1, f32)], [],
                    _row_tile(MLA_HEADS * S, 512), "att_delta")[0].reshape(MLA_HEADS, 1, S)
    dqT, dkp, dvp = flash_bwd(qp, kp, kp.transpose(0, 2, 1), vp, do.astype(bf16), lse, delta)
    dqp = dqT.transpose(2, 0, 1)
    dkp = dkp.transpose(1, 0, 2)
    dv = dvp.transpose(1, 0, 2).reshape(S, MLA_HEADS * MLA_V)
    lane_pad = ((0, 0), (0, LANES - ROPE_HALF))
    dq1r = dqp[:, :, 64:80].reshape(S, LANES)
    dq2r = dqp[:, :, 80:96].reshape(S, LANES)
    dk1r = jnp.pad(jnp.sum(dkp[:, :, 64:80], axis=1), lane_pad)
    dk2r = jnp.pad(jnp.sum(dkp[:, :, 80:96], axis=1), lane_pad)

    def rope_bwd(d1, d2, d3, d4, cos, sin):
        return d1 * cos + d2 * sin, d2 * cos - d1 * sin, d3 * cos + d4 * sin, d4 * cos - d3 * sin

    dq1, dq2, dk1, dk2 = rowwise(rope_bwd, [dq1r, dq2r, dk1r, dk2r, cos8, sin8], [], [(LANES, f32)] * 4, [],
                                 _row_tile(S, 512), "rope_bwd")
    dqraw = jnp.concatenate([dqp[:, :, :64].reshape(S, 512), dq1, dq2], axis=1).astype(bf16)
    dkv = jnp.concatenate([dkp[:, :, :64].reshape(S, 512), dv], axis=1).astype(bf16)
    gr['uq'] = mm_tn(cqn, dqraw, name="mla_uq_dw")
    dcqn = mm(dqraw, w['uqT'], name="mla_uq_dx")
    gr['ukv'] = mm_tn(ckvn, dkv, name="mla_ukv_dw")
    dckvn = mm(dkv, w['ukvT'], name="mla_ukv_dx")

    def prep_bwd(cq, ckv, d1, d2, qg, kvg):
        _, vjp = jax.vjp(_mla_prep, cq, ckv, qg, kvg)
        return vjp((d1, d2))

    dcq, dckv, gr['mla_q_norm_g'], gr['mla_kv_norm_g'] = rowwise(
        prep_bwd, [(H, H_CQ, Q_LORA), (H, H_CKV, KV_LORA), dcqn, dckvn], [w['qg'], w['kvg']],
        [(Q_LORA, f32), (KV_LORA, f32)], [(1, Q_LORA), (1, KV_LORA)], _row_tile(S, 256), "mla_prep_bwd")

    def gmlp_bwd(hu, hv, dya, ng, ws, bsb):
        _, vjp = jax.vjp(_gmlp, hu, hv, ng, ws, bsb)
        return vjp(dya)

    dhu, dhv, gr['gmlp_norm_g'], gr['gmlp_ws'], gr['gm_bsb'] = rowwise(
        gmlp_bwd, [(H, H_UG, 256), (H, H_VG, 256), dya], [w['gm_ng'], w['gm_ws'], w['gm_bsb']],
        [(GM_WIDTH, f32), (GM_WIDTH, f32)], [(1, GM_WIDTH), (GM_HEADS, GM_CHUNK, GM_CHUNK), (GM_CHUNK, GM_WIDTH)],
        GM_CHUNK, "gmlp_bwd")

    dH = jnp.concatenate([dhu, dhv, dcq, dus, dckv, dk1, dk2], axis=1).astype(bf16)
    gr['in'] = mm_tn(xb, dH, name="mix_in_dw")
    dx = mm(dH, w['inT'], add=dz, add_scale=ALPHA, name="mix_in_dx")
    return dx, gr


def _block_diag(blocks):
    G, a, b = blocks.shape
    eye = jnp.eye(G, dtype=blocks.dtype)
    return (eye[:, None, :, None] * blocks[:, :, None, :]).reshape(G * a, G * b)


def _diag_blocks(mat, G):
    a, b = mat.shape[0] // G, mat.shape[1] // G
    m4 = mat.reshape(G, a, G, b)
    eye = jnp.eye(G, dtype=mat.dtype)
    return jnp.sum(m4 * eye[:, None, :, None], axis=2)


def prep_layer(W, rep, l):
    w = {}
    for f in ('ffn1', 'ffn2'):
        gu = _interleave_gate_up(W[f + '_w_gate'], W[f + '_w_up'])
        w[f] = {'gu': gu, 'guT': gu.T, 'd': W[f + '_w_down'], 'dT': W[f + '_w_down'].T}
    wi = W['w_in']
    z112 = jnp.zeros((D_MODEL, LANES - ROPE_HALF), wi.dtype)
    w['in'] = jnp.concatenate([wi[:, :768], wi[:, 928:1184], wi[:, 768:896], wi[:, 896:912], z112, wi[:, 912:928], z112], axis=1)
    w['inT'] = w['in'].T
    uq = W['mla_w_uq'].reshape(Q_LORA, MLA_HEADS, MLA_NOPE + MLA_ROPE)
    w['uq'] = jnp.concatenate([uq[:, :, :64].reshape(Q_LORA, 512), uq[:, :, 64:80].reshape(Q_LORA, LANES),
                               uq[:, :, 80:96].reshape(Q_LORA, LANES)], axis=1)
    w['uqT'] = w['uq'].T
    ukv = W['mla_w_ukv'].reshape(KV_LORA, MLA_HEADS, MLA_NOPE + MLA_V)
    w['ukv'] = jnp.concatenate([ukv[:, :, :64].reshape(KV_LORA, 512), ukv[:, :, 64:].reshape(KV_LORA, 512)], axis=1)
    w['ukvT'] = w['ukv'].T
    w['out'] = W['w_out']
    w['outT'] = W['w_out'].T
    w['glu_w'] = W['ssm_glu_w']
    w['gm_ng'] = rep['gmlp_norm_g'][l].reshape(1, GM_WIDTH)
    w['gm_ws'] = rep['gmlp_ws'][l]
    w['gm_bsb'] = jnp.repeat(rep['gmlp_bs'][l].T, GM_HEAD_DIM, axis=1)
    w['qg'] = rep['mla_q_norm_g'][l].reshape(1, Q_LORA)
    w['kvg'] = rep['mla_kv_norm_g'][l].reshape(1, KV_LORA)
    w['ssm_d'] = rep['ssm_d'][l].reshape(1, SSM_WIDTH)
    w['glu_b'] = rep['ssm_glu_b'][l].reshape(1, SSM_WIDTH)
    w['gmix'] = rep['mix_norm_g'][l].reshape(1, D_MODEL)
    ar = rep['ssm_a_re'][l].reshape(1, N_STATE)
    ai = rep['ssm_a_im'][l].reshape(1, N_STATE)
    ldt = jnp.repeat(rep['ssm_log_dt'][l], SSM_STATE).reshape(1, N_STATE)
    brT = rep['ssm_b_re'][l].transpose(2, 0, 1).reshape(SSM_GROUP_CH, N_STATE)
    biT = rep['ssm_b_im'][l].transpose(2, 0, 1).reshape(SSM_GROUP_CH, N_STATE)
    w['ssm_prep_in'] = (ar, ai, ldt, brT, biT)
    abr, abi, bbrT, bbiT = whole(_ssm_prep, w['ssm_prep_in'], [(1, N_STATE)] * 2 + [(SSM_GROUP_CH, N_STATE)] * 2, "ssm_prep")
    w['ssm_a16'] = jnp.concatenate([abr.reshape(8, LANES), abi.reshape(8, LANES)], axis=0)

    def to_gcp(t):
        return t.reshape(SSM_GROUP_CH, SSM_GROUPS, SSM_STATE).transpose(1, 0, 2)

    w['ssm_wb'] = jnp.concatenate([_block_diag(to_gcp(bbrT)), _block_diag(to_gcp(bbiT))], axis=1).astype(bf16)
    w['ssm_wbT'] = w['ssm_wb'].T
    cre = rep['ssm_c_re'][l].transpose(0, 2, 1)
    cim = rep['ssm_c_im'][l].transpose(0, 2, 1)
    w['ssm_wc'] = jnp.concatenate([_block_diag(cre), -_block_diag(cim)], axis=0).astype(bf16)
    w['ssm_wcT'] = w['ssm_wc'].T
    return w


def unprep_grads(gr, w):
    out = {}
    for k in ('ln_g', 'ln_b', 'w_out', 'mla_q_norm_g', 'mla_kv_norm_g', 'ssm_glu_w', 'gmlp_ws'):
        out[k] = gr[k]
    out['gmlp_norm_g'] = gr['gmlp_norm_g'].reshape(GM_WIDTH)
    out['mla_q_norm_g'] = gr['mla_q_norm_g'].reshape(Q_LORA)
    out['mla_kv_norm_g'] = gr['mla_kv_norm_g'].reshape(KV_LORA)
    out['ssm_d'] = gr['ssm_d'].reshape(SSM_GROUPS, SSM_GROUP_CH)
    out['ssm_glu_b'] = gr['ssm_glu_b'].reshape(SSM_WIDTH)
    out['mix_norm_g'] = gr['mix_norm_g'].reshape(D_MODEL)
    out['gmlp_bs'] = gr['gm_bsb'].reshape(GM_CHUNK, GM_HEADS, GM_HEAD_DIM).sum(axis=-1).T
    d = gr['in']
    out['w_in'] = jnp.concatenate([d[:, :768], d[:, H_CKV:H_CKV + KV_LORA], d[:, H_K1:H_K1 + ROPE_HALF],
                                   d[:, H_K2:H_K2 + ROPE_HALF], d[:, H_US:H_US + SSM_WIDTH]], axis=1)
    d = gr['uq']
    out['mla_w_uq'] = jnp.concatenate([d[:, :512].reshape(Q_LORA, MLA_HEADS, 64), d[:, 512:640].reshape(Q_LORA, MLA_HEADS, ROPE_HALF),
                                       d[:, 640:768].reshape(Q_LORA, MLA_HEADS, ROPE_HALF)], axis=2).reshape(Q_LORA, 768)
    d = gr['ukv']
    out['mla_w_ukv'] = jnp.concatenate([d[:, :512].reshape(KV_LORA, MLA_HEADS, 64), d[:, 512:].reshape(KV_LORA, MLA_HEADS, 64)],
                                       axis=2).reshape(KV_LORA, 1024)
    dwc = gr['ssm_wc']
    out['ssm_c_re'] = _diag_blocks(dwc[:N_STATE], SSM_GROUPS).transpose(0, 2, 1)
    out['ssm_c_im'] = -_diag_blocks(dwc[N_STATE:], SSM_GROUPS).transpose(0, 2, 1)
    dwb = gr['ssm_wb']

    def from_blocks(m):
        return _diag_blocks(m, SSM_GROUPS).transpose(1, 0, 2).reshape(SSM_GROUP_CH, N_STATE)

    dbbrT, dbbiT = from_blocks(dwb[:, :N_STATE]), from_blocks(dwb[:, N_STATE:])
    da16 = gr['ssm_a16']
    dabr, dabi = da16[0:8].reshape(1, N_STATE), da16[8:16].reshape(1, N_STATE)

    def prep_bwd(ar, ai, ldt, brT, biT, d1, d2, d3, d4):
        _, vjp = jax.vjp(_ssm_prep, ar, ai, ldt, brT, biT)
        return vjp((d1, d2, d3, d4))

    dar, dai, dldt, dbrT, dbiT = whole(prep_bwd, w['ssm_prep_in'] + (dabr, dabi, dbbrT, dbbiT),
                                       [(1, N_STATE)] * 3 + [(SSM_GROUP_CH, N_STATE)] * 2, "ssm_prep_bwd")
    out['ssm_a_re'] = dar.reshape(SSM_GROUPS, SSM_STATE)
    out['ssm_a_im'] = dai.reshape(SSM_GROUPS, SSM_STATE)
    out['ssm_log_dt'] = dldt.reshape(SSM_GROUPS, SSM_STATE).sum(axis=-1)
    out['ssm_b_re'] = dbrT.reshape(SSM_GROUP_CH, SSM_GROUPS, SSM_STATE).transpose(1, 2, 0)
    out['ssm_b_im'] = dbiT.reshape(SSM_GROUP_CH, SSM_GROUPS, SSM_STATE).transpose(1, 2, 0)
    return out


def adamw(w, g, m, v, name):
    R, C = w.shape

    def fn(w, g, m, v):
        m = ADAM_B1 * m + (1.0 - ADAM_B1) * g
        v = ADAM_B2 * v + (1.0 - ADAM_B2) * jnp.square(g)
        m_hat = m / (1.0 - ADAM_B1 ** ADAM_STEP)
        v_hat = v / (1.0 - ADAM_B2 ** ADAM_STEP)
        delta = -ADAM_LR * (m_hat / (jnp.sqrt(v_hat) + ADAM_EPS) + ADAM_WD * w)
        return delta, m, v

    return rowwise(fn, [w, g, m, v], [], [(C, f32)] * 3, [], _pick(R, (256, 128, 64, 32, 16, 8)), name)


def kernel(x, positions, ln_g, ln_b, ffn1_w_gate, ffn1_w_up, ffn1_w_down, w_in, gmlp_norm_g, gmlp_ws, gmlp_bs, mla_q_norm_g, mla_w_uq, mla_kv_norm_g, mla_w_ukv, ssm_a_re, ssm_a_im, ssm_b_re, ssm_b_im, ssm_c_re, ssm_c_im, ssm_d, ssm_log_dt, ssm_glu_w, ssm_glu_b, mix_norm_g, w_out, ffn2_w_gate, ffn2_w_up, ffn2_w_down, loss_target, m_ln_g, m_ln_b, m_ffn1_w_gate, m_ffn1_w_up, m_ffn1_w_down, m_w_in, m_gmlp_norm_g, m_gmlp_ws, m_gmlp_bs, m_mla_q_norm_g, m_mla_w_uq, m_mla_kv_norm_g, m_mla_w_ukv, m_ssm_a_re, m_ssm_a_im, m_ssm_b_re, m_ssm_b_im, m_ssm_c_re, m_ssm_c_im, m_ssm_d, m_ssm_log_dt, m_ssm_glu_w, m_ssm_glu_b, m_mix_norm_g, m_w_out, m_ffn2_w_gate, m_ffn2_w_up, m_ffn2_w_down, v_ln_g, v_ln_b, v_ffn1_w_gate, v_ffn1_w_up, v_ffn1_w_down, v_w_in, v_gmlp_norm_g, v_gmlp_ws, v_gmlp_bs, v_mla_q_norm_g, v_mla_w_uq, v_mla_kv_norm_g, v_mla_w_ukv, v_ssm_a_re, v_ssm_a_im, v_ssm_b_re, v_ssm_b_im, v_ssm_c_re, v_ssm_c_im, v_ssm_d, v_ssm_log_dt, v_ssm_glu_w, v_ssm_glu_b, v_mix_norm_g, v_w_out, v_ffn2_w_gate, v_ffn2_w_up, v_ffn2_w_down):
    Wp = dict(zip(W_NAMES, (ln_g, ln_b, ffn1_w_gate, ffn1_w_up, ffn1_w_down, w_in, gmlp_norm_g, gmlp_ws, gmlp_bs, mla_q_norm_g, mla_w_uq, mla_kv_norm_g, mla_w_ukv, ssm_a_re, ssm_a_im, ssm_b_re, ssm_b_im, ssm_c_re, ssm_c_im, ssm_d, ssm_log_dt, ssm_glu_w, ssm_glu_b, mix_norm_g, w_out, ffn2_w_gate, ffn2_w_up, ffn2_w_down)))
    Mp = dict(zip(W_NAMES, (m_ln_g, m_ln_b, m_ffn1_w_gate, m_ffn1_w_up, m_ffn1_w_down, m_w_in, m_gmlp_norm_g, m_gmlp_ws, m_gmlp_bs, m_mla_q_norm_g, m_mla_w_uq, m_mla_kv_norm_g, m_mla_w_ukv, m_ssm_a_re, m_ssm_a_im, m_ssm_b_re, m_ssm_b_im, m_ssm_c_re, m_ssm_c_im, m_ssm_d, m_ssm_log_dt, m_ssm_glu_w, m_ssm_glu_b, m_mix_norm_g, m_w_out, m_ffn2_w_gate, m_ffn2_w_up, m_ffn2_w_down)))
    Vp = dict(zip(W_NAMES, (v_ln_g, v_ln_b, v_ffn1_w_gate, v_ffn1_w_up, v_ffn1_w_down, v_w_in, v_gmlp_norm_g, v_gmlp_ws, v_gmlp_bs, v_mla_q_norm_g, v_mla_w_uq, v_mla_kv_norm_g, v_mla_w_ukv, v_ssm_a_re, v_ssm_a_im, v_ssm_b_re, v_ssm_b_im, v_ssm_c_re, v_ssm_c_im, v_ssm_d, v_ssm_log_dt, v_ssm_glu_w, v_ssm_glu_b, v_mix_norm_g, v_w_out, v_ffn2_w_gate, v_ffn2_w_up, v_ffn2_w_down)))
    S = x.shape[1]
    my = _my_id()

    big_flat = jnp.concatenate([Wp[n][l].reshape(-1) for l in range(DEPTH) for n in BIG]).astype(bf16)
    ln_flat = jnp.concatenate([Wp[n].reshape(-1) for n in LN_NAMES])
    big_all, ln_all = all_gather([_pad_rows(big_flat, 16), _pad_rows(ln_flat, 8)], "gather_weights")
    big_all = big_all.reshape(N_DEV, -1)
    ln_all = ln_all.reshape(N_DEV, -1)
    lnsz = DEPTH * 3 * (D_MODEL // N_DEV)
    ln_full = {}
    for t, n in enumerate(LN_NAMES):
        sh = ln_all[:, t * lnsz:(t + 1) * lnsz].reshape(N_DEV, DEPTH, 3, D_MODEL // N_DEV)
        ln_full[n] = sh.transpose(1, 2, 0, 3).reshape(DEPTH, 3, 1, D_MODEL)
    Wl, off = [], 0
    for l in range(DEPTH):
        d = {}
        for n in BIG:
            r, c = _shard_shape(n)
            d[n] = _join_from_devices(n, big_all[:, off:off + r * c])
            off += r * c
        Wl.append(d)
    n_big = off
    rep = {n: Wp[n] for n in REPL}

    inv_freq = 1.0 / (ROPE_BASE ** (jnp.arange(0, MLA_ROPE, 2, dtype=f32) / MLA_ROPE))
    ang = positions.astype(f32).reshape(S, 1) * inv_freq[None, :]
    cos8 = jnp.tile(jnp.cos(ang), (1, MLA_HEADS))
    sin8 = jnp.tile(jnp.sin(ang), (1, MLA_HEADS))

    xs = x.reshape(S, D_MODEL)
    xb = xs.astype(bf16)
    ws, saved = [], []
    for l in range(DEPTH):
        w = prep_layer(Wl[l], rep, l)
        lg, lb = ln_full['ln_g'][l], ln_full['ln_b'][l]
        xs, xb, r1 = ffn_fwd(xs, xb, w['ffn1'], lg[0], lb[0])
        xs, xb, r2 = mixer_fwd(xs, xb, w, lg[1], lb[1], cos8, sin8)
        xs, xb, r3 = ffn_fwd(xs, xb, w['ffn2'], lg[2], lb[2])
        ws.append(w)
        saved.append((r1, r2, r3))

    def loss_fn(y, t):
        d = y - t
        part = jnp.sum(jnp.mean(jnp.square(d), axis=-1, keepdims=True), axis=0, keepdims=True)
        return d * (1.0 / D_MODEL), 0.5 * part

    dx, loss_part = rowwise(loss_fn, [xs, loss_target.reshape(S, D_MODEL)], [], [(D_MODEL, f32)], [(1, 1)],
                            _row_tile(S, 512), "loss")
    loss = lax.psum(loss_part[0, 0], ("x", "y", "c"))

    grads = [None] * DEPTH
    for l in reversed(range(DEPTH)):
        w = ws[l]
        lg, lb = ln_full['ln_g'][l], ln_full['ln_b'][l]
        r1, r2, r3 = saved[l]
        dx, g2g, g2u, g2d, dg2, db2 = ffn_bwd(dx, r3, w['ffn2'], lg[2], lb[2])
        dx, gm = mixer_bwd(dx, r2, w, lg[1], lb[1], cos8, sin8)
        dx, g1g, g1u, g1d, dg0, db0 = ffn_bwd(dx, r1, w['ffn1'], lg[0], lb[0])
        g = unprep_grads(gm, w)
        g.update({'ffn1_w_gate': g1g, 'ffn1_w_up': g1u, 'ffn1_w_down': g1d,
                  'ffn2_w_gate': g2g, 'ffn2_w_up': g2u, 'ffn2_w_down': g2d})
        g['ln_g'] = jnp.concatenate([dg0, g['ln_g'], dg2], axis=0)
        g['ln_b'] = jnp.concatenate([db0, g['ln_b'], db2], axis=0)
        grads[l] = g
    grad_x = dx.reshape(1, S, D_MODEL)

    gpack = jnp.concatenate([_split_for_devices(n, grads[l][n]).astype(bf16) for l in range(DEPTH) for n in BIG], axis=1)
    gpack = jnp.pad(gpack, ((0, 0), (0, -n_big % (16 * PACK_W)))).reshape(N_DEV, -1, PACK_W)
    gbig = sum_slots(all_to_all([gpack], "scatter_grads")[0], "sum_big").reshape(-1)
    small_names = LN_NAMES + REPL
    spack = jnp.concatenate([jnp.stack([grads[l][n] for l in range(DEPTH)]).reshape(-1) for n in small_names])
    n_small = spack.shape[0]
    gsmall = sum_slots(all_gather([_pad_rows(spack, 8)], "gather_small")[0], "sum_small").reshape(-1)

    G, off = {}, 0
    big_parts = {n: [] for n in BIG}
    for l in range(DEPTH):
        for n in BIG:
            r, c = _shard_shape(n)
            big_parts[n].append(gbig[off:off + r * c].reshape(r, c))
            off += r * c
    for n in BIG:
        G[n] = jnp.stack(big_parts[n])
    off = 0
    for n in small_names:
        shp = (DEPTH, 3, D_MODEL) if n in LN_NAMES else Wp[n].shape
        sz = math.prod(shp)
        G[n] = gsmall[off:off + sz].reshape(shp)
        off += sz
    for n in LN_NAMES:
        G[n] = lax.dynamic_slice_in_dim(G[n], my * (D_MODEL // N_DEV), D_MODEL // N_DEV, axis=2)

    delta, new_m, new_v = {}, {}, {}
    for n in BIG:
        shp = Wp[n].shape
        two = (shp[0] * shp[1], shp[2])
        d_, m_, v_ = adamw(Wp[n].reshape(two), G[n].reshape(two), Mp[n].reshape(two), Vp[n].reshape(two), "adamw_" + n)
        delta[n], new_m[n], new_v[n] = d_.reshape(shp), m_.reshape(shp), v_.reshape(shp)

    def pack_small(src):
        return _pad_rows(jnp.concatenate([src[n].reshape(-1) for n in small_names]), 8)

    d_, m_, v_ = adamw(pack_small(Wp), pack_small(G), pack_small(Mp), pack_small(Vp), "adamw_small")
    d_, m_, v_ = d_.reshape(-1), m_.reshape(-1), v_.reshape(-1)
    off = 0
    for n in small_names:
        shp = Wp[n].shape
        sz = math.prod(shp)
        delta[n], new_m[n], new_v[n] = (t[off:off + sz].reshape(shp) for t in (d_, m_, v_))
        off += sz

    return (loss, grad_x, *[G[n] for n in W_NAMES], *[delta[n] for n in W_NAMES],
            *[new_m[n] for n in W_NAMES], *[new_v[n] for n in W_NAMES])
```

```python
import functools
import math

import jax
import jax.numpy as jnp
from jax import lax
from jax.experimental import pallas as pl
from jax.experimental.pallas import tpu as pltpu

f32 = jnp.float32
bf16 = jnp.bfloat16

D_MODEL = 1024
DEPTH = 4
D_FF = 2816
GM_HEADS, GM_HEAD_DIM, GM_WIDTH, GM_CHUNK = 4, 64, 256, 128
MLA_HEADS, MLA_NOPE, MLA_ROPE, MLA_V = 8, 64, 32, 64
ROPE_HALF = MLA_ROPE // 2
Q_LORA, KV_LORA = 256, 128
ROPE_BASE = 10000.0
SSM_GROUPS, SSM_GROUP_CH, SSM_WIDTH, SSM_STATE = 16, 16, 256, 64
N_STATE = SSM_GROUPS * SSM_STATE
ALPHA = (2 * DEPTH) ** 0.25
LN_EPS = 1e-5
RMS_EPS = 1e-6
NEG_BIG = -1e30
ATT_SCALE = (MLA_NOPE + MLA_ROPE) ** -0.5
ADAM_LR, ADAM_B1, ADAM_B2, ADAM_EPS, ADAM_WD, ADAM_STEP = 0.001, 0.9, 0.999, 1e-08, 0.01, 10

N_DEV = 8
LANES = 128
VMEM_LIMIT = 48 * 1024 * 1024
MM_TILE_BUDGET = 32 * 1024 * 1024
MESH = pl.DeviceIdType.MESH

H_UG, H_VG, H_CQ, H_US, H_CKV, H_K1, H_K2, H_COLS = 0, 256, 512, 768, 1024, 1152, 1280, 1408

W_NAMES = ['ln_g', 'ln_b', 'ffn1_w_gate', 'ffn1_w_up', 'ffn1_w_down', 'w_in', 'gmlp_norm_g', 'gmlp_ws', 'gmlp_bs',
           'mla_q_norm_g', 'mla_w_uq', 'mla_kv_norm_g', 'mla_w_ukv', 'ssm_a_re', 'ssm_a_im', 'ssm_b_re', 'ssm_b_im',
           'ssm_c_re', 'ssm_c_im', 'ssm_d', 'ssm_log_dt', 'ssm_glu_w', 'ssm_glu_b', 'mix_norm_g', 'w_out',
           'ffn2_w_gate', 'ffn2_w_up', 'ffn2_w_down']
BIG = {'ffn1_w_gate': 1, 'ffn1_w_up': 1, 'ffn1_w_down': 0, 'w_in': 1, 'mla_w_uq': 1, 'mla_w_ukv': 1,
       'ssm_glu_w': 0, 'w_out': 0, 'ffn2_w_gate': 1, 'ffn2_w_up': 1, 'ffn2_w_down': 0}
BIG_SHAPE = {'ffn1_w_gate': (D_MODEL, D_FF), 'ffn1_w_up': (D_MODEL, D_FF), 'ffn1_w_down': (D_FF, D_MODEL),
             'w_in': (D_MODEL, 1184), 'mla_w_uq': (Q_LORA, 768), 'mla_w_ukv': (KV_LORA, 1024),
             'ssm_glu_w': (SSM_WIDTH, SSM_WIDTH), 'w_out': (D_MODEL, D_MODEL),
             'ffn2_w_gate': (D_MODEL, D_FF), 'ffn2_w_up': (D_MODEL, D_FF), 'ffn2_w_down': (D_FF, D_MODEL)}
LN_NAMES = ['ln_g', 'ln_b']
REPL = [n for n in W_NAMES if n not in BIG and n not in LN_NAMES]


def _pick(n, cands):
    for c in cands:
        if n % c == 0:
            return c
    return n


def _params(sem):
    return pltpu.CompilerParams(dimension_semantics=sem, vmem_limit_bytes=VMEM_LIMIT)


def mm(a, b, *, out_dtype=f32, add=None, add_scale=1.0, a_col0=0, name):
    M = a.shape[0]
    K, N = b.shape
    tn = _pick(N, (512, 384, 256) if N <= 1536 else (512, 384, 256, 128))
    tk = K if K <= 2 * D_FF else _pick(K, (1024, 512, 256, 128))
    nk = K // tk

    def tile_bytes(tm):
        return 2 * (tm * tk * a.dtype.itemsize + tk * tn * b.dtype.itemsize + tm * tn * 4 * (2 if add is not None else 1))

    tm = next(t for t in (1024, 512, 256, 128, 64, 32, 16, 8) if M % t == 0 and (tile_bytes(t) <= MM_TILE_BUDGET or t == 8))
    assert a_col0 % tk == 0 and a_col0 + K <= a.shape[1]
    kb0 = a_col0 // tk
    has_add = add is not None

    def body(*refs):
        if has_add:
            a_ref, b_ref, add_ref, o_ref, acc = refs
        else:
            a_ref, b_ref, o_ref, acc = refs
        k = pl.program_id(2)
        part = jnp.dot(a_ref[...].astype(bf16), b_ref[...].astype(bf16), preferred_element_type=f32)

        def finish(total):
            if has_add:
                total = total + add_scale * add_ref[...]
            o_ref[...] = total.astype(o_ref.dtype)

        if nk == 1:
            finish(part)
        else:
            @pl.when(k == 0)
            def _():
                acc[...] = part

            @pl.when(k > 0)
            def _():
                acc[...] += part

            @pl.when(k == nk - 1)
            def _():
                finish(acc[...])

    in_specs = [pl.BlockSpec((tm, tk), lambda i, j, k: (i, kb0 + k)), pl.BlockSpec((tk, tn), lambda i, j, k: (k, j))]
    ops = [a, b]
    if has_add:
        in_specs.append(pl.BlockSpec((tm, tn), lambda i, j, k: (i, j)))
        ops.append(add)
    return pl.pallas_call(
        body, grid=(M // tm, N // tn, nk), in_specs=in_specs,
        out_specs=pl.BlockSpec((tm, tn), lambda i, j, k: (i, j)),
        out_shape=jax.ShapeDtypeStruct((M, N), out_dtype),
        scratch_shapes=[pltpu.VMEM((tm, tn) if nk > 1 else (8, LANES), f32)],
        compiler_params=_params(("parallel", "parallel", "arbitrary")), name=name)(*ops)


def mm_tn(a, b, *, a_col0=0, m_dim=None, name):
    K = a.shape[0]
    M = a.shape[1] if m_dim is None else m_dim
    N = b.shape[1]
    tm = _pick(M, (1024, 768, 512, 384, 256, 128))
    tn = _pick(N, (1024, 768, 512, 384, 256) if N <= 1536 else (1024, 768, 512, 384, 256, 128))

    def tile_bytes(tk):
        return 2 * tk * (tm * a.dtype.itemsize + tn * b.dtype.itemsize) + 3 * tm * tn * 4

    tk = next(t for t in (2048, 1024, 512, 256, 128, 64, 32, 16) if K % t == 0 and (tile_bytes(t) <= MM_TILE_BUDGET or t == 16))
    nk = K // tk
    assert a_col0 % tm == 0
    mb0 = a_col0 // tm

    def body(a_ref, b_ref, o_ref, acc):
        k = pl.program_id(2)
        part = lax.dot_general(a_ref[...].astype(bf16), b_ref[...].astype(bf16), (((0,), (0,)), ((), ())),
                               preferred_element_type=f32)

        @pl.when(k == 0)
        def _():
            acc[...] = part

        @pl.when(k > 0)
        def _():
            acc[...] += part

        @pl.when(k == nk - 1)
        def _():
            o_ref[...] = acc[...]

    return pl.pallas_call(
        body, grid=(M // tm, N // tn, nk),
        in_specs=[pl.BlockSpec((tk, tm), lambda i, j, k: (k, mb0 + i)), pl.BlockSpec((tk, tn), lambda i, j, k: (k, j))],
        out_specs=pl.BlockSpec((tm, tn), lambda i, j, k: (i, j)),
        out_shape=jax.ShapeDtypeStruct((M, N), f32),
        scratch_shapes=[pltpu.VMEM((tm, tn), f32)],
        compiler_params=_params(("parallel", "parallel", "arbitrary")), name=name)(a, b)


def rowwise(fn, rows, pars, out_rows, out_accs, tm, name):
    first = rows[0]
    if isinstance(first, tuple):
        R = first[0].shape[1] if first[1] == 'lead' else first[0].shape[0]
    else:
        R = first.shape[0]
    assert R % tm == 0, (R, tm, name)
    n_rows, n_pars, n_or, n_oa = len(rows), len(pars), len(out_rows), len(out_accs)

    in_specs, ops = [], []
    for r in rows:
        if isinstance(r, tuple) and r[1] == 'lead':
            arr, _, kk = r
            in_specs.append(pl.BlockSpec((None, tm, arr.shape[2]), lambda i, kk=kk: (kk, i, 0)))
        elif isinstance(r, tuple):
            arr, c0, w = r
            assert c0 % w == 0
            in_specs.append(pl.BlockSpec((tm, w), lambda i, cb=c0 // w: (i, cb)))
        else:
            arr = r
            in_specs.append(pl.BlockSpec((tm, arr.shape[1]), lambda i: (i, 0)))
        ops.append(arr)
    for p in pars:
        in_specs.append(pl.BlockSpec(p.shape, lambda i, nd=p.ndim: (0,) * nd))
        ops.append(p)
    out_specs = [pl.BlockSpec((tm, w), lambda i: (i, 0)) for (w, _) in out_rows]
    out_specs += [pl.BlockSpec(s, lambda i, nd=len(s): (0,) * nd) for s in out_accs]
    out_shape = [jax.ShapeDtypeStruct((R, w), dt) for (w, dt) in out_rows]
    out_shape += [jax.ShapeDtypeStruct(s, f32) for s in out_accs]

    def body(*refs):
        ins = [r[...] for r in refs[:n_rows + n_pars]]
        o_refs = refs[n_rows + n_pars:]
        res = fn(*ins)
        if not isinstance(res, (tuple, list)):
            res = (res,)
        assert len(res) == n_or + n_oa, (len(res), n_or, n_oa, name)
        for o, v in zip(o_refs[:n_or], res[:n_or]):
            o[...] = v.astype(o.dtype)
        if n_oa:
            i = pl.program_id(0)

            @pl.when(i == 0)
            def _():
                for o, v in zip(o_refs[n_or:], res[n_or:]):
                    o[...] = v.astype(f32)

            @pl.when(i > 0)
            def _():
                for o, v in zip(o_refs[n_or:], res[n_or:]):
                    o[...] += v.astype(f32)

    return pl.pallas_call(
        body, grid=(R // tm,), in_specs=in_specs, out_specs=out_specs, out_shape=out_shape,
        compiler_params=_params(("arbitrary",)), name=name)(*ops)


def whole(fn, ins, out_shapes, name):
    def body(*refs):
        res = fn(*[r[...] for r in refs[:len(ins)]])
        for o, v in zip(refs[len(ins):], res):
            o[...] = v

    return pl.pallas_call(body, out_shape=[jax.ShapeDtypeStruct(s, f32) for s in out_shapes], name=name)(*ins)


@jax.custom_vjp
def _bdot(a, b):
    return jnp.dot(a.astype(bf16), b.astype(bf16), preferred_element_type=f32)


def _bdot_fwd(a, b):
    return _bdot(a, b), (a, b)


def _bdot_bwd(res, g):
    a, b = res
    gb = g.astype(bf16)
    da = lax.dot_general(gb, b.astype(bf16), (((1,), (1,)), ((), ())), preferred_element_type=f32)
    db = lax.dot_general(a.astype(bf16), gb, (((0,), (0,)), ((), ())), preferred_element_type=f32)
    return da, db


_bdot.defvjp(_bdot_fwd, _bdot_bwd)


def _ln(z, g, b):
    mu = jnp.mean(z, axis=-1, keepdims=True)
    var = jnp.mean(jnp.square(z - mu), axis=-1, keepdims=True)
    return (z - mu) * lax.rsqrt(var + LN_EPS) * g + b


def _rms_only(x):
    return x * lax.rsqrt(jnp.mean(jnp.square(x), axis=-1, keepdims=True) + RMS_EPS)


def _swiglu(a, b):
    return jax.nn.silu(a) * b


def _gmlp(hu, hv, ng, ws, bsb):
    u = jax.nn.gelu(hu)
    v = jax.nn.gelu(hv)
    lane = lax.broadcasted_iota(jnp.int32, (1, GM_WIDTH), 1)
    masks = [((lane >= GM_HEAD_DIM * h) & (lane < GM_HEAD_DIM * (h + 1))).astype(f32) for h in range(GM_HEADS)]
    mu = jnp.zeros_like(v)
    for m in masks:
        mu = mu + m * (jnp.sum(v * m, axis=-1, keepdims=True) / GM_HEAD_DIM)
    d = v - mu
    var = jnp.zeros_like(v)
    for m in masks:
        var = var + m * (jnp.sum(d * d * m, axis=-1, keepdims=True) / GM_HEAD_DIM)
    vn = d * lax.rsqrt(var + LN_EPS) * ng
    r = lax.broadcasted_iota(jnp.int32, (GM_CHUNK, GM_CHUNK), 0)
    c = lax.broadcasted_iota(jnp.int32, (GM_CHUNK, GM_CHUNK), 1)
    tril = (c <= r).astype(f32)
    z = bsb
    for h, m in enumerate(masks):
        z = z + _bdot(ws[h] * tril, vn * m)
    return u * z


def _mla_prep(cq, ckv, qg, kvg):
    return _rms_only(cq) * qg, _rms_only(ckv) * kvg


def _rope(q1, q2, k1, k2, cos, sin):
    return q1 * cos - q2 * sin, q2 * cos + q1 * sin, k1 * cos - k2 * sin, k2 * cos + k1 * sin


def _mix_post(ya, ob, y1, us, dskip, gluw, glub, gmix):
    y = jax.nn.gelu(y1 + dskip * us)
    yc = y * jax.nn.sigmoid(_bdot(y, gluw) + glub)
    return jnp.concatenate([_rms_only(ya), _rms_only(ob), _rms_only(yc)], axis=1) * gmix


def _ssm_prep(ar, ai, ldt, brT, biT):
    dt = jnp.exp(ldt)
    mag = jnp.exp(ar * dt)
    abr = mag * jnp.cos(ai * dt)
    abi = mag * jnp.sin(ai * dt)
    den = ar * ar + ai * ai
    cr = ((abr - 1.0) * ar + abi * ai) / den
    ci = (abi * ar - (abr - 1.0) * ai) / den
    return abr, abi, cr * brT - ci * biT, cr * biT + ci * brT


def _att_tile(S):
    return _pick(S, (512, 256, 128))


def _nt(a, b):
    return lax.dot_general(a, b, (((1,), (1,)), ((), ())), preferred_element_type=f32)


def _diag_keep(T):
    krow = lax.broadcasted_iota(jnp.int32, (T, T), 0)
    qcol = lax.broadcasted_iota(jnp.int32, (T, T), 1)
    return qcol >= krow


def flash_fwd(q, k, vT):
    Hh, S, _ = q.shape
    T = _att_tile(S)

    def body(q_ref, k_ref, vT_ref, o_ref, lse_ref, m_sc, l_sc, acc_sc):
        i = pl.program_id(1)
        qi = q_ref[0]
        m_sc[...] = jnp.full_like(m_sc, NEG_BIG)
        l_sc[...] = jnp.zeros_like(l_sc)
        acc_sc[...] = jnp.zeros_like(acc_sc)

        def block(j, diagonal):
            rows = pl.ds(pl.multiple_of(j * T, T), T)
            sT = _nt(k_ref[0, rows, :], qi) * ATT_SCALE
            if diagonal:
                sT = jnp.where(_diag_keep(T), sT, NEG_BIG)
            m_old = m_sc[...]
            m_new = jnp.maximum(m_old, jnp.max(sT, axis=0, keepdims=True))
            alpha = jnp.exp(m_old - m_new)
            pT = jnp.exp(sT - m_new)
            l_sc[...] = alpha * l_sc[...] + jnp.sum(pT, axis=0, keepdims=True)
            acc_sc[...] = alpha * acc_sc[...] + jnp.dot(vT_ref[0, :, rows], pT.astype(bf16), preferred_element_type=f32)
            m_sc[...] = m_new

        def loop_body(j, c):
            block(j, False)
            return c

        lax.fori_loop(0, i, loop_body, 0)
        block(i, True)
        o_ref[0] = acc_sc[...] / l_sc[...]
        lse_ref[0] = m_sc[...] + jnp.log(l_sc[...])

    return pl.pallas_call(
        body, grid=(Hh, S // T),
        in_specs=[pl.BlockSpec((1, T, LANES), lambda h, i: (h, i, 0)), pl.BlockSpec((1, S, LANES), lambda h, i: (h, 0, 0)),
                  pl.BlockSpec((1, MLA_V, S), lambda h, i: (h, 0, 0))],
        out_specs=[pl.BlockSpec((1, MLA_V, T), lambda h, i: (h, 0, i)), pl.BlockSpec((1, 1, T), lambda h, i: (h, 0, i))],
        out_shape=[jax.ShapeDtypeStruct((Hh, MLA_V, S), f32), jax.ShapeDtypeStruct((Hh, 1, S), f32)],
        scratch_shapes=[pltpu.VMEM((1, T), f32), pltpu.VMEM((1, T), f32), pltpu.VMEM((MLA_V, T), f32)],
        compiler_params=_params(("parallel", "arbitrary")), name="flash_fwd")(q, k, vT)


def flash_bwd(q, k, kT, v, do, lse, delta):
    Hh, S, _ = q.shape
    T = _att_tile(S)

    def body(q_ref, do_ref, lse_ref, dl_ref, k_ref, kT_ref, v_ref, dq_ref, dk_ref, dv_ref, dq_sc):
        i = pl.program_id(1)

        @pl.when(i == 0)
        def _():
            dk_ref[...] = jnp.zeros_like(dk_ref)
            dv_ref[...] = jnp.zeros_like(dv_ref)

        qi, doi = q_ref[0], do_ref[0]
        lse_i, dl_i = lse_ref[0], dl_ref[0]
        dq_sc[...] = jnp.zeros_like(dq_sc)

        def block(j, diagonal):
            rows = pl.ds(pl.multiple_of(j * T, T), T)
            sT = _nt(k_ref[0, rows, :], qi) * ATT_SCALE
            pT = jnp.exp(sT - lse_i)
            if diagonal:
                pT = jnp.where(_diag_keep(T), pT, 0.0)
            dpT = _nt(v_ref[0, rows, :], doi)
            dsT = (pT * (dpT - dl_i) * ATT_SCALE).astype(bf16)
            dv_ref[0, rows, :] += jnp.dot(pT.astype(bf16), doi, preferred_element_type=f32)
            dk_ref[0, rows, :] += jnp.dot(dsT, qi, preferred_element_type=f32)
            dq_sc[...] += jnp.dot(kT_ref[0, :, rows], dsT, preferred_element_type=f32)

        def loop_body(j, c):
            block(j, False)
            return c

        lax.fori_loop(0, i, loop_body, 0)
        block(i, True)
        dq_ref[0] = dq_sc[...]

    tile = lambda w: pl.BlockSpec((1, T, w), lambda h, i: (h, i, 0))
    row = pl.BlockSpec((1, 1, T), lambda h, i: (h, 0, i))
    full = lambda w: pl.BlockSpec((1, S, w), lambda h, i: (h, 0, 0))
    return pl.pallas_call(
        body, grid=(Hh, S // T),
        in_specs=[tile(LANES), tile(MLA_V), row, row, full(LANES), pl.BlockSpec((1, LANES, S), lambda h, i: (h, 0, 0)), full(MLA_V)],
        out_specs=[pl.BlockSpec((1, LANES, T), lambda h, i: (h, 0, i)), full(LANES), full(MLA_V)],
        out_shape=[jax.ShapeDtypeStruct((Hh, LANES, S), f32), jax.ShapeDtypeStruct((Hh, S, LANES), f32),
                   jax.ShapeDtypeStruct((Hh, S, MLA_V), f32)],
        scratch_shapes=[pltpu.VMEM((LANES, T), f32)],
        compiler_params=_params(("parallel", "arbitrary")), name="flash_bwd")(q, do, lse, delta, k, kT, v)


def _scan_tile(S):
    return _pick(S, (256, 128, 64, 32, 16, 8))


def scan_fwd(bu3, a16):
    S = bu3.shape[0]
    ts = _scan_tile(S)

    def body(bu_ref, a_ref, o_ref, h_sc):
        @pl.when(pl.program_id(0) == 0)
        def _():
            h_sc[...] = jnp.zeros_like(h_sc)

        ar, ai = a_ref[0:8, :], a_ref[8:16, :]

        def step(t, carry):
            hr, hi = carry
            nr = ar * hr - ai * hi + bu_ref[t, 0:8, :]
            ni = ar * hi + ai * hr + bu_ref[t, 8:16, :]
            o_ref[t, 0:8, :] = nr
            o_ref[t, 8:16, :] = ni
            return nr, ni

        hr, hi = lax.fori_loop(0, ts, step, (h_sc[0:8, :], h_sc[8:16, :]), unroll=8)
        h_sc[0:8, :] = hr
        h_sc[8:16, :] = hi

    blk = pl.BlockSpec((ts, 16, LANES), lambda i: (i, 0, 0))
    return pl.pallas_call(
        body, grid=(S // ts,), in_specs=[blk, pl.BlockSpec((16, LANES), lambda i: (0, 0))], out_specs=blk,
        out_shape=jax.ShapeDtypeStruct(bu3.shape, f32), scratch_shapes=[pltpu.VMEM((16, LANES), f32)],
        compiler_params=_params(("arbitrary",)), name="scan_fwd")(bu3, a16)


def scan_bwd(g3, h3, a16):
    S = g3.shape[0]
    ts = _scan_tile(S)
    nb = S // ts

    def body(g_ref, h_ref, a_ref, o_ref, da_ref, lam_sc, da_sc):
        @pl.when(pl.program_id(0) == 0)
        def _():
            lam_sc[...] = jnp.zeros_like(lam_sc)
            da_sc[...] = jnp.zeros_like(da_sc)

        ar, ai = a_ref[0:8, :], a_ref[8:16, :]

        def step(kk, carry):
            lr, li, dar, dai = carry
            t = ts - 1 - kk
            hr, hi = h_ref[t, 0:8, :], h_ref[t, 8:16, :]
            dar = dar + lr * hr + li * hi
            dai = dai + li * hr - lr * hi
            nlr = ar * lr + ai * li + g_ref[t, 0:8, :]
            nli = ar * li - ai * lr + g_ref[t, 8:16, :]
            o_ref[t, 0:8, :] = nlr
            o_ref[t, 8:16, :] = nli
            return nlr, nli, dar, dai

        lr, li, dar, dai = lax.fori_loop(
            0, ts, step, (lam_sc[0:8, :], lam_sc[8:16, :], da_sc[0:8, :], da_sc[8:16, :]), unroll=8)
        lam_sc[0:8, :] = lr
        lam_sc[8:16, :] = li
        da_sc[0:8, :] = dar
        da_sc[8:16, :] = dai
        da_ref[0:8, :] = dar
        da_ref[8:16, :] = dai

    blk = pl.BlockSpec((ts, 16, LANES), lambda i: (nb - 1 - i, 0, 0))
    small = pl.BlockSpec((16, LANES), lambda i: (0, 0))
    return pl.pallas_call(
        body, grid=(nb,), in_specs=[blk, blk, small], out_specs=[blk, small],
        out_shape=[jax.ShapeDtypeStruct(g3.shape, f32), jax.ShapeDtypeStruct((16, LANES), f32)],
        scratch_shapes=[pltpu.VMEM((16, LANES), f32), pltpu.VMEM((16, LANES), f32)],
        compiler_params=_params(("arbitrary",)), name="scan_bwd")(g3, h3, a16)


HBM_SPEC = pl.BlockSpec(memory_space=pltpu.HBM)


def _my_id():
    return 4 * lax.axis_index("x") + 2 * lax.axis_index("y") + lax.axis_index("c")


def all_gather(xs, name):
    n = len(xs)

    def body(*refs):
        x_refs, o_refs = refs[:n], refs[n:2 * n]
        send_sems, recv_sems, local_sems = refs[2 * n:]
        x, y, c = lax.axis_index("x"), lax.axis_index("y"), lax.axis_index("c")
        me, sibling = (x, y, c), (x, y, 1 - c)
        chips = [(1 - x, y), (x, 1 - y), (1 - x, 1 - y)]

        def slot(o, p):
            return o.at[4 * p[0] + 2 * p[1] + p[2]]

        def copy(a, k, block, to, src=None):
            o = o_refs[a]
            return pltpu.make_async_remote_copy(
                src_ref=slot(o, block) if src is None else src, dst_ref=slot(o, block),
                send_sem=send_sems.at[7 * a + k], recv_sem=recv_sems.at[7 * a + k], device_id=to, device_id_type=MESH)

        own, sends = [], []
        for a in range(n):
            mine = pltpu.make_async_copy(x_refs[a], slot(o_refs[a], me), local_sems.at[a])
            mine.start()
            own.append(mine)
            first = [copy(a, 0, me, sibling, src=x_refs[a])]
            first += [copy(a, 1 + j, me, (*chip, c), src=x_refs[a]) for j, chip in enumerate(chips)]
            for cp in first:
                cp.start()
            sends += first
        for a in range(n):
            for j, chip in enumerate(chips):
                copy(a, 1 + j, (*chip, c), me).wait_recv()
                fwd = copy(a, 4 + j, (*chip, c), sibling)
                fwd.start()
                sends.append(fwd)
        for a in range(n):
            copy(a, 0, sibling, me).wait_recv()
            for j, chip in enumerate(chips):
                copy(a, 4 + j, (*chip, 1 - c), me).wait_recv()
        for cp in sends:
            cp.wait_send()
        for cp in own:
            cp.wait()

    return pl.pallas_call(
        body, out_shape=[jax.ShapeDtypeStruct((N_DEV,) + v.shape, v.dtype) for v in xs],
        in_specs=[HBM_SPEC] * n, out_specs=[HBM_SPEC] * n,
        scratch_shapes=[pltpu.SemaphoreType.DMA((7 * n,)), pltpu.SemaphoreType.DMA((7 * n,)), pltpu.SemaphoreType.DMA((n,))],
        name=name)(*xs)


def all_to_all(xs, name):
    n = len(xs)

    def body(*refs):
        x_refs, o_refs = refs[:n], refs[n:2 * n]
        send_sems, recv_sems, local_sems = refs[2 * n:]
        x, y, c = lax.axis_index("x"), lax.axis_index("y"), lax.axis_index("c")
        my = 4 * x + 2 * y + c
        waits = []
        for a in range(n):
            mine = pltpu.make_async_copy(x_refs[a].at[my], o_refs[a].at[my], local_sems.at[a])
            mine.start()
            for k in range(1, N_DEV):
                px = 1 - x if k & 4 else x
                py = 1 - y if k & 2 else y
                pc = 1 - c if k & 1 else c
                pid = 4 * px + 2 * py + pc
                out = pltpu.make_async_remote_copy(
                    src_ref=x_refs[a].at[pid], dst_ref=o_refs[a].at[my],
                    send_sem=send_sems.at[7 * a + k - 1], recv_sem=recv_sems.at[7 * a + k - 1],
                    device_id=(px, py, pc), device_id_type=MESH)
                out.start()
                back = pltpu.make_async_remote_copy(
                    src_ref=x_refs[a].at[pid], dst_ref=o_refs[a].at[pid],
                    send_sem=send_sems.at[7 * a + k - 1], recv_sem=recv_sems.at[7 * a + k - 1],
                    device_id=(px, py, pc), device_id_type=MESH)
                waits.append((out, back))
            waits.append((mine, None))
        for out, back in waits:
            if back is None:
                out.wait()
            else:
                out.wait_send()
                back.wait_recv()

    return pl.pallas_call(
        body, out_shape=[jax.ShapeDtypeStruct(v.shape, v.dtype) for v in xs],
        in_specs=[HBM_SPEC] * n, out_specs=[HBM_SPEC] * n,
        scratch_shapes=[pltpu.SemaphoreType.DMA((7 * n,)), pltpu.SemaphoreType.DMA((7 * n,)), pltpu.SemaphoreType.DMA((n,))],
        name=name)(*xs)


def sum_slots(g8, name):
    R, C = g8.shape[1:]

    def fn(*tiles):
        tot = tiles[0].astype(f32)
        for t in tiles[1:]:
            tot = tot + t.astype(f32)
        return tot

    return rowwise(fn, [(g8, 'lead', k) for k in range(N_DEV)], [], [(C, f32)], [], _pick(R, (256, 128, 64, 32, 16, 8)), name)[0]


PACK_W = 1024


def _pad_rows(flat, mult):
    n = flat.shape[0]
    tot = -(-n // (PACK_W * mult)) * PACK_W * mult
    return jnp.pad(flat, (0, tot - n)).reshape(tot // PACK_W, PACK_W)


def _shard_shape(name):
    r, c = BIG_SHAPE[name]
    return (r // N_DEV, c) if BIG[name] == 0 else (r, c // N_DEV)


def _pack_rows(name):
    r, c = _shard_shape(name)
    assert (r * c) % PACK_W == 0
    return r * c // PACK_W


def _split_for_devices(name, full):
    r, c = BIG_SHAPE[name]
    if BIG[name] == 0:
        return full.reshape(N_DEV, _pack_rows(name), PACK_W)
    return full.reshape(r, N_DEV, c // N_DEV).transpose(1, 0, 2).reshape(N_DEV, _pack_rows(name), PACK_W)


def _join_from_devices(name, parts):
    r, c = BIG_SHAPE[name]
    if BIG[name] == 0:
        return parts.reshape(r, c)
    return parts.reshape(N_DEV, r, c // N_DEV).transpose(1, 0, 2).reshape(r, c)


def _row_tile(S, want):
    return _pick(S, tuple(t for t in (512, 256, 128, 64, 32, 16) if t <= want))


def resid_ln(x, f, g, b, scale, name):
    D = x.shape[1]

    def fn(x, f, g, b):
        z = ALPHA * x + scale * f
        xo = _ln(z, g, b)
        return z, xo, xo

    return rowwise(fn, [x, f], [g, b], [(D, f32), (D, f32), (D, bf16)], [], _row_tile(x.shape[0], 512), name)


def ln_bwd(z, g, b, dxo, scale, name):
    D = z.shape[1]

    def fn(z, dxo, g, b):
        _, vjp = jax.vjp(_ln, z, g, b)
        dz, dg, db = vjp(dxo)
        return dz, scale * dz, dg, db

    return rowwise(fn, [z, dxo], [g, b], [(D, f32), (D, bf16)], [(1, D), (1, D)], _row_tile(z.shape[0], 256), name)


FF_TILE = 256


def _interleave_gate_up_rows(wgT, wuT):
    d = wgT.shape[1]
    t = jnp.stack([wgT.reshape(D_FF // FF_TILE, FF_TILE, d), wuT.reshape(D_FF // FF_TILE, FF_TILE, d)], axis=1)
    return t.reshape(2 * D_FF, d)


def _split_gate_up(w):
    t = w.reshape(w.shape[0], D_FF // FF_TILE, 2, FF_TILE)
    return t[:, :, 0, :].reshape(w.shape[0], D_FF), t[:, :, 1, :].reshape(w.shape[0], D_FF)


def ffn_up_act(xb, wg, wu):
    M, K = xb.shape
    tm = _pick(M, (1024, 512, 256, 128, 64, 32, 16))

    def body(x_ref, wg_ref, wu_ref, ab_ref, h_ref):
        x = x_ref[...]
        a = jnp.dot(x, wg_ref[...], preferred_element_type=f32)
        b = jnp.dot(x, wu_ref[...], preferred_element_type=f32)
        ab_ref[...] = jnp.concatenate([a, b], axis=1)
        h_ref[...] = _swiglu(a, b).astype(bf16)

    wspec = pl.BlockSpec((K, FF_TILE), lambda i, j: (0, j))
    return pl.pallas_call(
        body, grid=(M // tm, D_FF // FF_TILE),
        in_specs=[pl.BlockSpec((tm, K), lambda i, j: (i, 0)), wspec, wspec],
        out_specs=[pl.BlockSpec((tm, 2 * FF_TILE), lambda i, j: (i, j)), pl.BlockSpec((tm, FF_TILE), lambda i, j: (i, j))],
        out_shape=[jax.ShapeDtypeStruct((M, 2 * D_FF), f32), jax.ShapeDtypeStruct((M, D_FF), bf16)],
        compiler_params=_params(("parallel", "parallel")), name="ffn_up_act")(xb, wg, wu)


def ffn_down_dx_act(dzs, wdT, ab):
    M, K = dzs.shape
    tm = _pick(M, (1024, 512, 256, 128, 64, 32, 16))

    def body(dz_ref, w_ref, ab_ref, dab_ref, h_ref):
        dh = jnp.dot(dz_ref[...], w_ref[...], preferred_element_type=f32)
        ab = ab_ref[...]
        h, vjp = jax.vjp(_swiglu, ab[:, :FF_TILE], ab[:, FF_TILE:])
        da, db = vjp(dh)
        dab_ref[...] = jnp.concatenate([da, db], axis=1).astype(bf16)
        h_ref[...] = h.astype(bf16)

    pair = pl.BlockSpec((tm, 2 * FF_TILE), lambda i, j: (i, j))
    return pl.pallas_call(
        body, grid=(M // tm, D_FF // FF_TILE),
        in_specs=[pl.BlockSpec((tm, K), lambda i, j: (i, 0)), pl.BlockSpec((K, FF_TILE), lambda i, j: (0, j)), pair],
        out_specs=[pair, pl.BlockSpec((tm, FF_TILE), lambda i, j: (i, j))],
        out_shape=[jax.ShapeDtypeStruct((M, 2 * D_FF), bf16), jax.ShapeDtypeStruct((M, D_FF), bf16)],
        compiler_params=_params(("parallel", "parallel")), name="ffn_down_dx_act")(dzs, wdT, ab)


def ffn_fwd(x, xb, w, g, b):
    ab, h = ffn_up_act(xb, w['g'], w['u'])
    f = mm(h, w['d'], name="ffn_down")
    z, xo, xob = resid_ln(x, f, g, b, 0.5, "ffn_ln")
    return xo, xob, (xb, ab, z)


def ffn_bwd(dxo, res, w, g, b):
    xb, ab, z = res
    dz, dzs, dg, db = ln_bwd(z, g, b, dxo, 0.5, "ffn_ln_bwd")
    dab, h = ffn_down_dx_act(dzs, w['dT'], ab)
    dwd = mm_tn(h, dzs, name="ffn_down_dw")
    dwg, dwu = _split_gate_up(mm_tn(xb, dab, name="ffn_up_dw"))
    dx = mm(dab, w['guT'], add=dz, add_scale=ALPHA, name="ffn_up_dx")
    return dx, dwg, dwu, dwd, dg, db


def _heads_first(a, width):
    return a.reshape(a.shape[0], MLA_HEADS, width)


def mixer_fwd(x, xb, w, g, b, cos8, sin8):
    S = x.shape[0]
    H = mm(xb, w['in'], name="mix_in")
    ya = rowwise(_gmlp, [(H, H_UG, 256), (H, H_VG, 256)], [w['gm_ng'], w['gm_ws'], w['gm_bsb']], [(GM_WIDTH, f32)], [],
                 GM_CHUNK, "gmlp")[0]
    cqn, ckvn = rowwise(_mla_prep, [(H, H_CQ, Q_LORA), (H, H_CKV, KV_LORA)], [w['qg'], w['kvg']],
                        [(Q_LORA, bf16), (KV_LORA, bf16)], [], _row_tile(S, 512), "mla_prep")
    qraw = mm(cqn, w['uq'], name="mla_uq")
    kv = mm(ckvn, w['ukv'], name="mla_ukv")
    q1, q2, k1, k2 = rowwise(_rope, [(qraw, 512, LANES), (qraw, 640, LANES), (H, H_K1, LANES), (H, H_K2, LANES), cos8, sin8],
                             [], [(LANES, f32)] * 4, [], _row_tile(S, 512), "rope")
    zpad = jnp.zeros((S, MLA_HEADS, LANES - MLA_NOPE - MLA_ROPE), f32)
    qp = jnp.concatenate([_heads_first(qraw[:, :512], 64), _heads_first(q1, ROPE_HALF), _heads_first(q2, ROPE_HALF), zpad], axis=2)
    k1b = jnp.broadcast_to(k1[:, None, :ROPE_HALF], (S, MLA_HEADS, ROPE_HALF))
    k2b = jnp.broadcast_to(k2[:, None, :ROPE_HALF], (S, MLA_HEADS, ROPE_HALF))
    kp = jnp.concatenate([_heads_first(kv[:, :512], 64), k1b, k2b, zpad], axis=2)
    qp = qp.transpose(1, 0, 2).astype(bf16)
    kp = kp.transpose(1, 0, 2).astype(bf16)
    v3 = _heads_first(kv[:, 512:], 64).astype(bf16)
    vp = v3.transpose(1, 0, 2)
    oT, lse = flash_fwd(qp, kp, v3.transpose(1, 2, 0))
    ob = oT.transpose(2, 0, 1).reshape(S, MLA_HEADS * MLA_V)
    bu = mm(H, w['ssm_wb'], a_col0=H_US, name="ssm_bu")
    hs3 = scan_fwd(bu.reshape(S, 16, LANES), w['ssm_a16'])
    hs = hs3.reshape(S, 2 * N_STATE)
    y1 = mm(hs, w['ssm_wc'], name="ssm_c")
    y = rowwise(_mix_post, [ya, ob, y1, (H, H_US, SSM_WIDTH)], [w['ssm_d'], w['glu_w'], w['glu_b'], w['gmix']],
                [(D_MODEL, bf16)], [], _row_tile(S, 256), "mix_post")[0]
    f = mm(y, w['out'], name="mix_out")
    z, xo, xob = resid_ln(x, f, g, b, 1.0, "mix_ln")
    return xo, xob, (xb, H, cqn, ckvn, qp, kp, vp, lse, hs3, ya, ob, y1, y, z)


def mixer_bwd(dxo, res, w, g, b, cos8, sin8):
    xb, H, cqn, ckvn, qp, kp, vp, lse, hs3, ya, ob, y1, y, z = res
    S = z.shape[0]
    gr = {}
    dz, dzs, gr['ln_g'], gr['ln_b'] = ln_bwd(z, g, b, dxo, 1.0, "mix_ln_bwd")
    gr['w_out'] = mm_tn(y, dzs, name="mix_out_dw")
    dy = mm(dzs, w['outT'], name="mix_out_dx")

    def post_bwd(ya, ob, y1, us, dy, dskip, gluw, glub, gmix):
        _, vjp = jax.vjp(_mix_post, ya, ob, y1, us, dskip, gluw, glub, gmix)
        dya, dob, dy1, dus, *dpars = vjp(dy)
        prod = dob * ob
        col = lax.broadcasted_iota(jnp.int32, (1, MLA_HEADS * MLA_V), 1)
        lane = lax.broadcasted_iota(jnp.int32, (1, LANES), 1)
        delta = jnp.zeros((prod.shape[0], LANES), f32)
        for h in range(MLA_HEADS):
            in_head = ((col >= MLA_V * h) & (col < MLA_V * (h + 1))).astype(f32)
            delta = jnp.where(lane == h, jnp.sum(prod * in_head, axis=-1, keepdims=True), delta)
        return (dya, dob, dy1, dus, delta, *dpars)

    dya, dob, dy1, dus_skip, delta, gr['ssm_d'], gr['ssm_glu_w'], gr['ssm_glu_b'], gr['mix_norm_g'] = rowwise(
        post_bwd, [ya, ob, y1, (H, H_US, SSM_WIDTH), dy], [w['ssm_d'], w['glu_w'], w['glu_b'], w['gmix']],
        [(GM_WIDTH, f32), (MLA_HEADS * MLA_V, f32), (SSM_WIDTH, f32), (SSM_WIDTH, f32), (LANES, f32)],
        [(1, SSM_WIDTH), (SSM_WIDTH, SSM_WIDTH), (1, SSM_WIDTH), (1, D_MODEL)], _row_tile(S, 128), "mix_post_bwd")

    hs = hs3.reshape(S, 2 * N_STATE)
    gr['ssm_wc'] = mm_tn(hs, dy1, name="ssm_c_dw")
    dhs = mm(dy1, w['ssm_wcT'], name="ssm_c_dx")
    dbu3, gr['ssm_a16'] = scan_bwd(dhs.reshape(S, 16, LANES), hs3, w['ssm_a16'])
    dbu = dbu3.reshape(S, 2 * N_STATE)
    gr['ssm_wb'] = mm_tn(H, dbu, a_col0=H_US, m_dim=SSM_WIDTH, name="ssm_bu_dw")
    dus = mm(dbu, w['ssm_wbT'], add=dus_skip, add_scale=1.0, name="ssm_bu_dx")

    do = _heads_first(dob, MLA_V).transpose(1, 0, 2)
    delta = delta[:, :MLA_HEADS].T.reshape(MLA_HEADS, 1, S)
    dqT, dkp, dvp = flash_bwd(qp, kp, kp.transpose(0, 2, 1), vp, do.astype(bf16), lse, delta)
    dqp = dqT.transpose(2, 0, 1)
    dkp = dkp.transpose(1, 0, 2)
    dv = dvp.transpose(1, 0, 2).reshape(S, MLA_HEADS * MLA_V)
    lane_pad = ((0, 0), (0, LANES - ROPE_HALF))
    dq1r = dqp[:, :, 64:80].reshape(S, LANES)
    dq2r = dqp[:, :, 80:96].reshape(S, LANES)
    dk1r = jnp.pad(jnp.sum(dkp[:, :, 64:80], axis=1), lane_pad)
    dk2r = jnp.pad(jnp.sum(dkp[:, :, 80:96], axis=1), lane_pad)

    def rope_bwd(d1, d2, d3, d4, cos, sin):
        return d1 * cos + d2 * sin, d2 * cos - d1 * sin, d3 * cos + d4 * sin, d4 * cos - d3 * sin

    dq1, dq2, dk1, dk2 = rowwise(rope_bwd, [dq1r, dq2r, dk1r, dk2r, cos8, sin8], [], [(LANES, f32)] * 4, [],
                                 _row_tile(S, 512), "rope_bwd")
    dqraw = jnp.concatenate([dqp[:, :, :64].reshape(S, 512), dq1, dq2], axis=1).astype(bf16)
    dkv = jnp.concatenate([dkp[:, :, :64].reshape(S, 512), dv], axis=1).astype(bf16)
    gr['uq'] = mm_tn(cqn, dqraw, name="mla_uq_dw")
    dcqn = mm(dqraw, w['uqT'], name="mla_uq_dx")
    gr['ukv'] = mm_tn(ckvn, dkv, name="mla_ukv_dw")
    dckvn = mm(dkv, w['ukvT'], name="mla_ukv_dx")

    def prep_bwd(cq, ckv, d1, d2, qg, kvg):
        _, vjp = jax.vjp(_mla_prep, cq, ckv, qg, kvg)
        return vjp((d1, d2))

    dcq, dckv, gr['mla_q_norm_g'], gr['mla_kv_norm_g'] = rowwise(
        prep_bwd, [(H, H_CQ, Q_LORA), (H, H_CKV, KV_LORA), dcqn, dckvn], [w['qg'], w['kvg']],
        [(Q_LORA, f32), (KV_LORA, f32)], [(1, Q_LORA), (1, KV_LORA)], _row_tile(S, 256), "mla_prep_bwd")

    def gmlp_bwd(hu, hv, dya, ng, ws, bsb):
        _, vjp = jax.vjp(_gmlp, hu, hv, ng, ws, bsb)
        return vjp(dya)

    dhu, dhv, gr['gmlp_norm_g'], gr['gmlp_ws'], gr['gm_bsb'] = rowwise(
        gmlp_bwd, [(H, H_UG, 256), (H, H_VG, 256), dya], [w['gm_ng'], w['gm_ws'], w['gm_bsb']],
        [(GM_WIDTH, f32), (GM_WIDTH, f32)], [(1, GM_WIDTH), (GM_HEADS, GM_CHUNK, GM_CHUNK), (GM_CHUNK, GM_WIDTH)],
        GM_CHUNK, "gmlp_bwd")

    dH = jnp.concatenate([dhu, dhv, dcq, dus, dckv, dk1, dk2], axis=1).astype(bf16)
    gr['in'] = mm_tn(xb, dH, name="mix_in_dw")
    dx = mm(dH, w['inT'], add=dz, add_scale=ALPHA, name="mix_in_dx")
    return dx, gr


def _block_diag(blocks):
    G, a, b = blocks.shape
    eye = jnp.eye(G, dtype=blocks.dtype)
    return (eye[:, None, :, None] * blocks[:, :, None, :]).reshape(G * a, G * b)


def _diag_blocks(mat, G):
    a, b = mat.shape[0] // G, mat.shape[1] // G
    m4 = mat.reshape(G, a, G, b)
    eye = jnp.eye(G, dtype=mat.dtype)
    return jnp.sum(m4 * eye[:, None, :, None], axis=2)


def prep_layer(W, rep, l):
    w = {}
    for f in ('ffn1', 'ffn2'):
        wg, wu = W[f + '_w_gate'], W[f + '_w_up']
        w[f] = {'g': wg, 'u': wu, 'guT': _interleave_gate_up_rows(wg.T, wu.T), 'd': W[f + '_w_down'], 'dT': W[f + '_w_down'].T}
    wi = W['w_in']
    z112 = jnp.zeros((D_MODEL, LANES - ROPE_HALF), wi.dtype)
    w['in'] = jnp.concatenate([wi[:, :768], wi[:, 928:1184], wi[:, 768:896], wi[:, 896:912], z112, wi[:, 912:928], z112], axis=1)
    w['inT'] = w['in'].T
    uq = W['mla_w_uq'].reshape(Q_LORA, MLA_HEADS, MLA_NOPE + MLA_ROPE)
    w['uq'] = jnp.concatenate([uq[:, :, :64].reshape(Q_LORA, 512), uq[:, :, 64:80].reshape(Q_LORA, LANES),
                               uq[:, :, 80:96].reshape(Q_LORA, LANES)], axis=1)
    w['uqT'] = w['uq'].T
    ukv = W['mla_w_ukv'].reshape(KV_LORA, MLA_HEADS, MLA_NOPE + MLA_V)
    w['ukv'] = jnp.concatenate([ukv[:, :, :64].reshape(KV_LORA, 512), ukv[:, :, 64:].reshape(KV_LORA, 512)], axis=1)
    w['ukvT'] = w['ukv'].T
    w['out'] = W['w_out']
    w['outT'] = W['w_out'].T
    w['glu_w'] = W['ssm_glu_w']
    w['gm_ng'] = rep['gmlp_norm_g'][l].reshape(1, GM_WIDTH)
    w['gm_ws'] = rep['gmlp_ws'][l]
    w['gm_bsb'] = jnp.repeat(rep['gmlp_bs'][l].T, GM_HEAD_DIM, axis=1)
    w['qg'] = rep['mla_q_norm_g'][l].reshape(1, Q_LORA)
    w['kvg'] = rep['mla_kv_norm_g'][l].reshape(1, KV_LORA)
    w['ssm_d'] = rep['ssm_d'][l].reshape(1, SSM_WIDTH)
    w['glu_b'] = rep['ssm_glu_b'][l].reshape(1, SSM_WIDTH)
    w['gmix'] = rep['mix_norm_g'][l].reshape(1, D_MODEL)
    ar = rep['ssm_a_re'][l].reshape(1, N_STATE)
    ai = rep['ssm_a_im'][l].reshape(1, N_STATE)
    ldt = jnp.repeat(rep['ssm_log_dt'][l], SSM_STATE).reshape(1, N_STATE)
    brT = rep['ssm_b_re'][l].transpose(2, 0, 1).reshape(SSM_GROUP_CH, N_STATE)
    biT = rep['ssm_b_im'][l].transpose(2, 0, 1).reshape(SSM_GROUP_CH, N_STATE)
    w['ssm_prep_in'] = (ar, ai, ldt, brT, biT)
    abr, abi, bbrT, bbiT = whole(_ssm_prep, w['ssm_prep_in'], [(1, N_STATE)] * 2 + [(SSM_GROUP_CH, N_STATE)] * 2, "ssm_prep")
    w['ssm_a16'] = jnp.concatenate([abr.reshape(8, LANES), abi.reshape(8, LANES)], axis=0)

    def to_gcp(t):
        return t.reshape(SSM_GROUP_CH, SSM_GROUPS, SSM_STATE).transpose(1, 0, 2)

    w['ssm_wb'] = jnp.concatenate([_block_diag(to_gcp(bbrT)), _block_diag(to_gcp(bbiT))], axis=1).astype(bf16)
    w['ssm_wbT'] = w['ssm_wb'].T
    cre = rep['ssm_c_re'][l].transpose(0, 2, 1)
    cim = rep['ssm_c_im'][l].transpose(0, 2, 1)
    w['ssm_wc'] = jnp.concatenate([_block_diag(cre), -_block_diag(cim)], axis=0).astype(bf16)
    w['ssm_wcT'] = w['ssm_wc'].T
    return w


def unprep_grads(gr, w):
    out = {}
    for k in ('ln_g', 'ln_b', 'w_out', 'mla_q_norm_g', 'mla_kv_norm_g', 'ssm_glu_w', 'gmlp_ws'):
        out[k] = gr[k]
    out['gmlp_norm_g'] = gr['gmlp_norm_g'].reshape(GM_WIDTH)
    out['mla_q_norm_g'] = gr['mla_q_norm_g'].reshape(Q_LORA)
    out['mla_kv_norm_g'] = gr['mla_kv_norm_g'].reshape(KV_LORA)
    out['ssm_d'] = gr['ssm_d'].reshape(SSM_GROUPS, SSM_GROUP_CH)
    out['ssm_glu_b'] = gr['ssm_glu_b'].reshape(SSM_WIDTH)
    out['mix_norm_g'] = gr['mix_norm_g'].reshape(D_MODEL)
    out['gmlp_bs'] = gr['gm_bsb'].reshape(GM_CHUNK, GM_HEADS, GM_HEAD_DIM).sum(axis=-1).T
    d = gr['in']
    out['w_in'] = jnp.concatenate([d[:, :768], d[:, H_CKV:H_CKV + KV_LORA], d[:, H_K1:H_K1 + ROPE_HALF],
                                   d[:, H_K2:H_K2 + ROPE_HALF], d[:, H_US:H_US + SSM_WIDTH]], axis=1)
    d = gr['uq']
    out['mla_w_uq'] = jnp.concatenate([d[:, :512].reshape(Q_LORA, MLA_HEADS, 64), d[:, 512:640].reshape(Q_LORA, MLA_HEADS, ROPE_HALF),
                                       d[:, 640:768].reshape(Q_LORA, MLA_HEADS, ROPE_HALF)], axis=2).reshape(Q_LORA, 768)
    d = gr['ukv']
    out['mla_w_ukv'] = jnp.concatenate([d[:, :512].reshape(KV_LORA, MLA_HEADS, 64), d[:, 512:].reshape(KV_LORA, MLA_HEADS, 64)],
                                       axis=2).reshape(KV_LORA, 1024)
    dwc = gr['ssm_wc']
    out['ssm_c_re'] = _diag_blocks(dwc[:N_STATE], SSM_GROUPS).transpose(0, 2, 1)
    out['ssm_c_im'] = -_diag_blocks(dwc[N_STATE:], SSM_GROUPS).transpose(0, 2, 1)
    dwb = gr['ssm_wb']

    def from_blocks(m):
        return _diag_blocks(m, SSM_GROUPS).transpose(1, 0, 2).reshape(SSM_GROUP_CH, N_STATE)

    dbbrT, dbbiT = from_blocks(dwb[:, :N_STATE]), from_blocks(dwb[:, N_STATE:])
    da16 = gr['ssm_a16']
    dabr, dabi = da16[0:8].reshape(1, N_STATE), da16[8:16].reshape(1, N_STATE)

    def prep_bwd(ar, ai, ldt, brT, biT, d1, d2, d3, d4):
        _, vjp = jax.vjp(_ssm_prep, ar, ai, ldt, brT, biT)
        return vjp((d1, d2, d3, d4))

    dar, dai, dldt, dbrT, dbiT = whole(prep_bwd, w['ssm_prep_in'] + (dabr, dabi, dbbrT, dbbiT),
                                       [(1, N_STATE)] * 3 + [(SSM_GROUP_CH, N_STATE)] * 2, "ssm_prep_bwd")
    out['ssm_a_re'] = dar.reshape(SSM_GROUPS, SSM_STATE)
    out['ssm_a_im'] = dai.reshape(SSM_GROUPS, SSM_STATE)
    out['ssm_log_dt'] = dldt.reshape(SSM_GROUPS, SSM_STATE).sum(axis=-1)
    out['ssm_b_re'] = dbrT.reshape(SSM_GROUP_CH, SSM_GROUPS, SSM_STATE).transpose(1, 2, 0)
    out['ssm_b_im'] = dbiT.reshape(SSM_GROUP_CH, SSM_GROUPS, SSM_STATE).transpose(1, 2, 0)
    return out


def adamw(w, g, m, v, name):
    R, C = w.shape

    def fn(w, g, m, v):
        m = ADAM_B1 * m + (1.0 - ADAM_B1) * g
        v = ADAM_B2 * v + (1.0 - ADAM_B2) * jnp.square(g)
        m_hat = m / (1.0 - ADAM_B1 ** ADAM_STEP)
        v_hat = v / (1.0 - ADAM_B2 ** ADAM_STEP)
        delta = -ADAM_LR * (m_hat / (jnp.sqrt(v_hat) + ADAM_EPS) + ADAM_WD * w)
        return delta, m, v

    return rowwise(fn, [w, g, m, v], [], [(C, f32)] * 3, [], _pick(R, (256, 128, 64, 32, 16, 8)), name)


def kernel(x, positions, ln_g, ln_b, ffn1_w_gate, ffn1_w_up, ffn1_w_down, w_in, gmlp_norm_g, gmlp_ws, gmlp_bs, mla_q_norm_g, mla_w_uq, mla_kv_norm_g, mla_w_ukv, ssm_a_re, ssm_a_im, ssm_b_re, ssm_b_im, ssm_c_re, ssm_c_im, ssm_d, ssm_log_dt, ssm_glu_w, ssm_glu_b, mix_norm_g, w_out, ffn2_w_gate, ffn2_w_up, ffn2_w_down, loss_target, m_ln_g, m_ln_b, m_ffn1_w_gate, m_ffn1_w_up, m_ffn1_w_down, m_w_in, m_gmlp_norm_g, m_gmlp_ws, m_gmlp_bs, m_mla_q_norm_g, m_mla_w_uq, m_mla_kv_norm_g, m_mla_w_ukv, m_ssm_a_re, m_ssm_a_im, m_ssm_b_re, m_ssm_b_im, m_ssm_c_re, m_ssm_c_im, m_ssm_d, m_ssm_log_dt, m_ssm_glu_w, m_ssm_glu_b, m_mix_norm_g, m_w_out, m_ffn2_w_gate, m_ffn2_w_up, m_ffn2_w_down, v_ln_g, v_ln_b, v_ffn1_w_gate, v_ffn1_w_up, v_ffn1_w_down, v_w_in, v_gmlp_norm_g, v_gmlp_ws, v_gmlp_bs, v_mla_q_norm_g, v_mla_w_uq, v_mla_kv_norm_g, v_mla_w_ukv, v_ssm_a_re, v_ssm_a_im, v_ssm_b_re, v_ssm_b_im, v_ssm_c_re, v_ssm_c_im, v_ssm_d, v_ssm_log_dt, v_ssm_glu_w, v_ssm_glu_b, v_mix_norm_g, v_w_out, v_ffn2_w_gate, v_ffn2_w_up, v_ffn2_w_down):
    Wp = dict(zip(W_NAMES, (ln_g, ln_b, ffn1_w_gate, ffn1_w_up, ffn1_w_down, w_in, gmlp_norm_g, gmlp_ws, gmlp_bs, mla_q_norm_g, mla_w_uq, mla_kv_norm_g, mla_w_ukv, ssm_a_re, ssm_a_im, ssm_b_re, ssm_b_im, ssm_c_re, ssm_c_im, ssm_d, ssm_log_dt, ssm_glu_w, ssm_glu_b, mix_norm_g, w_out, ffn2_w_gate, ffn2_w_up, ffn2_w_down)))
    Mp = dict(zip(W_NAMES, (m_ln_g, m_ln_b, m_ffn1_w_gate, m_ffn1_w_up, m_ffn1_w_down, m_w_in, m_gmlp_norm_g, m_gmlp_ws, m_gmlp_bs, m_mla_q_norm_g, m_mla_w_uq, m_mla_kv_norm_g, m_mla_w_ukv, m_ssm_a_re, m_ssm_a_im, m_ssm_b_re, m_ssm_b_im, m_ssm_c_re, m_ssm_c_im, m_ssm_d, m_ssm_log_dt, m_ssm_glu_w, m_ssm_glu_b, m_mix_norm_g, m_w_out, m_ffn2_w_gate, m_ffn2_w_up, m_ffn2_w_down)))
    Vp = dict(zip(W_NAMES, (v_ln_g, v_ln_b, v_ffn1_w_gate, v_ffn1_w_up, v_ffn1_w_down, v_w_in, v_gmlp_norm_g, v_gmlp_ws, v_gmlp_bs, v_mla_q_norm_g, v_mla_w_uq, v_mla_kv_norm_g, v_mla_w_ukv, v_ssm_a_re, v_ssm_a_im, v_ssm_b_re, v_ssm_b_im, v_ssm_c_re, v_ssm_c_im, v_ssm_d, v_ssm_log_dt, v_ssm_glu_w, v_ssm_glu_b, v_mix_norm_g, v_w_out, v_ffn2_w_gate, v_ffn2_w_up, v_ffn2_w_down)))
    S = x.shape[1]
    my = _my_id()

    big_rows = jnp.concatenate([Wp[n][l].astype(bf16).reshape(_pack_rows(n), PACK_W) for l in range(DEPTH) for n in BIG])
    ln_flat = jnp.concatenate([Wp[n].reshape(-1) for n in LN_NAMES])
    big_all, ln_all = all_gather([big_rows, _pad_rows(ln_flat, 8)], "gather_weights")
    ln_all = ln_all.reshape(N_DEV, -1)
    lnsz = DEPTH * 3 * (D_MODEL // N_DEV)
    ln_full = {}
    for t, n in enumerate(LN_NAMES):
        sh = ln_all[:, t * lnsz:(t + 1) * lnsz].reshape(N_DEV, DEPTH, 3, D_MODEL // N_DEV)
        ln_full[n] = sh.transpose(1, 2, 0, 3).reshape(DEPTH, 3, 1, D_MODEL)
    Wl, off = [], 0
    for l in range(DEPTH):
        d = {}
        for n in BIG:
            d[n] = _join_from_devices(n, big_all[:, off:off + _pack_rows(n)])
            off += _pack_rows(n)
        Wl.append(d)
    rep = {n: Wp[n] for n in REPL}

    inv_freq = 1.0 / (ROPE_BASE ** (jnp.arange(0, MLA_ROPE, 2, dtype=f32) / MLA_ROPE))
    ang = positions.astype(f32).reshape(S, 1) * inv_freq[None, :]
    cos8 = jnp.tile(jnp.cos(ang), (1, MLA_HEADS))
    sin8 = jnp.tile(jnp.sin(ang), (1, MLA_HEADS))

    xs = x.reshape(S, D_MODEL)
    xb = xs.astype(bf16)
    ws, saved = [], []
    for l in range(DEPTH):
        w = prep_layer(Wl[l], rep, l)
        lg, lb = ln_full['ln_g'][l], ln_full['ln_b'][l]
        xs, xb, r1 = ffn_fwd(xs, xb, w['ffn1'], lg[0], lb[0])
        xs, xb, r2 = mixer_fwd(xs, xb, w, lg[1], lb[1], cos8, sin8)
        xs, xb, r3 = ffn_fwd(xs, xb, w['ffn2'], lg[2], lb[2])
        ws.append(w)
        saved.append((r1, r2, r3))

    def loss_fn(y, t):
        d = y - t
        part = jnp.sum(jnp.mean(jnp.square(d), axis=-1, keepdims=True), axis=0, keepdims=True)
        return d * (1.0 / D_MODEL), 0.5 * part

    dx, loss_part = rowwise(loss_fn, [xs, loss_target.reshape(S, D_MODEL)], [], [(D_MODEL, f32)], [(1, 1)],
                            _row_tile(S, 512), "loss")
    loss = lax.psum(loss_part[0, 0], ("x", "y", "c"))

    grads = [None] * DEPTH
    for l in reversed(range(DEPTH)):
        w = ws[l]
        lg, lb = ln_full['ln_g'][l], ln_full['ln_b'][l]
        r1, r2, r3 = saved[l]
        dx, g2g, g2u, g2d, dg2, db2 = ffn_bwd(dx, r3, w['ffn2'], lg[2], lb[2])
        dx, gm = mixer_bwd(dx, r2, w, lg[1], lb[1], cos8, sin8)
        dx, g1g, g1u, g1d, dg0, db0 = ffn_bwd(dx, r1, w['ffn1'], lg[0], lb[0])
        g = unprep_grads(gm, w)
        g.update({'ffn1_w_gate': g1g, 'ffn1_w_up': g1u, 'ffn1_w_down': g1d,
                  'ffn2_w_gate': g2g, 'ffn2_w_up': g2u, 'ffn2_w_down': g2d})
        g['ln_g'] = jnp.concatenate([dg0, g['ln_g'], dg2], axis=0)
        g['ln_b'] = jnp.concatenate([db0, g['ln_b'], db2], axis=0)
        grads[l] = g
    grad_x = dx.reshape(1, S, D_MODEL)

    gpack = jnp.concatenate([_split_for_devices(n, grads[l][n].astype(bf16)) for l in range(DEPTH) for n in BIG], axis=1)
    gbig = sum_slots(all_to_all([gpack], "scatter_grads")[0], "sum_big")
    small_names = LN_NAMES + REPL
    spack = jnp.concatenate([jnp.stack([grads[l][n] for l in range(DEPTH)]).reshape(-1) for n in small_names])
    n_small = spack.shape[0]
    gsmall = sum_slots(all_gather([_pad_rows(spack, 8)], "gather_small")[0], "sum_small").reshape(-1)

    G, off = {}, 0
    big_parts = {n: [] for n in BIG}
    for l in range(DEPTH):
        for n in BIG:
            big_parts[n].append(gbig[off:off + _pack_rows(n)].reshape(_shard_shape(n)))
            off += _pack_rows(n)
    for n in BIG:
        G[n] = jnp.stack(big_parts[n])
    off = 0
    for n in small_names:
        shp = (DEPTH, 3, D_MODEL) if n in LN_NAMES else Wp[n].shape
        sz = math.prod(shp)
        G[n] = gsmall[off:off + sz].reshape(shp)
        off += sz
    for n in LN_NAMES:
        G[n] = lax.dynamic_slice_in_dim(G[n], my * (D_MODEL // N_DEV), D_MODEL // N_DEV, axis=2)

    delta, new_m, new_v = {}, {}, {}
    for n in BIG:
        shp = Wp[n].shape
        two = (shp[0] * shp[1], shp[2])
        d_, m_, v_ = adamw(Wp[n].reshape(two), G[n].reshape(two), Mp[n].reshape(two), Vp[n].reshape(two), "adamw_" + n)
        delta[n], new_m[n], new_v[n] = d_.reshape(shp), m_.reshape(shp), v_.reshape(shp)

    def pack_small(src):
        return _pad_rows(jnp.concatenate([src[n].reshape(-1) for n in small_names]), 8)

    d_, m_, v_ = adamw(pack_small(Wp), pack_small(G), pack_small(Mp), pack_small(Vp), "adamw_small")
    d_, m_, v_ = d_.reshape(-1), m_.reshape(-1), v_.reshape(-1)
    off = 0
    for n in small_names:
        shp = Wp[n].shape
        sz = math.prod(shp)
        delta[n], new_m[n], new_v[n] = (t[off:off + sz].reshape(shp) for t in (d_, m_, v_))
        off += sz

    return (loss, grad_x, *[G[n] for n in W_NAMES], *[delta[n] for n in W_NAMES],
            *[new_m[n] for n in W_NAMES], *[new_v[n] for n in W_NAMES])
```

```python
import functools
import math

import jax
import jax.numpy as jnp
from jax import lax
from jax.experimental import pallas as pl
from jax.experimental.pallas import tpu as pltpu

f32 = jnp.float32
bf16 = jnp.bfloat16

D_MODEL = 1024
DEPTH = 4
D_FF = 2816
GM_HEADS, GM_HEAD_DIM, GM_WIDTH, GM_CHUNK = 4, 64, 256, 128
MLA_HEADS, MLA_NOPE, MLA_ROPE, MLA_V = 8, 64, 32, 64
ROPE_HALF = MLA_ROPE // 2
Q_LORA, KV_LORA = 256, 128
ROPE_BASE = 10000.0
SSM_GROUPS, SSM_GROUP_CH, SSM_WIDTH, SSM_STATE = 16, 16, 256, 64
N_STATE = SSM_GROUPS * SSM_STATE
ALPHA = (2 * DEPTH) ** 0.25
LN_EPS = 1e-5
RMS_EPS = 1e-6
NEG_BIG = -1e30
ATT_SCALE = (MLA_NOPE + MLA_ROPE) ** -0.5
ADAM_LR, ADAM_B1, ADAM_B2, ADAM_EPS, ADAM_WD, ADAM_STEP = 0.001, 0.9, 0.999, 1e-08, 0.01, 10

N_DEV = 8
LANES = 128
VMEM_LIMIT = 48 * 1024 * 1024
MM_TILE_BUDGET = 32 * 1024 * 1024
MESH = pl.DeviceIdType.MESH

H_UG, H_VG, H_CQ, H_US, H_CKV, H_K1, H_K2, H_COLS = 0, 256, 512, 768, 1024, 1152, 1280, 1408

W_NAMES = ['ln_g', 'ln_b', 'ffn1_w_gate', 'ffn1_w_up', 'ffn1_w_down', 'w_in', 'gmlp_norm_g', 'gmlp_ws', 'gmlp_bs',
           'mla_q_norm_g', 'mla_w_uq', 'mla_kv_norm_g', 'mla_w_ukv', 'ssm_a_re', 'ssm_a_im', 'ssm_b_re', 'ssm_b_im',
           'ssm_c_re', 'ssm_c_im', 'ssm_d', 'ssm_log_dt', 'ssm_glu_w', 'ssm_glu_b', 'mix_norm_g', 'w_out',
           'ffn2_w_gate', 'ffn2_w_up', 'ffn2_w_down']
BIG = {'ffn1_w_gate': 1, 'ffn1_w_up': 1, 'ffn1_w_down': 0, 'w_in': 1, 'mla_w_uq': 1, 'mla_w_ukv': 1,
       'ssm_glu_w': 0, 'w_out': 0, 'ffn2_w_gate': 1, 'ffn2_w_up': 1, 'ffn2_w_down': 0}
BIG_SHAPE = {'ffn1_w_gate': (D_MODEL, D_FF), 'ffn1_w_up': (D_MODEL, D_FF), 'ffn1_w_down': (D_FF, D_MODEL),
             'w_in': (D_MODEL, 1184), 'mla_w_uq': (Q_LORA, 768), 'mla_w_ukv': (KV_LORA, 1024),
             'ssm_glu_w': (SSM_WIDTH, SSM_WIDTH), 'w_out': (D_MODEL, D_MODEL),
             'ffn2_w_gate': (D_MODEL, D_FF), 'ffn2_w_up': (D_MODEL, D_FF), 'ffn2_w_down': (D_FF, D_MODEL)}
LN_NAMES = ['ln_g', 'ln_b']
REPL = [n for n in W_NAMES if n not in BIG and n not in LN_NAMES]


def _pick(n, cands):
    for c in cands:
        if n % c == 0:
            return c
    return n


def _params(sem):
    return pltpu.CompilerParams(dimension_semantics=sem, vmem_limit_bytes=VMEM_LIMIT)


S3_ROWS = 2 * N_STATE // LANES


def _from_s3(ref):
    return jnp.concatenate([ref[:, c, :] for c in range(S3_ROWS)], axis=1)


def _to_s3(ref, val):
    for c in range(S3_ROWS):
        ref[:, c, :] = val[:, c * LANES:(c + 1) * LANES].astype(ref.dtype)


def mm(a, b, *, out_dtype=f32, add=None, add_scale=1.0, a_col0=0, a_s3=False, out_s3=False, grouped=False, name):
    G = a.shape[0] if grouped else 1
    M = a.shape[1] if grouped else a.shape[0]
    K, N = b.shape[-2:]
    tn = N if out_s3 else _pick(N, (512, 384, 256) if N <= 1536 else (512, 384, 256, 128))
    tk = K if (K <= 2 * D_FF or a_s3) else _pick(K, (1024, 512, 256, 128))
    nk = G * (K // tk)
    assert not grouped or tk == K

    def tile_bytes(tm):
        return 2 * (tm * tk * a.dtype.itemsize + tk * tn * b.dtype.itemsize + tm * tn * 4 * (2 if add is not None else 1))

    tm = next(t for t in (1024, 512, 256, 128, 64, 32, 16, 8) if M % t == 0 and (tile_bytes(t) <= MM_TILE_BUDGET or t == 8))
    assert a_s3 or grouped or (a_col0 % tk == 0 and a_col0 + K <= a.shape[1])
    kb0 = a_col0 // tk
    has_add = add is not None

    def body(*refs):
        if has_add:
            a_ref, b_ref, add_ref, o_ref, acc = refs
        else:
            a_ref, b_ref, o_ref, acc = refs
        k = pl.program_id(2)
        a_val = _from_s3(a_ref) if a_s3 else a_ref[...]
        part = jnp.dot(a_val.astype(bf16), b_ref[...].astype(bf16), preferred_element_type=f32)

        def finish(total):
            if has_add:
                total = total + add_scale * add_ref[...]
            if out_s3:
                _to_s3(o_ref, total)
            else:
                o_ref[...] = total.astype(o_ref.dtype)

        if nk == 1:
            finish(part)
        else:
            @pl.when(k == 0)
            def _():
                acc[...] = part

            @pl.when(k > 0)
            def _():
                acc[...] += part

            @pl.when(k == nk - 1)
            def _():
                finish(acc[...])

    if a_s3:
        a_spec = pl.BlockSpec((tm, S3_ROWS, LANES), lambda i, j, k: (i, 0, 0))
    elif grouped:
        a_spec = pl.BlockSpec((None, tm, tk), lambda i, j, k: (k, i, 0))
    else:
        a_spec = pl.BlockSpec((tm, tk), lambda i, j, k: (i, kb0 + k))
    b_spec = pl.BlockSpec((None, tk, tn), lambda i, j, k: (k, 0, j)) if grouped else pl.BlockSpec((tk, tn), lambda i, j, k: (k, j))
    in_specs, ops = [a_spec, b_spec], [a, b]
    if has_add:
        in_specs.append(pl.BlockSpec((tm, tn), lambda i, j, k: (i, j)))
        ops.append(add)
    if out_s3:
        out_spec = pl.BlockSpec((tm, S3_ROWS, LANES), lambda i, j, k: (i, 0, 0))
        out_shape = jax.ShapeDtypeStruct((M, S3_ROWS, LANES), out_dtype)
    else:
        out_spec = pl.BlockSpec((tm, tn), lambda i, j, k: (i, j))
        out_shape = jax.ShapeDtypeStruct((M, N), out_dtype)
    return pl.pallas_call(
        body, grid=(M // tm, N // tn, nk), in_specs=in_specs, out_specs=out_spec, out_shape=out_shape,
        scratch_shapes=[pltpu.VMEM((tm, tn) if nk > 1 else (8, LANES), f32)],
        compiler_params=_params(("parallel", "parallel", "arbitrary")), name=name)(*ops)


def mm_tn(a, b, *, a_col0=0, m_dim=None, a_s3=False, b_s3=False, b_lead=None, name):
    K = a.shape[0]
    M = 2 * N_STATE if a_s3 else (a.shape[1] if m_dim is None else m_dim)
    N = 2 * N_STATE if b_s3 else b.shape[-1]
    tm = M if a_s3 else _pick(M, (1024, 768, 512, 384, 256, 128))
    tn = N if b_s3 else _pick(N, (H_COLS, 1024, 768, 512, 384, 256))

    def tile_bytes(tk):
        return 2 * tk * (tm * a.dtype.itemsize + tn * b.dtype.itemsize) + 3 * tm * tn * 4

    tk = next(t for t in (2048, 1024, 512, 256, 128, 64, 32, 16) if K % t == 0 and (tile_bytes(t) <= MM_TILE_BUDGET or t == 16))
    nk = K // tk
    assert a_col0 % tm == 0
    mb0 = a_col0 // tm

    def body(a_ref, b_ref, o_ref, acc):
        k = pl.program_id(2)
        a_val = _from_s3(a_ref) if a_s3 else a_ref[...]
        b_val = _from_s3(b_ref) if b_s3 else b_ref[...]
        part = lax.dot_general(a_val.astype(bf16), b_val.astype(bf16), (((0,), (0,)), ((), ())), preferred_element_type=f32)

        @pl.when(k == 0)
        def _():
            acc[...] = part

        @pl.when(k > 0)
        def _():
            acc[...] += part

        @pl.when(k == nk - 1)
        def _():
            o_ref[...] = acc[...]

    s3_spec = pl.BlockSpec((tk, S3_ROWS, LANES), lambda i, j, k: (k, 0, 0))
    if b_s3:
        b_spec = s3_spec
    elif b_lead is not None:
        b_spec = pl.BlockSpec((None, tk, tn), lambda i, j, k: (b_lead, k, j))
    else:
        b_spec = pl.BlockSpec((tk, tn), lambda i, j, k: (k, j))
    return pl.pallas_call(
        body, grid=(M // tm, N // tn, nk),
        in_specs=[s3_spec if a_s3 else pl.BlockSpec((tk, tm), lambda i, j, k: (k, mb0 + i)), b_spec],
        out_specs=pl.BlockSpec((tm, tn), lambda i, j, k: (i, j)),
        out_shape=jax.ShapeDtypeStruct((M, N), f32),
        scratch_shapes=[pltpu.VMEM((tm, tn), f32)],
        compiler_params=_params(("parallel", "parallel", "arbitrary")), name=name)(a, b)


def rowwise(fn, rows, pars, out_rows, out_accs, tm, name):
    first = rows[0]
    if isinstance(first, tuple):
        R = first[0].shape[1] if first[1] == 'lead' else first[0].shape[0]
    else:
        R = first.shape[0]
    assert R % tm == 0, (R, tm, name)
    n_rows, n_pars, n_or, n_oa = len(rows), len(pars), len(out_rows), len(out_accs)

    in_specs, ops = [], []
    for r in rows:
        if isinstance(r, tuple) and r[1] == 'lead':
            arr, _, kk = r
            in_specs.append(pl.BlockSpec((None, tm, arr.shape[2]), lambda i, kk=kk: (kk, i, 0)))
        elif isinstance(r, tuple):
            arr, c0, w = r
            assert c0 % w == 0
            in_specs.append(pl.BlockSpec((tm, w), lambda i, cb=c0 // w: (i, cb)))
        else:
            arr = r
            in_specs.append(pl.BlockSpec((tm, arr.shape[1]), lambda i: (i, 0)))
        ops.append(arr)
    for p in pars:
        in_specs.append(pl.BlockSpec(p.shape, lambda i, nd=p.ndim: (0,) * nd))
        ops.append(p)
    out_specs = [pl.BlockSpec((tm, w), lambda i: (i, 0)) for (w, _) in out_rows]
    out_specs += [pl.BlockSpec(s, lambda i, nd=len(s): (0,) * nd) for s in out_accs]
    out_shape = [jax.ShapeDtypeStruct((R, w), dt) for (w, dt) in out_rows]
    out_shape += [jax.ShapeDtypeStruct(s, f32) for s in out_accs]

    def body(*refs):
        ins = [r[...] for r in refs[:n_rows + n_pars]]
        o_refs = refs[n_rows + n_pars:]
        res = fn(*ins)
        if not isinstance(res, (tuple, list)):
            res = (res,)
        assert len(res) == n_or + n_oa, (len(res), n_or, n_oa, name)
        for o, v in zip(o_refs[:n_or], res[:n_or]):
            o[...] = v.astype(o.dtype)
        if n_oa:
            i = pl.program_id(0)

            @pl.when(i == 0)
            def _():
                for o, v in zip(o_refs[n_or:], res[n_or:]):
                    o[...] = v.astype(f32)

            @pl.when(i > 0)
            def _():
                for o, v in zip(o_refs[n_or:], res[n_or:]):
                    o[...] += v.astype(f32)

    return pl.pallas_call(
        body, grid=(R // tm,), in_specs=in_specs, out_specs=out_specs, out_shape=out_shape,
        compiler_params=_params(("arbitrary",)), name=name)(*ops)


def whole(fn, ins, out_shapes, name):
    def body(*refs):
        res = fn(*[r[...] for r in refs[:len(ins)]])
        for o, v in zip(refs[len(ins):], res):
            o[...] = v

    return pl.pallas_call(body, out_shape=[jax.ShapeDtypeStruct(s, f32) for s in out_shapes], name=name)(*ins)


@jax.custom_vjp
def _bdot(a, b):
    return jnp.dot(a.astype(bf16), b.astype(bf16), preferred_element_type=f32)


def _bdot_fwd(a, b):
    return _bdot(a, b), (a, b)


def _bdot_bwd(res, g):
    a, b = res
    gb = g.astype(bf16)
    da = lax.dot_general(gb, b.astype(bf16), (((1,), (1,)), ((), ())), preferred_element_type=f32)
    db = lax.dot_general(a.astype(bf16), gb, (((0,), (0,)), ((), ())), preferred_element_type=f32)
    return da, db


_bdot.defvjp(_bdot_fwd, _bdot_bwd)


def _ln(z, g, b):
    mu = jnp.mean(z, axis=-1, keepdims=True)
    var = jnp.mean(jnp.square(z - mu), axis=-1, keepdims=True)
    return (z - mu) * lax.rsqrt(var + LN_EPS) * g + b


def _rms_only(x):
    return x * lax.rsqrt(jnp.mean(jnp.square(x), axis=-1, keepdims=True) + RMS_EPS)


def _swiglu(a, b):
    return jax.nn.silu(a) * b


def _gmlp(hu, hv, ng, ws, bsb):
    u = jax.nn.gelu(hu)
    v = jax.nn.gelu(hv)
    lane = lax.broadcasted_iota(jnp.int32, (1, GM_WIDTH), 1)
    masks = [((lane >= GM_HEAD_DIM * h) & (lane < GM_HEAD_DIM * (h + 1))).astype(f32) for h in range(GM_HEADS)]
    mu = jnp.zeros_like(v)
    for m in masks:
        mu = mu + m * (jnp.sum(v * m, axis=-1, keepdims=True) / GM_HEAD_DIM)
    d = v - mu
    var = jnp.zeros_like(v)
    for m in masks:
        var = var + m * (jnp.sum(d * d * m, axis=-1, keepdims=True) / GM_HEAD_DIM)
    vn = d * lax.rsqrt(var + LN_EPS) * ng
    r = lax.broadcasted_iota(jnp.int32, (GM_CHUNK, GM_CHUNK), 0)
    c = lax.broadcasted_iota(jnp.int32, (GM_CHUNK, GM_CHUNK), 1)
    tril = (c <= r).astype(f32)
    z = bsb
    for h, m in enumerate(masks):
        z = z + _bdot(ws[h] * tril, vn * m)
    return u * z


def _mla_prep(cq, ckv, qg, kvg):
    return _rms_only(cq) * qg, _rms_only(ckv) * kvg


def _rope(q1, q2, k1, k2, cos, sin):
    return q1 * cos - q2 * sin, q2 * cos + q1 * sin, k1 * cos - k2 * sin, k2 * cos + k1 * sin


def _mix_post(ya, ob, y1, us, dskip, gluw, glub, gmix):
    y = jax.nn.gelu(y1 + dskip * us)
    yc = y * jax.nn.sigmoid(_bdot(y, gluw) + glub)
    return jnp.concatenate([_rms_only(ya), _rms_only(ob), _rms_only(yc)], axis=1) * gmix


def _ssm_prep(ar, ai, ldt, brT, biT):
    dt = jnp.exp(ldt)
    mag = jnp.exp(ar * dt)
    abr = mag * jnp.cos(ai * dt)
    abi = mag * jnp.sin(ai * dt)
    den = ar * ar + ai * ai
    cr = ((abr - 1.0) * ar + abi * ai) / den
    ci = (abi * ar - (abr - 1.0) * ai) / den
    return abr, abi, cr * brT - ci * biT, cr * biT + ci * brT


def _att_tile(S):
    return _pick(S, (512, 256, 128))


def _nt(a, b):
    return lax.dot_general(a, b, (((1,), (1,)), ((), ())), preferred_element_type=f32)


def _diag_keep(T):
    krow = lax.broadcasted_iota(jnp.int32, (T, T), 0)
    qcol = lax.broadcasted_iota(jnp.int32, (T, T), 1)
    return qcol >= krow


def flash_fwd(q, k, vT):
    Hh, S, _ = q.shape
    T = _att_tile(S)

    def body(q_ref, k_ref, vT_ref, o_ref, lse_ref, m_sc, l_sc, acc_sc):
        i = pl.program_id(1)
        qi = q_ref[0]
        m_sc[...] = jnp.full_like(m_sc, NEG_BIG)
        l_sc[...] = jnp.zeros_like(l_sc)
        acc_sc[...] = jnp.zeros_like(acc_sc)

        def block(j, diagonal):
            rows = pl.ds(pl.multiple_of(j * T, T), T)
            sT = _nt(k_ref[0, rows, :], qi) * ATT_SCALE
            if diagonal:
                sT = jnp.where(_diag_keep(T), sT, NEG_BIG)
            m_old = m_sc[...]
            m_new = jnp.maximum(m_old, jnp.max(sT, axis=0, keepdims=True))
            alpha = jnp.exp(m_old - m_new)
            pT = jnp.exp(sT - m_new)
            l_sc[...] = alpha * l_sc[...] + jnp.sum(pT, axis=0, keepdims=True)
            acc_sc[...] = alpha * acc_sc[...] + jnp.dot(vT_ref[0, :, rows], pT.astype(bf16), preferred_element_type=f32)
            m_sc[...] = m_new

        def loop_body(j, c):
            block(j, False)
            return c

        lax.fori_loop(0, i, loop_body, 0)
        block(i, True)
        o_ref[0] = acc_sc[...] / l_sc[...]
        lse_ref[0] = m_sc[...] + jnp.log(l_sc[...])

    return pl.pallas_call(
        body, grid=(Hh, S // T),
        in_specs=[pl.BlockSpec((1, T, LANES), lambda h, i: (h, i, 0)), pl.BlockSpec((1, S, LANES), lambda h, i: (h, 0, 0)),
                  pl.BlockSpec((1, MLA_V, S), lambda h, i: (h, 0, 0))],
        out_specs=[pl.BlockSpec((1, MLA_V, T), lambda h, i: (h, 0, i)), pl.BlockSpec((1, 1, T), lambda h, i: (h, 0, i))],
        out_shape=[jax.ShapeDtypeStruct((Hh, MLA_V, S), f32), jax.ShapeDtypeStruct((Hh, 1, S), f32)],
        scratch_shapes=[pltpu.VMEM((1, T), f32), pltpu.VMEM((1, T), f32), pltpu.VMEM((MLA_V, T), f32)],
        compiler_params=_params(("parallel", "arbitrary")), name="flash_fwd")(q, k, vT)


def flash_bwd(q, k, kT, v, do, lse, delta):
    Hh, S, _ = q.shape
    T = _att_tile(S)

    def body(q_ref, do_ref, lse_ref, dl_ref, k_ref, kT_ref, v_ref, dq_ref, dk_ref, dv_ref, dq_sc):
        i = pl.program_id(1)

        @pl.when(i == 0)
        def _():
            dk_ref[...] = jnp.zeros_like(dk_ref)
            dv_ref[...] = jnp.zeros_like(dv_ref)

        qi, doi = q_ref[0], do_ref[0]
        lse_i, dl_i = lse_ref[0], dl_ref[0]
        dq_sc[...] = jnp.zeros_like(dq_sc)

        def block(j, diagonal):
            rows = pl.ds(pl.multiple_of(j * T, T), T)
            sT = _nt(k_ref[0, rows, :], qi) * ATT_SCALE
            pT = jnp.exp(sT - lse_i)
            if diagonal:
                pT = jnp.where(_diag_keep(T), pT, 0.0)
            dpT = _nt(v_ref[0, rows, :], doi)
            dsT = (pT * (dpT - dl_i) * ATT_SCALE).astype(bf16)
            dv_ref[0, rows, :] += jnp.dot(pT.astype(bf16), doi, preferred_element_type=f32)
            dk_ref[0, rows, :] += jnp.dot(dsT, qi, preferred_element_type=f32)
            dq_sc[...] += jnp.dot(kT_ref[0, :, rows], dsT, preferred_element_type=f32)

        def loop_body(j, c):
            block(j, False)
            return c

        lax.fori_loop(0, i, loop_body, 0)
        block(i, True)
        dq_ref[0] = dq_sc[...]

    tile = lambda w: pl.BlockSpec((1, T, w), lambda h, i: (h, i, 0))
    row = pl.BlockSpec((1, 1, T), lambda h, i: (h, 0, i))
    full = lambda w: pl.BlockSpec((1, S, w), lambda h, i: (h, 0, 0))
    return pl.pallas_call(
        body, grid=(Hh, S // T),
        in_specs=[tile(LANES), tile(MLA_V), row, row, full(LANES), pl.BlockSpec((1, LANES, S), lambda h, i: (h, 0, 0)), full(MLA_V)],
        out_specs=[pl.BlockSpec((1, LANES, T), lambda h, i: (h, 0, i)), full(LANES), full(MLA_V)],
        out_shape=[jax.ShapeDtypeStruct((Hh, LANES, S), f32), jax.ShapeDtypeStruct((Hh, S, LANES), f32),
                   jax.ShapeDtypeStruct((Hh, S, MLA_V), f32)],
        scratch_shapes=[pltpu.VMEM((LANES, T), f32)],
        compiler_params=_params(("parallel", "arbitrary")), name="flash_bwd")(q, do, lse, delta, k, kT, v)


def _scan_tile(S):
    return _pick(S, (256, 128, 64, 32, 16, 8))


def scan_fwd(bu3, a16):
    S = bu3.shape[0]
    ts = _scan_tile(S)

    def body(bu_ref, a_ref, o_ref, h_sc):
        @pl.when(pl.program_id(0) == 0)
        def _():
            h_sc[...] = jnp.zeros_like(h_sc)

        ar, ai = a_ref[0:8, :], a_ref[8:16, :]

        def step(t, carry):
            hr, hi = carry
            nr = ar * hr - ai * hi + bu_ref[t, 0:8, :]
            ni = ar * hi + ai * hr + bu_ref[t, 8:16, :]
            o_ref[t, 0:8, :] = nr
            o_ref[t, 8:16, :] = ni
            return nr, ni

        hr, hi = lax.fori_loop(0, ts, step, (h_sc[0:8, :], h_sc[8:16, :]), unroll=8)
        h_sc[0:8, :] = hr
        h_sc[8:16, :] = hi

    blk = pl.BlockSpec((ts, 16, LANES), lambda i: (i, 0, 0))
    return pl.pallas_call(
        body, grid=(S // ts,), in_specs=[blk, pl.BlockSpec((16, LANES), lambda i: (0, 0))], out_specs=blk,
        out_shape=jax.ShapeDtypeStruct(bu3.shape, f32), scratch_shapes=[pltpu.VMEM((16, LANES), f32)],
        compiler_params=_params(("arbitrary",)), name="scan_fwd")(bu3, a16)


def scan_bwd(g3, h3, a16):
    S = g3.shape[0]
    ts = _scan_tile(S)
    nb = S // ts

    def body(g_ref, h_ref, a_ref, o_ref, da_ref, lam_sc, da_sc):
        @pl.when(pl.program_id(0) == 0)
        def _():
            lam_sc[...] = jnp.zeros_like(lam_sc)
            da_sc[...] = jnp.zeros_like(da_sc)

        ar, ai = a_ref[0:8, :], a_ref[8:16, :]

        def step(kk, carry):
            lr, li, dar, dai = carry
            t = ts - 1 - kk
            hr, hi = h_ref[t, 0:8, :], h_ref[t, 8:16, :]
            dar = dar + lr * hr + li * hi
            dai = dai + li * hr - lr * hi
            nlr = ar * lr + ai * li + g_ref[t, 0:8, :]
            nli = ar * li - ai * lr + g_ref[t, 8:16, :]
            o_ref[t, 0:8, :] = nlr
            o_ref[t, 8:16, :] = nli
            return nlr, nli, dar, dai

        lr, li, dar, dai = lax.fori_loop(
            0, ts, step, (lam_sc[0:8, :], lam_sc[8:16, :], da_sc[0:8, :], da_sc[8:16, :]), unroll=8)
        lam_sc[0:8, :] = lr
        lam_sc[8:16, :] = li
        da_sc[0:8, :] = dar
        da_sc[8:16, :] = dai
        da_ref[0:8, :] = dar
        da_ref[8:16, :] = dai

    blk = pl.BlockSpec((ts, 16, LANES), lambda i: (nb - 1 - i, 0, 0))
    small = pl.BlockSpec((16, LANES), lambda i: (0, 0))
    return pl.pallas_call(
        body, grid=(nb,), in_specs=[blk, blk, small], out_specs=[blk, small],
        out_shape=[jax.ShapeDtypeStruct(g3.shape, f32), jax.ShapeDtypeStruct((16, LANES), f32)],
        scratch_shapes=[pltpu.VMEM((16, LANES), f32), pltpu.VMEM((16, LANES), f32)],
        compiler_params=_params(("arbitrary",)), name="scan_bwd")(g3, h3, a16)


HBM_SPEC = pl.BlockSpec(memory_space=pltpu.HBM)


def _my_id():
    return 4 * lax.axis_index("x") + 2 * lax.axis_index("y") + lax.axis_index("c")


def all_gather(xs, name):
    n = len(xs)

    def body(*refs):
        x_refs, o_refs = refs[:n], refs[n:2 * n]
        send_sems, recv_sems, local_sems = refs[2 * n:]
        x, y, c = lax.axis_index("x"), lax.axis_index("y"), lax.axis_index("c")
        me, sibling = (x, y, c), (x, y, 1 - c)
        chips = [(1 - x, y), (x, 1 - y), (1 - x, 1 - y)]

        def slot(o, p):
            return o.at[4 * p[0] + 2 * p[1] + p[2]]

        def copy(a, k, block, to, src=None):
            o = o_refs[a]
            return pltpu.make_async_remote_copy(
                src_ref=slot(o, block) if src is None else src, dst_ref=slot(o, block),
                send_sem=send_sems.at[7 * a + k], recv_sem=recv_sems.at[7 * a + k], device_id=to, device_id_type=MESH)

        own, sends = [], []
        for a in range(n):
            mine = pltpu.make_async_copy(x_refs[a], slot(o_refs[a], me), local_sems.at[a])
            mine.start()
            own.append(mine)
            first = [copy(a, 0, me, sibling, src=x_refs[a])]
            first += [copy(a, 1 + j, me, (*chip, c), src=x_refs[a]) for j, chip in enumerate(chips)]
            for cp in first:
                cp.start()
            sends += first
        for a in range(n):
            for j, chip in enumerate(chips):
                copy(a, 1 + j, (*chip, c), me).wait_recv()
                fwd = copy(a, 4 + j, (*chip, c), sibling)
                fwd.start()
                sends.append(fwd)
        for a in range(n):
            copy(a, 0, sibling, me).wait_recv()
            for j, chip in enumerate(chips):
                copy(a, 4 + j, (*chip, 1 - c), me).wait_recv()
        for cp in sends:
            cp.wait_send()
        for cp in own:
            cp.wait()

    return pl.pallas_call(
        body, out_shape=[jax.ShapeDtypeStruct((N_DEV,) + v.shape, v.dtype) for v in xs],
        in_specs=[HBM_SPEC] * n, out_specs=[HBM_SPEC] * n,
        scratch_shapes=[pltpu.SemaphoreType.DMA((7 * n,)), pltpu.SemaphoreType.DMA((7 * n,)), pltpu.SemaphoreType.DMA((n,))],
        name=name)(*xs)


def all_to_all(xs, name):
    n = len(xs)

    def body(*refs):
        x_refs, o_refs = refs[:n], refs[n:2 * n]
        send_sems, recv_sems, local_sems = refs[2 * n:]
        x, y, c = lax.axis_index("x"), lax.axis_index("y"), lax.axis_index("c")
        my = 4 * x + 2 * y + c
        waits = []
        for a in range(n):
            mine = pltpu.make_async_copy(x_refs[a].at[my], o_refs[a].at[my], local_sems.at[a])
            mine.start()
            for k in range(1, N_DEV):
                px = 1 - x if k & 4 else x
                py = 1 - y if k & 2 else y
                pc = 1 - c if k & 1 else c
                pid = 4 * px + 2 * py + pc
                out = pltpu.make_async_remote_copy(
                    src_ref=x_refs[a].at[pid], dst_ref=o_refs[a].at[my],
                    send_sem=send_sems.at[7 * a + k - 1], recv_sem=recv_sems.at[7 * a + k - 1],
                    device_id=(px, py, pc), device_id_type=MESH)
                out.start()
                back = pltpu.make_async_remote_copy(
                    src_ref=x_refs[a].at[pid], dst_ref=o_refs[a].at[pid],
                    send_sem=send_sems.at[7 * a + k - 1], recv_sem=recv_sems.at[7 * a + k - 1],
                    device_id=(px, py, pc), device_id_type=MESH)
                waits.append((out, back))
            waits.append((mine, None))
        for out, back in waits:
            if back is None:
                out.wait()
            else:
                out.wait_send()
                back.wait_recv()

    return pl.pallas_call(
        body, out_shape=[jax.ShapeDtypeStruct(v.shape, v.dtype) for v in xs],
        in_specs=[HBM_SPEC] * n, out_specs=[HBM_SPEC] * n,
        scratch_shapes=[pltpu.SemaphoreType.DMA((7 * n,)), pltpu.SemaphoreType.DMA((7 * n,)), pltpu.SemaphoreType.DMA((n,))],
        name=name)(*xs)


def sum_slots(g8, name):
    R, C = g8.shape[1:]

    def fn(*tiles):
        tot = tiles[0].astype(f32)
        for t in tiles[1:]:
            tot = tot + t.astype(f32)
        return tot

    return rowwise(fn, [(g8, 'lead', k) for k in range(N_DEV)], [], [(C, f32)], [], _pick(R, (256, 128, 64, 32, 16, 8)), name)[0]


PACK_W = 1024


def _pad_rows(flat, mult):
    n = flat.shape[0]
    tot = -(-n // (PACK_W * mult)) * PACK_W * mult
    return jnp.pad(flat, (0, tot - n)).reshape(tot // PACK_W, PACK_W)


def _shard_shape(name):
    r, c = BIG_SHAPE[name]
    return (r // N_DEV, c) if BIG[name] == 0 else (r, c // N_DEV)


def _pack_rows(name):
    r, c = _shard_shape(name)
    assert (r * c) % PACK_W == 0
    return r * c // PACK_W


def _split_for_devices(name, full):
    r, c = BIG_SHAPE[name]
    if BIG[name] == 0:
        return full.reshape(N_DEV, _pack_rows(name), PACK_W)
    return full.reshape(r, N_DEV, c // N_DEV).transpose(1, 0, 2).reshape(N_DEV, _pack_rows(name), PACK_W)


def _join_from_devices(name, parts):
    r, c = BIG_SHAPE[name]
    if BIG[name] == 0:
        return parts.reshape(r, c)
    return parts.reshape(N_DEV, r, c // N_DEV).transpose(1, 0, 2).reshape(r, c)


def _row_tile(S, want):
    return _pick(S, tuple(t for t in (512, 256, 128, 64, 32, 16) if t <= want))


def resid_ln(x, f, g, b, scale, name):
    D = x.shape[1]

    def fn(x, f, g, b):
        z = ALPHA * x + scale * f
        xo = _ln(z, g, b)
        return z, xo, xo

    return rowwise(fn, [x, f], [g, b], [(D, f32), (D, f32), (D, bf16)], [], _row_tile(x.shape[0], 512), name)


def ln_bwd(z, g, b, dxo, scale, name):
    D = z.shape[1]

    def fn(z, dxo, g, b):
        _, vjp = jax.vjp(_ln, z, g, b)
        dz, dg, db = vjp(dxo)
        return dz, scale * dz, dg, db

    return rowwise(fn, [z, dxo], [g, b], [(D, f32), (D, bf16)], [(1, D), (1, D)], _row_tile(z.shape[0], 256), name)


FF_TILE = 256


def ffn_up_act(xb, wg, wu):
    M, K = xb.shape
    tm = _pick(M, (1024, 512, 256, 128, 64, 32, 16))

    def body(x_ref, wg_ref, wu_ref, ab_ref, h_ref):
        x = x_ref[...]
        a = jnp.dot(x, wg_ref[...], preferred_element_type=f32)
        b = jnp.dot(x, wu_ref[...], preferred_element_type=f32)
        ab_ref[0] = a
        ab_ref[1] = b
        h_ref[...] = _swiglu(a, b).astype(bf16)

    wspec = pl.BlockSpec((K, FF_TILE), lambda i, j: (0, j))
    return pl.pallas_call(
        body, grid=(M // tm, D_FF // FF_TILE),
        in_specs=[pl.BlockSpec((tm, K), lambda i, j: (i, 0)), wspec, wspec],
        out_specs=[pl.BlockSpec((2, tm, FF_TILE), lambda i, j: (0, i, j)), pl.BlockSpec((tm, FF_TILE), lambda i, j: (i, j))],
        out_shape=[jax.ShapeDtypeStruct((2, M, D_FF), f32), jax.ShapeDtypeStruct((M, D_FF), bf16)],
        compiler_params=_params(("parallel", "parallel")), name="ffn_up_act")(xb, wg, wu)


def ffn_down_dx_act(dzs, wdT, ab):
    M, K = dzs.shape
    tm = _pick(M, (1024, 512, 256, 128, 64, 32, 16))

    def body(dz_ref, w_ref, ab_ref, dab_ref, h_ref):
        dh = jnp.dot(dz_ref[...], w_ref[...], preferred_element_type=f32)
        h, vjp = jax.vjp(_swiglu, ab_ref[0], ab_ref[1])
        da, db = vjp(dh)
        dab_ref[0] = da.astype(bf16)
        dab_ref[1] = db.astype(bf16)
        h_ref[...] = h.astype(bf16)

    pair = pl.BlockSpec((2, tm, FF_TILE), lambda i, j: (0, i, j))
    return pl.pallas_call(
        body, grid=(M // tm, D_FF // FF_TILE),
        in_specs=[pl.BlockSpec((tm, K), lambda i, j: (i, 0)), pl.BlockSpec((K, FF_TILE), lambda i, j: (0, j)), pair],
        out_specs=[pair, pl.BlockSpec((tm, FF_TILE), lambda i, j: (i, j))],
        out_shape=[jax.ShapeDtypeStruct((2, M, D_FF), bf16), jax.ShapeDtypeStruct((M, D_FF), bf16)],
        compiler_params=_params(("parallel", "parallel")), name="ffn_down_dx_act")(dzs, wdT, ab)


def ffn_fwd(x, xb, w, g, b):
    ab, h = ffn_up_act(xb, w['g'], w['u'])
    f = mm(h, w['d'], name="ffn_down")
    z, xo, xob = resid_ln(x, f, g, b, 0.5, "ffn_ln")
    return xo, xob, (xb, ab, z)


def ffn_bwd(dxo, res, w, g, b):
    xb, ab, z = res
    dz, dzs, dg, db = ln_bwd(z, g, b, dxo, 0.5, "ffn_ln_bwd")
    dab, h = ffn_down_dx_act(dzs, w['dT'], ab)
    dwd = mm_tn(h, dzs, name="ffn_down_dw")
    dwg = mm_tn(xb, dab, b_lead=0, name="ffn_gate_dw")
    dwu = mm_tn(xb, dab, b_lead=1, name="ffn_up_dw")
    dx = mm(dab, w['guT'], add=dz, add_scale=ALPHA, grouped=True, name="ffn_up_dx")
    return dx, dwg, dwu, dwd, dg, db


def _heads_first(a, width):
    return a.reshape(a.shape[0], MLA_HEADS, width)


def mixer_fwd(x, xb, w, g, b, cos8, sin8):
    S = x.shape[0]
    H = mm(xb, w['in'], name="mix_in")
    ya = rowwise(_gmlp, [(H, H_UG, 256), (H, H_VG, 256)], [w['gm_ng'], w['gm_ws'], w['gm_bsb']], [(GM_WIDTH, f32)], [],
                 GM_CHUNK, "gmlp")[0]
    cqn, ckvn = rowwise(_mla_prep, [(H, H_CQ, Q_LORA), (H, H_CKV, KV_LORA)], [w['qg'], w['kvg']],
                        [(Q_LORA, bf16), (KV_LORA, bf16)], [], _row_tile(S, 512), "mla_prep")
    qraw = mm(cqn, w['uq'], name="mla_uq")
    kv = mm(ckvn, w['ukv'], name="mla_ukv")
    q1, q2, k1, k2 = rowwise(_rope, [(qraw, 512, LANES), (qraw, 640, LANES), (H, H_K1, LANES), (H, H_K2, LANES), cos8, sin8],
                             [], [(LANES, f32)] * 4, [], _row_tile(S, 512), "rope")
    zpad = jnp.zeros((S, MLA_HEADS, LANES - MLA_NOPE - MLA_ROPE), f32)
    qp = jnp.concatenate([_heads_first(qraw[:, :512], 64), _heads_first(q1, ROPE_HALF), _heads_first(q2, ROPE_HALF), zpad], axis=2)
    k1b = jnp.broadcast_to(k1[:, None, :ROPE_HALF], (S, MLA_HEADS, ROPE_HALF))
    k2b = jnp.broadcast_to(k2[:, None, :ROPE_HALF], (S, MLA_HEADS, ROPE_HALF))
    kp = jnp.concatenate([_heads_first(kv[:, :512], 64), k1b, k2b, zpad], axis=2)
    qp = qp.transpose(1, 0, 2).astype(bf16)
    kp = kp.transpose(1, 0, 2).astype(bf16)
    v3 = _heads_first(kv[:, 512:], 64).astype(bf16)
    vp = v3.transpose(1, 0, 2)
    oT, lse = flash_fwd(qp, kp, v3.transpose(1, 2, 0))
    ob = oT.transpose(2, 0, 1).reshape(S, MLA_HEADS * MLA_V)
    bu3 = mm(H, w['ssm_wb'], a_col0=H_US, out_s3=True, name="ssm_bu")
    hs3 = scan_fwd(bu3, w['ssm_a16'])
    y1 = mm(hs3, w['ssm_wc'], a_s3=True, name="ssm_c")
    y = rowwise(_mix_post, [ya, ob, y1, (H, H_US, SSM_WIDTH)], [w['ssm_d'], w['glu_w'], w['glu_b'], w['gmix']],
                [(D_MODEL, bf16)], [], _row_tile(S, 256), "mix_post")[0]
    f = mm(y, w['out'], name="mix_out")
    z, xo, xob = resid_ln(x, f, g, b, 1.0, "mix_ln")
    return xo, xob, (xb, H, cqn, ckvn, qp, kp, vp, lse, hs3, ya, ob, y1, y, z)


def mixer_bwd(dxo, res, w, g, b, cos8, sin8):
    xb, H, cqn, ckvn, qp, kp, vp, lse, hs3, ya, ob, y1, y, z = res
    S = z.shape[0]
    gr = {}
    dz, dzs, gr['ln_g'], gr['ln_b'] = ln_bwd(z, g, b, dxo, 1.0, "mix_ln_bwd")
    gr['w_out'] = mm_tn(y, dzs, name="mix_out_dw")
    dy = mm(dzs, w['outT'], name="mix_out_dx")

    def post_bwd(ya, ob, y1, us, dy, dskip, gluw, glub, gmix):
        _, vjp = jax.vjp(_mix_post, ya, ob, y1, us, dskip, gluw, glub, gmix)
        dya, dob, dy1, dus, *dpars = vjp(dy)
        prod = dob * ob
        col = lax.broadcasted_iota(jnp.int32, (1, MLA_HEADS * MLA_V), 1)
        lane = lax.broadcasted_iota(jnp.int32, (1, LANES), 1)
        delta = jnp.zeros((prod.shape[0], LANES), f32)
        for h in range(MLA_HEADS):
            in_head = ((col >= MLA_V * h) & (col < MLA_V * (h + 1))).astype(f32)
            delta = jnp.where(lane == h, jnp.sum(prod * in_head, axis=-1, keepdims=True), delta)
        return (dya, dob, dy1, dus, delta, *dpars)

    dya, dob, dy1, dus_skip, delta, gr['ssm_d'], gr['ssm_glu_w'], gr['ssm_glu_b'], gr['mix_norm_g'] = rowwise(
        post_bwd, [ya, ob, y1, (H, H_US, SSM_WIDTH), dy], [w['ssm_d'], w['glu_w'], w['glu_b'], w['gmix']],
        [(GM_WIDTH, f32), (MLA_HEADS * MLA_V, f32), (SSM_WIDTH, f32), (SSM_WIDTH, f32), (LANES, f32)],
        [(1, SSM_WIDTH), (SSM_WIDTH, SSM_WIDTH), (1, SSM_WIDTH), (1, D_MODEL)], _row_tile(S, 128), "mix_post_bwd")

    gr['ssm_wc'] = mm_tn(hs3, dy1, a_s3=True, name="ssm_c_dw")
    dhs3 = mm(dy1, w['ssm_wcT'], out_s3=True, name="ssm_c_dx")
    dbu3, gr['ssm_a16'] = scan_bwd(dhs3, hs3, w['ssm_a16'])
    gr['ssm_wb'] = mm_tn(H, dbu3, a_col0=H_US, m_dim=SSM_WIDTH, b_s3=True, name="ssm_bu_dw")
    dus = mm(dbu3, w['ssm_wbT'], add=dus_skip, add_scale=1.0, a_s3=True, name="ssm_bu_dx")

    do = _heads_first(dob, MLA_V).transpose(1, 0, 2)
    delta = delta[:, :MLA_HEADS].T.reshape(MLA_HEADS, 1, S)
    dqT, dkp, dvp = flash_bwd(qp, kp, kp.transpose(0, 2, 1), vp, do.astype(bf16), lse, delta)
    dqp = dqT.transpose(2, 0, 1)
    dkp = dkp.transpose(1, 0, 2)
    dv = dvp.transpose(1, 0, 2).reshape(S, MLA_HEADS * MLA_V)
    lane_pad = ((0, 0), (0, LANES - ROPE_HALF))
    dq1r = dqp[:, :, 64:80].reshape(S, LANES)
    dq2r = dqp[:, :, 80:96].reshape(S, LANES)
    dk1r = jnp.pad(jnp.sum(dkp[:, :, 64:80], axis=1), lane_pad)
    dk2r = jnp.pad(jnp.sum(dkp[:, :, 80:96], axis=1), lane_pad)

    def rope_bwd(d1, d2, d3, d4, cos, sin):
        return d1 * cos + d2 * sin, d2 * cos - d1 * sin, d3 * cos + d4 * sin, d4 * cos - d3 * sin

    dq1, dq2, dk1, dk2 = rowwise(rope_bwd, [dq1r, dq2r, dk1r, dk2r, cos8, sin8], [], [(LANES, f32)] * 4, [],
                                 _row_tile(S, 512), "rope_bwd")
    dqraw = jnp.concatenate([dqp[:, :, :64].reshape(S, 512), dq1, dq2], axis=1).astype(bf16)
    dkv = jnp.concatenate([dkp[:, :, :64].reshape(S, 512), dv], axis=1).astype(bf16)
    gr['uq'] = mm_tn(cqn, dqraw, name="mla_uq_dw")
    dcqn = mm(dqraw, w['uqT'], name="mla_uq_dx")
    gr['ukv'] = mm_tn(ckvn, dkv, name="mla_ukv_dw")
    dckvn = mm(dkv, w['ukvT'], name="mla_ukv_dx")

    def prep_bwd(cq, ckv, d1, d2, qg, kvg):
        _, vjp = jax.vjp(_mla_prep, cq, ckv, qg, kvg)
        return vjp((d1, d2))

    dcq, dckv, gr['mla_q_norm_g'], gr['mla_kv_norm_g'] = rowwise(
        prep_bwd, [(H, H_CQ, Q_LORA), (H, H_CKV, KV_LORA), dcqn, dckvn], [w['qg'], w['kvg']],
        [(Q_LORA, f32), (KV_LORA, f32)], [(1, Q_LORA), (1, KV_LORA)], _row_tile(S, 256), "mla_prep_bwd")

    def gmlp_bwd(hu, hv, dya, ng, ws, bsb):
        _, vjp = jax.vjp(_gmlp, hu, hv, ng, ws, bsb)
        return vjp(dya)

    dhu, dhv, gr['gmlp_norm_g'], gr['gmlp_ws'], gr['gm_bsb'] = rowwise(
        gmlp_bwd, [(H, H_UG, 256), (H, H_VG, 256), dya], [w['gm_ng'], w['gm_ws'], w['gm_bsb']],
        [(GM_WIDTH, f32), (GM_WIDTH, f32)], [(1, GM_WIDTH), (GM_HEADS, GM_CHUNK, GM_CHUNK), (GM_CHUNK, GM_WIDTH)],
        GM_CHUNK, "gmlp_bwd")

    dH = jnp.concatenate([dhu, dhv, dcq, dus, dckv, dk1, dk2], axis=1).astype(bf16)
    gr['in'] = mm_tn(xb, dH, name="mix_in_dw")
    dx = mm(dH, w['inT'], add=dz, add_scale=ALPHA, name="mix_in_dx")
    return dx, gr


def _block_diag(blocks):
    G, a, b = blocks.shape
    eye = jnp.eye(G, dtype=blocks.dtype)
    return (eye[:, None, :, None] * blocks[:, :, None, :]).reshape(G * a, G * b)


def _diag_blocks(mat, G):
    a, b = mat.shape[0] // G, mat.shape[1] // G
    m4 = mat.reshape(G, a, G, b)
    eye = jnp.eye(G, dtype=mat.dtype)
    return jnp.sum(m4 * eye[:, None, :, None], axis=2)


def prep_layer(W, rep, l):
    w = {}
    for f in ('ffn1', 'ffn2'):
        wg, wu = W[f + '_w_gate'], W[f + '_w_up']
        w[f] = {'g': wg, 'u': wu, 'guT': jnp.stack([wg.T, wu.T]), 'd': W[f + '_w_down'], 'dT': W[f + '_w_down'].T}
    wi = W['w_in']
    z112 = jnp.zeros((D_MODEL, LANES - ROPE_HALF), wi.dtype)
    w['in'] = jnp.concatenate([wi[:, :768], wi[:, 928:1184], wi[:, 768:896], wi[:, 896:912], z112, wi[:, 912:928], z112], axis=1)
    w['inT'] = w['in'].T
    uq = W['mla_w_uq'].reshape(Q_LORA, MLA_HEADS, MLA_NOPE + MLA_ROPE)
    w['uq'] = jnp.concatenate([uq[:, :, :64].reshape(Q_LORA, 512), uq[:, :, 64:80].reshape(Q_LORA, LANES),
                               uq[:, :, 80:96].reshape(Q_LORA, LANES)], axis=1)
    w['uqT'] = w['uq'].T
    ukv = W['mla_w_ukv'].reshape(KV_LORA, MLA_HEADS, MLA_NOPE + MLA_V)
    w['ukv'] = jnp.concatenate([ukv[:, :, :64].reshape(KV_LORA, 512), ukv[:, :, 64:].reshape(KV_LORA, 512)], axis=1)
    w['ukvT'] = w['ukv'].T
    w['out'] = W['w_out']
    w['outT'] = W['w_out'].T
    w['glu_w'] = W['ssm_glu_w']
    w['gm_ng'] = rep['gmlp_norm_g'][l].reshape(1, GM_WIDTH)
    w['gm_ws'] = rep['gmlp_ws'][l]
    w['gm_bsb'] = jnp.repeat(rep['gmlp_bs'][l].T, GM_HEAD_DIM, axis=1)
    w['qg'] = rep['mla_q_norm_g'][l].reshape(1, Q_LORA)
    w['kvg'] = rep['mla_kv_norm_g'][l].reshape(1, KV_LORA)
    w['ssm_d'] = rep['ssm_d'][l].reshape(1, SSM_WIDTH)
    w['glu_b'] = rep['ssm_glu_b'][l].reshape(1, SSM_WIDTH)
    w['gmix'] = rep['mix_norm_g'][l].reshape(1, D_MODEL)
    ar = rep['ssm_a_re'][l].reshape(1, N_STATE)
    ai = rep['ssm_a_im'][l].reshape(1, N_STATE)
    ldt = jnp.repeat(rep['ssm_log_dt'][l], SSM_STATE).reshape(1, N_STATE)
    brT = rep['ssm_b_re'][l].transpose(2, 0, 1).reshape(SSM_GROUP_CH, N_STATE)
    biT = rep['ssm_b_im'][l].transpose(2, 0, 1).reshape(SSM_GROUP_CH, N_STATE)
    w['ssm_prep_in'] = (ar, ai, ldt, brT, biT)
    abr, abi, bbrT, bbiT = whole(_ssm_prep, w['ssm_prep_in'], [(1, N_STATE)] * 2 + [(SSM_GROUP_CH, N_STATE)] * 2, "ssm_prep")
    w['ssm_a16'] = jnp.concatenate([abr.reshape(8, LANES), abi.reshape(8, LANES)], axis=0)

    def to_gcp(t):
        return t.reshape(SSM_GROUP_CH, SSM_GROUPS, SSM_STATE).transpose(1, 0, 2)

    w['ssm_wb'] = jnp.concatenate([_block_diag(to_gcp(bbrT)), _block_diag(to_gcp(bbiT))], axis=1).astype(bf16)
    w['ssm_wbT'] = w['ssm_wb'].T
    cre = rep['ssm_c_re'][l].transpose(0, 2, 1)
    cim = rep['ssm_c_im'][l].transpose(0, 2, 1)
    w['ssm_wc'] = jnp.concatenate([_block_diag(cre), -_block_diag(cim)], axis=0).astype(bf16)
    w['ssm_wcT'] = w['ssm_wc'].T
    return w


def unprep_grads(gr, w):
    out = {}
    for k in ('ln_g', 'ln_b', 'w_out', 'mla_q_norm_g', 'mla_kv_norm_g', 'ssm_glu_w', 'gmlp_ws'):
        out[k] = gr[k]
    out['gmlp_norm_g'] = gr['gmlp_norm_g'].reshape(GM_WIDTH)
    out['mla_q_norm_g'] = gr['mla_q_norm_g'].reshape(Q_LORA)
    out['mla_kv_norm_g'] = gr['mla_kv_norm_g'].reshape(KV_LORA)
    out['ssm_d'] = gr['ssm_d'].reshape(SSM_GROUPS, SSM_GROUP_CH)
    out['ssm_glu_b'] = gr['ssm_glu_b'].reshape(SSM_WIDTH)
    out['mix_norm_g'] = gr['mix_norm_g'].reshape(D_MODEL)
    out['gmlp_bs'] = gr['gm_bsb'].reshape(GM_CHUNK, GM_HEADS, GM_HEAD_DIM).sum(axis=-1).T
    d = gr['in']
    out['w_in'] = jnp.concatenate([d[:, :768], d[:, H_CKV:H_CKV + KV_LORA], d[:, H_K1:H_K1 + ROPE_HALF],
                                   d[:, H_K2:H_K2 + ROPE_HALF], d[:, H_US:H_US + SSM_WIDTH]], axis=1)
    d = gr['uq']
    out['mla_w_uq'] = jnp.concatenate([d[:, :512].reshape(Q_LORA, MLA_HEADS, 64), d[:, 512:640].reshape(Q_LORA, MLA_HEADS, ROPE_HALF),
                                       d[:, 640:768].reshape(Q_LORA, MLA_HEADS, ROPE_HALF)], axis=2).reshape(Q_LORA, 768)
    d = gr['ukv']
    out['mla_w_ukv'] = jnp.concatenate([d[:, :512].reshape(KV_LORA, MLA_HEADS, 64), d[:, 512:].reshape(KV_LORA, MLA_HEADS, 64)],
                                       axis=2).reshape(KV_LORA, 1024)
    dwc = gr['ssm_wc']
    out['ssm_c_re'] = _diag_blocks(dwc[:N_STATE], SSM_GROUPS).transpose(0, 2, 1)
    out['ssm_c_im'] = -_diag_blocks(dwc[N_STATE:], SSM_GROUPS).transpose(0, 2, 1)
    dwb = gr['ssm_wb']

    def from_blocks(m):
        return _diag_blocks(m, SSM_GROUPS).transpose(1, 0, 2).reshape(SSM_GROUP_CH, N_STATE)

    dbbrT, dbbiT = from_blocks(dwb[:, :N_STATE]), from_blocks(dwb[:, N_STATE:])
    da16 = gr['ssm_a16']
    dabr, dabi = da16[0:8].reshape(1, N_STATE), da16[8:16].reshape(1, N_STATE)

    def prep_bwd(ar, ai, ldt, brT, biT, d1, d2, d3, d4):
        _, vjp = jax.vjp(_ssm_prep, ar, ai, ldt, brT, biT)
        return vjp((d1, d2, d3, d4))

    dar, dai, dldt, dbrT, dbiT = whole(prep_bwd, w['ssm_prep_in'] + (dabr, dabi, dbbrT, dbbiT),
                                       [(1, N_STATE)] * 3 + [(SSM_GROUP_CH, N_STATE)] * 2, "ssm_prep_bwd")
    out['ssm_a_re'] = dar.reshape(SSM_GROUPS, SSM_STATE)
    out['ssm_a_im'] = dai.reshape(SSM_GROUPS, SSM_STATE)
    out['ssm_log_dt'] = dldt.reshape(SSM_GROUPS, SSM_STATE).sum(axis=-1)
    out['ssm_b_re'] = dbrT.reshape(SSM_GROUP_CH, SSM_GROUPS, SSM_STATE).transpose(1, 2, 0)
    out['ssm_b_im'] = dbiT.reshape(SSM_GROUP_CH, SSM_GROUPS, SSM_STATE).transpose(1, 2, 0)
    return out


def adamw(w, g, m, v, name):
    R, C = w.shape

    def fn(w, g, m, v):
        m = ADAM_B1 * m + (1.0 - ADAM_B1) * g
        v = ADAM_B2 * v + (1.0 - ADAM_B2) * jnp.square(g)
        m_hat = m / (1.0 - ADAM_B1 ** ADAM_STEP)
        v_hat = v / (1.0 - ADAM_B2 ** ADAM_STEP)
        delta = -ADAM_LR * (m_hat / (jnp.sqrt(v_hat) + ADAM_EPS) + ADAM_WD * w)
        return delta, m, v

    return rowwise(fn, [w, g, m, v], [], [(C, f32)] * 3, [], _pick(R, (256, 128, 64, 32, 16, 8)), name)


def kernel(x, positions, ln_g, ln_b, ffn1_w_gate, ffn1_w_up, ffn1_w_down, w_in, gmlp_norm_g, gmlp_ws, gmlp_bs, mla_q_norm_g, mla_w_uq, mla_kv_norm_g, mla_w_ukv, ssm_a_re, ssm_a_im, ssm_b_re, ssm_b_im, ssm_c_re, ssm_c_im, ssm_d, ssm_log_dt, ssm_glu_w, ssm_glu_b, mix_norm_g, w_out, ffn2_w_gate, ffn2_w_up, ffn2_w_down, loss_target, m_ln_g, m_ln_b, m_ffn1_w_gate, m_ffn1_w_up, m_ffn1_w_down, m_w_in, m_gmlp_norm_g, m_gmlp_ws, m_gmlp_bs, m_mla_q_norm_g, m_mla_w_uq, m_mla_kv_norm_g, m_mla_w_ukv, m_ssm_a_re, m_ssm_a_im, m_ssm_b_re, m_ssm_b_im, m_ssm_c_re, m_ssm_c_im, m_ssm_d, m_ssm_log_dt, m_ssm_glu_w, m_ssm_glu_b, m_mix_norm_g, m_w_out, m_ffn2_w_gate, m_ffn2_w_up, m_ffn2_w_down, v_ln_g, v_ln_b, v_ffn1_w_gate, v_ffn1_w_up, v_ffn1_w_down, v_w_in, v_gmlp_norm_g, v_gmlp_ws, v_gmlp_bs, v_mla_q_norm_g, v_mla_w_uq, v_mla_kv_norm_g, v_mla_w_ukv, v_ssm_a_re, v_ssm_a_im, v_ssm_b_re, v_ssm_b_im, v_ssm_c_re, v_ssm_c_im, v_ssm_d, v_ssm_log_dt, v_ssm_glu_w, v_ssm_glu_b, v_mix_norm_g, v_w_out, v_ffn2_w_gate, v_ffn2_w_up, v_ffn2_w_down):
    Wp = dict(zip(W_NAMES, (ln_g, ln_b, ffn1_w_gate, ffn1_w_up, ffn1_w_down, w_in, gmlp_norm_g, gmlp_ws, gmlp_bs, mla_q_norm_g, mla_w_uq, mla_kv_norm_g, mla_w_ukv, ssm_a_re, ssm_a_im, ssm_b_re, ssm_b_im, ssm_c_re, ssm_c_im, ssm_d, ssm_log_dt, ssm_glu_w, ssm_glu_b, mix_norm_g, w_out, ffn2_w_gate, ffn2_w_up, ffn2_w_down)))
    Mp = dict(zip(W_NAMES, (m_ln_g, m_ln_b, m_ffn1_w_gate, m_ffn1_w_up, m_ffn1_w_down, m_w_in, m_gmlp_norm_g, m_gmlp_ws, m_gmlp_bs, m_mla_q_norm_g, m_mla_w_uq, m_mla_kv_norm_g, m_mla_w_ukv, m_ssm_a_re, m_ssm_a_im, m_ssm_b_re, m_ssm_b_im, m_ssm_c_re, m_ssm_c_im, m_ssm_d, m_ssm_log_dt, m_ssm_glu_w, m_ssm_glu_b, m_mix_norm_g, m_w_out, m_ffn2_w_gate, m_ffn2_w_up, m_ffn2_w_down)))
    Vp = dict(zip(W_NAMES, (v_ln_g, v_ln_b, v_ffn1_w_gate, v_ffn1_w_up, v_ffn1_w_down, v_w_in, v_gmlp_norm_g, v_gmlp_ws, v_gmlp_bs, v_mla_q_norm_g, v_mla_w_uq, v_mla_kv_norm_g, v_mla_w_ukv, v_ssm_a_re, v_ssm_a_im, v_ssm_b_re, v_ssm_b_im, v_ssm_c_re, v_ssm_c_im, v_ssm_d, v_ssm_log_dt, v_ssm_glu_w, v_ssm_glu_b, v_mix_norm_g, v_w_out, v_ffn2_w_gate, v_ffn2_w_up, v_ffn2_w_down)))
    S = x.shape[1]
    my = _my_id()

    shard_rows = [Wp[n].astype(bf16).reshape(DEPTH * _pack_rows(n), PACK_W) for n in BIG]
    ln_flat = jnp.concatenate([Wp[n].reshape(-1) for n in LN_NAMES])
    *big_all, ln_all = all_gather(shard_rows + [_pad_rows(ln_flat, 8)], "gather_weights")
    big_all = dict(zip(BIG, big_all))
    ln_all = ln_all.reshape(N_DEV, -1)
    lnsz = DEPTH * 3 * (D_MODEL // N_DEV)
    ln_full = {}
    for t, n in enumerate(LN_NAMES):
        sh = ln_all[:, t * lnsz:(t + 1) * lnsz].reshape(N_DEV, DEPTH, 3, D_MODEL // N_DEV)
        ln_full[n] = sh.transpose(1, 2, 0, 3).reshape(DEPTH, 3, 1, D_MODEL)
    Wl = [{n: _join_from_devices(n, big_all[n][:, l * _pack_rows(n):(l + 1) * _pack_rows(n)]) for n in BIG}
          for l in range(DEPTH)]
    rep = {n: Wp[n] for n in REPL}

    inv_freq = 1.0 / (ROPE_BASE ** (jnp.arange(0, MLA_ROPE, 2, dtype=f32) / MLA_ROPE))
    ang = positions.astype(f32).reshape(S, 1) * inv_freq[None, :]
    cos8 = jnp.tile(jnp.cos(ang), (1, MLA_HEADS))
    sin8 = jnp.tile(jnp.sin(ang), (1, MLA_HEADS))

    xs = x.reshape(S, D_MODEL)
    xb = xs.astype(bf16)
    ws, saved = [], []
    for l in range(DEPTH):
        w = prep_layer(Wl[l], rep, l)
        lg, lb = ln_full['ln_g'][l], ln_full['ln_b'][l]
        xs, xb, r1 = ffn_fwd(xs, xb, w['ffn1'], lg[0], lb[0])
        xs, xb, r2 = mixer_fwd(xs, xb, w, lg[1], lb[1], cos8, sin8)
        xs, xb, r3 = ffn_fwd(xs, xb, w['ffn2'], lg[2], lb[2])
        ws.append(w)
        saved.append((r1, r2, r3))

    def loss_fn(y, t):
        d = y - t
        part = jnp.sum(jnp.mean(jnp.square(d), axis=-1, keepdims=True), axis=0, keepdims=True)
        return d * (1.0 / D_MODEL), 0.5 * part

    dx, loss_part = rowwise(loss_fn, [xs, loss_target.reshape(S, D_MODEL)], [], [(D_MODEL, f32)], [(1, 1)],
                            _row_tile(S, 512), "loss")
    loss = lax.psum(loss_part[0, 0], ("x", "y", "c"))

    grads = [None] * DEPTH
    for l in reversed(range(DEPTH)):
        w = ws[l]
        lg, lb = ln_full['ln_g'][l], ln_full['ln_b'][l]
        r1, r2, r3 = saved[l]
        dx, g2g, g2u, g2d, dg2, db2 = ffn_bwd(dx, r3, w['ffn2'], lg[2], lb[2])
        dx, gm = mixer_bwd(dx, r2, w, lg[1], lb[1], cos8, sin8)
        dx, g1g, g1u, g1d, dg0, db0 = ffn_bwd(dx, r1, w['ffn1'], lg[0], lb[0])
        g = unprep_grads(gm, w)
        g.update({'ffn1_w_gate': g1g, 'ffn1_w_up': g1u, 'ffn1_w_down': g1d,
                  'ffn2_w_gate': g2g, 'ffn2_w_up': g2u, 'ffn2_w_down': g2d})
        g['ln_g'] = jnp.concatenate([dg0, g['ln_g'], dg2], axis=0)
        g['ln_b'] = jnp.concatenate([db0, g['ln_b'], db2], axis=0)
        grads[l] = g
    grad_x = dx.reshape(1, S, D_MODEL)

    gpacks = [jnp.concatenate([_split_for_devices(n, grads[l][n].astype(bf16)) for l in range(DEPTH)], axis=1) for n in BIG]
    G = {n: sum_slots(got, "sum_" + n).reshape((DEPTH,) + _shard_shape(n))
         for n, got in zip(BIG, all_to_all(gpacks, "scatter_grads"))}
    small_names = LN_NAMES + REPL
    spack = jnp.concatenate([jnp.stack([grads[l][n] for l in range(DEPTH)]).reshape(-1) for n in small_names])
    n_small = spack.shape[0]
    gsmall = sum_slots(all_gather([_pad_rows(spack, 8)], "gather_small")[0], "sum_small").reshape(-1)

    off = 0
    for n in small_names:
        shp = (DEPTH, 3, D_MODEL) if n in LN_NAMES else Wp[n].shape
        sz = math.prod(shp)
        G[n] = gsmall[off:off + sz].reshape(shp)
        off += sz
    for n in LN_NAMES:
        G[n] = lax.dynamic_slice_in_dim(G[n], my * (D_MODEL // N_DEV), D_MODEL // N_DEV, axis=2)

    delta, new_m, new_v = {}, {}, {}
    for n in BIG:
        shp = Wp[n].shape
        two = (shp[0] * shp[1], shp[2])
        d_, m_, v_ = adamw(Wp[n].reshape(two), G[n].reshape(two), Mp[n].reshape(two), Vp[n].reshape(two), "adamw_" + n)
        delta[n], new_m[n], new_v[n] = d_.reshape(shp), m_.reshape(shp), v_.reshape(shp)

    def pack_small(src):
        return _pad_rows(jnp.concatenate([src[n].reshape(-1) for n in small_names]), 8)

    d_, m_, v_ = adamw(pack_small(Wp), pack_small(G), pack_small(Mp), pack_small(Vp), "adamw_small")
    d_, m_, v_ = d_.reshape(-1), m_.reshape(-1), v_.reshape(-1)
    off = 0
    for n in small_names:
        shp = Wp[n].shape
        sz = math.prod(shp)
        delta[n], new_m[n], new_v[n] = (t[off:off + sz].reshape(shp) for t in (d_, m_, v_))
        off += sz

    return (loss, grad_x, *[G[n] for n in W_NAMES], *[delta[n] for n in W_NAMES],
            *[new_m[n] for n in W_NAMES], *[new_v[n] for n in W_NAMES])
```

```python
import functools
import math

import jax
import jax.numpy as jnp
from jax import lax
from jax.experimental import pallas as pl
from jax.experimental.pallas import tpu as pltpu

f32 = jnp.float32
bf16 = jnp.bfloat16

D_MODEL = 1024
DEPTH = 4
D_FF = 2816
GM_HEADS, GM_HEAD_DIM, GM_WIDTH, GM_CHUNK = 4, 64, 256, 128
MLA_HEADS, MLA_NOPE, MLA_ROPE, MLA_V = 8, 64, 32, 64
ROPE_HALF = MLA_ROPE // 2
Q_LORA, KV_LORA = 256, 128
ROPE_BASE = 10000.0
SSM_GROUPS, SSM_GROUP_CH, SSM_WIDTH, SSM_STATE = 16, 16, 256, 64
N_STATE = SSM_GROUPS * SSM_STATE
ALPHA = (2 * DEPTH) ** 0.25
LN_EPS = 1e-5
RMS_EPS = 1e-6
NEG_BIG = -1e30
ATT_SCALE = (MLA_NOPE + MLA_ROPE) ** -0.5
ADAM_LR, ADAM_B1, ADAM_B2, ADAM_EPS, ADAM_WD, ADAM_STEP = 0.001, 0.9, 0.999, 1e-08, 0.01, 10

N_DEV = 8
LANES = 128
VMEM_LIMIT = 48 * 1024 * 1024
MM_TILE_BUDGET = 32 * 1024 * 1024
MESH = pl.DeviceIdType.MESH

H_UG, H_VG, H_CQ, H_US, H_CKV, H_K1, H_K2, H_COLS = 0, 256, 512, 768, 1024, 1152, 1280, 1408

W_NAMES = ['ln_g', 'ln_b', 'ffn1_w_gate', 'ffn1_w_up', 'ffn1_w_down', 'w_in', 'gmlp_norm_g', 'gmlp_ws', 'gmlp_bs',
           'mla_q_norm_g', 'mla_w_uq', 'mla_kv_norm_g', 'mla_w_ukv', 'ssm_a_re', 'ssm_a_im', 'ssm_b_re', 'ssm_b_im',
           'ssm_c_re', 'ssm_c_im', 'ssm_d', 'ssm_log_dt', 'ssm_glu_w', 'ssm_glu_b', 'mix_norm_g', 'w_out',
           'ffn2_w_gate', 'ffn2_w_up', 'ffn2_w_down']
BIG = {'ffn1_w_gate': 1, 'ffn1_w_up': 1, 'ffn1_w_down': 0, 'w_in': 1, 'mla_w_uq': 1, 'mla_w_ukv': 1,
       'ssm_glu_w': 0, 'w_out': 0, 'ffn2_w_gate': 1, 'ffn2_w_up': 1, 'ffn2_w_down': 0}
BIG_SHAPE = {'ffn1_w_gate': (D_MODEL, D_FF), 'ffn1_w_up': (D_MODEL, D_FF), 'ffn1_w_down': (D_FF, D_MODEL),
             'w_in': (D_MODEL, 1184), 'mla_w_uq': (Q_LORA, 768), 'mla_w_ukv': (KV_LORA, 1024),
             'ssm_glu_w': (SSM_WIDTH, SSM_WIDTH), 'w_out': (D_MODEL, D_MODEL),
             'ffn2_w_gate': (D_MODEL, D_FF), 'ffn2_w_up': (D_MODEL, D_FF), 'ffn2_w_down': (D_FF, D_MODEL)}
LN_NAMES = ['ln_g', 'ln_b']
REPL = [n for n in W_NAMES if n not in BIG and n not in LN_NAMES]


def _pick(n, cands):
    for c in cands:
        if n % c == 0:
            return c
    return n


def _params(sem):
    return pltpu.CompilerParams(dimension_semantics=sem, vmem_limit_bytes=VMEM_LIMIT)


S3_ROWS = 2 * N_STATE // LANES


def _from_s3(ref):
    return jnp.concatenate([ref[:, c, :] for c in range(S3_ROWS)], axis=1)


def _to_s3(ref, val):
    for c in range(S3_ROWS):
        ref[:, c, :] = val[:, c * LANES:(c + 1) * LANES].astype(ref.dtype)


def mm(a, b, *, out_dtype=f32, add=None, add_scale=1.0, a_col0=0, a_s3=False, out_s3=False, grouped=False, name):
    G = a.shape[0] if grouped else 1
    M = a.shape[1] if grouped else a.shape[0]
    K, N = b.shape[-2:]
    tn = N if out_s3 else _pick(N, (512, 384, 256) if N <= 1536 else (512, 384, 256, 128))
    tk = K if (K <= 2 * D_FF or a_s3) else _pick(K, (1024, 512, 256, 128))
    nk = G * (K // tk)
    assert not grouped or tk == K

    def tile_bytes(tm):
        return 2 * (tm * tk * a.dtype.itemsize + tk * tn * b.dtype.itemsize + tm * tn * 4 * (2 if add is not None else 1))

    tm = next(t for t in (1024, 512, 256, 128, 64, 32, 16, 8) if M % t == 0 and (tile_bytes(t) <= MM_TILE_BUDGET or t == 8))
    assert a_s3 or grouped or (a_col0 % tk == 0 and a_col0 + K <= a.shape[1])
    kb0 = a_col0 // tk
    has_add = add is not None

    def body(*refs):
        if has_add:
            a_ref, b_ref, add_ref, o_ref, acc = refs
        else:
            a_ref, b_ref, o_ref, acc = refs
        k = pl.program_id(2)
        a_val = _from_s3(a_ref) if a_s3 else a_ref[...]
        part = jnp.dot(a_val.astype(bf16), b_ref[...].astype(bf16), preferred_element_type=f32)

        def finish(total):
            if has_add:
                total = total + add_scale * add_ref[...]
            if out_s3:
                _to_s3(o_ref, total)
            else:
                o_ref[...] = total.astype(o_ref.dtype)

        if nk == 1:
            finish(part)
        else:
            @pl.when(k == 0)
            def _():
                acc[...] = part

            @pl.when(k > 0)
            def _():
                acc[...] += part

            @pl.when(k == nk - 1)
            def _():
                finish(acc[...])

    if a_s3:
        a_spec = pl.BlockSpec((tm, S3_ROWS, LANES), lambda i, j, k: (i, 0, 0))
    elif grouped:
        a_spec = pl.BlockSpec((None, tm, tk), lambda i, j, k: (k, i, 0))
    else:
        a_spec = pl.BlockSpec((tm, tk), lambda i, j, k: (i, kb0 + k))
    b_spec = pl.BlockSpec((None, tk, tn), lambda i, j, k: (k, 0, j)) if grouped else pl.BlockSpec((tk, tn), lambda i, j, k: (k, j))
    in_specs, ops = [a_spec, b_spec], [a, b]
    if has_add:
        in_specs.append(pl.BlockSpec((tm, tn), lambda i, j, k: (i, j)))
        ops.append(add)
    if out_s3:
        out_spec = pl.BlockSpec((tm, S3_ROWS, LANES), lambda i, j, k: (i, 0, 0))
        out_shape = jax.ShapeDtypeStruct((M, S3_ROWS, LANES), out_dtype)
    else:
        out_spec = pl.BlockSpec((tm, tn), lambda i, j, k: (i, j))
        out_shape = jax.ShapeDtypeStruct((M, N), out_dtype)
    return pl.pallas_call(
        body, grid=(M // tm, N // tn, nk), in_specs=in_specs, out_specs=out_spec, out_shape=out_shape,
        scratch_shapes=[pltpu.VMEM((tm, tn) if nk > 1 else (8, LANES), f32)],
        compiler_params=_params(("parallel", "parallel", "arbitrary")), name=name)(*ops)


def mm_tn(a, b, *, a_col0=0, m_dim=None, a_s3=False, b_s3=False, b_lead=None, name):
    K = a.shape[0]
    M = 2 * N_STATE if a_s3 else (a.shape[1] if m_dim is None else m_dim)
    N = 2 * N_STATE if b_s3 else b.shape[-1]
    tm = M if a_s3 else _pick(M, (1024, 768, 512, 384, 256, 128))
    tn = N if b_s3 else _pick(N, (H_COLS, 1024, 768, 512, 384, 256))

    def tile_bytes(tk):
        return 2 * tk * (tm * a.dtype.itemsize + tn * b.dtype.itemsize) + 3 * tm * tn * 4

    tk = next(t for t in (2048, 1024, 512, 256, 128, 64, 32, 16) if K % t == 0 and (tile_bytes(t) <= MM_TILE_BUDGET or t == 16))
    nk = K // tk
    assert a_col0 % tm == 0
    mb0 = a_col0 // tm

    def body(a_ref, b_ref, o_ref, acc):
        k = pl.program_id(2)
        a_val = _from_s3(a_ref) if a_s3 else a_ref[...]
        b_val = _from_s3(b_ref) if b_s3 else b_ref[...]
        part = lax.dot_general(a_val.astype(bf16), b_val.astype(bf16), (((0,), (0,)), ((), ())), preferred_element_type=f32)

        @pl.when(k == 0)
        def _():
            acc[...] = part

        @pl.when(k > 0)
        def _():
            acc[...] += part

        @pl.when(k == nk - 1)
        def _():
            o_ref[...] = acc[...]

    s3_spec = pl.BlockSpec((tk, S3_ROWS, LANES), lambda i, j, k: (k, 0, 0))
    if b_s3:
        b_spec = s3_spec
    elif b_lead is not None:
        b_spec = pl.BlockSpec((None, tk, tn), lambda i, j, k: (b_lead, k, j))
    else:
        b_spec = pl.BlockSpec((tk, tn), lambda i, j, k: (k, j))
    return pl.pallas_call(
        body, grid=(M // tm, N // tn, nk),
        in_specs=[s3_spec if a_s3 else pl.BlockSpec((tk, tm), lambda i, j, k: (k, mb0 + i)), b_spec],
        out_specs=pl.BlockSpec((tm, tn), lambda i, j, k: (i, j)),
        out_shape=jax.ShapeDtypeStruct((M, N), f32),
        scratch_shapes=[pltpu.VMEM((tm, tn), f32)],
        compiler_params=_params(("parallel", "parallel", "arbitrary")), name=name)(a, b)


def rowwise(fn, rows, pars, out_rows, out_accs, tm, name):
    first = rows[0]
    if isinstance(first, tuple):
        R = first[0].shape[1] if first[1] == 'lead' else first[0].shape[0]
    else:
        R = first.shape[0]
    assert R % tm == 0, (R, tm, name)
    n_rows, n_pars, n_or, n_oa = len(rows), len(pars), len(out_rows), len(out_accs)

    in_specs, ops = [], []
    for r in rows:
        if isinstance(r, tuple) and r[1] == 'lead':
            arr, _, kk = r
            in_specs.append(pl.BlockSpec((None, tm, arr.shape[2]), lambda i, kk=kk: (kk, i, 0)))
        elif isinstance(r, tuple):
            arr, c0, w = r
            assert c0 % w == 0
            in_specs.append(pl.BlockSpec((tm, w), lambda i, cb=c0 // w: (i, cb)))
        else:
            arr = r
            in_specs.append(pl.BlockSpec((tm, arr.shape[1]), lambda i: (i, 0)))
        ops.append(arr)
    for p in pars:
        in_specs.append(pl.BlockSpec(p.shape, lambda i, nd=p.ndim: (0,) * nd))
        ops.append(p)
    out_specs = [pl.BlockSpec((tm, w), lambda i: (i, 0)) for (w, _) in out_rows]
    out_specs += [pl.BlockSpec(s, lambda i, nd=len(s): (0,) * nd) for s in out_accs]
    out_shape = [jax.ShapeDtypeStruct((R, w), dt) for (w, dt) in out_rows]
    out_shape += [jax.ShapeDtypeStruct(s, f32) for s in out_accs]

    def body(*refs):
        ins = [r[...] for r in refs[:n_rows + n_pars]]
        o_refs = refs[n_rows + n_pars:]
        res = fn(*ins)
        if not isinstance(res, (tuple, list)):
            res = (res,)
        assert len(res) == n_or + n_oa, (len(res), n_or, n_oa, name)
        for o, v in zip(o_refs[:n_or], res[:n_or]):
            o[...] = v.astype(o.dtype)
        if n_oa:
            i = pl.program_id(0)

            @pl.when(i == 0)
            def _():
                for o, v in zip(o_refs[n_or:], res[n_or:]):
                    o[...] = v.astype(f32)

            @pl.when(i > 0)
            def _():
                for o, v in zip(o_refs[n_or:], res[n_or:]):
                    o[...] += v.astype(f32)

    return pl.pallas_call(
        body, grid=(R // tm,), in_specs=in_specs, out_specs=out_specs, out_shape=out_shape,
        compiler_params=_params(("arbitrary",)), name=name)(*ops)


def whole(fn, ins, out_shapes, name):
    def body(*refs):
        res = fn(*[r[...] for r in refs[:len(ins)]])
        for o, v in zip(refs[len(ins):], res):
            o[...] = v

    return pl.pallas_call(body, out_shape=[jax.ShapeDtypeStruct(s, f32) for s in out_shapes], name=name)(*ins)


@jax.custom_vjp
def _bdot(a, b):
    return jnp.dot(a.astype(bf16), b.astype(bf16), preferred_element_type=f32)


def _bdot_fwd(a, b):
    return _bdot(a, b), (a, b)


def _bdot_bwd(res, g):
    a, b = res
    gb = g.astype(bf16)
    da = lax.dot_general(gb, b.astype(bf16), (((1,), (1,)), ((), ())), preferred_element_type=f32)
    db = lax.dot_general(a.astype(bf16), gb, (((0,), (0,)), ((), ())), preferred_element_type=f32)
    return da, db


_bdot.defvjp(_bdot_fwd, _bdot_bwd)


def _ln(z, g, b):
    mu = jnp.mean(z, axis=-1, keepdims=True)
    var = jnp.mean(jnp.square(z - mu), axis=-1, keepdims=True)
    return (z - mu) * lax.rsqrt(var + LN_EPS) * g + b


def _rms_only(x):
    return x * lax.rsqrt(jnp.mean(jnp.square(x), axis=-1, keepdims=True) + RMS_EPS)


def _swiglu(a, b):
    return jax.nn.silu(a) * b


def _gmlp(hu, hv, ng, ws, bsb):
    u = jax.nn.gelu(hu)
    v = jax.nn.gelu(hv)
    lane = lax.broadcasted_iota(jnp.int32, (1, GM_WIDTH), 1)
    masks = [((lane >= GM_HEAD_DIM * h) & (lane < GM_HEAD_DIM * (h + 1))).astype(f32) for h in range(GM_HEADS)]
    mu = jnp.zeros_like(v)
    for m in masks:
        mu = mu + m * (jnp.sum(v * m, axis=-1, keepdims=True) / GM_HEAD_DIM)
    d = v - mu
    var = jnp.zeros_like(v)
    for m in masks:
        var = var + m * (jnp.sum(d * d * m, axis=-1, keepdims=True) / GM_HEAD_DIM)
    vn = d * lax.rsqrt(var + LN_EPS) * ng
    r = lax.broadcasted_iota(jnp.int32, (GM_CHUNK, GM_CHUNK), 0)
    c = lax.broadcasted_iota(jnp.int32, (GM_CHUNK, GM_CHUNK), 1)
    tril = (c <= r).astype(f32)
    z = bsb
    for h, m in enumerate(masks):
        z = z + _bdot(ws[h] * tril, vn * m)
    return u * z


def _mla_prep(cq, ckv, qg, kvg):
    return _rms_only(cq) * qg, _rms_only(ckv) * kvg


def _rope(q1, q2, k1, k2, cos, sin):
    return q1 * cos - q2 * sin, q2 * cos + q1 * sin, k1 * cos - k2 * sin, k2 * cos + k1 * sin


def _mix_post(ya, ob, y1, us, dskip, gluw, glub, gmix):
    y = jax.nn.gelu(y1 + dskip * us)
    yc = y * jax.nn.sigmoid(_bdot(y, gluw) + glub)
    return jnp.concatenate([_rms_only(ya), _rms_only(ob), _rms_only(yc)], axis=1) * gmix


def _ssm_prep(ar, ai, ldt, brT, biT):
    dt = jnp.exp(ldt)
    mag = jnp.exp(ar * dt)
    abr = mag * jnp.cos(ai * dt)
    abi = mag * jnp.sin(ai * dt)
    den = ar * ar + ai * ai
    cr = ((abr - 1.0) * ar + abi * ai) / den
    ci = (abi * ar - (abr - 1.0) * ai) / den
    return abr, abi, cr * brT - ci * biT, cr * biT + ci * brT


def _att_tile(S):
    return _pick(S, (512, 256, 128))


def _nt(a, b):
    return lax.dot_general(a, b, (((1,), (1,)), ((), ())), preferred_element_type=f32)


def _diag_keep(T):
    krow = lax.broadcasted_iota(jnp.int32, (T, T), 0)
    qcol = lax.broadcasted_iota(jnp.int32, (T, T), 1)
    return qcol >= krow


def grid_call(body, exch, *, grid, in_specs, out_specs, out_shape, scratch_shapes, name, args):
    params = _params(("arbitrary", "arbitrary"))
    if exch is None:
        return pl.pallas_call(body, grid=grid, in_specs=in_specs, out_specs=out_specs, out_shape=out_shape,
                              scratch_shapes=scratch_shapes, compiler_params=params, name=name)(*args)
    gather, xs = exch
    n, n_in, n_out, n_sc = len(xs), len(in_specs), len(out_specs), len(scratch_shapes)

    def riding(*refs):
        ins, x_refs = refs[:n_in], refs[n_in:n_in + n]
        outs, xo_refs = refs[n_in + n:n_in + n + n_out], refs[n_in + n + n_out:n_in + 2 * n + n_out]
        rest = refs[n_in + 2 * n + n_out:]
        scratch, sems = rest[:n_sc], rest[n_sc:]
        h, i = pl.program_id(0), pl.program_id(1)

        @pl.when((h == 0) & (i == 0))
        def _():
            _direct_exchange(gather, x_refs, xo_refs, *sems, start=True)

        body(*ins, *outs, *scratch)

        @pl.when((h == grid[0] - 1) & (i == grid[1] - 1))
        def _():
            _direct_exchange(gather, x_refs, xo_refs, *sems, start=False)

    return pl.pallas_call(
        riding, grid=grid, in_specs=list(in_specs) + [HBM_SPEC] * n, out_specs=list(out_specs) + [HBM_SPEC] * n,
        out_shape=list(out_shape) + _exchange_out_shapes(gather, xs),
        scratch_shapes=list(scratch_shapes) + _exchange_scratch(n), compiler_params=params,
        name=name + ("_gather" if gather else "_scatter"))(*args, *xs)


def flash_fwd(q, k, vT, exch=None):
    Hh, S, _ = q.shape
    T = _att_tile(S)

    def body(q_ref, k_ref, vT_ref, o_ref, lse_ref, m_sc, l_sc, acc_sc):
        i = pl.program_id(1)
        qi = q_ref[0]
        m_sc[...] = jnp.full_like(m_sc, NEG_BIG)
        l_sc[...] = jnp.zeros_like(l_sc)
        acc_sc[...] = jnp.zeros_like(acc_sc)

        def block(j, diagonal):
            rows = pl.ds(pl.multiple_of(j * T, T), T)
            sT = _nt(k_ref[0, rows, :], qi) * ATT_SCALE
            if diagonal:
                sT = jnp.where(_diag_keep(T), sT, NEG_BIG)
            m_old = m_sc[...]
            m_new = jnp.maximum(m_old, jnp.max(sT, axis=0, keepdims=True))
            alpha = jnp.exp(m_old - m_new)
            pT = jnp.exp(sT - m_new)
            l_sc[...] = alpha * l_sc[...] + jnp.sum(pT, axis=0, keepdims=True)
            acc_sc[...] = alpha * acc_sc[...] + jnp.dot(vT_ref[0, :, rows], pT.astype(bf16), preferred_element_type=f32)
            m_sc[...] = m_new

        def loop_body(j, c):
            block(j, False)
            return c

        lax.fori_loop(0, i, loop_body, 0)
        block(i, True)
        o_ref[0] = acc_sc[...] / l_sc[...]
        lse_ref[0] = m_sc[...] + jnp.log(l_sc[...])

    return grid_call(
        body, exch, grid=(Hh, S // T),
        in_specs=[pl.BlockSpec((1, T, LANES), lambda h, i: (h, i, 0)), pl.BlockSpec((1, S, LANES), lambda h, i: (h, 0, 0)),
                  pl.BlockSpec((1, MLA_V, S), lambda h, i: (h, 0, 0))],
        out_specs=[pl.BlockSpec((1, MLA_V, T), lambda h, i: (h, 0, i)), pl.BlockSpec((1, 1, T), lambda h, i: (h, 0, i))],
        out_shape=[jax.ShapeDtypeStruct((Hh, MLA_V, S), f32), jax.ShapeDtypeStruct((Hh, 1, S), f32)],
        scratch_shapes=[pltpu.VMEM((1, T), f32), pltpu.VMEM((1, T), f32), pltpu.VMEM((MLA_V, T), f32)],
        name="flash_fwd", args=(q, k, vT))


def flash_bwd(q, k, kT, v, do, lse, delta, exch=None):
    Hh, S, _ = q.shape
    T = _att_tile(S)

    def body(q_ref, do_ref, lse_ref, dl_ref, k_ref, kT_ref, v_ref, dq_ref, dk_ref, dv_ref, dq_sc):
        i = pl.program_id(1)

        @pl.when(i == 0)
        def _():
            dk_ref[...] = jnp.zeros_like(dk_ref)
            dv_ref[...] = jnp.zeros_like(dv_ref)

        qi, doi = q_ref[0], do_ref[0]
        lse_i, dl_i = lse_ref[0], dl_ref[0]
        dq_sc[...] = jnp.zeros_like(dq_sc)

        def block(j, diagonal):
            rows = pl.ds(pl.multiple_of(j * T, T), T)
            sT = _nt(k_ref[0, rows, :], qi) * ATT_SCALE
            pT = jnp.exp(sT - lse_i)
            if diagonal:
                pT = jnp.where(_diag_keep(T), pT, 0.0)
            dpT = _nt(v_ref[0, rows, :], doi)
            dsT = (pT * (dpT - dl_i) * ATT_SCALE).astype(bf16)
            dv_ref[0, rows, :] += jnp.dot(pT.astype(bf16), doi, preferred_element_type=f32)
            dk_ref[0, rows, :] += jnp.dot(dsT, qi, preferred_element_type=f32)
            dq_sc[...] += jnp.dot(kT_ref[0, :, rows], dsT, preferred_element_type=f32)

        def loop_body(j, c):
            block(j, False)
            return c

        lax.fori_loop(0, i, loop_body, 0)
        block(i, True)
        dq_ref[0] = dq_sc[...]

    tile = lambda w: pl.BlockSpec((1, T, w), lambda h, i: (h, i, 0))
    row = pl.BlockSpec((1, 1, T), lambda h, i: (h, 0, i))
    full = lambda w: pl.BlockSpec((1, S, w), lambda h, i: (h, 0, 0))
    return grid_call(
        body, exch, grid=(Hh, S // T),
        in_specs=[tile(LANES), tile(MLA_V), row, row, full(LANES), pl.BlockSpec((1, LANES, S), lambda h, i: (h, 0, 0)), full(MLA_V)],
        out_specs=[pl.BlockSpec((1, LANES, T), lambda h, i: (h, 0, i)), full(LANES), full(MLA_V)],
        out_shape=[jax.ShapeDtypeStruct((Hh, LANES, S), f32), jax.ShapeDtypeStruct((Hh, S, LANES), f32),
                   jax.ShapeDtypeStruct((Hh, S, MLA_V), f32)],
        scratch_shapes=[pltpu.VMEM((LANES, T), f32)],
        name="flash_bwd", args=(q, do, lse, delta, k, kT, v))


def _scan_tile(S):
    return _pick(S, (256, 128, 64, 32, 16, 8))


def scan_fwd(bu3, a16):
    S = bu3.shape[0]
    ts = _scan_tile(S)

    def body(bu_ref, a_ref, o_ref, h_sc):
        @pl.when(pl.program_id(0) == 0)
        def _():
            h_sc[...] = jnp.zeros_like(h_sc)

        ar, ai = a_ref[0:8, :], a_ref[8:16, :]

        def step(t, carry):
            hr, hi = carry
            nr = ar * hr - ai * hi + bu_ref[t, 0:8, :]
            ni = ar * hi + ai * hr + bu_ref[t, 8:16, :]
            o_ref[t, 0:8, :] = nr
            o_ref[t, 8:16, :] = ni
            return nr, ni

        hr, hi = lax.fori_loop(0, ts, step, (h_sc[0:8, :], h_sc[8:16, :]), unroll=8)
        h_sc[0:8, :] = hr
        h_sc[8:16, :] = hi

    blk = pl.BlockSpec((ts, 16, LANES), lambda i: (i, 0, 0))
    return pl.pallas_call(
        body, grid=(S // ts,), in_specs=[blk, pl.BlockSpec((16, LANES), lambda i: (0, 0))], out_specs=blk,
        out_shape=jax.ShapeDtypeStruct(bu3.shape, f32), scratch_shapes=[pltpu.VMEM((16, LANES), f32)],
        compiler_params=_params(("arbitrary",)), name="scan_fwd")(bu3, a16)


def scan_bwd(g3, h3, a16):
    S = g3.shape[0]
    ts = _scan_tile(S)
    nb = S // ts

    def body(g_ref, h_ref, a_ref, o_ref, da_ref, lam_sc, da_sc):
        @pl.when(pl.program_id(0) == 0)
        def _():
            lam_sc[...] = jnp.zeros_like(lam_sc)
            da_sc[...] = jnp.zeros_like(da_sc)

        ar, ai = a_ref[0:8, :], a_ref[8:16, :]

        def step(kk, carry):
            lr, li, dar, dai = carry
            t = ts - 1 - kk
            hr, hi = h_ref[t, 0:8, :], h_ref[t, 8:16, :]
            dar = dar + lr * hr + li * hi
            dai = dai + li * hr - lr * hi
            nlr = ar * lr + ai * li + g_ref[t, 0:8, :]
            nli = ar * li - ai * lr + g_ref[t, 8:16, :]
            o_ref[t, 0:8, :] = nlr
            o_ref[t, 8:16, :] = nli
            return nlr, nli, dar, dai

        lr, li, dar, dai = lax.fori_loop(
            0, ts, step, (lam_sc[0:8, :], lam_sc[8:16, :], da_sc[0:8, :], da_sc[8:16, :]), unroll=8)
        lam_sc[0:8, :] = lr
        lam_sc[8:16, :] = li
        da_sc[0:8, :] = dar
        da_sc[8:16, :] = dai
        da_ref[0:8, :] = dar
        da_ref[8:16, :] = dai

    blk = pl.BlockSpec((ts, 16, LANES), lambda i: (nb - 1 - i, 0, 0))
    small = pl.BlockSpec((16, LANES), lambda i: (0, 0))
    return pl.pallas_call(
        body, grid=(nb,), in_specs=[blk, blk, small], out_specs=[blk, small],
        out_shape=[jax.ShapeDtypeStruct(g3.shape, f32), jax.ShapeDtypeStruct((16, LANES), f32)],
        scratch_shapes=[pltpu.VMEM((16, LANES), f32), pltpu.VMEM((16, LANES), f32)],
        compiler_params=_params(("arbitrary",)), name="scan_bwd")(g3, h3, a16)


HBM_SPEC = pl.BlockSpec(memory_space=pltpu.HBM)


def _my_id():
    return 4 * lax.axis_index("x") + 2 * lax.axis_index("y") + lax.axis_index("c")


def all_gather(xs, name):
    n = len(xs)

    def body(*refs):
        x_refs, o_refs = refs[:n], refs[n:2 * n]
        send_sems, recv_sems, local_sems = refs[2 * n:]
        x, y, c = lax.axis_index("x"), lax.axis_index("y"), lax.axis_index("c")
        me, sibling = (x, y, c), (x, y, 1 - c)
        chips = [(1 - x, y), (x, 1 - y), (1 - x, 1 - y)]

        def slot(o, p):
            return o.at[4 * p[0] + 2 * p[1] + p[2]]

        def copy(a, k, block, to, src=None):
            o = o_refs[a]
            return pltpu.make_async_remote_copy(
                src_ref=slot(o, block) if src is None else src, dst_ref=slot(o, block),
                send_sem=send_sems.at[7 * a + k], recv_sem=recv_sems.at[7 * a + k], device_id=to, device_id_type=MESH)

        own, sends = [], []
        for a in range(n):
            mine = pltpu.make_async_copy(x_refs[a], slot(o_refs[a], me), local_sems.at[a])
            mine.start()
            own.append(mine)
            first = [copy(a, 0, me, sibling, src=x_refs[a])]
            first += [copy(a, 1 + j, me, (*chip, c), src=x_refs[a]) for j, chip in enumerate(chips)]
            for cp in first:
                cp.start()
            sends += first
        for a in range(n):
            for j, chip in enumerate(chips):
                copy(a, 1 + j, (*chip, c), me).wait_recv()
                fwd = copy(a, 4 + j, (*chip, c), sibling)
                fwd.start()
                sends.append(fwd)
        for a in range(n):
            copy(a, 0, sibling, me).wait_recv()
            for j, chip in enumerate(chips):
                copy(a, 4 + j, (*chip, 1 - c), me).wait_recv()
        for cp in sends:
            cp.wait_send()
        for cp in own:
            cp.wait()

    return pl.pallas_call(
        body, out_shape=[jax.ShapeDtypeStruct((N_DEV,) + v.shape, v.dtype) for v in xs],
        in_specs=[HBM_SPEC] * n, out_specs=[HBM_SPEC] * n,
        scratch_shapes=[pltpu.SemaphoreType.DMA((7 * n,)), pltpu.SemaphoreType.DMA((7 * n,)), pltpu.SemaphoreType.DMA((n,))],
        name=name)(*xs)


def _direct_exchange(gather, x_refs, o_refs, send_sems, recv_sems, local_sems, start):
    x, y, c = lax.axis_index("x"), lax.axis_index("y"), lax.axis_index("c")
    my = 4 * x + 2 * y + c
    for a, (x_ref, o_ref) in enumerate(zip(x_refs, o_refs)):
        mine = pltpu.make_async_copy(x_ref if gather else x_ref.at[my], o_ref.at[my], local_sems.at[a])
        if start:
            mine.start()
        else:
            mine.wait()
        for k in range(1, N_DEV):
            px = 1 - x if k & 4 else x
            py = 1 - y if k & 2 else y
            pc = 1 - c if k & 1 else c
            pid = 4 * px + 2 * py + pc
            src = x_ref if gather else x_ref.at[pid]
            sems = dict(send_sem=send_sems.at[7 * a + k - 1], recv_sem=recv_sems.at[7 * a + k - 1],
                        device_id=(px, py, pc), device_id_type=MESH)
            if start:
                pltpu.make_async_remote_copy(src_ref=src, dst_ref=o_ref.at[my], **sems).start()
            else:
                pltpu.make_async_remote_copy(src_ref=src, dst_ref=o_ref.at[my], **sems).wait_send()
                pltpu.make_async_remote_copy(src_ref=src, dst_ref=o_ref.at[pid], **sems).wait_recv()


def _exchange_scratch(n):
    return [pltpu.SemaphoreType.DMA((7 * n,)), pltpu.SemaphoreType.DMA((7 * n,)), pltpu.SemaphoreType.DMA((n,))]


def _exchange_out_shapes(gather, xs):
    return [jax.ShapeDtypeStruct(((N_DEV,) + v.shape) if gather else v.shape, v.dtype) for v in xs]


def all_to_all(xs, name):
    n = len(xs)

    def body(*refs):
        ex = (False, refs[:n], refs[n:2 * n], *refs[2 * n:])
        _direct_exchange(*ex, start=True)
        _direct_exchange(*ex, start=False)

    return pl.pallas_call(
        body, out_shape=_exchange_out_shapes(False, xs), in_specs=[HBM_SPEC] * n, out_specs=[HBM_SPEC] * n,
        scratch_shapes=_exchange_scratch(n), name=name)(*xs)


def sum_slots(g8, name):
    R, C = g8.shape[1:]

    def fn(*tiles):
        tot = tiles[0].astype(f32)
        for t in tiles[1:]:
            tot = tot + t.astype(f32)
        return tot

    return rowwise(fn, [(g8, 'lead', k) for k in range(N_DEV)], [], [(C, f32)], [], _pick(R, (256, 128, 64, 32, 16, 8)), name)[0]


PACK_W = 1024


def _pad_rows(flat, mult):
    n = flat.shape[0]
    tot = -(-n // (PACK_W * mult)) * PACK_W * mult
    return jnp.pad(flat, (0, tot - n)).reshape(tot // PACK_W, PACK_W)


BF16_TILE_ROWS = 16


def _pad_tile_rows(a, axis):
    pad = [(0, 0)] * a.ndim
    pad[axis] = (0, -a.shape[axis] % BF16_TILE_ROWS)
    return jnp.pad(a, pad)


def _shard_shape(name):
    r, c = BIG_SHAPE[name]
    return (r // N_DEV, c) if BIG[name] == 0 else (r, c // N_DEV)


def _pack_rows(name):
    r, c = _shard_shape(name)
    assert (r * c) % PACK_W == 0
    return r * c // PACK_W


def _split_for_devices(name, full):
    r, c = BIG_SHAPE[name]
    if BIG[name] == 0:
        return full.reshape(N_DEV, _pack_rows(name), PACK_W)
    return full.reshape(r, N_DEV, c // N_DEV).transpose(1, 0, 2).reshape(N_DEV, _pack_rows(name), PACK_W)


def _join_from_devices(name, parts):
    r, c = BIG_SHAPE[name]
    if BIG[name] == 0:
        return parts.reshape(r, c)
    return parts.reshape(N_DEV, r, c // N_DEV).transpose(1, 0, 2).reshape(r, c)


def _row_tile(S, want):
    return _pick(S, tuple(t for t in (512, 256, 128, 64, 32, 16) if t <= want))


def resid_ln(x, f, g, b, scale, name):
    D = x.shape[1]

    def fn(x, f, g, b):
        z = ALPHA * x + scale * f
        xo = _ln(z, g, b)
        return z, xo, xo

    return rowwise(fn, [x, f], [g, b], [(D, f32), (D, f32), (D, bf16)], [], _row_tile(x.shape[0], 512), name)


def ln_bwd(z, g, b, dxo, scale, name):
    D = z.shape[1]

    def fn(z, dxo, g, b):
        _, vjp = jax.vjp(_ln, z, g, b)
        dz, dg, db = vjp(dxo)
        return dz, scale * dz, dg, db

    return rowwise(fn, [z, dxo], [g, b], [(D, f32), (D, bf16)], [(1, D), (1, D)], _row_tile(z.shape[0], 256), name)


FF_TILE = 256


def ffn_up_act(xb, wg, wu):
    M, K = xb.shape
    tm = _pick(M, (1024, 512, 256, 128, 64, 32, 16))

    def body(x_ref, wg_ref, wu_ref, ab_ref, h_ref):
        x = x_ref[...]
        a = jnp.dot(x, wg_ref[...], preferred_element_type=f32)
        b = jnp.dot(x, wu_ref[...], preferred_element_type=f32)
        ab_ref[0] = a
        ab_ref[1] = b
        h_ref[...] = _swiglu(a, b).astype(bf16)

    wspec = pl.BlockSpec((K, FF_TILE), lambda i, j: (0, j))
    return pl.pallas_call(
        body, grid=(M // tm, D_FF // FF_TILE),
        in_specs=[pl.BlockSpec((tm, K), lambda i, j: (i, 0)), wspec, wspec],
        out_specs=[pl.BlockSpec((2, tm, FF_TILE), lambda i, j: (0, i, j)), pl.BlockSpec((tm, FF_TILE), lambda i, j: (i, j))],
        out_shape=[jax.ShapeDtypeStruct((2, M, D_FF), f32), jax.ShapeDtypeStruct((M, D_FF), bf16)],
        compiler_params=_params(("parallel", "parallel")), name="ffn_up_act")(xb, wg, wu)


def ffn_down_dx_act(dzs, wdT, ab):
    M, K = dzs.shape
    tm = _pick(M, (1024, 512, 256, 128, 64, 32, 16))

    def body(dz_ref, w_ref, ab_ref, dab_ref, h_ref):
        dh = jnp.dot(dz_ref[...], w_ref[...], preferred_element_type=f32)
        h, vjp = jax.vjp(_swiglu, ab_ref[0], ab_ref[1])
        da, db = vjp(dh)
        dab_ref[0] = da.astype(bf16)
        dab_ref[1] = db.astype(bf16)
        h_ref[...] = h.astype(bf16)

    pair = pl.BlockSpec((2, tm, FF_TILE), lambda i, j: (0, i, j))
    return pl.pallas_call(
        body, grid=(M // tm, D_FF // FF_TILE),
        in_specs=[pl.BlockSpec((tm, K), lambda i, j: (i, 0)), pl.BlockSpec((K, FF_TILE), lambda i, j: (0, j)), pair],
        out_specs=[pair, pl.BlockSpec((tm, FF_TILE), lambda i, j: (i, j))],
        out_shape=[jax.ShapeDtypeStruct((2, M, D_FF), bf16), jax.ShapeDtypeStruct((M, D_FF), bf16)],
        compiler_params=_params(("parallel", "parallel")), name="ffn_down_dx_act")(dzs, wdT, ab)


def ffn_fwd(x, xb, w, g, b):
    ab, h = ffn_up_act(xb, w['g'], w['u'])
    f = mm(h, w['d'], name="ffn_down")
    z, xo, xob = resid_ln(x, f, g, b, 0.5, "ffn_ln")
    return xo, xob, (xb, ab, z)


def ffn_bwd(dxo, res, w, g, b):
    xb, ab, z = res
    dz, dzs, dg, db = ln_bwd(z, g, b, dxo, 0.5, "ffn_ln_bwd")
    dab, h = ffn_down_dx_act(dzs, w['dT'], ab)
    dwd = mm_tn(h, dzs, name="ffn_down_dw")
    dwg = mm_tn(xb, dab, b_lead=0, name="ffn_gate_dw")
    dwu = mm_tn(xb, dab, b_lead=1, name="ffn_up_dw")
    dx = mm(dab, w['guT'], add=dz, add_scale=ALPHA, grouped=True, name="ffn_up_dx")
    return dx, dwg, dwu, dwd, dg, db


def _heads_first(a, width):
    return a.reshape(a.shape[0], MLA_HEADS, width)


def mixer_fwd(x, xb, w, g, b, cos8, sin8, exch=None):
    S = x.shape[0]
    H = mm(xb, w['in'], name="mix_in")
    ya = rowwise(_gmlp, [(H, H_UG, 256), (H, H_VG, 256)], [w['gm_ng'], w['gm_ws'], w['gm_bsb']], [(GM_WIDTH, f32)], [],
                 GM_CHUNK, "gmlp")[0]
    cqn, ckvn = rowwise(_mla_prep, [(H, H_CQ, Q_LORA), (H, H_CKV, KV_LORA)], [w['qg'], w['kvg']],
                        [(Q_LORA, bf16), (KV_LORA, bf16)], [], _row_tile(S, 512), "mla_prep")
    qraw = mm(cqn, w['uq'], name="mla_uq")
    kv = mm(ckvn, w['ukv'], name="mla_ukv")
    q1, q2, k1, k2 = rowwise(_rope, [(qraw, 512, LANES), (qraw, 640, LANES), (H, H_K1, LANES), (H, H_K2, LANES), cos8, sin8],
                             [], [(LANES, f32)] * 4, [], _row_tile(S, 512), "rope")
    zpad = jnp.zeros((S, MLA_HEADS, LANES - MLA_NOPE - MLA_ROPE), f32)
    qp = jnp.concatenate([_heads_first(qraw[:, :512], 64), _heads_first(q1, ROPE_HALF), _heads_first(q2, ROPE_HALF), zpad], axis=2)
    k1b = jnp.broadcast_to(k1[:, None, :ROPE_HALF], (S, MLA_HEADS, ROPE_HALF))
    k2b = jnp.broadcast_to(k2[:, None, :ROPE_HALF], (S, MLA_HEADS, ROPE_HALF))
    kp = jnp.concatenate([_heads_first(kv[:, :512], 64), k1b, k2b, zpad], axis=2)
    qp = qp.transpose(1, 0, 2).astype(bf16)
    kp = kp.transpose(1, 0, 2).astype(bf16)
    v3 = _heads_first(kv[:, 512:], 64).astype(bf16)
    vp = v3.transpose(1, 0, 2)
    oT, lse, *received = flash_fwd(qp, kp, v3.transpose(1, 2, 0), exch)
    ob = oT.transpose(2, 0, 1).reshape(S, MLA_HEADS * MLA_V)
    bu3 = mm(H, w['ssm_wb'], a_col0=H_US, out_s3=True, name="ssm_bu")
    hs3 = scan_fwd(bu3, w['ssm_a16'])
    y1 = mm(hs3, w['ssm_wc'], a_s3=True, name="ssm_c")
    y = rowwise(_mix_post, [ya, ob, y1, (H, H_US, SSM_WIDTH)], [w['ssm_d'], w['glu_w'], w['glu_b'], w['gmix']],
                [(D_MODEL, bf16)], [], _row_tile(S, 256), "mix_post")[0]
    f = mm(y, w['out'], name="mix_out")
    z, xo, xob = resid_ln(x, f, g, b, 1.0, "mix_ln")
    return xo, xob, (xb, H, cqn, ckvn, qp, kp, vp, lse, hs3, ya, ob, y1, y, z), received


def mixer_bwd(dxo, res, w, g, b, cos8, sin8, exch=None):
    xb, H, cqn, ckvn, qp, kp, vp, lse, hs3, ya, ob, y1, y, z = res
    S = z.shape[0]
    gr = {}
    dz, dzs, gr['ln_g'], gr['ln_b'] = ln_bwd(z, g, b, dxo, 1.0, "mix_ln_bwd")
    gr['w_out'] = mm_tn(y, dzs, name="mix_out_dw")
    dy = mm(dzs, w['outT'], name="mix_out_dx")

    def post_bwd(ya, ob, y1, us, dy, dskip, gluw, glub, gmix):
        _, vjp = jax.vjp(_mix_post, ya, ob, y1, us, dskip, gluw, glub, gmix)
        dya, dob, dy1, dus, *dpars = vjp(dy)
        prod = dob * ob
        col = lax.broadcasted_iota(jnp.int32, (1, MLA_HEADS * MLA_V), 1)
        lane = lax.broadcasted_iota(jnp.int32, (1, LANES), 1)
        delta = jnp.zeros((prod.shape[0], LANES), f32)
        for h in range(MLA_HEADS):
            in_head = ((col >= MLA_V * h) & (col < MLA_V * (h + 1))).astype(f32)
            delta = jnp.where(lane == h, jnp.sum(prod * in_head, axis=-1, keepdims=True), delta)
        return (dya, dob, dy1, dus, delta, *dpars)

    dya, dob, dy1, dus_skip, delta, gr['ssm_d'], gr['ssm_glu_w'], gr['ssm_glu_b'], gr['mix_norm_g'] = rowwise(
        post_bwd, [ya, ob, y1, (H, H_US, SSM_WIDTH), dy], [w['ssm_d'], w['glu_w'], w['glu_b'], w['gmix']],
        [(GM_WIDTH, f32), (MLA_HEADS * MLA_V, f32), (SSM_WIDTH, f32), (SSM_WIDTH, f32), (LANES, f32)],
        [(1, SSM_WIDTH), (SSM_WIDTH, SSM_WIDTH), (1, SSM_WIDTH), (1, D_MODEL)], _row_tile(S, 128), "mix_post_bwd")

    gr['ssm_wc'] = mm_tn(hs3, dy1, a_s3=True, name="ssm_c_dw")
    dhs3 = mm(dy1, w['ssm_wcT'], out_s3=True, name="ssm_c_dx")
    dbu3, gr['ssm_a16'] = scan_bwd(dhs3, hs3, w['ssm_a16'])
    gr['ssm_wb'] = mm_tn(H, dbu3, a_col0=H_US, m_dim=SSM_WIDTH, b_s3=True, name="ssm_bu_dw")
    dus = mm(dbu3, w['ssm_wbT'], add=dus_skip, add_scale=1.0, a_s3=True, name="ssm_bu_dx")

    do = _heads_first(dob, MLA_V).transpose(1, 0, 2)
    delta = delta[:, :MLA_HEADS].T.reshape(MLA_HEADS, 1, S)
    dqT, dkp, dvp, *received = flash_bwd(qp, kp, kp.transpose(0, 2, 1), vp, do.astype(bf16), lse, delta, exch)
    dqp = dqT.transpose(2, 0, 1)
    dkp = dkp.transpose(1, 0, 2)
    dv = dvp.transpose(1, 0, 2).reshape(S, MLA_HEADS * MLA_V)
    lane_pad = ((0, 0), (0, LANES - ROPE_HALF))
    dq1r = dqp[:, :, 64:80].reshape(S, LANES)
    dq2r = dqp[:, :, 80:96].reshape(S, LANES)
    dk1r = jnp.pad(jnp.sum(dkp[:, :, 64:80], axis=1), lane_pad)
    dk2r = jnp.pad(jnp.sum(dkp[:, :, 80:96], axis=1), lane_pad)

    def rope_bwd(d1, d2, d3, d4, cos, sin):
        return d1 * cos + d2 * sin, d2 * cos - d1 * sin, d3 * cos + d4 * sin, d4 * cos - d3 * sin

    dq1, dq2, dk1, dk2 = rowwise(rope_bwd, [dq1r, dq2r, dk1r, dk2r, cos8, sin8], [], [(LANES, f32)] * 4, [],
                                 _row_tile(S, 512), "rope_bwd")
    dqraw = jnp.concatenate([dqp[:, :, :64].reshape(S, 512), dq1, dq2], axis=1).astype(bf16)
    dkv = jnp.concatenate([dkp[:, :, :64].reshape(S, 512), dv], axis=1).astype(bf16)
    gr['uq'] = mm_tn(cqn, dqraw, name="mla_uq_dw")
    dcqn = mm(dqraw, w['uqT'], name="mla_uq_dx")
    gr['ukv'] = mm_tn(ckvn, dkv, name="mla_ukv_dw")
    dckvn = mm(dkv, w['ukvT'], name="mla_ukv_dx")

    def prep_bwd(cq, ckv, d1, d2, qg, kvg):
        _, vjp = jax.vjp(_mla_prep, cq, ckv, qg, kvg)
        return vjp((d1, d2))

    dcq, dckv, gr['mla_q_norm_g'], gr['mla_kv_norm_g'] = rowwise(
        prep_bwd, [(H, H_CQ, Q_LORA), (H, H_CKV, KV_LORA), dcqn, dckvn], [w['qg'], w['kvg']],
        [(Q_LORA, f32), (KV_LORA, f32)], [(1, Q_LORA), (1, KV_LORA)], _row_tile(S, 256), "mla_prep_bwd")

    def gmlp_bwd(hu, hv, dya, ng, ws, bsb):
        _, vjp = jax.vjp(_gmlp, hu, hv, ng, ws, bsb)
        return vjp(dya)

    dhu, dhv, gr['gmlp_norm_g'], gr['gmlp_ws'], gr['gm_bsb'] = rowwise(
        gmlp_bwd, [(H, H_UG, 256), (H, H_VG, 256), dya], [w['gm_ng'], w['gm_ws'], w['gm_bsb']],
        [(GM_WIDTH, f32), (GM_WIDTH, f32)], [(1, GM_WIDTH), (GM_HEADS, GM_CHUNK, GM_CHUNK), (GM_CHUNK, GM_WIDTH)],
        GM_CHUNK, "gmlp_bwd")

    dH = jnp.concatenate([dhu, dhv, dcq, dus, dckv, dk1, dk2], axis=1).astype(bf16)
    gr['in'] = mm_tn(xb, dH, name="mix_in_dw")
    dx = mm(dH, w['inT'], add=dz, add_scale=ALPHA, name="mix_in_dx")
    return dx, gr, received


def _block_diag(blocks):
    G, a, b = blocks.shape
    eye = jnp.eye(G, dtype=blocks.dtype)
    return (eye[:, None, :, None] * blocks[:, :, None, :]).reshape(G * a, G * b)


def _diag_blocks(mat, G):
    a, b = mat.shape[0] // G, mat.shape[1] // G
    m4 = mat.reshape(G, a, G, b)
    eye = jnp.eye(G, dtype=mat.dtype)
    return jnp.sum(m4 * eye[:, None, :, None], axis=2)


def prep_layer(W, rep, l):
    w = {}
    for f in ('ffn1', 'ffn2'):
        wg, wu = W[f + '_w_gate'], W[f + '_w_up']
        w[f] = {'g': wg, 'u': wu, 'guT': jnp.stack([wg.T, wu.T]), 'd': W[f + '_w_down'], 'dT': W[f + '_w_down'].T}
    wi = W['w_in']
    z112 = jnp.zeros((D_MODEL, LANES - ROPE_HALF), wi.dtype)
    w['in'] = jnp.concatenate([wi[:, :768], wi[:, 928:1184], wi[:, 768:896], wi[:, 896:912], z112, wi[:, 912:928], z112], axis=1)
    w['inT'] = w['in'].T
    uq = W['mla_w_uq'].reshape(Q_LORA, MLA_HEADS, MLA_NOPE + MLA_ROPE)
    w['uq'] = jnp.concatenate([uq[:, :, :64].reshape(Q_LORA, 512), uq[:, :, 64:80].reshape(Q_LORA, LANES),
                               uq[:, :, 80:96].reshape(Q_LORA, LANES)], axis=1)
    w['uqT'] = w['uq'].T
    ukv = W['mla_w_ukv'].reshape(KV_LORA, MLA_HEADS, MLA_NOPE + MLA_V)
    w['ukv'] = jnp.concatenate([ukv[:, :, :64].reshape(KV_LORA, 512), ukv[:, :, 64:].reshape(KV_LORA, 512)], axis=1)
    w['ukvT'] = w['ukv'].T
    w['out'] = W['w_out']
    w['outT'] = W['w_out'].T
    w['glu_w'] = W['ssm_glu_w']
    w['gm_ng'] = rep['gmlp_norm_g'][l].reshape(1, GM_WIDTH)
    w['gm_ws'] = rep['gmlp_ws'][l]
    w['gm_bsb'] = jnp.repeat(rep['gmlp_bs'][l].T, GM_HEAD_DIM, axis=1)
    w['qg'] = rep['mla_q_norm_g'][l].reshape(1, Q_LORA)
    w['kvg'] = rep['mla_kv_norm_g'][l].reshape(1, KV_LORA)
    w['ssm_d'] = rep['ssm_d'][l].reshape(1, SSM_WIDTH)
    w['glu_b'] = rep['ssm_glu_b'][l].reshape(1, SSM_WIDTH)
    w['gmix'] = rep['mix_norm_g'][l].reshape(1, D_MODEL)
    ar = rep['ssm_a_re'][l].reshape(1, N_STATE)
    ai = rep['ssm_a_im'][l].reshape(1, N_STATE)
    ldt = jnp.repeat(rep['ssm_log_dt'][l], SSM_STATE).reshape(1, N_STATE)
    brT = rep['ssm_b_re'][l].transpose(2, 0, 1).reshape(SSM_GROUP_CH, N_STATE)
    biT = rep['ssm_b_im'][l].transpose(2, 0, 1).reshape(SSM_GROUP_CH, N_STATE)
    w['ssm_prep_in'] = (ar, ai, ldt, brT, biT)
    abr, abi, bbrT, bbiT = whole(_ssm_prep, w['ssm_prep_in'], [(1, N_STATE)] * 2 + [(SSM_GROUP_CH, N_STATE)] * 2, "ssm_prep")
    w['ssm_a16'] = jnp.concatenate([abr.reshape(8, LANES), abi.reshape(8, LANES)], axis=0)

    def to_gcp(t):
        return t.reshape(SSM_GROUP_CH, SSM_GROUPS, SSM_STATE).transpose(1, 0, 2)

    w['ssm_wb'] = jnp.concatenate([_block_diag(to_gcp(bbrT)), _block_diag(to_gcp(bbiT))], axis=1).astype(bf16)
    w['ssm_wbT'] = w['ssm_wb'].T
    cre = rep['ssm_c_re'][l].transpose(0, 2, 1)
    cim = rep['ssm_c_im'][l].transpose(0, 2, 1)
    w['ssm_wc'] = jnp.concatenate([_block_diag(cre), -_block_diag(cim)], axis=0).astype(bf16)
    w['ssm_wcT'] = w['ssm_wc'].T
    return w


def unprep_grads(gr, w):
    out = {}
    for k in ('ln_g', 'ln_b', 'w_out', 'mla_q_norm_g', 'mla_kv_norm_g', 'ssm_glu_w', 'gmlp_ws'):
        out[k] = gr[k]
    out['gmlp_norm_g'] = gr['gmlp_norm_g'].reshape(GM_WIDTH)
    out['mla_q_norm_g'] = gr['mla_q_norm_g'].reshape(Q_LORA)
    out['mla_kv_norm_g'] = gr['mla_kv_norm_g'].reshape(KV_LORA)
    out['ssm_d'] = gr['ssm_d'].reshape(SSM_GROUPS, SSM_GROUP_CH)
    out['ssm_glu_b'] = gr['ssm_glu_b'].reshape(SSM_WIDTH)
    out['mix_norm_g'] = gr['mix_norm_g'].reshape(D_MODEL)
    out['gmlp_bs'] = gr['gm_bsb'].reshape(GM_CHUNK, GM_HEADS, GM_HEAD_DIM).sum(axis=-1).T
    d = gr['in']
    out['w_in'] = jnp.concatenate([d[:, :768], d[:, H_CKV:H_CKV + KV_LORA], d[:, H_K1:H_K1 + ROPE_HALF],
                                   d[:, H_K2:H_K2 + ROPE_HALF], d[:, H_US:H_US + SSM_WIDTH]], axis=1)
    d = gr['uq']
    out['mla_w_uq'] = jnp.concatenate([d[:, :512].reshape(Q_LORA, MLA_HEADS, 64), d[:, 512:640].reshape(Q_LORA, MLA_HEADS, ROPE_HALF),
                                       d[:, 640:768].reshape(Q_LORA, MLA_HEADS, ROPE_HALF)], axis=2).reshape(Q_LORA, 768)
    d = gr['ukv']
    out['mla_w_ukv'] = jnp.concatenate([d[:, :512].reshape(KV_LORA, MLA_HEADS, 64), d[:, 512:].reshape(KV_LORA, MLA_HEADS, 64)],
                                       axis=2).reshape(KV_LORA, 1024)
    dwc = gr['ssm_wc']
    out['ssm_c_re'] = _diag_blocks(dwc[:N_STATE], SSM_GROUPS).transpose(0, 2, 1)
    out['ssm_c_im'] = -_diag_blocks(dwc[N_STATE:], SSM_GROUPS).transpose(0, 2, 1)
    dwb = gr['ssm_wb']

    def from_blocks(m):
        return _diag_blocks(m, SSM_GROUPS).transpose(1, 0, 2).reshape(SSM_GROUP_CH, N_STATE)

    dbbrT, dbbiT = from_blocks(dwb[:, :N_STATE]), from_blocks(dwb[:, N_STATE:])
    da16 = gr['ssm_a16']
    dabr, dabi = da16[0:8].reshape(1, N_STATE), da16[8:16].reshape(1, N_STATE)

    def prep_bwd(ar, ai, ldt, brT, biT, d1, d2, d3, d4):
        _, vjp = jax.vjp(_ssm_prep, ar, ai, ldt, brT, biT)
        return vjp((d1, d2, d3, d4))

    dar, dai, dldt, dbrT, dbiT = whole(prep_bwd, w['ssm_prep_in'] + (dabr, dabi, dbbrT, dbbiT),
                                       [(1, N_STATE)] * 3 + [(SSM_GROUP_CH, N_STATE)] * 2, "ssm_prep_bwd")
    out['ssm_a_re'] = dar.reshape(SSM_GROUPS, SSM_STATE)
    out['ssm_a_im'] = dai.reshape(SSM_GROUPS, SSM_STATE)
    out['ssm_log_dt'] = dldt.reshape(SSM_GROUPS, SSM_STATE).sum(axis=-1)
    out['ssm_b_re'] = dbrT.reshape(SSM_GROUP_CH, SSM_GROUPS, SSM_STATE).transpose(1, 2, 0)
    out['ssm_b_im'] = dbiT.reshape(SSM_GROUP_CH, SSM_GROUPS, SSM_STATE).transpose(1, 2, 0)
    return out


def adamw(w, g, m, v, name):
    R, C = w.shape

    def fn(w, g, m, v):
        m = ADAM_B1 * m + (1.0 - ADAM_B1) * g
        v = ADAM_B2 * v + (1.0 - ADAM_B2) * jnp.square(g)
        m_hat = m / (1.0 - ADAM_B1 ** ADAM_STEP)
        v_hat = v / (1.0 - ADAM_B2 ** ADAM_STEP)
        delta = -ADAM_LR * (m_hat / (jnp.sqrt(v_hat) + ADAM_EPS) + ADAM_WD * w)
        return delta, m, v

    return rowwise(fn, [w, g, m, v], [], [(C, f32)] * 3, [], _pick(R, (256, 128, 64, 32, 16, 8)), name)


def kernel(x, positions, ln_g, ln_b, ffn1_w_gate, ffn1_w_up, ffn1_w_down, w_in, gmlp_norm_g, gmlp_ws, gmlp_bs, mla_q_norm_g, mla_w_uq, mla_kv_norm_g, mla_w_ukv, ssm_a_re, ssm_a_im, ssm_b_re, ssm_b_im, ssm_c_re, ssm_c_im, ssm_d, ssm_log_dt, ssm_glu_w, ssm_glu_b, mix_norm_g, w_out, ffn2_w_gate, ffn2_w_up, ffn2_w_down, loss_target, m_ln_g, m_ln_b, m_ffn1_w_gate, m_ffn1_w_up, m_ffn1_w_down, m_w_in, m_gmlp_norm_g, m_gmlp_ws, m_gmlp_bs, m_mla_q_norm_g, m_mla_w_uq, m_mla_kv_norm_g, m_mla_w_ukv, m_ssm_a_re, m_ssm_a_im, m_ssm_b_re, m_ssm_b_im, m_ssm_c_re, m_ssm_c_im, m_ssm_d, m_ssm_log_dt, m_ssm_glu_w, m_ssm_glu_b, m_mix_norm_g, m_w_out, m_ffn2_w_gate, m_ffn2_w_up, m_ffn2_w_down, v_ln_g, v_ln_b, v_ffn1_w_gate, v_ffn1_w_up, v_ffn1_w_down, v_w_in, v_gmlp_norm_g, v_gmlp_ws, v_gmlp_bs, v_mla_q_norm_g, v_mla_w_uq, v_mla_kv_norm_g, v_mla_w_ukv, v_ssm_a_re, v_ssm_a_im, v_ssm_b_re, v_ssm_b_im, v_ssm_c_re, v_ssm_c_im, v_ssm_d, v_ssm_log_dt, v_ssm_glu_w, v_ssm_glu_b, v_mix_norm_g, v_w_out, v_ffn2_w_gate, v_ffn2_w_up, v_ffn2_w_down):
    Wp = dict(zip(W_NAMES, (ln_g, ln_b, ffn1_w_gate, ffn1_w_up, ffn1_w_down, w_in, gmlp_norm_g, gmlp_ws, gmlp_bs, mla_q_norm_g, mla_w_uq, mla_kv_norm_g, mla_w_ukv, ssm_a_re, ssm_a_im, ssm_b_re, ssm_b_im, ssm_c_re, ssm_c_im, ssm_d, ssm_log_dt, ssm_glu_w, ssm_glu_b, mix_norm_g, w_out, ffn2_w_gate, ffn2_w_up, ffn2_w_down)))
    Mp = dict(zip(W_NAMES, (m_ln_g, m_ln_b, m_ffn1_w_gate, m_ffn1_w_up, m_ffn1_w_down, m_w_in, m_gmlp_norm_g, m_gmlp_ws, m_gmlp_bs, m_mla_q_norm_g, m_mla_w_uq, m_mla_kv_norm_g, m_mla_w_ukv, m_ssm_a_re, m_ssm_a_im, m_ssm_b_re, m_ssm_b_im, m_ssm_c_re, m_ssm_c_im, m_ssm_d, m_ssm_log_dt, m_ssm_glu_w, m_ssm_glu_b, m_mix_norm_g, m_w_out, m_ffn2_w_gate, m_ffn2_w_up, m_ffn2_w_down)))
    Vp = dict(zip(W_NAMES, (v_ln_g, v_ln_b, v_ffn1_w_gate, v_ffn1_w_up, v_ffn1_w_down, v_w_in, v_gmlp_norm_g, v_gmlp_ws, v_gmlp_bs, v_mla_q_norm_g, v_mla_w_uq, v_mla_kv_norm_g, v_mla_w_ukv, v_ssm_a_re, v_ssm_a_im, v_ssm_b_re, v_ssm_b_im, v_ssm_c_re, v_ssm_c_im, v_ssm_d, v_ssm_log_dt, v_ssm_glu_w, v_ssm_glu_b, v_mix_norm_g, v_w_out, v_ffn2_w_gate, v_ffn2_w_up, v_ffn2_w_down)))
    S = x.shape[1]
    my = _my_id()

    def shard_rows(l):
        return [_pad_tile_rows(Wp[n][l].astype(bf16).reshape(_pack_rows(n), PACK_W), 0) for n in BIG]

    def joined(got):
        return {n: _join_from_devices(n, g[:, :_pack_rows(n)]) for n, g in zip(BIG, got)}

    ln_flat = jnp.concatenate([Wp[n].reshape(-1) for n in LN_NAMES])
    *got, ln_all = all_gather(shard_rows(0) + [_pad_rows(ln_flat, 8)], "gather_weights")
    ln_all = ln_all.reshape(N_DEV, -1)
    lnsz = DEPTH * 3 * (D_MODEL // N_DEV)
    ln_full = {}
    for t, n in enumerate(LN_NAMES):
        sh = ln_all[:, t * lnsz:(t + 1) * lnsz].reshape(N_DEV, DEPTH, 3, D_MODEL // N_DEV)
        ln_full[n] = sh.transpose(1, 2, 0, 3).reshape(DEPTH, 3, 1, D_MODEL)
    rep = {n: Wp[n] for n in REPL}

    inv_freq = 1.0 / (ROPE_BASE ** (jnp.arange(0, MLA_ROPE, 2, dtype=f32) / MLA_ROPE))
    ang = positions.astype(f32).reshape(S, 1) * inv_freq[None, :]
    cos8 = jnp.tile(jnp.cos(ang), (1, MLA_HEADS))
    sin8 = jnp.tile(jnp.sin(ang), (1, MLA_HEADS))

    xs = x.reshape(S, D_MODEL)
    xb = xs.astype(bf16)
    ws, saved = [], []
    for l in range(DEPTH):
        w = prep_layer(joined(got), rep, l)
        lg, lb = ln_full['ln_g'][l], ln_full['ln_b'][l]
        xs, xb, r1 = ffn_fwd(xs, xb, w['ffn1'], lg[0], lb[0])
        xs, xb, r2, got = mixer_fwd(xs, xb, w, lg[1], lb[1], cos8, sin8, (True, shard_rows(l + 1)) if l + 1 < DEPTH else None)
        xs, xb, r3 = ffn_fwd(xs, xb, w['ffn2'], lg[2], lb[2])
        ws.append(w)
        saved.append((r1, r2, r3))

    def loss_fn(y, t):
        d = y - t
        part = jnp.sum(jnp.mean(jnp.square(d), axis=-1, keepdims=True), axis=0, keepdims=True)
        return d * (1.0 / D_MODEL), 0.5 * part

    dx, loss_part = rowwise(loss_fn, [xs, loss_target.reshape(S, D_MODEL)], [], [(D_MODEL, f32)], [(1, 1)],
                            _row_tile(S, 512), "loss")
    loss = lax.psum(loss_part[0, 0], ("x", "y", "c"))

    def grad_packs(g):
        return [_pad_tile_rows(_split_for_devices(n, g[n].astype(bf16)), 1) for n in BIG]

    grads, arrived = [None] * DEPTH, [None] * DEPTH
    for l in reversed(range(DEPTH)):
        w = ws[l]
        lg, lb = ln_full['ln_g'][l], ln_full['ln_b'][l]
        r1, r2, r3 = saved[l]
        dx, g2g, g2u, g2d, dg2, db2 = ffn_bwd(dx, r3, w['ffn2'], lg[2], lb[2])
        dx, gm, got = mixer_bwd(dx, r2, w, lg[1], lb[1], cos8, sin8, (False, grad_packs(grads[l + 1])) if l + 1 < DEPTH else None)
        if got:
            arrived[l + 1] = got
        dx, g1g, g1u, g1d, dg0, db0 = ffn_bwd(dx, r1, w['ffn1'], lg[0], lb[0])
        g = unprep_grads(gm, w)
        g.update({'ffn1_w_gate': g1g, 'ffn1_w_up': g1u, 'ffn1_w_down': g1d,
                  'ffn2_w_gate': g2g, 'ffn2_w_up': g2u, 'ffn2_w_down': g2d})
        g['ln_g'] = jnp.concatenate([dg0, g['ln_g'], dg2], axis=0)
        g['ln_b'] = jnp.concatenate([db0, g['ln_b'], db2], axis=0)
        grads[l] = g
    grad_x = dx.reshape(1, S, D_MODEL)

    arrived[0] = all_to_all(grad_packs(grads[0]), "scatter_grads")
    G = {n: jnp.stack([sum_slots(arrived[l][t], "sum_" + n)[:_pack_rows(n)].reshape(_shard_shape(n)) for l in range(DEPTH)])
         for t, n in enumerate(BIG)}
    small_names = LN_NAMES + REPL
    spack = jnp.concatenate([jnp.stack([grads[l][n] for l in range(DEPTH)]).reshape(-1) for n in small_names])
    n_small = spack.shape[0]
    gsmall = sum_slots(all_gather([_pad_rows(spack, 8)], "gather_small")[0], "sum_small").reshape(-1)

    off = 0
    for n in small_names:
        shp = (DEPTH, 3, D_MODEL) if n in LN_NAMES else Wp[n].shape
        sz = math.prod(shp)
        G[n] = gsmall[off:off + sz].reshape(shp)
        off += sz
    for n in LN_NAMES:
        G[n] = lax.dynamic_slice_in_dim(G[n], my * (D_MODEL // N_DEV), D_MODEL // N_DEV, axis=2)

    delta, new_m, new_v = {}, {}, {}
    for n in BIG:
        shp = Wp[n].shape
        two = (shp[0] * shp[1], shp[2])
        d_, m_, v_ = adamw(Wp[n].reshape(two), G[n].reshape(two), Mp[n].reshape(two), Vp[n].reshape(two), "adamw_" + n)
        delta[n], new_m[n], new_v[n] = d_.reshape(shp), m_.reshape(shp), v_.reshape(shp)

    def pack_small(src):
        return _pad_rows(jnp.concatenate([src[n].reshape(-1) for n in small_names]), 8)

    d_, m_, v_ = adamw(pack_small(Wp), pack_small(G), pack_small(Mp), pack_small(Vp), "adamw_small")
    d_, m_, v_ = d_.reshape(-1), m_.reshape(-1), v_.reshape(-1)
    off = 0
    for n in small_names:
        shp = Wp[n].shape
        sz = math.prod(shp)
        delta[n], new_m[n], new_v[n] = (t[off:off + sz].reshape(shp) for t in (d_, m_, v_))
        off += sz

    return (loss, grad_x, *[G[n] for n in W_NAMES], *[delta[n] for n in W_NAMES],
            *[new_m[n] for n in W_NAMES], *[new_v[n] for n in W_NAMES])
```

```python
import functools
import math

import jax
import jax.numpy as jnp
from jax import lax
from jax.experimental import pallas as pl
from jax.experimental.pallas import tpu as pltpu

f32 = jnp.float32
bf16 = jnp.bfloat16

D_MODEL = 1024
DEPTH = 4
D_FF = 2816
GM_HEADS, GM_HEAD_DIM, GM_WIDTH, GM_CHUNK = 4, 64, 256, 128
MLA_HEADS, MLA_NOPE, MLA_ROPE, MLA_V = 8, 64, 32, 64
ROPE_HALF = MLA_ROPE // 2
Q_LORA, KV_LORA = 256, 128
ROPE_BASE = 10000.0
SSM_GROUPS, SSM_GROUP_CH, SSM_WIDTH, SSM_STATE = 16, 16, 256, 64
N_STATE = SSM_GROUPS * SSM_STATE
ALPHA = (2 * DEPTH) ** 0.25
LN_EPS = 1e-5
RMS_EPS = 1e-6
NEG_BIG = -1e30
ATT_SCALE = (MLA_NOPE + MLA_ROPE) ** -0.5
ADAM_LR, ADAM_B1, ADAM_B2, ADAM_EPS, ADAM_WD, ADAM_STEP = 0.001, 0.9, 0.999, 1e-08, 0.01, 10

N_DEV = 8
LANES = 128
VMEM_LIMIT = 48 * 1024 * 1024
MM_TILE_BUDGET = 32 * 1024 * 1024
MESH = pl.DeviceIdType.MESH

H_UG, H_VG, H_CQ, H_US, H_CKV, H_K1, H_K2, H_COLS = 0, 256, 512, 768, 1024, 1152, 1280, 1408

W_NAMES = ['ln_g', 'ln_b', 'ffn1_w_gate', 'ffn1_w_up', 'ffn1_w_down', 'w_in', 'gmlp_norm_g', 'gmlp_ws', 'gmlp_bs',
           'mla_q_norm_g', 'mla_w_uq', 'mla_kv_norm_g', 'mla_w_ukv', 'ssm_a_re', 'ssm_a_im', 'ssm_b_re', 'ssm_b_im',
           'ssm_c_re', 'ssm_c_im', 'ssm_d', 'ssm_log_dt', 'ssm_glu_w', 'ssm_glu_b', 'mix_norm_g', 'w_out',
           'ffn2_w_gate', 'ffn2_w_up', 'ffn2_w_down']
BIG = {'ffn1_w_gate': 1, 'ffn1_w_up': 1, 'ffn1_w_down': 0, 'w_in': 1, 'mla_w_uq': 1, 'mla_w_ukv': 1,
       'ssm_glu_w': 0, 'w_out': 0, 'ffn2_w_gate': 1, 'ffn2_w_up': 1, 'ffn2_w_down': 0}
BIG_SHAPE = {'ffn1_w_gate': (D_MODEL, D_FF), 'ffn1_w_up': (D_MODEL, D_FF), 'ffn1_w_down': (D_FF, D_MODEL),
             'w_in': (D_MODEL, 1184), 'mla_w_uq': (Q_LORA, 768), 'mla_w_ukv': (KV_LORA, 1024),
             'ssm_glu_w': (SSM_WIDTH, SSM_WIDTH), 'w_out': (D_MODEL, D_MODEL),
             'ffn2_w_gate': (D_MODEL, D_FF), 'ffn2_w_up': (D_MODEL, D_FF), 'ffn2_w_down': (D_FF, D_MODEL)}
LN_NAMES = ['ln_g', 'ln_b']
REPL = [n for n in W_NAMES if n not in BIG and n not in LN_NAMES]


def _pick(n, cands):
    for c in cands:
        if n % c == 0:
            return c
    return n


def _params(sem):
    return pltpu.CompilerParams(dimension_semantics=sem, vmem_limit_bytes=VMEM_LIMIT)


S3_ROWS = 2 * N_STATE // LANES


def _from_s3(ref):
    return jnp.concatenate([ref[:, c, :] for c in range(S3_ROWS)], axis=1)


def _to_s3(ref, val):
    for c in range(S3_ROWS):
        ref[:, c, :] = val[:, c * LANES:(c + 1) * LANES].astype(ref.dtype)


def mm(a, b, *, out_dtype=f32, add=None, add_scale=1.0, a_col0=0, a_s3=False, out_s3=False, grouped=False, b_nt=False, name):
    G = a.shape[0] if grouped else 1
    M = a.shape[1] if grouped else a.shape[0]
    K, N = (b.shape[-1], b.shape[-2]) if b_nt else b.shape[-2:]
    tn = N if out_s3 else _pick(N, (512, 384, 256) if N <= 1536 else (512, 384, 256, 128))
    tk = K if (K <= 2 * D_FF or a_s3) else _pick(K, (1024, 512, 256, 128))
    nk = G * (K // tk)
    assert not grouped or tk == K

    def tile_bytes(tm):
        return 2 * (tm * tk * a.dtype.itemsize + tk * tn * b.dtype.itemsize + tm * tn * 4 * (2 if add is not None else 1))

    tm = next(t for t in (1024, 512, 256, 128, 64, 32, 16, 8) if M % t == 0 and (tile_bytes(t) <= MM_TILE_BUDGET or t == 8))
    assert a_s3 or grouped or (a_col0 % tk == 0 and a_col0 + K <= a.shape[1])
    kb0 = a_col0 // tk
    has_add = add is not None

    def body(*refs):
        if has_add:
            a_ref, b_ref, add_ref, o_ref, acc = refs
        else:
            a_ref, b_ref, o_ref, acc = refs
        k = pl.program_id(2)
        a_val = _from_s3(a_ref) if a_s3 else a_ref[...]
        part = (_nt if b_nt else functools.partial(jnp.dot, preferred_element_type=f32))(a_val.astype(bf16), b_ref[...].astype(bf16))

        def finish(total):
            if has_add:
                total = total + add_scale * add_ref[...]
            if out_s3:
                _to_s3(o_ref, total)
            else:
                o_ref[...] = total.astype(o_ref.dtype)

        if nk == 1:
            finish(part)
        else:
            @pl.when(k == 0)
            def _():
                acc[...] = part

            @pl.when(k > 0)
            def _():
                acc[...] += part

            @pl.when(k == nk - 1)
            def _():
                finish(acc[...])

    if a_s3:
        a_spec = pl.BlockSpec((tm, S3_ROWS, LANES), lambda i, j, k: (i, 0, 0))
    elif grouped:
        a_spec = pl.BlockSpec((None, tm, tk), lambda i, j, k: (k, i, 0))
    else:
        a_spec = pl.BlockSpec((tm, tk), lambda i, j, k: (i, kb0 + k))
    if b_nt:
        b_spec = pl.BlockSpec((None, tn, tk), lambda i, j, k: (k, j, 0)) if grouped else pl.BlockSpec((tn, tk), lambda i, j, k: (j, k))
    else:
        b_spec = pl.BlockSpec((None, tk, tn), lambda i, j, k: (k, 0, j)) if grouped else pl.BlockSpec((tk, tn), lambda i, j, k: (k, j))
    in_specs, ops = [a_spec, b_spec], [a, b]
    if has_add:
        in_specs.append(pl.BlockSpec((tm, tn), lambda i, j, k: (i, j)))
        ops.append(add)
    if out_s3:
        out_spec = pl.BlockSpec((tm, S3_ROWS, LANES), lambda i, j, k: (i, 0, 0))
        out_shape = jax.ShapeDtypeStruct((M, S3_ROWS, LANES), out_dtype)
    else:
        out_spec = pl.BlockSpec((tm, tn), lambda i, j, k: (i, j))
        out_shape = jax.ShapeDtypeStruct((M, N), out_dtype)
    return pl.pallas_call(
        body, grid=(M // tm, N // tn, nk), in_specs=in_specs, out_specs=out_spec, out_shape=out_shape,
        scratch_shapes=[pltpu.VMEM((tm, tn) if nk > 1 else (8, LANES), f32)],
        compiler_params=_params(("parallel", "parallel", "arbitrary")), name=name)(*ops)


def mm_tn(a, b, *, a_col0=0, m_dim=None, a_s3=False, b_s3=False, b_lead=None, name):
    K = a.shape[0]
    M = 2 * N_STATE if a_s3 else (a.shape[1] if m_dim is None else m_dim)
    N = 2 * N_STATE if b_s3 else b.shape[-1]
    tm = M if a_s3 else _pick(M, (H_COLS, 1024, 768, 512, 384, 256, 128))
    tn = N if b_s3 else _pick(N, (H_COLS, 1024, 768, 512, 384, 256))

    def tile_bytes(tk):
        return 2 * tk * (tm * a.dtype.itemsize + tn * b.dtype.itemsize) + 3 * tm * tn * 4

    tk = next(t for t in (2048, 1024, 512, 256, 128, 64, 32, 16) if K % t == 0 and (tile_bytes(t) <= MM_TILE_BUDGET or t == 16))
    nk = K // tk
    assert a_col0 % tm == 0
    mb0 = a_col0 // tm

    def body(a_ref, b_ref, o_ref, acc):
        k = pl.program_id(2)
        a_val = _from_s3(a_ref) if a_s3 else a_ref[...]
        b_val = _from_s3(b_ref) if b_s3 else b_ref[...]
        part = lax.dot_general(a_val.astype(bf16), b_val.astype(bf16), (((0,), (0,)), ((), ())), preferred_element_type=f32)

        @pl.when(k == 0)
        def _():
            acc[...] = part

        @pl.when(k > 0)
        def _():
            acc[...] += part

        @pl.when(k == nk - 1)
        def _():
            o_ref[...] = acc[...]

    s3_spec = pl.BlockSpec((tk, S3_ROWS, LANES), lambda i, j, k: (k, 0, 0))
    if b_s3:
        b_spec = s3_spec
    elif b_lead is not None:
        b_spec = pl.BlockSpec((None, tk, tn), lambda i, j, k: (b_lead, k, j))
    else:
        b_spec = pl.BlockSpec((tk, tn), lambda i, j, k: (k, j))
    return pl.pallas_call(
        body, grid=(M // tm, N // tn, nk),
        in_specs=[s3_spec if a_s3 else pl.BlockSpec((tk, tm), lambda i, j, k: (k, mb0 + i)), b_spec],
        out_specs=pl.BlockSpec((tm, tn), lambda i, j, k: (i, j)),
        out_shape=jax.ShapeDtypeStruct((M, N), f32),
        scratch_shapes=[pltpu.VMEM((tm, tn), f32)],
        compiler_params=_params(("parallel", "parallel", "arbitrary")), name=name)(a, b)


def rowwise(fn, rows, pars, out_rows, out_accs, tm, name):
    first = rows[0]
    if isinstance(first, tuple):
        R = first[0].shape[1] if first[1] == 'lead' else first[0].shape[0]
    else:
        R = first.shape[0]
    assert R % tm == 0, (R, tm, name)
    n_rows, n_pars, n_or, n_oa = len(rows), len(pars), len(out_rows), len(out_accs)

    in_specs, ops = [], []
    for r in rows:
        if isinstance(r, tuple) and r[1] == 'lead':
            arr, _, kk = r
            in_specs.append(pl.BlockSpec((None, tm, arr.shape[2]), lambda i, kk=kk: (kk, i, 0)))
        elif isinstance(r, tuple):
            arr, c0, w = r
            assert c0 % w == 0
            in_specs.append(pl.BlockSpec((tm, w), lambda i, cb=c0 // w: (i, cb)))
        else:
            arr = r
            in_specs.append(pl.BlockSpec((tm, arr.shape[1]), lambda i: (i, 0)))
        ops.append(arr)
    for p in pars:
        in_specs.append(pl.BlockSpec(p.shape, lambda i, nd=p.ndim: (0,) * nd))
        ops.append(p)
    out_specs = [pl.BlockSpec((tm, w), lambda i: (i, 0)) for (w, _) in out_rows]
    out_specs += [pl.BlockSpec(s, lambda i, nd=len(s): (0,) * nd) for s in out_accs]
    out_shape = [jax.ShapeDtypeStruct((R, w), dt) for (w, dt) in out_rows]
    out_shape += [jax.ShapeDtypeStruct(s, f32) for s in out_accs]

    def body(*refs):
        ins = [r[...] for r in refs[:n_rows + n_pars]]
        o_refs = refs[n_rows + n_pars:]
        res = fn(*ins)
        if not isinstance(res, (tuple, list)):
            res = (res,)
        assert len(res) == n_or + n_oa, (len(res), n_or, n_oa, name)
        for o, v in zip(o_refs[:n_or], res[:n_or]):
            o[...] = v.astype(o.dtype)
        if n_oa:
            i = pl.program_id(0)

            @pl.when(i == 0)
            def _():
                for o, v in zip(o_refs[n_or:], res[n_or:]):
                    o[...] = v.astype(f32)

            @pl.when(i > 0)
            def _():
                for o, v in zip(o_refs[n_or:], res[n_or:]):
                    o[...] += v.astype(f32)

    return pl.pallas_call(
        body, grid=(R // tm,), in_specs=in_specs, out_specs=out_specs, out_shape=out_shape,
        compiler_params=_params(("arbitrary",)), name=name)(*ops)


def whole(fn, ins, out_shapes, name):
    def body(*refs):
        res = fn(*[r[...] for r in refs[:len(ins)]])
        for o, v in zip(refs[len(ins):], res):
            o[...] = v

    return pl.pallas_call(body, out_shape=[jax.ShapeDtypeStruct(s, f32) for s in out_shapes], name=name)(*ins)


@jax.custom_vjp
def _bdot(a, b):
    return jnp.dot(a.astype(bf16), b.astype(bf16), preferred_element_type=f32)


def _bdot_fwd(a, b):
    return _bdot(a, b), (a, b)


def _bdot_bwd(res, g):
    a, b = res
    gb = g.astype(bf16)
    da = lax.dot_general(gb, b.astype(bf16), (((1,), (1,)), ((), ())), preferred_element_type=f32)
    db = lax.dot_general(a.astype(bf16), gb, (((0,), (0,)), ((), ())), preferred_element_type=f32)
    return da, db


_bdot.defvjp(_bdot_fwd, _bdot_bwd)


def _ln(z, g, b):
    mu = jnp.mean(z, axis=-1, keepdims=True)
    var = jnp.mean(jnp.square(z - mu), axis=-1, keepdims=True)
    return (z - mu) * lax.rsqrt(var + LN_EPS) * g + b


def _rms_only(x):
    return x * lax.rsqrt(jnp.mean(jnp.square(x), axis=-1, keepdims=True) + RMS_EPS)


def _swiglu(a, b):
    return jax.nn.silu(a) * b


def _gmlp(hu, hv, ng, ws, bsb):
    u = jax.nn.gelu(hu)
    v = jax.nn.gelu(hv)
    lane = lax.broadcasted_iota(jnp.int32, (1, GM_WIDTH), 1)
    masks = [((lane >= GM_HEAD_DIM * h) & (lane < GM_HEAD_DIM * (h + 1))).astype(f32) for h in range(GM_HEADS)]
    mu = jnp.zeros_like(v)
    for m in masks:
        mu = mu + m * (jnp.sum(v * m, axis=-1, keepdims=True) / GM_HEAD_DIM)
    d = v - mu
    var = jnp.zeros_like(v)
    for m in masks:
        var = var + m * (jnp.sum(d * d * m, axis=-1, keepdims=True) / GM_HEAD_DIM)
    vn = d * lax.rsqrt(var + LN_EPS) * ng
    r = lax.broadcasted_iota(jnp.int32, (GM_CHUNK, GM_CHUNK), 0)
    c = lax.broadcasted_iota(jnp.int32, (GM_CHUNK, GM_CHUNK), 1)
    tril = (c <= r).astype(f32)
    z = bsb
    for h, m in enumerate(masks):
        z = z + _bdot(ws[h] * tril, vn * m)
    return u * z


def _mla_prep(cq, ckv, qg, kvg):
    return _rms_only(cq) * qg, _rms_only(ckv) * kvg


def _rope(q1, q2, k1, k2, cos, sin):
    return q1 * cos - q2 * sin, q2 * cos + q1 * sin, k1 * cos - k2 * sin, k2 * cos + k1 * sin


def _mix_post(ya, ob, y1, us, dskip, gluw, glub, gmix):
    y = jax.nn.gelu(y1 + dskip * us)
    yc = y * jax.nn.sigmoid(_bdot(y, gluw) + glub)
    return jnp.concatenate([_rms_only(ya), _rms_only(ob), _rms_only(yc)], axis=1) * gmix


def _ssm_prep(ar, ai, ldt, brT, biT):
    dt = jnp.exp(ldt)
    mag = jnp.exp(ar * dt)
    abr = mag * jnp.cos(ai * dt)
    abi = mag * jnp.sin(ai * dt)
    den = ar * ar + ai * ai
    cr = ((abr - 1.0) * ar + abi * ai) / den
    ci = (abi * ar - (abr - 1.0) * ai) / den
    return abr, abi, cr * brT - ci * biT, cr * biT + ci * brT


ATT_FWD_HEADS = 2


def _att_tile(S):
    return _pick(S, (512, 256, 128))


def _nt(a, b):
    return lax.dot_general(a, b, (((1,), (1,)), ((), ())), preferred_element_type=f32)


def _diag_keep(T):
    krow = lax.broadcasted_iota(jnp.int32, (T, T), 0)
    qcol = lax.broadcasted_iota(jnp.int32, (T, T), 1)
    return qcol >= krow


def grid_call(body, exch, *, grid, in_specs, out_specs, out_shape, scratch_shapes, name, args):
    params = _params(("arbitrary", "arbitrary"))
    if exch is None:
        return pl.pallas_call(body, grid=grid, in_specs=in_specs, out_specs=out_specs, out_shape=out_shape,
                              scratch_shapes=scratch_shapes, compiler_params=params, name=name)(*args)
    gather, xs = exch
    n, n_in, n_out, n_sc = len(xs), len(in_specs), len(out_specs), len(scratch_shapes)

    def riding(*refs):
        ins, x_refs = refs[:n_in], refs[n_in:n_in + n]
        outs, xo_refs = refs[n_in + n:n_in + n + n_out], refs[n_in + n + n_out:n_in + 2 * n + n_out]
        rest = refs[n_in + 2 * n + n_out:]
        scratch, sems = rest[:n_sc], rest[n_sc:]
        h, i = pl.program_id(0), pl.program_id(1)

        @pl.when((h == 0) & (i == 0))
        def _():
            _direct_exchange(gather, x_refs, xo_refs, *sems, start=True)

        body(*ins, *outs, *scratch)

        @pl.when((h == grid[0] - 1) & (i == grid[1] - 1))
        def _():
            _direct_exchange(gather, x_refs, xo_refs, *sems, start=False)

    return pl.pallas_call(
        riding, grid=grid, in_specs=list(in_specs) + [HBM_SPEC] * n, out_specs=list(out_specs) + [HBM_SPEC] * n,
        out_shape=list(out_shape) + _exchange_out_shapes(gather, xs),
        scratch_shapes=list(scratch_shapes) + _exchange_scratch(n), compiler_params=params,
        name=name + ("_gather" if gather else "_scatter"))(*args, *xs)


def flash_fwd(q, k, vT, exch=None):
    Hh, S, _ = q.shape
    T = _att_tile(S)
    HB = ATT_FWD_HEADS

    def body(q_ref, k_ref, vT_ref, o_ref, lse_ref, m_sc, l_sc, acc_sc):
        i = pl.program_id(1)
        m_sc[...] = jnp.full_like(m_sc, NEG_BIG)
        l_sc[...] = jnp.zeros_like(l_sc)
        acc_sc[...] = jnp.zeros_like(acc_sc)

        def block(j, diagonal):
            rows = pl.ds(pl.multiple_of(j * T, T), T)
            for hh in range(HB):
                sT = _nt(k_ref[hh, rows, :], q_ref[hh]) * ATT_SCALE
                if diagonal:
                    sT = jnp.where(_diag_keep(T), sT, NEG_BIG)
                m_old = m_sc[hh]
                m_new = jnp.maximum(m_old, jnp.max(sT, axis=0, keepdims=True))
                alpha = jnp.exp(m_old - m_new)
                pT = jnp.exp(sT - m_new)
                l_sc[hh] = alpha * l_sc[hh] + jnp.sum(pT, axis=0, keepdims=True)
                acc_sc[hh] = alpha * acc_sc[hh] + jnp.dot(vT_ref[hh, :, rows], pT.astype(bf16), preferred_element_type=f32)
                m_sc[hh] = m_new

        def loop_body(j, c):
            block(j, False)
            return c

        lax.fori_loop(0, i, loop_body, 0)
        block(i, True)
        o_ref[...] = acc_sc[...] / l_sc[...]
        lse_ref[...] = m_sc[...] + jnp.log(l_sc[...])

    return grid_call(
        body, exch, grid=(Hh // HB, S // T),
        in_specs=[pl.BlockSpec((HB, T, LANES), lambda h, i: (h, i, 0)), pl.BlockSpec((HB, S, LANES), lambda h, i: (h, 0, 0)),
                  pl.BlockSpec((HB, MLA_V, S), lambda h, i: (h, 0, 0))],
        out_specs=[pl.BlockSpec((HB, MLA_V, T), lambda h, i: (h, 0, i)), pl.BlockSpec((HB, 1, T), lambda h, i: (h, 0, i))],
        out_shape=[jax.ShapeDtypeStruct((Hh, MLA_V, S), f32), jax.ShapeDtypeStruct((Hh, 1, S), f32)],
        scratch_shapes=[pltpu.VMEM((HB, 1, T), f32), pltpu.VMEM((HB, 1, T), f32), pltpu.VMEM((HB, MLA_V, T), f32)],
        name="flash_fwd", args=(q, k, vT))


def flash_bwd(q, k, kT, v, do, lse, delta, exch=None):
    Hh, S, _ = q.shape
    T = _att_tile(S)

    def body(q_ref, do_ref, lse_ref, dl_ref, k_ref, kT_ref, v_ref, dq_ref, dk_ref, dv_ref, dq_sc):
        i = pl.program_id(1)

        @pl.when(i == 0)
        def _():
            dk_ref[...] = jnp.zeros_like(dk_ref)
            dv_ref[...] = jnp.zeros_like(dv_ref)

        qi, doi = q_ref[0], do_ref[0]
        lse_i, dl_i = lse_ref[0], dl_ref[0]
        dq_sc[...] = jnp.zeros_like(dq_sc)

        def block(j, diagonal):
            rows = pl.ds(pl.multiple_of(j * T, T), T)
            sT = _nt(k_ref[0, rows, :], qi) * ATT_SCALE
            pT = jnp.exp(sT - lse_i)
            if diagonal:
                pT = jnp.where(_diag_keep(T), pT, 0.0)
            dpT = _nt(v_ref[0, rows, :], doi)
            dsT = (pT * (dpT - dl_i) * ATT_SCALE).astype(bf16)
            dv_ref[0, rows, :] += jnp.dot(pT.astype(bf16), doi, preferred_element_type=f32)
            dk_ref[0, rows, :] += jnp.dot(dsT, qi, preferred_element_type=f32)
            dq_sc[...] += jnp.dot(kT_ref[0, :, rows], dsT, preferred_element_type=f32)

        def loop_body(j, c):
            block(j, False)
            return c

        lax.fori_loop(0, i, loop_body, 0)
        block(i, True)
        dq_ref[0] = dq_sc[...]

    tile = lambda w: pl.BlockSpec((1, T, w), lambda h, i: (h, i, 0))
    row = pl.BlockSpec((1, 1, T), lambda h, i: (h, 0, i))
    full = lambda w: pl.BlockSpec((1, S, w), lambda h, i: (h, 0, 0))
    return grid_call(
        body, exch, grid=(Hh, S // T),
        in_specs=[tile(LANES), tile(MLA_V), row, row, full(LANES), pl.BlockSpec((1, LANES, S), lambda h, i: (h, 0, 0)), full(MLA_V)],
        out_specs=[pl.BlockSpec((1, LANES, T), lambda h, i: (h, 0, i)), full(LANES), full(MLA_V)],
        out_shape=[jax.ShapeDtypeStruct((Hh, LANES, S), f32), jax.ShapeDtypeStruct((Hh, S, LANES), f32),
                   jax.ShapeDtypeStruct((Hh, S, MLA_V), f32)],
        scratch_shapes=[pltpu.VMEM((LANES, T), f32)],
        name="flash_bwd", args=(q, do, lse, delta, k, kT, v))


def _scan_tile(S):
    return _pick(S, (256, 128, 64, 32, 16, 8))


def scan_fwd(bu3, a16):
    S = bu3.shape[0]
    ts = _scan_tile(S)

    def body(bu_ref, a_ref, o_ref, h_sc):
        @pl.when(pl.program_id(0) == 0)
        def _():
            h_sc[...] = jnp.zeros_like(h_sc)

        ar, ai = a_ref[0:8, :], a_ref[8:16, :]

        def step(t, carry):
            hr, hi = carry
            nr = ar * hr - ai * hi + bu_ref[t, 0:8, :]
            ni = ar * hi + ai * hr + bu_ref[t, 8:16, :]
            o_ref[t, 0:8, :] = nr
            o_ref[t, 8:16, :] = ni
            return nr, ni

        hr, hi = lax.fori_loop(0, ts, step, (h_sc[0:8, :], h_sc[8:16, :]), unroll=8)
        h_sc[0:8, :] = hr
        h_sc[8:16, :] = hi

    blk = pl.BlockSpec((ts, 16, LANES), lambda i: (i, 0, 0))
    return pl.pallas_call(
        body, grid=(S // ts,), in_specs=[blk, pl.BlockSpec((16, LANES), lambda i: (0, 0))], out_specs=blk,
        out_shape=jax.ShapeDtypeStruct(bu3.shape, f32), scratch_shapes=[pltpu.VMEM((16, LANES), f32)],
        compiler_params=_params(("arbitrary",)), name="scan_fwd")(bu3, a16)


def scan_bwd(g3, h3, a16):
    S = g3.shape[0]
    ts = _scan_tile(S)
    nb = S // ts

    def body(g_ref, h_ref, a_ref, o_ref, da_ref, lam_sc, da_sc):
        @pl.when(pl.program_id(0) == 0)
        def _():
            lam_sc[...] = jnp.zeros_like(lam_sc)
            da_sc[...] = jnp.zeros_like(da_sc)

        ar, ai = a_ref[0:8, :], a_ref[8:16, :]

        def step(kk, carry):
            lr, li, dar, dai = carry
            t = ts - 1 - kk
            hr, hi = h_ref[t, 0:8, :], h_ref[t, 8:16, :]
            dar = dar + lr * hr + li * hi
            dai = dai + li * hr - lr * hi
            nlr = ar * lr + ai * li + g_ref[t, 0:8, :]
            nli = ar * li - ai * lr + g_ref[t, 8:16, :]
            o_ref[t, 0:8, :] = nlr
            o_ref[t, 8:16, :] = nli
            return nlr, nli, dar, dai

        lr, li, dar, dai = lax.fori_loop(
            0, ts, step, (lam_sc[0:8, :], lam_sc[8:16, :], da_sc[0:8, :], da_sc[8:16, :]), unroll=8)
        lam_sc[0:8, :] = lr
        lam_sc[8:16, :] = li
        da_sc[0:8, :] = dar
        da_sc[8:16, :] = dai
        da_ref[0:8, :] = dar
        da_ref[8:16, :] = dai

    blk = pl.BlockSpec((ts, 16, LANES), lambda i: (nb - 1 - i, 0, 0))
    small = pl.BlockSpec((16, LANES), lambda i: (0, 0))
    return pl.pallas_call(
        body, grid=(nb,), in_specs=[blk, blk, small], out_specs=[blk, small],
        out_shape=[jax.ShapeDtypeStruct(g3.shape, f32), jax.ShapeDtypeStruct((16, LANES), f32)],
        scratch_shapes=[pltpu.VMEM((16, LANES), f32), pltpu.VMEM((16, LANES), f32)],
        compiler_params=_params(("arbitrary",)), name="scan_bwd")(g3, h3, a16)


HBM_SPEC = pl.BlockSpec(memory_space=pltpu.HBM)


def _my_id():
    return 4 * lax.axis_index("x") + 2 * lax.axis_index("y") + lax.axis_index("c")


def all_gather(xs, name):
    n = len(xs)

    def body(*refs):
        x_refs, o_refs = refs[:n], refs[n:2 * n]
        send_sems, recv_sems, local_sems = refs[2 * n:]
        x, y, c = lax.axis_index("x"), lax.axis_index("y"), lax.axis_index("c")
        me, sibling = (x, y, c), (x, y, 1 - c)
        chips = [(1 - x, y), (x, 1 - y), (1 - x, 1 - y)]

        def slot(o, p):
            return o.at[4 * p[0] + 2 * p[1] + p[2]]

        def copy(a, k, block, to, src=None):
            o = o_refs[a]
            return pltpu.make_async_remote_copy(
                src_ref=slot(o, block) if src is None else src, dst_ref=slot(o, block),
                send_sem=send_sems.at[7 * a + k], recv_sem=recv_sems.at[7 * a + k], device_id=to, device_id_type=MESH)

        own, sends = [], []
        for a in range(n):
            mine = pltpu.make_async_copy(x_refs[a], slot(o_refs[a], me), local_sems.at[a])
            mine.start()
            own.append(mine)
            first = [copy(a, 0, me, sibling, src=x_refs[a])]
            first += [copy(a, 1 + j, me, (*chip, c), src=x_refs[a]) for j, chip in enumerate(chips)]
            for cp in first:
                cp.start()
            sends += first
        for a in range(n):
            for j, chip in enumerate(chips):
                copy(a, 1 + j, (*chip, c), me).wait_recv()
                fwd = copy(a, 4 + j, (*chip, c), sibling)
                fwd.start()
                sends.append(fwd)
        for a in range(n):
            copy(a, 0, sibling, me).wait_recv()
            for j, chip in enumerate(chips):
                copy(a, 4 + j, (*chip, 1 - c), me).wait_recv()
        for cp in sends:
            cp.wait_send()
        for cp in own:
            cp.wait()

    return pl.pallas_call(
        body, out_shape=[jax.ShapeDtypeStruct((N_DEV,) + v.shape, v.dtype) for v in xs],
        in_specs=[HBM_SPEC] * n, out_specs=[HBM_SPEC] * n,
        scratch_shapes=[pltpu.SemaphoreType.DMA((7 * n,)), pltpu.SemaphoreType.DMA((7 * n,)), pltpu.SemaphoreType.DMA((n,))],
        name=name)(*xs)


def _direct_exchange(gather, x_refs, o_refs, send_sems, recv_sems, local_sems, start):
    x, y, c = lax.axis_index("x"), lax.axis_index("y"), lax.axis_index("c")
    my = 4 * x + 2 * y + c
    for a, (x_ref, o_ref) in enumerate(zip(x_refs, o_refs)):
        mine = pltpu.make_async_copy(x_ref if gather else x_ref.at[my], o_ref.at[my], local_sems.at[a])
        if start:
            mine.start()
        else:
            mine.wait()
        for k in range(1, N_DEV):
            px = 1 - x if k & 4 else x
            py = 1 - y if k & 2 else y
            pc = 1 - c if k & 1 else c
            pid = 4 * px + 2 * py + pc
            src = x_ref if gather else x_ref.at[pid]
            sems = dict(send_sem=send_sems.at[7 * a + k - 1], recv_sem=recv_sems.at[7 * a + k - 1],
                        device_id=(px, py, pc), device_id_type=MESH)
            if start:
                pltpu.make_async_remote_copy(src_ref=src, dst_ref=o_ref.at[my], **sems).start()
            else:
                pltpu.make_async_remote_copy(src_ref=src, dst_ref=o_ref.at[my], **sems).wait_send()
                pltpu.make_async_remote_copy(src_ref=src, dst_ref=o_ref.at[pid], **sems).wait_recv()


def _exchange_scratch(n):
    return [pltpu.SemaphoreType.DMA((7 * n,)), pltpu.SemaphoreType.DMA((7 * n,)), pltpu.SemaphoreType.DMA((n,))]


def _exchange_out_shapes(gather, xs):
    return [jax.ShapeDtypeStruct(((N_DEV,) + v.shape) if gather else v.shape, v.dtype) for v in xs]


def all_to_all(xs, name):
    n = len(xs)

    def body(*refs):
        ex = (False, refs[:n], refs[n:2 * n], *refs[2 * n:])
        _direct_exchange(*ex, start=True)
        _direct_exchange(*ex, start=False)

    return pl.pallas_call(
        body, out_shape=_exchange_out_shapes(False, xs), in_specs=[HBM_SPEC] * n, out_specs=[HBM_SPEC] * n,
        scratch_shapes=_exchange_scratch(n), name=name)(*xs)


def sum_slots(g8, name):
    R, C = g8.shape[1:]

    def fn(*tiles):
        tot = tiles[0].astype(f32)
        for t in tiles[1:]:
            tot = tot + t.astype(f32)
        return tot

    return rowwise(fn, [(g8, 'lead', k) for k in range(N_DEV)], [], [(C, f32)], [], _pick(R, (256, 128, 64, 32, 16, 8)), name)[0]


PACK_W = 1024


def _pad_rows(flat, mult):
    n = flat.shape[0]
    tot = -(-n // (PACK_W * mult)) * PACK_W * mult
    return jnp.pad(flat, (0, tot - n)).reshape(tot // PACK_W, PACK_W)


BF16_TILE_ROWS = 16


def _pad_tile_rows(a, axis):
    pad = [(0, 0)] * a.ndim
    pad[axis] = (0, -a.shape[axis] % BF16_TILE_ROWS)
    return jnp.pad(a, pad)


def _shard_shape(name):
    r, c = BIG_SHAPE[name]
    return (r // N_DEV, c) if BIG[name] == 0 else (r, c // N_DEV)


def _pack_rows(name):
    r, c = _shard_shape(name)
    assert (r * c) % PACK_W == 0
    return r * c // PACK_W


def _split_for_devices(name, full):
    r, c = BIG_SHAPE[name]
    if BIG[name] == 0:
        return full.reshape(N_DEV, _pack_rows(name), PACK_W)
    return full.reshape(r, N_DEV, c // N_DEV).transpose(1, 0, 2).reshape(N_DEV, _pack_rows(name), PACK_W)


def _join_from_devices(name, parts):
    r, c = BIG_SHAPE[name]
    if BIG[name] == 0:
        return parts.reshape(r, c)
    return parts.reshape(N_DEV, r, c // N_DEV).transpose(1, 0, 2).reshape(r, c)


def _row_tile(S, want):
    return _pick(S, tuple(t for t in (512, 256, 128, 64, 32, 16) if t <= want))


def resid_ln(x, f, g, b, scale, name):
    D = x.shape[1]

    def fn(x, f, g, b):
        z = ALPHA * x + scale * f
        xo = _ln(z, g, b)
        return z, xo, xo

    return rowwise(fn, [x, f], [g, b], [(D, f32), (D, f32), (D, bf16)], [], _row_tile(x.shape[0], 512), name)


def ln_bwd(z, g, b, dxo, scale, name):
    D = z.shape[1]

    def fn(z, dxo, g, b):
        _, vjp = jax.vjp(_ln, z, g, b)
        dz, dg, db = vjp(dxo)
        return dz, scale * dz, dg, db

    return rowwise(fn, [z, dxo], [g, b], [(D, f32), (D, bf16)], [(1, D), (1, D)], _row_tile(z.shape[0], 256), name)


FF_TILE = 256


def ffn_up_act(xb, wg, wu):
    M, K = xb.shape
    tm = _pick(M, (1024, 512, 256, 128, 64, 32, 16))

    def body(x_ref, wg_ref, wu_ref, ab_ref, h_ref):
        x = x_ref[...]
        a = jnp.dot(x, wg_ref[...], preferred_element_type=f32)
        b = jnp.dot(x, wu_ref[...], preferred_element_type=f32)
        ab_ref[0] = a.astype(bf16)
        ab_ref[1] = b.astype(bf16)
        h_ref[...] = _swiglu(a, b).astype(bf16)

    wspec = pl.BlockSpec((K, FF_TILE), lambda i, j: (0, j))
    return pl.pallas_call(
        body, grid=(M // tm, D_FF // FF_TILE),
        in_specs=[pl.BlockSpec((tm, K), lambda i, j: (i, 0)), wspec, wspec],
        out_specs=[pl.BlockSpec((2, tm, FF_TILE), lambda i, j: (0, i, j)), pl.BlockSpec((tm, FF_TILE), lambda i, j: (i, j))],
        out_shape=[jax.ShapeDtypeStruct((2, M, D_FF), bf16), jax.ShapeDtypeStruct((M, D_FF), bf16)],
        compiler_params=_params(("parallel", "parallel")), name="ffn_up_act")(xb, wg, wu)


def ffn_down_dx_act(dzs, wd, ab):
    M, K = dzs.shape
    tm = _pick(M, (1024, 512, 256, 128, 64, 32, 16))

    def body(dz_ref, w_ref, ab_ref, dab_ref, h_ref):
        dh = _nt(dz_ref[...], w_ref[...])
        h, vjp = jax.vjp(_swiglu, ab_ref[0].astype(f32), ab_ref[1].astype(f32))
        da, db = vjp(dh)
        dab_ref[0] = da.astype(bf16)
        dab_ref[1] = db.astype(bf16)
        h_ref[...] = h.astype(bf16)

    pair = pl.BlockSpec((2, tm, FF_TILE), lambda i, j: (0, i, j))
    return pl.pallas_call(
        body, grid=(M // tm, D_FF // FF_TILE),
        in_specs=[pl.BlockSpec((tm, K), lambda i, j: (i, 0)), pl.BlockSpec((FF_TILE, K), lambda i, j: (j, 0)), pair],
        out_specs=[pair, pl.BlockSpec((tm, FF_TILE), lambda i, j: (i, j))],
        out_shape=[jax.ShapeDtypeStruct((2, M, D_FF), bf16), jax.ShapeDtypeStruct((M, D_FF), bf16)],
        compiler_params=_params(("parallel", "parallel")), name="ffn_down_dx_act")(dzs, wd, ab)


def ffn_fwd(x, xb, w, g, b):
    ab, h = ffn_up_act(xb, w['g'], w['u'])
    f = mm(h, w['d'], name="ffn_down")
    z, xo, xob = resid_ln(x, f, g, b, 0.5, "ffn_ln")
    return xo, xob, (xb, ab, z)


def ffn_bwd(dxo, res, w, g, b):
    xb, ab, z = res
    dz, dzs, dg, db = ln_bwd(z, g, b, dxo, 0.5, "ffn_ln_bwd")
    dab, h = ffn_down_dx_act(dzs, w['d'], ab)
    dwd = mm_tn(h, dzs, name="ffn_down_dw")
    dwg = mm_tn(xb, dab, b_lead=0, name="ffn_gate_dw")
    dwu = mm_tn(xb, dab, b_lead=1, name="ffn_up_dw")
    dx = mm(dab, w['gu'], add=dz, add_scale=ALPHA, grouped=True, b_nt=True, name="ffn_up_dx")
    return dx, dwg, dwu, dwd, dg, db


def _heads_first(a, width):
    return a.reshape(a.shape[0], MLA_HEADS, width)


def mixer_fwd(x, xb, w, g, b, cos8, sin8, exch=None):
    S = x.shape[0]
    H = mm(xb, w['in'], name="mix_in")
    ya = rowwise(_gmlp, [(H, H_UG, 256), (H, H_VG, 256)], [w['gm_ng'], w['gm_ws'], w['gm_bsb']], [(GM_WIDTH, f32)], [],
                 GM_CHUNK, "gmlp")[0]
    cqn, ckvn = rowwise(_mla_prep, [(H, H_CQ, Q_LORA), (H, H_CKV, KV_LORA)], [w['qg'], w['kvg']],
                        [(Q_LORA, bf16), (KV_LORA, bf16)], [], _row_tile(S, 512), "mla_prep")
    qraw = mm(cqn, w['uq'], name="mla_uq")
    kv = mm(ckvn, w['ukv'], name="mla_ukv")
    q1, q2, k1, k2 = rowwise(_rope, [(qraw, 512, LANES), (qraw, 640, LANES), (H, H_K1, LANES), (H, H_K2, LANES), cos8, sin8],
                             [], [(LANES, f32)] * 4, [], _row_tile(S, 512), "rope")
    zpad = jnp.zeros((S, MLA_HEADS, LANES - MLA_NOPE - MLA_ROPE), f32)
    qp = jnp.concatenate([_heads_first(qraw[:, :512], 64), _heads_first(q1, ROPE_HALF), _heads_first(q2, ROPE_HALF), zpad], axis=2)
    k1b = jnp.broadcast_to(k1[:, None, :ROPE_HALF], (S, MLA_HEADS, ROPE_HALF))
    k2b = jnp.broadcast_to(k2[:, None, :ROPE_HALF], (S, MLA_HEADS, ROPE_HALF))
    kp = jnp.concatenate([_heads_first(kv[:, :512], 64), k1b, k2b, zpad], axis=2)
    qp = qp.transpose(1, 0, 2).astype(bf16)
    kp = kp.transpose(1, 0, 2).astype(bf16)
    v3 = _heads_first(kv[:, 512:], 64).astype(bf16)
    vp = v3.transpose(1, 0, 2)
    oT, lse, *received = flash_fwd(qp, kp, v3.transpose(1, 2, 0), exch)
    ob = oT.transpose(2, 0, 1).reshape(S, MLA_HEADS * MLA_V)
    bu3 = mm(H, w['ssm_wb'], a_col0=H_US, out_s3=True, name="ssm_bu")
    hs3 = scan_fwd(bu3, w['ssm_a16'])
    y1 = mm(hs3, w['ssm_wc'], a_s3=True, name="ssm_c")
    y = rowwise(_mix_post, [ya, ob, y1, (H, H_US, SSM_WIDTH)], [w['ssm_d'], w['glu_w'], w['glu_b'], w['gmix']],
                [(D_MODEL, bf16)], [], _row_tile(S, 256), "mix_post")[0]
    f = mm(y, w['out'], name="mix_out")
    z, xo, xob = resid_ln(x, f, g, b, 1.0, "mix_ln")
    return xo, xob, (xb, H, cqn, ckvn, qp, kp, vp, lse, hs3, ya, ob, y1, y, z), received


def mixer_bwd(dxo, res, w, g, b, cos8, sin8, exch=None):
    xb, H, cqn, ckvn, qp, kp, vp, lse, hs3, ya, ob, y1, y, z = res
    S = z.shape[0]
    gr = {}
    dz, dzs, gr['ln_g'], gr['ln_b'] = ln_bwd(z, g, b, dxo, 1.0, "mix_ln_bwd")
    gr['w_out'] = mm_tn(y, dzs, name="mix_out_dw")
    dy = mm(dzs, w['out'], b_nt=True, name="mix_out_dx")

    def post_bwd(ya, ob, y1, us, dy, dskip, gluw, glub, gmix):
        _, vjp = jax.vjp(_mix_post, ya, ob, y1, us, dskip, gluw, glub, gmix)
        dya, dob, dy1, dus, *dpars = vjp(dy)
        prod = dob * ob
        col = lax.broadcasted_iota(jnp.int32, (1, MLA_HEADS * MLA_V), 1)
        lane = lax.broadcasted_iota(jnp.int32, (1, LANES), 1)
        delta = jnp.zeros((prod.shape[0], LANES), f32)
        for h in range(MLA_HEADS):
            in_head = ((col >= MLA_V * h) & (col < MLA_V * (h + 1))).astype(f32)
            delta = jnp.where(lane == h, jnp.sum(prod * in_head, axis=-1, keepdims=True), delta)
        return (dya, dob, dy1, dus, delta, *dpars)

    dya, dob, dy1, dus_skip, delta, gr['ssm_d'], gr['ssm_glu_w'], gr['ssm_glu_b'], gr['mix_norm_g'] = rowwise(
        post_bwd, [ya, ob, y1, (H, H_US, SSM_WIDTH), dy], [w['ssm_d'], w['glu_w'], w['glu_b'], w['gmix']],
        [(GM_WIDTH, f32), (MLA_HEADS * MLA_V, f32), (SSM_WIDTH, f32), (SSM_WIDTH, f32), (LANES, f32)],
        [(1, SSM_WIDTH), (SSM_WIDTH, SSM_WIDTH), (1, SSM_WIDTH), (1, D_MODEL)], _row_tile(S, 128), "mix_post_bwd")

    gr['ssm_wc'] = mm_tn(hs3, dy1, a_s3=True, name="ssm_c_dw")
    dhs3 = mm(dy1, w['ssm_wc'], out_s3=True, b_nt=True, name="ssm_c_dx")
    dbu3, gr['ssm_a16'] = scan_bwd(dhs3, hs3, w['ssm_a16'])
    gr['ssm_wb'] = mm_tn(H, dbu3, a_col0=H_US, m_dim=SSM_WIDTH, b_s3=True, name="ssm_bu_dw")
    dus = mm(dbu3, w['ssm_wb'], add=dus_skip, add_scale=1.0, a_s3=True, b_nt=True, name="ssm_bu_dx")

    do = _heads_first(dob, MLA_V).transpose(1, 0, 2)
    delta = delta[:, :MLA_HEADS].T.reshape(MLA_HEADS, 1, S)
    dqT, dkp, dvp, *received = flash_bwd(qp, kp, kp.transpose(0, 2, 1), vp, do.astype(bf16), lse, delta, exch)
    dqp = dqT.transpose(2, 0, 1)
    dkp = dkp.transpose(1, 0, 2)
    dv = dvp.transpose(1, 0, 2).reshape(S, MLA_HEADS * MLA_V)
    lane_pad = ((0, 0), (0, LANES - ROPE_HALF))
    dq1r = dqp[:, :, 64:80].reshape(S, LANES)
    dq2r = dqp[:, :, 80:96].reshape(S, LANES)
    dk1r = jnp.pad(jnp.sum(dkp[:, :, 64:80], axis=1), lane_pad)
    dk2r = jnp.pad(jnp.sum(dkp[:, :, 80:96], axis=1), lane_pad)

    def rope_bwd(d1, d2, d3, d4, cos, sin):
        return d1 * cos + d2 * sin, d2 * cos - d1 * sin, d3 * cos + d4 * sin, d4 * cos - d3 * sin

    dq1, dq2, dk1, dk2 = rowwise(rope_bwd, [dq1r, dq2r, dk1r, dk2r, cos8, sin8], [], [(LANES, f32)] * 4, [],
                                 _row_tile(S, 512), "rope_bwd")
    dqraw = jnp.concatenate([dqp[:, :, :64].reshape(S, 512), dq1, dq2], axis=1).astype(bf16)
    dkv = jnp.concatenate([dkp[:, :, :64].reshape(S, 512), dv], axis=1).astype(bf16)
    gr['uq'] = mm_tn(cqn, dqraw, name="mla_uq_dw")
    dcqn = mm(dqraw, w['uq'], b_nt=True, name="mla_uq_dx")
    gr['ukv'] = mm_tn(ckvn, dkv, name="mla_ukv_dw")
    dckvn = mm(dkv, w['ukv'], b_nt=True, name="mla_ukv_dx")

    def prep_bwd(cq, ckv, d1, d2, qg, kvg):
        _, vjp = jax.vjp(_mla_prep, cq, ckv, qg, kvg)
        return vjp((d1, d2))

    dcq, dckv, gr['mla_q_norm_g'], gr['mla_kv_norm_g'] = rowwise(
        prep_bwd, [(H, H_CQ, Q_LORA), (H, H_CKV, KV_LORA), dcqn, dckvn], [w['qg'], w['kvg']],
        [(Q_LORA, f32), (KV_LORA, f32)], [(1, Q_LORA), (1, KV_LORA)], _row_tile(S, 256), "mla_prep_bwd")

    def gmlp_bwd(hu, hv, dya, ng, ws, bsb):
        _, vjp = jax.vjp(_gmlp, hu, hv, ng, ws, bsb)
        return vjp(dya)

    dhu, dhv, gr['gmlp_norm_g'], gr['gmlp_ws'], gr['gm_bsb'] = rowwise(
        gmlp_bwd, [(H, H_UG, 256), (H, H_VG, 256), dya], [w['gm_ng'], w['gm_ws'], w['gm_bsb']],
        [(GM_WIDTH, f32), (GM_WIDTH, f32)], [(1, GM_WIDTH), (GM_HEADS, GM_CHUNK, GM_CHUNK), (GM_CHUNK, GM_WIDTH)],
        GM_CHUNK, "gmlp_bwd")

    dH = jnp.concatenate([dhu, dhv, dcq, dus, dckv, dk1, dk2], axis=1).astype(bf16)
    gr['in'] = mm_tn(xb, dH, name="mix_in_dw")
    dx = mm(dH, w['in'], add=dz, add_scale=ALPHA, b_nt=True, name="mix_in_dx")
    return dx, gr, received


def _block_diag(blocks):
    G, a, b = blocks.shape
    eye = jnp.eye(G, dtype=blocks.dtype)
    return (eye[:, None, :, None] * blocks[:, :, None, :]).reshape(G * a, G * b)


def _diag_blocks(mat, G):
    a, b = mat.shape[0] // G, mat.shape[1] // G
    m4 = mat.reshape(G, a, G, b)
    eye = jnp.eye(G, dtype=mat.dtype)
    return jnp.sum(m4 * eye[:, None, :, None], axis=2)


def prep_layer(W, rep, l):
    w = {}
    for f in ('ffn1', 'ffn2'):
        wg, wu = W[f + '_w_gate'], W[f + '_w_up']
        w[f] = {'g': wg, 'u': wu, 'gu': jnp.stack([wg, wu]), 'd': W[f + '_w_down']}
    wi = W['w_in']
    z112 = jnp.zeros((D_MODEL, LANES - ROPE_HALF), wi.dtype)
    w['in'] = jnp.concatenate([wi[:, :768], wi[:, 928:1184], wi[:, 768:896], wi[:, 896:912], z112, wi[:, 912:928], z112], axis=1)
    uq = W['mla_w_uq'].reshape(Q_LORA, MLA_HEADS, MLA_NOPE + MLA_ROPE)
    w['uq'] = jnp.concatenate([uq[:, :, :64].reshape(Q_LORA, 512), uq[:, :, 64:80].reshape(Q_LORA, LANES),
                               uq[:, :, 80:96].reshape(Q_LORA, LANES)], axis=1)
    ukv = W['mla_w_ukv'].reshape(KV_LORA, MLA_HEADS, MLA_NOPE + MLA_V)
    w['ukv'] = jnp.concatenate([ukv[:, :, :64].reshape(KV_LORA, 512), ukv[:, :, 64:].reshape(KV_LORA, 512)], axis=1)
    w['out'] = W['w_out']
    w['glu_w'] = W['ssm_glu_w']
    w['gm_ng'] = rep['gmlp_norm_g'][l].reshape(1, GM_WIDTH)
    w['gm_ws'] = rep['gmlp_ws'][l]
    w['gm_bsb'] = jnp.repeat(rep['gmlp_bs'][l].T, GM_HEAD_DIM, axis=1)
    w['qg'] = rep['mla_q_norm_g'][l].reshape(1, Q_LORA)
    w['kvg'] = rep['mla_kv_norm_g'][l].reshape(1, KV_LORA)
    w['ssm_d'] = rep['ssm_d'][l].reshape(1, SSM_WIDTH)
    w['glu_b'] = rep['ssm_glu_b'][l].reshape(1, SSM_WIDTH)
    w['gmix'] = rep['mix_norm_g'][l].reshape(1, D_MODEL)
    ar = rep['ssm_a_re'][l].reshape(1, N_STATE)
    ai = rep['ssm_a_im'][l].reshape(1, N_STATE)
    ldt = jnp.repeat(rep['ssm_log_dt'][l], SSM_STATE).reshape(1, N_STATE)
    brT = rep['ssm_b_re'][l].transpose(2, 0, 1).reshape(SSM_GROUP_CH, N_STATE)
    biT = rep['ssm_b_im'][l].transpose(2, 0, 1).reshape(SSM_GROUP_CH, N_STATE)
    w['ssm_prep_in'] = (ar, ai, ldt, brT, biT)
    abr, abi, bbrT, bbiT = whole(_ssm_prep, w['ssm_prep_in'], [(1, N_STATE)] * 2 + [(SSM_GROUP_CH, N_STATE)] * 2, "ssm_prep")
    w['ssm_a16'] = jnp.concatenate([abr.reshape(8, LANES), abi.reshape(8, LANES)], axis=0)

    def to_gcp(t):
        return t.reshape(SSM_GROUP_CH, SSM_GROUPS, SSM_STATE).transpose(1, 0, 2)

    w['ssm_wb'] = jnp.concatenate([_block_diag(to_gcp(bbrT)), _block_diag(to_gcp(bbiT))], axis=1).astype(bf16)
    cre = rep['ssm_c_re'][l].transpose(0, 2, 1)
    cim = rep['ssm_c_im'][l].transpose(0, 2, 1)
    w['ssm_wc'] = jnp.concatenate([_block_diag(cre), -_block_diag(cim)], axis=0).astype(bf16)
    return w


def unprep_grads(gr, w):
    out = {}
    for k in ('ln_g', 'ln_b', 'w_out', 'mla_q_norm_g', 'mla_kv_norm_g', 'ssm_glu_w', 'gmlp_ws'):
        out[k] = gr[k]
    out['gmlp_norm_g'] = gr['gmlp_norm_g'].reshape(GM_WIDTH)
    out['mla_q_norm_g'] = gr['mla_q_norm_g'].reshape(Q_LORA)
    out['mla_kv_norm_g'] = gr['mla_kv_norm_g'].reshape(KV_LORA)
    out['ssm_d'] = gr['ssm_d'].reshape(SSM_GROUPS, SSM_GROUP_CH)
    out['ssm_glu_b'] = gr['ssm_glu_b'].reshape(SSM_WIDTH)
    out['mix_norm_g'] = gr['mix_norm_g'].reshape(D_MODEL)
    out['gmlp_bs'] = gr['gm_bsb'].reshape(GM_CHUNK, GM_HEADS, GM_HEAD_DIM).sum(axis=-1).T
    d = gr['in']
    out['w_in'] = jnp.concatenate([d[:, :768], d[:, H_CKV:H_CKV + KV_LORA], d[:, H_K1:H_K1 + ROPE_HALF],
                                   d[:, H_K2:H_K2 + ROPE_HALF], d[:, H_US:H_US + SSM_WIDTH]], axis=1)
    d = gr['uq']
    out['mla_w_uq'] = jnp.concatenate([d[:, :512].reshape(Q_LORA, MLA_HEADS, 64), d[:, 512:640].reshape(Q_LORA, MLA_HEADS, ROPE_HALF),
                                       d[:, 640:768].reshape(Q_LORA, MLA_HEADS, ROPE_HALF)], axis=2).reshape(Q_LORA, 768)
    d = gr['ukv']
    out['mla_w_ukv'] = jnp.concatenate([d[:, :512].reshape(KV_LORA, MLA_HEADS, 64), d[:, 512:].reshape(KV_LORA, MLA_HEADS, 64)],
                                       axis=2).reshape(KV_LORA, 1024)
    dwc = gr['ssm_wc']
    out['ssm_c_re'] = _diag_blocks(dwc[:N_STATE], SSM_GROUPS).transpose(0, 2, 1)
    out['ssm_c_im'] = -_diag_blocks(dwc[N_STATE:], SSM_GROUPS).transpose(0, 2, 1)
    dwb = gr['ssm_wb']

    def from_blocks(m):
        return _diag_blocks(m, SSM_GROUPS).transpose(1, 0, 2).reshape(SSM_GROUP_CH, N_STATE)

    dbbrT, dbbiT = from_blocks(dwb[:, :N_STATE]), from_blocks(dwb[:, N_STATE:])
    da16 = gr['ssm_a16']
    dabr, dabi = da16[0:8].reshape(1, N_STATE), da16[8:16].reshape(1, N_STATE)

    def prep_bwd(ar, ai, ldt, brT, biT, d1, d2, d3, d4):
        _, vjp = jax.vjp(_ssm_prep, ar, ai, ldt, brT, biT)
        return vjp((d1, d2, d3, d4))

    dar, dai, dldt, dbrT, dbiT = whole(prep_bwd, w['ssm_prep_in'] + (dabr, dabi, dbbrT, dbbiT),
                                       [(1, N_STATE)] * 3 + [(SSM_GROUP_CH, N_STATE)] * 2, "ssm_prep_bwd")
    out['ssm_a_re'] = dar.reshape(SSM_GROUPS, SSM_STATE)
    out['ssm_a_im'] = dai.reshape(SSM_GROUPS, SSM_STATE)
    out['ssm_log_dt'] = dldt.reshape(SSM_GROUPS, SSM_STATE).sum(axis=-1)
    out['ssm_b_re'] = dbrT.reshape(SSM_GROUP_CH, SSM_GROUPS, SSM_STATE).transpose(1, 2, 0)
    out['ssm_b_im'] = dbiT.reshape(SSM_GROUP_CH, SSM_GROUPS, SSM_STATE).transpose(1, 2, 0)
    return out


def adamw(w, g, m, v, name):
    R, C = w.shape

    def fn(w, g, m, v):
        m = ADAM_B1 * m + (1.0 - ADAM_B1) * g
        v = ADAM_B2 * v + (1.0 - ADAM_B2) * jnp.square(g)
        m_hat = m / (1.0 - ADAM_B1 ** ADAM_STEP)
        v_hat = v / (1.0 - ADAM_B2 ** ADAM_STEP)
        delta = -ADAM_LR * (m_hat / (jnp.sqrt(v_hat) + ADAM_EPS) + ADAM_WD * w)
        return delta, m, v

    return rowwise(fn, [w, g, m, v], [], [(C, f32)] * 3, [], _pick(R, (256, 128, 64, 32, 16, 8)), name)


def kernel(x, positions, ln_g, ln_b, ffn1_w_gate, ffn1_w_up, ffn1_w_down, w_in, gmlp_norm_g, gmlp_ws, gmlp_bs, mla_q_norm_g, mla_w_uq, mla_kv_norm_g, mla_w_ukv, ssm_a_re, ssm_a_im, ssm_b_re, ssm_b_im, ssm_c_re, ssm_c_im, ssm_d, ssm_log_dt, ssm_glu_w, ssm_glu_b, mix_norm_g, w_out, ffn2_w_gate, ffn2_w_up, ffn2_w_down, loss_target, m_ln_g, m_ln_b, m_ffn1_w_gate, m_ffn1_w_up, m_ffn1_w_down, m_w_in, m_gmlp_norm_g, m_gmlp_ws, m_gmlp_bs, m_mla_q_norm_g, m_mla_w_uq, m_mla_kv_norm_g, m_mla_w_ukv, m_ssm_a_re, m_ssm_a_im, m_ssm_b_re, m_ssm_b_im, m_ssm_c_re, m_ssm_c_im, m_ssm_d, m_ssm_log_dt, m_ssm_glu_w, m_ssm_glu_b, m_mix_norm_g, m_w_out, m_ffn2_w_gate, m_ffn2_w_up, m_ffn2_w_down, v_ln_g, v_ln_b, v_ffn1_w_gate, v_ffn1_w_up, v_ffn1_w_down, v_w_in, v_gmlp_norm_g, v_gmlp_ws, v_gmlp_bs, v_mla_q_norm_g, v_mla_w_uq, v_mla_kv_norm_g, v_mla_w_ukv, v_ssm_a_re, v_ssm_a_im, v_ssm_b_re, v_ssm_b_im, v_ssm_c_re, v_ssm_c_im, v_ssm_d, v_ssm_log_dt, v_ssm_glu_w, v_ssm_glu_b, v_mix_norm_g, v_w_out, v_ffn2_w_gate, v_ffn2_w_up, v_ffn2_w_down):
    Wp = dict(zip(W_NAMES, (ln_g, ln_b, ffn1_w_gate, ffn1_w_up, ffn1_w_down, w_in, gmlp_norm_g, gmlp_ws, gmlp_bs, mla_q_norm_g, mla_w_uq, mla_kv_norm_g, mla_w_ukv, ssm_a_re, ssm_a_im, ssm_b_re, ssm_b_im, ssm_c_re, ssm_c_im, ssm_d, ssm_log_dt, ssm_glu_w, ssm_glu_b, mix_norm_g, w_out, ffn2_w_gate, ffn2_w_up, ffn2_w_down)))
    Mp = dict(zip(W_NAMES, (m_ln_g, m_ln_b, m_ffn1_w_gate, m_ffn1_w_up, m_ffn1_w_down, m_w_in, m_gmlp_norm_g, m_gmlp_ws, m_gmlp_bs, m_mla_q_norm_g, m_mla_w_uq, m_mla_kv_norm_g, m_mla_w_ukv, m_ssm_a_re, m_ssm_a_im, m_ssm_b_re, m_ssm_b_im, m_ssm_c_re, m_ssm_c_im, m_ssm_d, m_ssm_log_dt, m_ssm_glu_w, m_ssm_glu_b, m_mix_norm_g, m_w_out, m_ffn2_w_gate, m_ffn2_w_up, m_ffn2_w_down)))
    Vp = dict(zip(W_NAMES, (v_ln_g, v_ln_b, v_ffn1_w_gate, v_ffn1_w_up, v_ffn1_w_down, v_w_in, v_gmlp_norm_g, v_gmlp_ws, v_gmlp_bs, v_mla_q_norm_g, v_mla_w_uq, v_mla_kv_norm_g, v_mla_w_ukv, v_ssm_a_re, v_ssm_a_im, v_ssm_b_re, v_ssm_b_im, v_ssm_c_re, v_ssm_c_im, v_ssm_d, v_ssm_log_dt, v_ssm_glu_w, v_ssm_glu_b, v_mix_norm_g, v_w_out, v_ffn2_w_gate, v_ffn2_w_up, v_ffn2_w_down)))
    S = x.shape[1]
    my = _my_id()

    def shard_rows(l):
        return [_pad_tile_rows(Wp[n][l].astype(bf16).reshape(_pack_rows(n), PACK_W), 0) for n in BIG]

    def joined(got):
        return {n: _join_from_devices(n, g[:, :_pack_rows(n)]) for n, g in zip(BIG, got)}

    ln_flat = jnp.concatenate([Wp[n].reshape(-1) for n in LN_NAMES])
    *got, ln_all = all_gather(shard_rows(0) + [_pad_rows(ln_flat, 8)], "gather_weights")
    ln_all = ln_all.reshape(N_DEV, -1)
    lnsz = DEPTH * 3 * (D_MODEL // N_DEV)
    ln_full = {}
    for t, n in enumerate(LN_NAMES):
        sh = ln_all[:, t * lnsz:(t + 1) * lnsz].reshape(N_DEV, DEPTH, 3, D_MODEL // N_DEV)
        ln_full[n] = sh.transpose(1, 2, 0, 3).reshape(DEPTH, 3, 1, D_MODEL)
    rep = {n: Wp[n] for n in REPL}

    inv_freq = 1.0 / (ROPE_BASE ** (jnp.arange(0, MLA_ROPE, 2, dtype=f32) / MLA_ROPE))
    ang = positions.astype(f32).reshape(S, 1) * inv_freq[None, :]
    cos8 = jnp.tile(jnp.cos(ang), (1, MLA_HEADS))
    sin8 = jnp.tile(jnp.sin(ang), (1, MLA_HEADS))

    xs = x.reshape(S, D_MODEL)
    xb = xs.astype(bf16)
    ws, saved = [], []
    for l in range(DEPTH):
        w = prep_layer(joined(got), rep, l)
        lg, lb = ln_full['ln_g'][l], ln_full['ln_b'][l]
        xs, xb, r1 = ffn_fwd(xs, xb, w['ffn1'], lg[0], lb[0])
        xs, xb, r2, got = mixer_fwd(xs, xb, w, lg[1], lb[1], cos8, sin8, (True, shard_rows(l + 1)) if l + 1 < DEPTH else None)
        xs, xb, r3 = ffn_fwd(xs, xb, w['ffn2'], lg[2], lb[2])
        ws.append(w)
        saved.append((r1, r2, r3))

    def loss_fn(y, t):
        d = y - t
        part = jnp.sum(jnp.mean(jnp.square(d), axis=-1, keepdims=True), axis=0, keepdims=True)
        return d * (1.0 / D_MODEL), 0.5 * part

    dx, loss_part = rowwise(loss_fn, [xs, loss_target.reshape(S, D_MODEL)], [], [(D_MODEL, f32)], [(1, 1)],
                            _row_tile(S, 512), "loss")
    loss = lax.psum(loss_part[0, 0], ("x", "y", "c"))

    def grad_packs(g):
        return [_pad_tile_rows(_split_for_devices(n, g[n].astype(bf16)), 1) for n in BIG]

    grads, arrived = [None] * DEPTH, [None] * DEPTH
    for l in reversed(range(DEPTH)):
        w = ws[l]
        lg, lb = ln_full['ln_g'][l], ln_full['ln_b'][l]
        r1, r2, r3 = saved[l]
        dx, g2g, g2u, g2d, dg2, db2 = ffn_bwd(dx, r3, w['ffn2'], lg[2], lb[2])
        dx, gm, got = mixer_bwd(dx, r2, w, lg[1], lb[1], cos8, sin8, (False, grad_packs(grads[l + 1])) if l + 1 < DEPTH else None)
        if got:
            arrived[l + 1] = got
        dx, g1g, g1u, g1d, dg0, db0 = ffn_bwd(dx, r1, w['ffn1'], lg[0], lb[0])
        g = unprep_grads(gm, w)
        g.update({'ffn1_w_gate': g1g, 'ffn1_w_up': g1u, 'ffn1_w_down': g1d,
                  'ffn2_w_gate': g2g, 'ffn2_w_up': g2u, 'ffn2_w_down': g2d})
        g['ln_g'] = jnp.concatenate([dg0, g['ln_g'], dg2], axis=0)
        g['ln_b'] = jnp.concatenate([db0, g['ln_b'], db2], axis=0)
        grads[l] = g
    grad_x = dx.reshape(1, S, D_MODEL)

    arrived[0] = all_to_all(grad_packs(grads[0]), "scatter_grads")
    G = {n: jnp.stack([sum_slots(arrived[l][t], "sum_" + n)[:_pack_rows(n)].reshape(_shard_shape(n)) for l in range(DEPTH)])
         for t, n in enumerate(BIG)}
    small_names = LN_NAMES + REPL
    spack = jnp.concatenate([jnp.stack([grads[l][n] for l in range(DEPTH)]).reshape(-1) for n in small_names])
    n_small = spack.shape[0]
    gsmall = sum_slots(all_gather([_pad_rows(spack, 8)], "gather_small")[0], "sum_small").reshape(-1)

    off = 0
    for n in small_names:
        shp = (DEPTH, 3, D_MODEL) if n in LN_NAMES else Wp[n].shape
        sz = math.prod(shp)
        G[n] = gsmall[off:off + sz].reshape(shp)
        off += sz
    for n in LN_NAMES:
        G[n] = lax.dynamic_slice_in_dim(G[n], my * (D_MODEL // N_DEV), D_MODEL // N_DEV, axis=2)

    delta, new_m, new_v = {}, {}, {}
    for n in BIG:
        shp = Wp[n].shape
        two = (shp[0] * shp[1], shp[2])
        d_, m_, v_ = adamw(Wp[n].reshape(two), G[n].reshape(two), Mp[n].reshape(two), Vp[n].reshape(two), "adamw_" + n)
        delta[n], new_m[n], new_v[n] = d_.reshape(shp), m_.reshape(shp), v_.reshape(shp)

    def pack_small(src):
        return _pad_rows(jnp.concatenate([src[n].reshape(-1) for n in small_names]), 8)

    d_, m_, v_ = adamw(pack_small(Wp), pack_small(G), pack_small(Mp), pack_small(Vp), "adamw_small")
    d_, m_, v_ = d_.reshape(-1), m_.reshape(-1), v_.reshape(-1)
    off = 0
    for n in small_names:
        shp = Wp[n].shape
        sz = math.prod(shp)
        delta[n], new_m[n], new_v[n] = (t[off:off + sz].reshape(shp) for t in (d_, m_, v_))
        off += sz

    return (loss, grad_x, *[G[n] for n in W_NAMES], *[delta[n] for n in W_NAMES],
            *[new_m[n] for n in W_NAMES], *[new_v[n] for n in W_NAMES])
```

```python
import functools
import math

import jax
import jax.numpy as jnp
from jax import lax
from jax.experimental import pallas as pl
from jax.experimental.pallas import tpu as pltpu

f32 = jnp.float32
bf16 = jnp.bfloat16

D_MODEL = 1024
DEPTH = 4
D_FF = 2816
GM_HEADS, GM_HEAD_DIM, GM_WIDTH, GM_CHUNK = 4, 64, 256, 128
MLA_HEADS, MLA_NOPE, MLA_ROPE, MLA_V = 8, 64, 32, 64
ROPE_HALF = MLA_ROPE // 2
Q_LORA, KV_LORA = 256, 128
ROPE_BASE = 10000.0
SSM_GROUPS, SSM_GROUP_CH, SSM_WIDTH, SSM_STATE = 16, 16, 256, 64
N_STATE = SSM_GROUPS * SSM_STATE
ALPHA = (2 * DEPTH) ** 0.25
LN_EPS = 1e-5
RMS_EPS = 1e-6
NEG_BIG = -1e30
ATT_SCALE = (MLA_NOPE + MLA_ROPE) ** -0.5
ADAM_LR, ADAM_B1, ADAM_B2, ADAM_EPS, ADAM_WD, ADAM_STEP = 0.001, 0.9, 0.999, 1e-08, 0.01, 10

N_DEV = 8
LANES = 128
VMEM_LIMIT = 48 * 1024 * 1024
MM_TILE_BUDGET = 32 * 1024 * 1024
MESH = pl.DeviceIdType.MESH

H_UG, H_VG, H_CQ, H_US, H_CKV, H_K1, H_K2, H_COLS = 0, 256, 512, 768, 1024, 1152, 1280, 1408

W_NAMES = ['ln_g', 'ln_b', 'ffn1_w_gate', 'ffn1_w_up', 'ffn1_w_down', 'w_in', 'gmlp_norm_g', 'gmlp_ws', 'gmlp_bs',
           'mla_q_norm_g', 'mla_w_uq', 'mla_kv_norm_g', 'mla_w_ukv', 'ssm_a_re', 'ssm_a_im', 'ssm_b_re', 'ssm_b_im',
           'ssm_c_re', 'ssm_c_im', 'ssm_d', 'ssm_log_dt', 'ssm_glu_w', 'ssm_glu_b', 'mix_norm_g', 'w_out',
           'ffn2_w_gate', 'ffn2_w_up', 'ffn2_w_down']
BIG = {'ffn1_w_gate': 1, 'ffn1_w_up': 1, 'ffn1_w_down': 0, 'w_in': 1, 'mla_w_uq': 1, 'mla_w_ukv': 1,
       'ssm_glu_w': 0, 'w_out': 0, 'ffn2_w_gate': 1, 'ffn2_w_up': 1, 'ffn2_w_down': 0}
BIG_SHAPE = {'ffn1_w_gate': (D_MODEL, D_FF), 'ffn1_w_up': (D_MODEL, D_FF), 'ffn1_w_down': (D_FF, D_MODEL),
             'w_in': (D_MODEL, 1184), 'mla_w_uq': (Q_LORA, 768), 'mla_w_ukv': (KV_LORA, 1024),
             'ssm_glu_w': (SSM_WIDTH, SSM_WIDTH), 'w_out': (D_MODEL, D_MODEL),
             'ffn2_w_gate': (D_MODEL, D_FF), 'ffn2_w_up': (D_MODEL, D_FF), 'ffn2_w_down': (D_FF, D_MODEL)}
LN_NAMES = ['ln_g', 'ln_b']
REPL = [n for n in W_NAMES if n not in BIG and n not in LN_NAMES]


def _pick(n, cands):
    for c in cands:
        if n % c == 0:
            return c
    return n


def _params(sem):
    return pltpu.CompilerParams(dimension_semantics=sem, vmem_limit_bytes=VMEM_LIMIT)


S3_ROWS = 2 * N_STATE // LANES


def _from_s3(ref):
    return jnp.concatenate([ref[:, c, :] for c in range(S3_ROWS)], axis=1)


def _to_s3(ref, val):
    for c in range(S3_ROWS):
        ref[:, c, :] = val[:, c * LANES:(c + 1) * LANES].astype(ref.dtype)


def mm(a, b, *, out_dtype=f32, add=None, add_scale=1.0, a_col0=0, a_s3=False, out_s3=False, grouped=False, b_nt=False, name):
    G = a.shape[0] if grouped else 1
    M = a.shape[1] if grouped else a.shape[0]
    K, N = (b.shape[-1], b.shape[-2]) if b_nt else b.shape[-2:]
    tn = N if out_s3 else _pick(N, (512, 384, 256) if N <= 1536 else (512, 384, 256, 128))
    tk = K if (K <= 2 * D_FF or a_s3) else _pick(K, (1024, 512, 256, 128))
    nk = G * (K // tk)
    assert not grouped or tk == K

    def tile_bytes(tm):
        return 2 * (tm * tk * a.dtype.itemsize + tk * tn * b.dtype.itemsize + tm * tn * 4 * (2 if add is not None else 1))

    tm = next(t for t in (1024, 512, 256, 128, 64, 32, 16, 8) if M % t == 0 and (tile_bytes(t) <= MM_TILE_BUDGET or t == 8))
    assert a_s3 or grouped or (a_col0 % tk == 0 and a_col0 + K <= a.shape[1])
    kb0 = a_col0 // tk
    has_add = add is not None

    def body(*refs):
        if has_add:
            a_ref, b_ref, add_ref, o_ref, acc = refs
        else:
            a_ref, b_ref, o_ref, acc = refs
        k = pl.program_id(2)
        a_val = _from_s3(a_ref) if a_s3 else a_ref[...]
        part = (_nt if b_nt else functools.partial(jnp.dot, preferred_element_type=f32))(a_val.astype(bf16), b_ref[...].astype(bf16))

        def finish(total):
            if has_add:
                total = total + add_scale * add_ref[...]
            if out_s3:
                _to_s3(o_ref, total)
            else:
                o_ref[...] = total.astype(o_ref.dtype)

        if nk == 1:
            finish(part)
        else:
            @pl.when(k == 0)
            def _():
                acc[...] = part

            @pl.when(k > 0)
            def _():
                acc[...] += part

            @pl.when(k == nk - 1)
            def _():
                finish(acc[...])

    if a_s3:
        a_spec = pl.BlockSpec((tm, S3_ROWS, LANES), lambda i, j, k: (i, 0, 0))
    elif grouped:
        a_spec = pl.BlockSpec((None, tm, tk), lambda i, j, k: (k, i, 0))
    else:
        a_spec = pl.BlockSpec((tm, tk), lambda i, j, k: (i, kb0 + k))
    if b_nt:
        b_spec = pl.BlockSpec((None, tn, tk), lambda i, j, k: (k, j, 0)) if grouped else pl.BlockSpec((tn, tk), lambda i, j, k: (j, k))
    else:
        b_spec = pl.BlockSpec((None, tk, tn), lambda i, j, k: (k, 0, j)) if grouped else pl.BlockSpec((tk, tn), lambda i, j, k: (k, j))
    in_specs, ops = [a_spec, b_spec], [a, b]
    if has_add:
        in_specs.append(pl.BlockSpec((tm, tn), lambda i, j, k: (i, j)))
        ops.append(add)
    if out_s3:
        out_spec = pl.BlockSpec((tm, S3_ROWS, LANES), lambda i, j, k: (i, 0, 0))
        out_shape = jax.ShapeDtypeStruct((M, S3_ROWS, LANES), out_dtype)
    else:
        out_spec = pl.BlockSpec((tm, tn), lambda i, j, k: (i, j))
        out_shape = jax.ShapeDtypeStruct((M, N), out_dtype)
    return pl.pallas_call(
        body, grid=(M // tm, N // tn, nk), in_specs=in_specs, out_specs=out_spec, out_shape=out_shape,
        scratch_shapes=[pltpu.VMEM((tm, tn) if nk > 1 else (8, LANES), f32)],
        compiler_params=_params(("parallel", "parallel", "arbitrary")), name=name)(*ops)


def mm_tn(a, b, *, a_col0=0, m_dim=None, a_s3=False, b_s3=False, a_lead=None, name):
    K = a.shape[-2] if a_lead is not None else a.shape[0]
    M = 2 * N_STATE if a_s3 else (a.shape[-1] if m_dim is None else m_dim)
    N = 2 * N_STATE if b_s3 else b.shape[-1]
    tm = M if a_s3 else _pick(M, (H_COLS, 1024, 768, 512, 384, 256, 128))
    tn = N if b_s3 else _pick(N, (H_COLS, 1024, 768, 512, 384, 256))

    def tile_bytes(tk):
        return 2 * tk * (tm * a.dtype.itemsize + tn * b.dtype.itemsize) + 3 * tm * tn * 4

    tk = next(t for t in (2048, 1024, 512, 256, 128, 64, 32, 16) if K % t == 0 and (tile_bytes(t) <= MM_TILE_BUDGET or t == 16))
    nk = K // tk
    assert a_col0 % tm == 0
    mb0 = a_col0 // tm

    def body(a_ref, b_ref, o_ref, acc):
        k = pl.program_id(2)
        a_val = _from_s3(a_ref) if a_s3 else a_ref[...]
        b_val = _from_s3(b_ref) if b_s3 else b_ref[...]
        part = lax.dot_general(a_val.astype(bf16), b_val.astype(bf16), (((0,), (0,)), ((), ())), preferred_element_type=f32)

        @pl.when(k == 0)
        def _():
            acc[...] = part

        @pl.when(k > 0)
        def _():
            acc[...] += part

        @pl.when(k == nk - 1)
        def _():
            o_ref[...] = acc[...]

    s3_spec = pl.BlockSpec((tk, S3_ROWS, LANES), lambda i, j, k: (k, 0, 0))
    if a_s3:
        a_spec = s3_spec
    elif a_lead is not None:
        a_spec = pl.BlockSpec((None, tk, tm), lambda i, j, k: (a_lead, k, i))
    else:
        a_spec = pl.BlockSpec((tk, tm), lambda i, j, k: (k, mb0 + i))
    return pl.pallas_call(
        body, grid=(M // tm, N // tn, nk),
        in_specs=[a_spec, s3_spec if b_s3 else pl.BlockSpec((tk, tn), lambda i, j, k: (k, j))],
        out_specs=pl.BlockSpec((tm, tn), lambda i, j, k: (i, j)),
        out_shape=jax.ShapeDtypeStruct((M, N), f32),
        scratch_shapes=[pltpu.VMEM((tm, tn), f32)],
        compiler_params=_params(("parallel", "parallel", "arbitrary")), name=name)(a, b)


def rowwise(fn, rows, pars, out_rows, out_accs, tm, name):
    first = rows[0]
    if isinstance(first, tuple):
        R = first[0].shape[1] if first[1] == 'lead' else first[0].shape[0]
    else:
        R = first.shape[0]
    assert R % tm == 0, (R, tm, name)
    n_rows, n_pars, n_or, n_oa = len(rows), len(pars), len(out_rows), len(out_accs)

    in_specs, ops = [], []
    for r in rows:
        if isinstance(r, tuple) and r[1] == 'lead':
            arr, _, kk = r
            in_specs.append(pl.BlockSpec((None, tm, arr.shape[2]), lambda i, kk=kk: (kk, i, 0)))
        elif isinstance(r, tuple):
            arr, c0, w = r
            assert c0 % w == 0
            in_specs.append(pl.BlockSpec((tm, w), lambda i, cb=c0 // w: (i, cb)))
        else:
            arr = r
            in_specs.append(pl.BlockSpec((tm, arr.shape[1]), lambda i: (i, 0)))
        ops.append(arr)
    for p in pars:
        in_specs.append(pl.BlockSpec(p.shape, lambda i, nd=p.ndim: (0,) * nd))
        ops.append(p)
    out_specs = [pl.BlockSpec((tm, w), lambda i: (i, 0)) for (w, _) in out_rows]
    out_specs += [pl.BlockSpec(s, lambda i, nd=len(s): (0,) * nd) for s in out_accs]
    out_shape = [jax.ShapeDtypeStruct((R, w), dt) for (w, dt) in out_rows]
    out_shape += [jax.ShapeDtypeStruct(s, f32) for s in out_accs]

    def body(*refs):
        ins = [r[...] for r in refs[:n_rows + n_pars]]
        o_refs = refs[n_rows + n_pars:]
        res = fn(*ins)
        if not isinstance(res, (tuple, list)):
            res = (res,)
        assert len(res) == n_or + n_oa, (len(res), n_or, n_oa, name)
        for o, v in zip(o_refs[:n_or], res[:n_or]):
            o[...] = v.astype(o.dtype)
        if n_oa:
            i = pl.program_id(0)

            @pl.when(i == 0)
            def _():
                for o, v in zip(o_refs[n_or:], res[n_or:]):
                    o[...] = v.astype(f32)

            @pl.when(i > 0)
            def _():
                for o, v in zip(o_refs[n_or:], res[n_or:]):
                    o[...] += v.astype(f32)

    return pl.pallas_call(
        body, grid=(R // tm,), in_specs=in_specs, out_specs=out_specs, out_shape=out_shape,
        compiler_params=_params(("arbitrary",)), name=name)(*ops)


def whole(fn, ins, out_shapes, name):
    def body(*refs):
        res = fn(*[r[...] for r in refs[:len(ins)]])
        for o, v in zip(refs[len(ins):], res):
            o[...] = v

    return pl.pallas_call(body, out_shape=[jax.ShapeDtypeStruct(s, f32) for s in out_shapes], name=name)(*ins)


@jax.custom_vjp
def _bdot(a, b):
    return jnp.dot(a.astype(bf16), b.astype(bf16), preferred_element_type=f32)


def _bdot_fwd(a, b):
    return _bdot(a, b), (a, b)


def _bdot_bwd(res, g):
    a, b = res
    gb = g.astype(bf16)
    da = lax.dot_general(gb, b.astype(bf16), (((1,), (1,)), ((), ())), preferred_element_type=f32)
    db = lax.dot_general(a.astype(bf16), gb, (((0,), (0,)), ((), ())), preferred_element_type=f32)
    return da, db


_bdot.defvjp(_bdot_fwd, _bdot_bwd)


def _ln(z, g, b):
    mu = jnp.mean(z, axis=-1, keepdims=True)
    var = jnp.mean(jnp.square(z - mu), axis=-1, keepdims=True)
    return (z - mu) * lax.rsqrt(var + LN_EPS) * g + b


def _rms_only(x):
    return x * lax.rsqrt(jnp.mean(jnp.square(x), axis=-1, keepdims=True) + RMS_EPS)


def _swiglu(a, b):
    return jax.nn.silu(a) * b


def _gmlp(hu, hv, ng, ws, bsb):
    u = jax.nn.gelu(hu)
    v = jax.nn.gelu(hv)
    lane = lax.broadcasted_iota(jnp.int32, (1, GM_WIDTH), 1)
    masks = [((lane >= GM_HEAD_DIM * h) & (lane < GM_HEAD_DIM * (h + 1))).astype(f32) for h in range(GM_HEADS)]
    mu = jnp.zeros_like(v)
    for m in masks:
        mu = mu + m * (jnp.sum(v * m, axis=-1, keepdims=True) / GM_HEAD_DIM)
    d = v - mu
    var = jnp.zeros_like(v)
    for m in masks:
        var = var + m * (jnp.sum(d * d * m, axis=-1, keepdims=True) / GM_HEAD_DIM)
    vn = d * lax.rsqrt(var + LN_EPS) * ng
    r = lax.broadcasted_iota(jnp.int32, (GM_CHUNK, GM_CHUNK), 0)
    c = lax.broadcasted_iota(jnp.int32, (GM_CHUNK, GM_CHUNK), 1)
    tril = (c <= r).astype(f32)
    z = bsb
    for h, m in enumerate(masks):
        z = z + _bdot(ws[h] * tril, vn * m)
    return u * z


def _mla_prep(cq, ckv, qg, kvg):
    return _rms_only(cq) * qg, _rms_only(ckv) * kvg


def _rope(q1, q2, k1, k2, cos, sin):
    return q1 * cos - q2 * sin, q2 * cos + q1 * sin, k1 * cos - k2 * sin, k2 * cos + k1 * sin


def _mix_post(ya, ob, y1, us, dskip, gluw, glub, gmix):
    y = jax.nn.gelu(y1 + dskip * us)
    yc = y * jax.nn.sigmoid(_bdot(y, gluw) + glub)
    return jnp.concatenate([_rms_only(ya), _rms_only(ob), _rms_only(yc)], axis=1) * gmix


def _ssm_prep(ar, ai, ldt, brT, biT):
    dt = jnp.exp(ldt)
    mag = jnp.exp(ar * dt)
    abr = mag * jnp.cos(ai * dt)
    abi = mag * jnp.sin(ai * dt)
    den = ar * ar + ai * ai
    cr = ((abr - 1.0) * ar + abi * ai) / den
    ci = (abi * ar - (abr - 1.0) * ai) / den
    return abr, abi, cr * brT - ci * biT, cr * biT + ci * brT


ATT_FWD_HEADS = 2


def _att_tile(S):
    return _pick(S, (512, 256, 128))


def _nt(a, b):
    return lax.dot_general(a, b, (((1,), (1,)), ((), ())), preferred_element_type=f32)


def _diag_keep(T):
    krow = lax.broadcasted_iota(jnp.int32, (T, T), 0)
    qcol = lax.broadcasted_iota(jnp.int32, (T, T), 1)
    return qcol >= krow


def grid_call(body, exch, *, grid, in_specs, out_specs, out_shape, scratch_shapes, name, args):
    params = _params(("arbitrary", "arbitrary"))
    if exch is None:
        return pl.pallas_call(body, grid=grid, in_specs=in_specs, out_specs=out_specs, out_shape=out_shape,
                              scratch_shapes=scratch_shapes, compiler_params=params, name=name)(*args)
    gather, xs = exch
    n, n_in, n_out, n_sc = len(xs), len(in_specs), len(out_specs), len(scratch_shapes)

    def riding(*refs):
        ins, x_refs = refs[:n_in], refs[n_in:n_in + n]
        outs, xo_refs = refs[n_in + n:n_in + n + n_out], refs[n_in + n + n_out:n_in + 2 * n + n_out]
        rest = refs[n_in + 2 * n + n_out:]
        scratch, sems = rest[:n_sc], rest[n_sc:]
        h, i = pl.program_id(0), pl.program_id(1)

        @pl.when((h == 0) & (i == 0))
        def _():
            _direct_exchange(gather, x_refs, xo_refs, *sems, start=True)

        body(*ins, *outs, *scratch)

        @pl.when((h == grid[0] - 1) & (i == grid[1] - 1))
        def _():
            _direct_exchange(gather, x_refs, xo_refs, *sems, start=False)

    return pl.pallas_call(
        riding, grid=grid, in_specs=list(in_specs) + [HBM_SPEC] * n, out_specs=list(out_specs) + [HBM_SPEC] * n,
        out_shape=list(out_shape) + _exchange_out_shapes(gather, xs),
        scratch_shapes=list(scratch_shapes) + _exchange_scratch(n), compiler_params=params,
        name=name + ("_gather" if gather else "_scatter"))(*args, *xs)


def flash_fwd(q, k, vT, exch=None):
    Hh, S, _ = q.shape
    T = _att_tile(S)
    HB = ATT_FWD_HEADS

    def body(q_ref, k_ref, vT_ref, o_ref, lse_ref, m_sc, l_sc, acc_sc):
        i = pl.program_id(1)
        m_sc[...] = jnp.full_like(m_sc, NEG_BIG)
        l_sc[...] = jnp.zeros_like(l_sc)
        acc_sc[...] = jnp.zeros_like(acc_sc)

        def block(j, diagonal):
            rows = pl.ds(pl.multiple_of(j * T, T), T)
            for hh in range(HB):
                sT = _nt(k_ref[hh, rows, :], q_ref[hh]) * ATT_SCALE
                if diagonal:
                    sT = jnp.where(_diag_keep(T), sT, NEG_BIG)
                m_old = m_sc[hh]
                m_new = jnp.maximum(m_old, jnp.max(sT, axis=0, keepdims=True))
                alpha = jnp.exp(m_old - m_new)
                pT = jnp.exp(sT - m_new)
                l_sc[hh] = alpha * l_sc[hh] + jnp.sum(pT, axis=0, keepdims=True)
                acc_sc[hh] = alpha * acc_sc[hh] + jnp.dot(vT_ref[hh, :, rows], pT.astype(bf16), preferred_element_type=f32)
                m_sc[hh] = m_new

        def loop_body(j, c):
            block(j, False)
            return c

        lax.fori_loop(0, i, loop_body, 0)
        block(i, True)
        o_ref[...] = acc_sc[...] / l_sc[...]
        lse_ref[...] = m_sc[...] + jnp.log(l_sc[...])

    return grid_call(
        body, exch, grid=(Hh // HB, S // T),
        in_specs=[pl.BlockSpec((HB, T, LANES), lambda h, i: (h, i, 0)), pl.BlockSpec((HB, S, LANES), lambda h, i: (h, 0, 0)),
                  pl.BlockSpec((HB, MLA_V, S), lambda h, i: (h, 0, 0))],
        out_specs=[pl.BlockSpec((HB, MLA_V, T), lambda h, i: (h, 0, i)), pl.BlockSpec((HB, 1, T), lambda h, i: (h, 0, i))],
        out_shape=[jax.ShapeDtypeStruct((Hh, MLA_V, S), f32), jax.ShapeDtypeStruct((Hh, 1, S), f32)],
        scratch_shapes=[pltpu.VMEM((HB, 1, T), f32), pltpu.VMEM((HB, 1, T), f32), pltpu.VMEM((HB, MLA_V, T), f32)],
        name="flash_fwd", args=(q, k, vT))


def flash_bwd(q, k, kT, v, do, lse, delta, exch=None):
    Hh, S, _ = q.shape
    T = _att_tile(S)

    def body(q_ref, do_ref, lse_ref, dl_ref, k_ref, kT_ref, v_ref, dq_ref, dk_ref, dv_ref, dq_sc):
        i = pl.program_id(1)

        @pl.when(i == 0)
        def _():
            dk_ref[...] = jnp.zeros_like(dk_ref)
            dv_ref[...] = jnp.zeros_like(dv_ref)

        qi, doi = q_ref[0], do_ref[0]
        lse_i, dl_i = lse_ref[0], dl_ref[0]
        dq_sc[...] = jnp.zeros_like(dq_sc)

        def block(j, diagonal):
            rows = pl.ds(pl.multiple_of(j * T, T), T)
            sT = _nt(k_ref[0, rows, :], qi) * ATT_SCALE
            pT = jnp.exp(sT - lse_i)
            if diagonal:
                pT = jnp.where(_diag_keep(T), pT, 0.0)
            dpT = _nt(v_ref[0, rows, :], doi)
            dsT = (pT * (dpT - dl_i) * ATT_SCALE).astype(bf16)
            dv_ref[0, rows, :] += jnp.dot(pT.astype(bf16), doi, preferred_element_type=f32)
            dk_ref[0, rows, :] += jnp.dot(dsT, qi, preferred_element_type=f32)
            dq_sc[...] += jnp.dot(kT_ref[0, :, rows], dsT, preferred_element_type=f32)

        def loop_body(j, c):
            block(j, False)
            return c

        lax.fori_loop(0, i, loop_body, 0)
        block(i, True)
        dq_ref[0] = dq_sc[...]

    tile = lambda w: pl.BlockSpec((1, T, w), lambda h, i: (h, i, 0))
    row = pl.BlockSpec((1, 1, T), lambda h, i: (h, 0, i))
    full = lambda w: pl.BlockSpec((1, S, w), lambda h, i: (h, 0, 0))
    return grid_call(
        body, exch, grid=(Hh, S // T),
        in_specs=[tile(LANES), tile(MLA_V), row, row, full(LANES), pl.BlockSpec((1, LANES, S), lambda h, i: (h, 0, 0)), full(MLA_V)],
        out_specs=[pl.BlockSpec((1, LANES, T), lambda h, i: (h, 0, i)), full(LANES), full(MLA_V)],
        out_shape=[jax.ShapeDtypeStruct((Hh, LANES, S), f32), jax.ShapeDtypeStruct((Hh, S, LANES), f32),
                   jax.ShapeDtypeStruct((Hh, S, MLA_V), f32)],
        scratch_shapes=[pltpu.VMEM((LANES, T), f32)],
        name="flash_bwd", args=(q, do, lse, delta, k, kT, v))


def _scan_tile(S):
    return _pick(S, (256, 128, 64, 32, 16, 8))


def scan_fwd(bu3, a16):
    S = bu3.shape[0]
    ts = _scan_tile(S)

    def body(bu_ref, a_ref, o_ref, h_sc):
        @pl.when(pl.program_id(0) == 0)
        def _():
            h_sc[...] = jnp.zeros_like(h_sc)

        ar, ai = a_ref[0:8, :], a_ref[8:16, :]

        def step(t, carry):
            hr, hi = carry
            nr = ar * hr - ai * hi + bu_ref[t, 0:8, :]
            ni = ar * hi + ai * hr + bu_ref[t, 8:16, :]
            o_ref[t, 0:8, :] = nr
            o_ref[t, 8:16, :] = ni
            return nr, ni

        hr, hi = lax.fori_loop(0, ts, step, (h_sc[0:8, :], h_sc[8:16, :]), unroll=8)
        h_sc[0:8, :] = hr
        h_sc[8:16, :] = hi

    blk = pl.BlockSpec((ts, 16, LANES), lambda i: (i, 0, 0))
    return pl.pallas_call(
        body, grid=(S // ts,), in_specs=[blk, pl.BlockSpec((16, LANES), lambda i: (0, 0))], out_specs=blk,
        out_shape=jax.ShapeDtypeStruct(bu3.shape, f32), scratch_shapes=[pltpu.VMEM((16, LANES), f32)],
        compiler_params=_params(("arbitrary",)), name="scan_fwd")(bu3, a16)


def scan_bwd(g3, h3, a16):
    S = g3.shape[0]
    ts = _scan_tile(S)
    nb = S // ts

    def body(g_ref, h_ref, a_ref, o_ref, da_ref, lam_sc, da_sc):
        @pl.when(pl.program_id(0) == 0)
        def _():
            lam_sc[...] = jnp.zeros_like(lam_sc)
            da_sc[...] = jnp.zeros_like(da_sc)

        ar, ai = a_ref[0:8, :], a_ref[8:16, :]

        def step(kk, carry):
            lr, li, dar, dai = carry
            t = ts - 1 - kk
            hr, hi = h_ref[t, 0:8, :], h_ref[t, 8:16, :]
            dar = dar + lr * hr + li * hi
            dai = dai + li * hr - lr * hi
            nlr = ar * lr + ai * li + g_ref[t, 0:8, :]
            nli = ar * li - ai * lr + g_ref[t, 8:16, :]
            o_ref[t, 0:8, :] = nlr
            o_ref[t, 8:16, :] = nli
            return nlr, nli, dar, dai

        lr, li, dar, dai = lax.fori_loop(
            0, ts, step, (lam_sc[0:8, :], lam_sc[8:16, :], da_sc[0:8, :], da_sc[8:16, :]), unroll=8)
        lam_sc[0:8, :] = lr
        lam_sc[8:16, :] = li
        da_sc[0:8, :] = dar
        da_sc[8:16, :] = dai
        da_ref[0:8, :] = dar
        da_ref[8:16, :] = dai

    blk = pl.BlockSpec((ts, 16, LANES), lambda i: (nb - 1 - i, 0, 0))
    small = pl.BlockSpec((16, LANES), lambda i: (0, 0))
    return pl.pallas_call(
        body, grid=(nb,), in_specs=[blk, blk, small], out_specs=[blk, small],
        out_shape=[jax.ShapeDtypeStruct(g3.shape, f32), jax.ShapeDtypeStruct((16, LANES), f32)],
        scratch_shapes=[pltpu.VMEM((16, LANES), f32), pltpu.VMEM((16, LANES), f32)],
        compiler_params=_params(("arbitrary",)), name="scan_bwd")(g3, h3, a16)


HBM_SPEC = pl.BlockSpec(memory_space=pltpu.HBM)


def _my_id():
    return 4 * lax.axis_index("x") + 2 * lax.axis_index("y") + lax.axis_index("c")


def all_gather(xs, name):
    n = len(xs)

    def body(*refs):
        x_refs, o_refs = refs[:n], refs[n:2 * n]
        send_sems, recv_sems, local_sems = refs[2 * n:]
        x, y, c = lax.axis_index("x"), lax.axis_index("y"), lax.axis_index("c")
        me, sibling = (x, y, c), (x, y, 1 - c)
        chips = [(1 - x, y), (x, 1 - y), (1 - x, 1 - y)]

        def slot(o, p):
            return o.at[4 * p[0] + 2 * p[1] + p[2]]

        def copy(a, k, block, to, src=None):
            o = o_refs[a]
            return pltpu.make_async_remote_copy(
                src_ref=slot(o, block) if src is None else src, dst_ref=slot(o, block),
                send_sem=send_sems.at[7 * a + k], recv_sem=recv_sems.at[7 * a + k], device_id=to, device_id_type=MESH)

        own, sends = [], []
        for a in range(n):
            mine = pltpu.make_async_copy(x_refs[a], slot(o_refs[a], me), local_sems.at[a])
            mine.start()
            own.append(mine)
            first = [copy(a, 0, me, sibling, src=x_refs[a])]
            first += [copy(a, 1 + j, me, (*chip, c), src=x_refs[a]) for j, chip in enumerate(chips)]
            for cp in first:
                cp.start()
            sends += first
        for a in range(n):
            for j, chip in enumerate(chips):
                copy(a, 1 + j, (*chip, c), me).wait_recv()
                fwd = copy(a, 4 + j, (*chip, c), sibling)
                fwd.start()
                sends.append(fwd)
        for a in range(n):
            copy(a, 0, sibling, me).wait_recv()
            for j, chip in enumerate(chips):
                copy(a, 4 + j, (*chip, 1 - c), me).wait_recv()
        for cp in sends:
            cp.wait_send()
        for cp in own:
            cp.wait()

    return pl.pallas_call(
        body, out_shape=[jax.ShapeDtypeStruct((N_DEV,) + v.shape, v.dtype) for v in xs],
        in_specs=[HBM_SPEC] * n, out_specs=[HBM_SPEC] * n,
        scratch_shapes=[pltpu.SemaphoreType.DMA((7 * n,)), pltpu.SemaphoreType.DMA((7 * n,)), pltpu.SemaphoreType.DMA((n,))],
        name=name)(*xs)


def _direct_exchange(gather, x_refs, o_refs, send_sems, recv_sems, local_sems, start):
    x, y, c = lax.axis_index("x"), lax.axis_index("y"), lax.axis_index("c")
    my = 4 * x + 2 * y + c
    for a, (x_ref, o_ref) in enumerate(zip(x_refs, o_refs)):
        mine = pltpu.make_async_copy(x_ref if gather else x_ref.at[my], o_ref.at[my], local_sems.at[a])
        if start:
            mine.start()
        else:
            mine.wait()
        for k in range(1, N_DEV):
            px = 1 - x if k & 4 else x
            py = 1 - y if k & 2 else y
            pc = 1 - c if k & 1 else c
            pid = 4 * px + 2 * py + pc
            src = x_ref if gather else x_ref.at[pid]
            sems = dict(send_sem=send_sems.at[7 * a + k - 1], recv_sem=recv_sems.at[7 * a + k - 1],
                        device_id=(px, py, pc), device_id_type=MESH)
            if start:
                pltpu.make_async_remote_copy(src_ref=src, dst_ref=o_ref.at[my], **sems).start()
            else:
                pltpu.make_async_remote_copy(src_ref=src, dst_ref=o_ref.at[my], **sems).wait_send()
                pltpu.make_async_remote_copy(src_ref=src, dst_ref=o_ref.at[pid], **sems).wait_recv()


def _exchange_scratch(n):
    return [pltpu.SemaphoreType.DMA((7 * n,)), pltpu.SemaphoreType.DMA((7 * n,)), pltpu.SemaphoreType.DMA((n,))]


def _exchange_out_shapes(gather, xs):
    return [jax.ShapeDtypeStruct(((N_DEV,) + v.shape) if gather else v.shape, v.dtype) for v in xs]


def all_to_all(xs, name):
    n = len(xs)

    def body(*refs):
        ex = (False, refs[:n], refs[n:2 * n], *refs[2 * n:])
        _direct_exchange(*ex, start=True)
        _direct_exchange(*ex, start=False)

    return pl.pallas_call(
        body, out_shape=_exchange_out_shapes(False, xs), in_specs=[HBM_SPEC] * n, out_specs=[HBM_SPEC] * n,
        scratch_shapes=_exchange_scratch(n), name=name)(*xs)


def sum_slots(g8, name):
    R, C = g8.shape[1:]

    def fn(*tiles):
        tot = tiles[0].astype(f32)
        for t in tiles[1:]:
            tot = tot + t.astype(f32)
        return tot

    return rowwise(fn, [(g8, 'lead', k) for k in range(N_DEV)], [], [(C, f32)], [], _pick(R, (256, 128, 64, 32, 16, 8)), name)[0]


PACK_W = 1024


def _pad_rows(flat, mult):
    n = flat.shape[0]
    tot = -(-n // (PACK_W * mult)) * PACK_W * mult
    return jnp.pad(flat, (0, tot - n)).reshape(tot // PACK_W, PACK_W)


BF16_TILE_ROWS = 16


def _pad_tile_rows(a, axis):
    pad = [(0, 0)] * a.ndim
    pad[axis] = (0, -a.shape[axis] % BF16_TILE_ROWS)
    return jnp.pad(a, pad)


def _shard_shape(name):
    r, c = BIG_SHAPE[name]
    return (r // N_DEV, c) if BIG[name] == 0 else (r, c // N_DEV)


def _pack_rows(name):
    r, c = _shard_shape(name)
    assert (r * c) % PACK_W == 0
    return r * c // PACK_W


TRANSPOSED = ('ffn1_w_gate', 'ffn1_w_up', 'ffn2_w_gate', 'ffn2_w_up')
assert all(BIG_SHAPE[n][0] == PACK_W and BIG[n] == 1 for n in TRANSPOSED)


def _shard_to_rows(name, shard):
    return (shard.T if name in TRANSPOSED else shard).reshape(_pack_rows(name), PACK_W)


def _rows_to_shard(name, rows):
    r, c = _shard_shape(name)
    return rows.reshape(c, r).T if name in TRANSPOSED else rows.reshape(r, c)


def _split_for_devices(name, full):
    r, c = BIG_SHAPE[name]
    if BIG[name] == 0 or name in TRANSPOSED:
        return full.reshape(N_DEV, _pack_rows(name), PACK_W)
    return full.reshape(r, N_DEV, c // N_DEV).transpose(1, 0, 2).reshape(N_DEV, _pack_rows(name), PACK_W)


def _join_from_devices(name, parts):
    r, c = BIG_SHAPE[name]
    if name in TRANSPOSED:
        return parts.reshape(c, r)
    if BIG[name] == 0:
        return parts.reshape(r, c)
    return parts.reshape(N_DEV, r, c // N_DEV).transpose(1, 0, 2).reshape(r, c)


def _row_tile(S, want):
    return _pick(S, tuple(t for t in (512, 256, 128, 64, 32, 16) if t <= want))


def resid_ln(x, f, g, b, scale, name):
    D = x.shape[1]

    def fn(x, f, g, b):
        z = ALPHA * x + scale * f
        xo = _ln(z, g, b)
        return z, xo, xo

    return rowwise(fn, [x, f], [g, b], [(D, f32), (D, f32), (D, bf16)], [], _row_tile(x.shape[0], 512), name)


def ln_bwd(z, g, b, dxo, scale, name):
    D = z.shape[1]

    def fn(z, dxo, g, b):
        _, vjp = jax.vjp(_ln, z, g, b)
        dz, dg, db = vjp(dxo)
        return dz, scale * dz, dg, db

    return rowwise(fn, [z, dxo], [g, b], [(D, f32), (D, bf16)], [(1, D), (1, D)], _row_tile(z.shape[0], 256), name)


FF_TILE = 256


def ffn_up_act(xb, wgT, wuT):
    M, K = xb.shape
    tm = _pick(M, (1024, 512, 256, 128, 64, 32, 16))

    def body(x_ref, wg_ref, wu_ref, ab_ref, h_ref):
        x = x_ref[...]
        a = _nt(x, wg_ref[...])
        b = _nt(x, wu_ref[...])
        ab_ref[0] = a.astype(bf16)
        ab_ref[1] = b.astype(bf16)
        h_ref[...] = _swiglu(a, b).astype(bf16)

    wspec = pl.BlockSpec((FF_TILE, K), lambda i, j: (j, 0))
    return pl.pallas_call(
        body, grid=(M // tm, D_FF // FF_TILE),
        in_specs=[pl.BlockSpec((tm, K), lambda i, j: (i, 0)), wspec, wspec],
        out_specs=[pl.BlockSpec((2, tm, FF_TILE), lambda i, j: (0, i, j)), pl.BlockSpec((tm, FF_TILE), lambda i, j: (i, j))],
        out_shape=[jax.ShapeDtypeStruct((2, M, D_FF), bf16), jax.ShapeDtypeStruct((M, D_FF), bf16)],
        compiler_params=_params(("parallel", "parallel")), name="ffn_up_act")(xb, wgT, wuT)


def ffn_down_dx_act(dzs, wd, ab):
    M, K = dzs.shape
    tm = _pick(M, (1024, 512, 256, 128, 64, 32, 16))

    def body(dz_ref, w_ref, ab_ref, dab_ref, h_ref):
        dh = _nt(dz_ref[...], w_ref[...])
        h, vjp = jax.vjp(_swiglu, ab_ref[0].astype(f32), ab_ref[1].astype(f32))
        da, db = vjp(dh)
        dab_ref[0] = da.astype(bf16)
        dab_ref[1] = db.astype(bf16)
        h_ref[...] = h.astype(bf16)

    pair = pl.BlockSpec((2, tm, FF_TILE), lambda i, j: (0, i, j))
    return pl.pallas_call(
        body, grid=(M // tm, D_FF // FF_TILE),
        in_specs=[pl.BlockSpec((tm, K), lambda i, j: (i, 0)), pl.BlockSpec((FF_TILE, K), lambda i, j: (j, 0)), pair],
        out_specs=[pair, pl.BlockSpec((tm, FF_TILE), lambda i, j: (i, j))],
        out_shape=[jax.ShapeDtypeStruct((2, M, D_FF), bf16), jax.ShapeDtypeStruct((M, D_FF), bf16)],
        compiler_params=_params(("parallel", "parallel")), name="ffn_down_dx_act")(dzs, wd, ab)


def ffn_fwd(x, xb, w, g, b):
    ab, h = ffn_up_act(xb, w['guT'][0], w['guT'][1])
    f = mm(h, w['d'], name="ffn_down")
    z, xo, xob = resid_ln(x, f, g, b, 0.5, "ffn_ln")
    return xo, xob, (xb, ab, z)


def ffn_bwd(dxo, res, w, g, b):
    xb, ab, z = res
    dz, dzs, dg, db = ln_bwd(z, g, b, dxo, 0.5, "ffn_ln_bwd")
    dab, h = ffn_down_dx_act(dzs, w['d'], ab)
    dwd = mm_tn(h, dzs, name="ffn_down_dw")
    dwgT = mm_tn(dab, xb, a_lead=0, name="ffn_gate_dw")
    dwuT = mm_tn(dab, xb, a_lead=1, name="ffn_up_dw")
    dx = mm(dab, w['guT'], add=dz, add_scale=ALPHA, grouped=True, name="ffn_up_dx")
    return dx, dwgT, dwuT, dwd, dg, db


def _heads_first(a, width):
    return a.reshape(a.shape[0], MLA_HEADS, width)


def mixer_fwd(x, xb, w, g, b, cos8, sin8, exch=None):
    S = x.shape[0]
    H = mm(xb, w['in'], name="mix_in")
    ya = rowwise(_gmlp, [(H, H_UG, 256), (H, H_VG, 256)], [w['gm_ng'], w['gm_ws'], w['gm_bsb']], [(GM_WIDTH, f32)], [],
                 GM_CHUNK, "gmlp")[0]
    cqn, ckvn = rowwise(_mla_prep, [(H, H_CQ, Q_LORA), (H, H_CKV, KV_LORA)], [w['qg'], w['kvg']],
                        [(Q_LORA, bf16), (KV_LORA, bf16)], [], _row_tile(S, 512), "mla_prep")
    qraw = mm(cqn, w['uq'], name="mla_uq")
    kv = mm(ckvn, w['ukv'], name="mla_ukv")
    q1, q2, k1, k2 = rowwise(_rope, [(qraw, 512, LANES), (qraw, 640, LANES), (H, H_K1, LANES), (H, H_K2, LANES), cos8, sin8],
                             [], [(LANES, f32)] * 4, [], _row_tile(S, 512), "rope")
    zpad = jnp.zeros((S, MLA_HEADS, LANES - MLA_NOPE - MLA_ROPE), f32)
    qp = jnp.concatenate([_heads_first(qraw[:, :512], 64), _heads_first(q1, ROPE_HALF), _heads_first(q2, ROPE_HALF), zpad], axis=2)
    k1b = jnp.broadcast_to(k1[:, None, :ROPE_HALF], (S, MLA_HEADS, ROPE_HALF))
    k2b = jnp.broadcast_to(k2[:, None, :ROPE_HALF], (S, MLA_HEADS, ROPE_HALF))
    kp = jnp.concatenate([_heads_first(kv[:, :512], 64), k1b, k2b, zpad], axis=2)
    qp = qp.transpose(1, 0, 2).astype(bf16)
    kp = kp.transpose(1, 0, 2).astype(bf16)
    v3 = _heads_first(kv[:, 512:], 64).astype(bf16)
    vp = v3.transpose(1, 0, 2)
    oT, lse, *received = flash_fwd(qp, kp, v3.transpose(1, 2, 0), exch)
    ob = oT.transpose(2, 0, 1).reshape(S, MLA_HEADS * MLA_V)
    bu3 = mm(H, w['ssm_wb'], a_col0=H_US, out_s3=True, name="ssm_bu")
    hs3 = scan_fwd(bu3, w['ssm_a16'])
    y1 = mm(hs3, w['ssm_wc'], a_s3=True, name="ssm_c")
    y = rowwise(_mix_post, [ya, ob, y1, (H, H_US, SSM_WIDTH)], [w['ssm_d'], w['glu_w'], w['glu_b'], w['gmix']],
                [(D_MODEL, bf16)], [], _row_tile(S, 256), "mix_post")[0]
    f = mm(y, w['out'], name="mix_out")
    z, xo, xob = resid_ln(x, f, g, b, 1.0, "mix_ln")
    return xo, xob, (xb, H, cqn, ckvn, qp, kp, vp, lse, hs3, ya, ob, y1, y, z), received


def mixer_bwd(dxo, res, w, g, b, cos8, sin8, exch=None):
    xb, H, cqn, ckvn, qp, kp, vp, lse, hs3, ya, ob, y1, y, z = res
    S = z.shape[0]
    gr = {}
    dz, dzs, gr['ln_g'], gr['ln_b'] = ln_bwd(z, g, b, dxo, 1.0, "mix_ln_bwd")
    gr['w_out'] = mm_tn(y, dzs, name="mix_out_dw")
    dy = mm(dzs, w['out'], b_nt=True, name="mix_out_dx")

    def post_bwd(ya, ob, y1, us, dy, dskip, gluw, glub, gmix):
        _, vjp = jax.vjp(_mix_post, ya, ob, y1, us, dskip, gluw, glub, gmix)
        dya, dob, dy1, dus, *dpars = vjp(dy)
        prod = dob * ob
        col = lax.broadcasted_iota(jnp.int32, (1, MLA_HEADS * MLA_V), 1)
        lane = lax.broadcasted_iota(jnp.int32, (1, LANES), 1)
        delta = jnp.zeros((prod.shape[0], LANES), f32)
        for h in range(MLA_HEADS):
            in_head = ((col >= MLA_V * h) & (col < MLA_V * (h + 1))).astype(f32)
            delta = jnp.where(lane == h, jnp.sum(prod * in_head, axis=-1, keepdims=True), delta)
        return (dya, dob, dy1, dus, delta, *dpars)

    dya, dob, dy1, dus_skip, delta, gr['ssm_d'], gr['ssm_glu_w'], gr['ssm_glu_b'], gr['mix_norm_g'] = rowwise(
        post_bwd, [ya, ob, y1, (H, H_US, SSM_WIDTH), dy], [w['ssm_d'], w['glu_w'], w['glu_b'], w['gmix']],
        [(GM_WIDTH, f32), (MLA_HEADS * MLA_V, f32), (SSM_WIDTH, f32), (SSM_WIDTH, f32), (LANES, f32)],
        [(1, SSM_WIDTH), (SSM_WIDTH, SSM_WIDTH), (1, SSM_WIDTH), (1, D_MODEL)], _row_tile(S, 128), "mix_post_bwd")

    gr['ssm_wc'] = mm_tn(hs3, dy1, a_s3=True, name="ssm_c_dw")
    dhs3 = mm(dy1, w['ssm_wc'], out_s3=True, b_nt=True, name="ssm_c_dx")
    dbu3, gr['ssm_a16'] = scan_bwd(dhs3, hs3, w['ssm_a16'])
    gr['ssm_wb'] = mm_tn(H, dbu3, a_col0=H_US, m_dim=SSM_WIDTH, b_s3=True, name="ssm_bu_dw")
    dus = mm(dbu3, w['ssm_wb'], add=dus_skip, add_scale=1.0, a_s3=True, b_nt=True, name="ssm_bu_dx")

    do = _heads_first(dob, MLA_V).transpose(1, 0, 2)
    delta = delta[:, :MLA_HEADS].T.reshape(MLA_HEADS, 1, S)
    dqT, dkp, dvp, *received = flash_bwd(qp, kp, kp.transpose(0, 2, 1), vp, do.astype(bf16), lse, delta, exch)
    dqp = dqT.transpose(2, 0, 1)
    dkp = dkp.transpose(1, 0, 2)
    dv = dvp.transpose(1, 0, 2).reshape(S, MLA_HEADS * MLA_V)
    lane_pad = ((0, 0), (0, LANES - ROPE_HALF))
    dq1r = dqp[:, :, 64:80].reshape(S, LANES)
    dq2r = dqp[:, :, 80:96].reshape(S, LANES)
    dk1r = jnp.pad(jnp.sum(dkp[:, :, 64:80], axis=1), lane_pad)
    dk2r = jnp.pad(jnp.sum(dkp[:, :, 80:96], axis=1), lane_pad)

    def rope_bwd(d1, d2, d3, d4, cos, sin):
        return d1 * cos + d2 * sin, d2 * cos - d1 * sin, d3 * cos + d4 * sin, d4 * cos - d3 * sin

    dq1, dq2, dk1, dk2 = rowwise(rope_bwd, [dq1r, dq2r, dk1r, dk2r, cos8, sin8], [], [(LANES, f32)] * 4, [],
                                 _row_tile(S, 512), "rope_bwd")
    dqraw = jnp.concatenate([dqp[:, :, :64].reshape(S, 512), dq1, dq2], axis=1).astype(bf16)
    dkv = jnp.concatenate([dkp[:, :, :64].reshape(S, 512), dv], axis=1).astype(bf16)
    gr['uq'] = mm_tn(cqn, dqraw, name="mla_uq_dw")
    dcqn = mm(dqraw, w['uq'], b_nt=True, name="mla_uq_dx")
    gr['ukv'] = mm_tn(ckvn, dkv, name="mla_ukv_dw")
    dckvn = mm(dkv, w['ukv'], b_nt=True, name="mla_ukv_dx")

    def prep_bwd(cq, ckv, d1, d2, qg, kvg):
        _, vjp = jax.vjp(_mla_prep, cq, ckv, qg, kvg)
        return vjp((d1, d2))

    dcq, dckv, gr['mla_q_norm_g'], gr['mla_kv_norm_g'] = rowwise(
        prep_bwd, [(H, H_CQ, Q_LORA), (H, H_CKV, KV_LORA), dcqn, dckvn], [w['qg'], w['kvg']],
        [(Q_LORA, f32), (KV_LORA, f32)], [(1, Q_LORA), (1, KV_LORA)], _row_tile(S, 256), "mla_prep_bwd")

    def gmlp_bwd(hu, hv, dya, ng, ws, bsb):
        _, vjp = jax.vjp(_gmlp, hu, hv, ng, ws, bsb)
        return vjp(dya)

    dhu, dhv, gr['gmlp_norm_g'], gr['gmlp_ws'], gr['gm_bsb'] = rowwise(
        gmlp_bwd, [(H, H_UG, 256), (H, H_VG, 256), dya], [w['gm_ng'], w['gm_ws'], w['gm_bsb']],
        [(GM_WIDTH, f32), (GM_WIDTH, f32)], [(1, GM_WIDTH), (GM_HEADS, GM_CHUNK, GM_CHUNK), (GM_CHUNK, GM_WIDTH)],
        GM_CHUNK, "gmlp_bwd")

    dH = jnp.concatenate([dhu, dhv, dcq, dus, dckv, dk1, dk2], axis=1).astype(bf16)
    gr['in'] = mm_tn(xb, dH, name="mix_in_dw")
    dx = mm(dH, w['in'], add=dz, add_scale=ALPHA, b_nt=True, name="mix_in_dx")
    return dx, gr, received


def _block_diag(blocks):
    G, a, b = blocks.shape
    eye = jnp.eye(G, dtype=blocks.dtype)
    return (eye[:, None, :, None] * blocks[:, :, None, :]).reshape(G * a, G * b)


def _diag_blocks(mat, G):
    a, b = mat.shape[0] // G, mat.shape[1] // G
    m4 = mat.reshape(G, a, G, b)
    eye = jnp.eye(G, dtype=mat.dtype)
    return jnp.sum(m4 * eye[:, None, :, None], axis=2)


def prep_layer(W, rep, l):
    w = {}
    for f in ('ffn1', 'ffn2'):
        w[f] = {'guT': jnp.stack([W[f + '_w_gate'], W[f + '_w_up']]), 'd': W[f + '_w_down']}
    wi = W['w_in']
    z112 = jnp.zeros((D_MODEL, LANES - ROPE_HALF), wi.dtype)
    w['in'] = jnp.concatenate([wi[:, :768], wi[:, 928:1184], wi[:, 768:896], wi[:, 896:912], z112, wi[:, 912:928], z112], axis=1)
    uq = W['mla_w_uq'].reshape(Q_LORA, MLA_HEADS, MLA_NOPE + MLA_ROPE)
    w['uq'] = jnp.concatenate([uq[:, :, :64].reshape(Q_LORA, 512), uq[:, :, 64:80].reshape(Q_LORA, LANES),
                               uq[:, :, 80:96].reshape(Q_LORA, LANES)], axis=1)
    ukv = W['mla_w_ukv'].reshape(KV_LORA, MLA_HEADS, MLA_NOPE + MLA_V)
    w['ukv'] = jnp.concatenate([ukv[:, :, :64].reshape(KV_LORA, 512), ukv[:, :, 64:].reshape(KV_LORA, 512)], axis=1)
    w['out'] = W['w_out']
    w['glu_w'] = W['ssm_glu_w']
    w['gm_ng'] = rep['gmlp_norm_g'][l].reshape(1, GM_WIDTH)
    w['gm_ws'] = rep['gmlp_ws'][l]
    w['gm_bsb'] = jnp.repeat(rep['gmlp_bs'][l].T, GM_HEAD_DIM, axis=1)
    w['qg'] = rep['mla_q_norm_g'][l].reshape(1, Q_LORA)
    w['kvg'] = rep['mla_kv_norm_g'][l].reshape(1, KV_LORA)
    w['ssm_d'] = rep['ssm_d'][l].reshape(1, SSM_WIDTH)
    w['glu_b'] = rep['ssm_glu_b'][l].reshape(1, SSM_WIDTH)
    w['gmix'] = rep['mix_norm_g'][l].reshape(1, D_MODEL)
    ar = rep['ssm_a_re'][l].reshape(1, N_STATE)
    ai = rep['ssm_a_im'][l].reshape(1, N_STATE)
    ldt = jnp.repeat(rep['ssm_log_dt'][l], SSM_STATE).reshape(1, N_STATE)
    brT = rep['ssm_b_re'][l].transpose(2, 0, 1).reshape(SSM_GROUP_CH, N_STATE)
    biT = rep['ssm_b_im'][l].transpose(2, 0, 1).reshape(SSM_GROUP_CH, N_STATE)
    w['ssm_prep_in'] = (ar, ai, ldt, brT, biT)
    abr, abi, bbrT, bbiT = whole(_ssm_prep, w['ssm_prep_in'], [(1, N_STATE)] * 2 + [(SSM_GROUP_CH, N_STATE)] * 2, "ssm_prep")
    w['ssm_a16'] = jnp.concatenate([abr.reshape(8, LANES), abi.reshape(8, LANES)], axis=0)

    def to_gcp(t):
        return t.reshape(SSM_GROUP_CH, SSM_GROUPS, SSM_STATE).transpose(1, 0, 2)

    w['ssm_wb'] = jnp.concatenate([_block_diag(to_gcp(bbrT)), _block_diag(to_gcp(bbiT))], axis=1).astype(bf16)
    cre = rep['ssm_c_re'][l].transpose(0, 2, 1)
    cim = rep['ssm_c_im'][l].transpose(0, 2, 1)
    w['ssm_wc'] = jnp.concatenate([_block_diag(cre), -_block_diag(cim)], axis=0).astype(bf16)
    return w


def unprep_grads(gr, w):
    out = {}
    for k in ('ln_g', 'ln_b', 'w_out', 'mla_q_norm_g', 'mla_kv_norm_g', 'ssm_glu_w', 'gmlp_ws'):
        out[k] = gr[k]
    out['gmlp_norm_g'] = gr['gmlp_norm_g'].reshape(GM_WIDTH)
    out['mla_q_norm_g'] = gr['mla_q_norm_g'].reshape(Q_LORA)
    out['mla_kv_norm_g'] = gr['mla_kv_norm_g'].reshape(KV_LORA)
    out['ssm_d'] = gr['ssm_d'].reshape(SSM_GROUPS, SSM_GROUP_CH)
    out['ssm_glu_b'] = gr['ssm_glu_b'].reshape(SSM_WIDTH)
    out['mix_norm_g'] = gr['mix_norm_g'].reshape(D_MODEL)
    out['gmlp_bs'] = gr['gm_bsb'].reshape(GM_CHUNK, GM_HEADS, GM_HEAD_DIM).sum(axis=-1).T
    d = gr['in']
    out['w_in'] = jnp.concatenate([d[:, :768], d[:, H_CKV:H_CKV + KV_LORA], d[:, H_K1:H_K1 + ROPE_HALF],
                                   d[:, H_K2:H_K2 + ROPE_HALF], d[:, H_US:H_US + SSM_WIDTH]], axis=1)
    d = gr['uq']
    out['mla_w_uq'] = jnp.concatenate([d[:, :512].reshape(Q_LORA, MLA_HEADS, 64), d[:, 512:640].reshape(Q_LORA, MLA_HEADS, ROPE_HALF),
                                       d[:, 640:768].reshape(Q_LORA, MLA_HEADS, ROPE_HALF)], axis=2).reshape(Q_LORA, 768)
    d = gr['ukv']
    out['mla_w_ukv'] = jnp.concatenate([d[:, :512].reshape(KV_LORA, MLA_HEADS, 64), d[:, 512:].reshape(KV_LORA, MLA_HEADS, 64)],
                                       axis=2).reshape(KV_LORA, 1024)
    dwc = gr['ssm_wc']
    out['ssm_c_re'] = _diag_blocks(dwc[:N_STATE], SSM_GROUPS).transpose(0, 2, 1)
    out['ssm_c_im'] = -_diag_blocks(dwc[N_STATE:], SSM_GROUPS).transpose(0, 2, 1)
    dwb = gr['ssm_wb']

    def from_blocks(m):
        return _diag_blocks(m, SSM_GROUPS).transpose(1, 0, 2).reshape(SSM_GROUP_CH, N_STATE)

    dbbrT, dbbiT = from_blocks(dwb[:, :N_STATE]), from_blocks(dwb[:, N_STATE:])
    da16 = gr['ssm_a16']
    dabr, dabi = da16[0:8].reshape(1, N_STATE), da16[8:16].reshape(1, N_STATE)

    def prep_bwd(ar, ai, ldt, brT, biT, d1, d2, d3, d4):
        _, vjp = jax.vjp(_ssm_prep, ar, ai, ldt, brT, biT)
        return vjp((d1, d2, d3, d4))

    dar, dai, dldt, dbrT, dbiT = whole(prep_bwd, w['ssm_prep_in'] + (dabr, dabi, dbbrT, dbbiT),
                                       [(1, N_STATE)] * 3 + [(SSM_GROUP_CH, N_STATE)] * 2, "ssm_prep_bwd")
    out['ssm_a_re'] = dar.reshape(SSM_GROUPS, SSM_STATE)
    out['ssm_a_im'] = dai.reshape(SSM_GROUPS, SSM_STATE)
    out['ssm_log_dt'] = dldt.reshape(SSM_GROUPS, SSM_STATE).sum(axis=-1)
    out['ssm_b_re'] = dbrT.reshape(SSM_GROUP_CH, SSM_GROUPS, SSM_STATE).transpose(1, 2, 0)
    out['ssm_b_im'] = dbiT.reshape(SSM_GROUP_CH, SSM_GROUPS, SSM_STATE).transpose(1, 2, 0)
    return out


def adamw(w, g, m, v, name):
    R, C = w.shape

    def fn(w, g, m, v):
        m = ADAM_B1 * m + (1.0 - ADAM_B1) * g
        v = ADAM_B2 * v + (1.0 - ADAM_B2) * jnp.square(g)
        m_hat = m / (1.0 - ADAM_B1 ** ADAM_STEP)
        v_hat = v / (1.0 - ADAM_B2 ** ADAM_STEP)
        delta = -ADAM_LR * (m_hat / (jnp.sqrt(v_hat) + ADAM_EPS) + ADAM_WD * w)
        return delta, m, v

    return rowwise(fn, [w, g, m, v], [], [(C, f32)] * 3, [], _pick(R, (256, 128, 64, 32, 16, 8)), name)


def kernel(x, positions, ln_g, ln_b, ffn1_w_gate, ffn1_w_up, ffn1_w_down, w_in, gmlp_norm_g, gmlp_ws, gmlp_bs, mla_q_norm_g, mla_w_uq, mla_kv_norm_g, mla_w_ukv, ssm_a_re, ssm_a_im, ssm_b_re, ssm_b_im, ssm_c_re, ssm_c_im, ssm_d, ssm_log_dt, ssm_glu_w, ssm_glu_b, mix_norm_g, w_out, ffn2_w_gate, ffn2_w_up, ffn2_w_down, loss_target, m_ln_g, m_ln_b, m_ffn1_w_gate, m_ffn1_w_up, m_ffn1_w_down, m_w_in, m_gmlp_norm_g, m_gmlp_ws, m_gmlp_bs, m_mla_q_norm_g, m_mla_w_uq, m_mla_kv_norm_g, m_mla_w_ukv, m_ssm_a_re, m_ssm_a_im, m_ssm_b_re, m_ssm_b_im, m_ssm_c_re, m_ssm_c_im, m_ssm_d, m_ssm_log_dt, m_ssm_glu_w, m_ssm_glu_b, m_mix_norm_g, m_w_out, m_ffn2_w_gate, m_ffn2_w_up, m_ffn2_w_down, v_ln_g, v_ln_b, v_ffn1_w_gate, v_ffn1_w_up, v_ffn1_w_down, v_w_in, v_gmlp_norm_g, v_gmlp_ws, v_gmlp_bs, v_mla_q_norm_g, v_mla_w_uq, v_mla_kv_norm_g, v_mla_w_ukv, v_ssm_a_re, v_ssm_a_im, v_ssm_b_re, v_ssm_b_im, v_ssm_c_re, v_ssm_c_im, v_ssm_d, v_ssm_log_dt, v_ssm_glu_w, v_ssm_glu_b, v_mix_norm_g, v_w_out, v_ffn2_w_gate, v_ffn2_w_up, v_ffn2_w_down):
    Wp = dict(zip(W_NAMES, (ln_g, ln_b, ffn1_w_gate, ffn1_w_up, ffn1_w_down, w_in, gmlp_norm_g, gmlp_ws, gmlp_bs, mla_q_norm_g, mla_w_uq, mla_kv_norm_g, mla_w_ukv, ssm_a_re, ssm_a_im, ssm_b_re, ssm_b_im, ssm_c_re, ssm_c_im, ssm_d, ssm_log_dt, ssm_glu_w, ssm_glu_b, mix_norm_g, w_out, ffn2_w_gate, ffn2_w_up, ffn2_w_down)))
    Mp = dict(zip(W_NAMES, (m_ln_g, m_ln_b, m_ffn1_w_gate, m_ffn1_w_up, m_ffn1_w_down, m_w_in, m_gmlp_norm_g, m_gmlp_ws, m_gmlp_bs, m_mla_q_norm_g, m_mla_w_uq, m_mla_kv_norm_g, m_mla_w_ukv, m_ssm_a_re, m_ssm_a_im, m_ssm_b_re, m_ssm_b_im, m_ssm_c_re, m_ssm_c_im, m_ssm_d, m_ssm_log_dt, m_ssm_glu_w, m_ssm_glu_b, m_mix_norm_g, m_w_out, m_ffn2_w_gate, m_ffn2_w_up, m_ffn2_w_down)))
    Vp = dict(zip(W_NAMES, (v_ln_g, v_ln_b, v_ffn1_w_gate, v_ffn1_w_up, v_ffn1_w_down, v_w_in, v_gmlp_norm_g, v_gmlp_ws, v_gmlp_bs, v_mla_q_norm_g, v_mla_w_uq, v_mla_kv_norm_g, v_mla_w_ukv, v_ssm_a_re, v_ssm_a_im, v_ssm_b_re, v_ssm_b_im, v_ssm_c_re, v_ssm_c_im, v_ssm_d, v_ssm_log_dt, v_ssm_glu_w, v_ssm_glu_b, v_mix_norm_g, v_w_out, v_ffn2_w_gate, v_ffn2_w_up, v_ffn2_w_down)))
    S = x.shape[1]
    my = _my_id()

    def shard_rows(l):
        return [_pad_tile_rows(_shard_to_rows(n, Wp[n][l].astype(bf16)), 0) for n in BIG]

    def joined(got):
        return {n: _join_from_devices(n, g[:, :_pack_rows(n)]) for n, g in zip(BIG, got)}

    ln_flat = jnp.concatenate([Wp[n].reshape(-1) for n in LN_NAMES])
    *got, ln_all = all_gather(shard_rows(0) + [_pad_rows(ln_flat, 8)], "gather_weights")
    ln_all = ln_all.reshape(N_DEV, -1)
    lnsz = DEPTH * 3 * (D_MODEL // N_DEV)
    ln_full = {}
    for t, n in enumerate(LN_NAMES):
        sh = ln_all[:, t * lnsz:(t + 1) * lnsz].reshape(N_DEV, DEPTH, 3, D_MODEL // N_DEV)
        ln_full[n] = sh.transpose(1, 2, 0, 3).reshape(DEPTH, 3, 1, D_MODEL)
    rep = {n: Wp[n] for n in REPL}

    inv_freq = 1.0 / (ROPE_BASE ** (jnp.arange(0, MLA_ROPE, 2, dtype=f32) / MLA_ROPE))
    ang = positions.astype(f32).reshape(S, 1) * inv_freq[None, :]
    cos8 = jnp.tile(jnp.cos(ang), (1, MLA_HEADS))
    sin8 = jnp.tile(jnp.sin(ang), (1, MLA_HEADS))

    xs = x.reshape(S, D_MODEL)
    xb = xs.astype(bf16)
    ws, saved = [], []
    for l in range(DEPTH):
        w = prep_layer(joined(got), rep, l)
        lg, lb = ln_full['ln_g'][l], ln_full['ln_b'][l]
        xs, xb, r1 = ffn_fwd(xs, xb, w['ffn1'], lg[0], lb[0])
        xs, xb, r2, got = mixer_fwd(xs, xb, w, lg[1], lb[1], cos8, sin8, (True, shard_rows(l + 1)) if l + 1 < DEPTH else None)
        xs, xb, r3 = ffn_fwd(xs, xb, w['ffn2'], lg[2], lb[2])
        ws.append(w)
        saved.append((r1, r2, r3))

    def loss_fn(y, t):
        d = y - t
        part = jnp.sum(jnp.mean(jnp.square(d), axis=-1, keepdims=True), axis=0, keepdims=True)
        return d * (1.0 / D_MODEL), 0.5 * part

    dx, loss_part = rowwise(loss_fn, [xs, loss_target.reshape(S, D_MODEL)], [], [(D_MODEL, f32)], [(1, 1)],
                            _row_tile(S, 512), "loss")
    loss = lax.psum(loss_part[0, 0], ("x", "y", "c"))

    def grad_packs(g):
        return [_pad_tile_rows(_split_for_devices(n, g[n].astype(bf16)), 1) for n in BIG]

    grads, arrived = [None] * DEPTH, [None] * DEPTH
    for l in reversed(range(DEPTH)):
        w = ws[l]
        lg, lb = ln_full['ln_g'][l], ln_full['ln_b'][l]
        r1, r2, r3 = saved[l]
        dx, g2g, g2u, g2d, dg2, db2 = ffn_bwd(dx, r3, w['ffn2'], lg[2], lb[2])
        dx, gm, got = mixer_bwd(dx, r2, w, lg[1], lb[1], cos8, sin8, (False, grad_packs(grads[l + 1])) if l + 1 < DEPTH else None)
        if got:
            arrived[l + 1] = got
        dx, g1g, g1u, g1d, dg0, db0 = ffn_bwd(dx, r1, w['ffn1'], lg[0], lb[0])
        g = unprep_grads(gm, w)
        g.update({'ffn1_w_gate': g1g, 'ffn1_w_up': g1u, 'ffn1_w_down': g1d,
                  'ffn2_w_gate': g2g, 'ffn2_w_up': g2u, 'ffn2_w_down': g2d})
        g['ln_g'] = jnp.concatenate([dg0, g['ln_g'], dg2], axis=0)
        g['ln_b'] = jnp.concatenate([db0, g['ln_b'], db2], axis=0)
        grads[l] = g
    grad_x = dx.reshape(1, S, D_MODEL)

    arrived[0] = all_to_all(grad_packs(grads[0]), "scatter_grads")
    G = {n: jnp.stack([_rows_to_shard(n, sum_slots(arrived[l][t], "sum_" + n)[:_pack_rows(n)]) for l in range(DEPTH)])
         for t, n in enumerate(BIG)}
    small_names = LN_NAMES + REPL
    spack = jnp.concatenate([jnp.stack([grads[l][n] for l in range(DEPTH)]).reshape(-1) for n in small_names])
    n_small = spack.shape[0]
    gsmall = sum_slots(all_gather([_pad_rows(spack, 8)], "gather_small")[0], "sum_small").reshape(-1)

    off = 0
    for n in small_names:
        shp = (DEPTH, 3, D_MODEL) if n in LN_NAMES else Wp[n].shape
        sz = math.prod(shp)
        G[n] = gsmall[off:off + sz].reshape(shp)
        off += sz
    for n in LN_NAMES:
        G[n] = lax.dynamic_slice_in_dim(G[n], my * (D_MODEL // N_DEV), D_MODEL // N_DEV, axis=2)

    delta, new_m, new_v = {}, {}, {}
    for n in BIG:
        shp = Wp[n].shape
        two = (shp[0] * shp[1], shp[2])
        d_, m_, v_ = adamw(Wp[n].reshape(two), G[n].reshape(two), Mp[n].reshape(two), Vp[n].reshape(two), "adamw_" + n)
        delta[n], new_m[n], new_v[n] = d_.reshape(shp), m_.reshape(shp), v_.reshape(shp)

    def pack_small(src):
        return _pad_rows(jnp.concatenate([src[n].reshape(-1) for n in small_names]), 8)

    d_, m_, v_ = adamw(pack_small(Wp), pack_small(G), pack_small(Mp), pack_small(Vp), "adamw_small")
    d_, m_, v_ = d_.reshape(-1), m_.reshape(-1), v_.reshape(-1)
    off = 0
    for n in small_names:
        shp = Wp[n].shape
        sz = math.prod(shp)
        delta[n], new_m[n], new_v[n] = (t[off:off + sz].reshape(shp) for t in (d_, m_, v_))
        off += sz

    return (loss, grad_x, *[G[n] for n in W_NAMES], *[delta[n] for n in W_NAMES],
            *[new_m[n] for n in W_NAMES], *[new_v[n] for n in W_NAMES])
```

```python
import functools
import math

import jax
import jax.numpy as jnp
from jax import lax
from jax.experimental import pallas as pl
from jax.experimental.pallas import tpu as pltpu

f32 = jnp.float32
bf16 = jnp.bfloat16

D_MODEL = 1024
DEPTH = 4
D_FF = 2816
GM_HEADS, GM_HEAD_DIM, GM_WIDTH, GM_CHUNK = 4, 64, 256, 128
MLA_HEADS, MLA_NOPE, MLA_ROPE, MLA_V = 8, 64, 32, 64
ROPE_HALF = MLA_ROPE // 2
Q_LORA, KV_LORA = 256, 128
ROPE_BASE = 10000.0
SSM_GROUPS, SSM_GROUP_CH, SSM_WIDTH, SSM_STATE = 16, 16, 256, 64
N_STATE = SSM_GROUPS * SSM_STATE
ALPHA = (2 * DEPTH) ** 0.25
LN_EPS = 1e-5
RMS_EPS = 1e-6
NEG_BIG = -1e30
ATT_SCALE = (MLA_NOPE + MLA_ROPE) ** -0.5
ADAM_LR, ADAM_B1, ADAM_B2, ADAM_EPS, ADAM_WD, ADAM_STEP = 0.001, 0.9, 0.999, 1e-08, 0.01, 10

N_DEV = 8
LANES = 128
VMEM_LIMIT = 48 * 1024 * 1024
MM_TILE_BUDGET = 32 * 1024 * 1024
MESH = pl.DeviceIdType.MESH

H_UG, H_VG, H_CQ, H_US, H_CKV, H_K1, H_K2, H_COLS = 0, 256, 512, 768, 1024, 1152, 1280, 1408

W_NAMES = ['ln_g', 'ln_b', 'ffn1_w_gate', 'ffn1_w_up', 'ffn1_w_down', 'w_in', 'gmlp_norm_g', 'gmlp_ws', 'gmlp_bs',
           'mla_q_norm_g', 'mla_w_uq', 'mla_kv_norm_g', 'mla_w_ukv', 'ssm_a_re', 'ssm_a_im', 'ssm_b_re', 'ssm_b_im',
           'ssm_c_re', 'ssm_c_im', 'ssm_d', 'ssm_log_dt', 'ssm_glu_w', 'ssm_glu_b', 'mix_norm_g', 'w_out',
           'ffn2_w_gate', 'ffn2_w_up', 'ffn2_w_down']
BIG = {'ffn1_w_gate': 1, 'ffn1_w_up': 1, 'ffn1_w_down': 0, 'w_in': 1, 'mla_w_uq': 1, 'mla_w_ukv': 1,
       'ssm_glu_w': 0, 'w_out': 0, 'ffn2_w_gate': 1, 'ffn2_w_up': 1, 'ffn2_w_down': 0}
BIG_SHAPE = {'ffn1_w_gate': (D_MODEL, D_FF), 'ffn1_w_up': (D_MODEL, D_FF), 'ffn1_w_down': (D_FF, D_MODEL),
             'w_in': (D_MODEL, 1184), 'mla_w_uq': (Q_LORA, 768), 'mla_w_ukv': (KV_LORA, 1024),
             'ssm_glu_w': (SSM_WIDTH, SSM_WIDTH), 'w_out': (D_MODEL, D_MODEL),
             'ffn2_w_gate': (D_MODEL, D_FF), 'ffn2_w_up': (D_MODEL, D_FF), 'ffn2_w_down': (D_FF, D_MODEL)}
LN_NAMES = ['ln_g', 'ln_b']
REPL = [n for n in W_NAMES if n not in BIG and n not in LN_NAMES]


def _pick(n, cands):
    for c in cands:
        if n % c == 0:
            return c
    return n


def _params(sem):
    return pltpu.CompilerParams(dimension_semantics=sem, vmem_limit_bytes=VMEM_LIMIT)


S3_ROWS = 2 * N_STATE // LANES


def _from_s3(ref):
    return jnp.concatenate([ref[:, c, :] for c in range(S3_ROWS)], axis=1)


def _to_s3(ref, val):
    for c in range(S3_ROWS):
        ref[:, c, :] = val[:, c * LANES:(c + 1) * LANES].astype(ref.dtype)


def mm(a, b, *, out_dtype=f32, add=None, add_scale=1.0, a_col0=0, a_s3=False, out_s3=False, grouped=False, b_nt=False, name):
    G = a.shape[0] if grouped else 1
    M = a.shape[1] if grouped else a.shape[0]
    K, N = (b.shape[-1], b.shape[-2]) if b_nt else b.shape[-2:]
    tn = N if out_s3 else _pick(N, (512, 384, 256) if N <= 1536 else (512, 384, 256, 128))
    tk = K if (K <= 2 * D_FF or a_s3) else _pick(K, (1024, 512, 256, 128))
    nk = G * (K // tk)
    assert not grouped or tk == K

    def tile_bytes(tm):
        return 2 * (tm * tk * a.dtype.itemsize + tk * tn * b.dtype.itemsize + tm * tn * 4 * (2 if add is not None else 1))

    tm = next(t for t in (1024, 512, 256, 128, 64, 32, 16, 8) if M % t == 0 and (tile_bytes(t) <= MM_TILE_BUDGET or t == 8))
    assert a_s3 or grouped or (a_col0 % tk == 0 and a_col0 + K <= a.shape[1])
    kb0 = a_col0 // tk
    has_add = add is not None

    def body(*refs):
        if has_add:
            a_ref, b_ref, add_ref, o_ref, acc = refs
        else:
            a_ref, b_ref, o_ref, acc = refs
        k = pl.program_id(2)
        a_val = _from_s3(a_ref) if a_s3 else a_ref[...]
        part = (_nt if b_nt else functools.partial(jnp.dot, preferred_element_type=f32))(a_val.astype(bf16), b_ref[...].astype(bf16))

        def finish(total):
            if has_add:
                total = total + add_scale * add_ref[...]
            if out_s3:
                _to_s3(o_ref, total)
            else:
                o_ref[...] = total.astype(o_ref.dtype)

        if nk == 1:
            finish(part)
        else:
            @pl.when(k == 0)
            def _():
                acc[...] = part

            @pl.when(k > 0)
            def _():
                acc[...] += part

            @pl.when(k == nk - 1)
            def _():
                finish(acc[...])

    if a_s3:
        a_spec = pl.BlockSpec((tm, S3_ROWS, LANES), lambda i, j, k: (i, 0, 0))
    elif grouped:
        a_spec = pl.BlockSpec((None, tm, tk), lambda i, j, k: (k, i, 0))
    else:
        a_spec = pl.BlockSpec((tm, tk), lambda i, j, k: (i, kb0 + k))
    if b_nt:
        b_spec = pl.BlockSpec((None, tn, tk), lambda i, j, k: (k, j, 0)) if grouped else pl.BlockSpec((tn, tk), lambda i, j, k: (j, k))
    else:
        b_spec = pl.BlockSpec((None, tk, tn), lambda i, j, k: (k, 0, j)) if grouped else pl.BlockSpec((tk, tn), lambda i, j, k: (k, j))
    in_specs, ops = [a_spec, b_spec], [a, b]
    if has_add:
        in_specs.append(pl.BlockSpec((tm, tn), lambda i, j, k: (i, j)))
        ops.append(add)
    if out_s3:
        out_spec = pl.BlockSpec((tm, S3_ROWS, LANES), lambda i, j, k: (i, 0, 0))
        out_shape = jax.ShapeDtypeStruct((M, S3_ROWS, LANES), out_dtype)
    else:
        out_spec = pl.BlockSpec((tm, tn), lambda i, j, k: (i, j))
        out_shape = jax.ShapeDtypeStruct((M, N), out_dtype)
    return pl.pallas_call(
        body, grid=(M // tm, N // tn, nk), in_specs=in_specs, out_specs=out_spec, out_shape=out_shape,
        scratch_shapes=[pltpu.VMEM((tm, tn) if nk > 1 else (8, LANES), f32)],
        compiler_params=_params(("parallel", "parallel", "arbitrary")), name=name)(*ops)


def mm_tn(a, b, *, a_col0=0, m_dim=None, a_s3=False, b_s3=False, a_lead=None, name):
    K = a.shape[-2] if a_lead is not None else a.shape[0]
    M = 2 * N_STATE if a_s3 else (a.shape[-1] if m_dim is None else m_dim)
    N = 2 * N_STATE if b_s3 else b.shape[-1]
    tm = M if a_s3 else _pick(M, (H_COLS, 1024, 768, 512, 384, 256, 128))
    tn = N if b_s3 else _pick(N, (H_COLS, 1024, 768, 512, 384, 256))

    def tile_bytes(tk):
        return 2 * tk * (tm * a.dtype.itemsize + tn * b.dtype.itemsize) + 3 * tm * tn * 4

    tk = next(t for t in (2048, 1024, 512, 256, 128, 64, 32, 16) if K % t == 0 and (tile_bytes(t) <= MM_TILE_BUDGET or t == 16))
    nk = K // tk
    assert a_col0 % tm == 0
    mb0 = a_col0 // tm

    def body(a_ref, b_ref, o_ref, acc):
        k = pl.program_id(2)
        a_val = _from_s3(a_ref) if a_s3 else a_ref[...]
        b_val = _from_s3(b_ref) if b_s3 else b_ref[...]
        part = lax.dot_general(a_val.astype(bf16), b_val.astype(bf16), (((0,), (0,)), ((), ())), preferred_element_type=f32)

        @pl.when(k == 0)
        def _():
            acc[...] = part

        @pl.when(k > 0)
        def _():
            acc[...] += part

        @pl.when(k == nk - 1)
        def _():
            o_ref[...] = acc[...]

    s3_spec = pl.BlockSpec((tk, S3_ROWS, LANES), lambda i, j, k: (k, 0, 0))
    if a_s3:
        a_spec = s3_spec
    elif a_lead is not None:
        a_spec = pl.BlockSpec((None, tk, tm), lambda i, j, k: (a_lead, k, i))
    else:
        a_spec = pl.BlockSpec((tk, tm), lambda i, j, k: (k, mb0 + i))
    return pl.pallas_call(
        body, grid=(M // tm, N // tn, nk),
        in_specs=[a_spec, s3_spec if b_s3 else pl.BlockSpec((tk, tn), lambda i, j, k: (k, j))],
        out_specs=pl.BlockSpec((tm, tn), lambda i, j, k: (i, j)),
        out_shape=jax.ShapeDtypeStruct((M, N), f32),
        scratch_shapes=[pltpu.VMEM((tm, tn), f32)],
        compiler_params=_params(("parallel", "parallel", "arbitrary")), name=name)(a, b)


def rowwise(fn, rows, pars, out_rows, out_accs, tm, name):
    first = rows[0]
    if isinstance(first, tuple):
        R = first[0].shape[1] if first[1] == 'lead' else first[0].shape[0]
    else:
        R = first.shape[0]
    assert R % tm == 0, (R, tm, name)
    n_rows, n_pars, n_or, n_oa = len(rows), len(pars), len(out_rows), len(out_accs)

    in_specs, ops = [], []
    for r in rows:
        if isinstance(r, tuple) and r[1] == 'lead':
            arr, _, kk = r
            in_specs.append(pl.BlockSpec((None, tm, arr.shape[2]), lambda i, kk=kk: (kk, i, 0)))
        elif isinstance(r, tuple):
            arr, c0, w = r
            assert c0 % w == 0
            in_specs.append(pl.BlockSpec((tm, w), lambda i, cb=c0 // w: (i, cb)))
        else:
            arr = r
            in_specs.append(pl.BlockSpec((tm, arr.shape[1]), lambda i: (i, 0)))
        ops.append(arr)
    for p in pars:
        in_specs.append(pl.BlockSpec(p.shape, lambda i, nd=p.ndim: (0,) * nd))
        ops.append(p)
    out_specs = [pl.BlockSpec((tm, w), lambda i: (i, 0)) for (w, _) in out_rows]
    out_specs += [pl.BlockSpec(s, lambda i, nd=len(s): (0,) * nd) for s in out_accs]
    out_shape = [jax.ShapeDtypeStruct((R, w), dt) for (w, dt) in out_rows]
    out_shape += [jax.ShapeDtypeStruct(s, f32) for s in out_accs]

    def body(*refs):
        ins = [r[...] for r in refs[:n_rows + n_pars]]
        o_refs = refs[n_rows + n_pars:]
        res = fn(*ins)
        if not isinstance(res, (tuple, list)):
            res = (res,)
        assert len(res) == n_or + n_oa, (len(res), n_or, n_oa, name)
        for o, v in zip(o_refs[:n_or], res[:n_or]):
            o[...] = v.astype(o.dtype)
        if n_oa:
            i = pl.program_id(0)

            @pl.when(i == 0)
            def _():
                for o, v in zip(o_refs[n_or:], res[n_or:]):
                    o[...] = v.astype(f32)

            @pl.when(i > 0)
            def _():
                for o, v in zip(o_refs[n_or:], res[n_or:]):
                    o[...] += v.astype(f32)

    return pl.pallas_call(
        body, grid=(R // tm,), in_specs=in_specs, out_specs=out_specs, out_shape=out_shape,
        compiler_params=_params(("arbitrary",)), name=name)(*ops)


def whole(fn, ins, out_shapes, name):
    def body(*refs):
        res = fn(*[r[...] for r in refs[:len(ins)]])
        for o, v in zip(refs[len(ins):], res):
            o[...] = v

    return pl.pallas_call(body, out_shape=[jax.ShapeDtypeStruct(s, f32) for s in out_shapes], name=name)(*ins)


@jax.custom_vjp
def _bdot(a, b):
    return jnp.dot(a.astype(bf16), b.astype(bf16), preferred_element_type=f32)


def _bdot_fwd(a, b):
    return _bdot(a, b), (a, b)


def _bdot_bwd(res, g):
    a, b = res
    gb = g.astype(bf16)
    da = lax.dot_general(gb, b.astype(bf16), (((1,), (1,)), ((), ())), preferred_element_type=f32)
    db = lax.dot_general(a.astype(bf16), gb, (((0,), (0,)), ((), ())), preferred_element_type=f32)
    return da, db


_bdot.defvjp(_bdot_fwd, _bdot_bwd)


def _ln(z, g, b):
    mu = jnp.mean(z, axis=-1, keepdims=True)
    var = jnp.mean(jnp.square(z - mu), axis=-1, keepdims=True)
    return (z - mu) * lax.rsqrt(var + LN_EPS) * g + b


def _rms_only(x):
    return x * lax.rsqrt(jnp.mean(jnp.square(x), axis=-1, keepdims=True) + RMS_EPS)


def _swiglu(a, b):
    return jax.nn.silu(a) * b


def _gmlp(hu, hv, ng, ws, bsb):
    u = jax.nn.gelu(hu)
    v = jax.nn.gelu(hv)
    lane = lax.broadcasted_iota(jnp.int32, (1, GM_WIDTH), 1)
    masks = [((lane >= GM_HEAD_DIM * h) & (lane < GM_HEAD_DIM * (h + 1))).astype(f32) for h in range(GM_HEADS)]
    mu = jnp.zeros_like(v)
    for m in masks:
        mu = mu + m * (jnp.sum(v * m, axis=-1, keepdims=True) / GM_HEAD_DIM)
    d = v - mu
    var = jnp.zeros_like(v)
    for m in masks:
        var = var + m * (jnp.sum(d * d * m, axis=-1, keepdims=True) / GM_HEAD_DIM)
    vn = d * lax.rsqrt(var + LN_EPS) * ng
    r = lax.broadcasted_iota(jnp.int32, (GM_CHUNK, GM_CHUNK), 0)
    c = lax.broadcasted_iota(jnp.int32, (GM_CHUNK, GM_CHUNK), 1)
    tril = (c <= r).astype(f32)
    z = bsb
    for h, m in enumerate(masks):
        z = z + _bdot(ws[h] * tril, vn * m)
    return u * z


def _mla_prep(cq, ckv, qg, kvg):
    return _rms_only(cq) * qg, _rms_only(ckv) * kvg


def _rope(q1, q2, k1, k2, cos, sin):
    return q1 * cos - q2 * sin, q2 * cos + q1 * sin, k1 * cos - k2 * sin, k2 * cos + k1 * sin


def _mix_post(ya, ob, y1, us, dskip, gluw, glub, gmix):
    y = jax.nn.gelu(y1 + dskip * us)
    yc = y * jax.nn.sigmoid(_bdot(y, gluw) + glub)
    return jnp.concatenate([_rms_only(ya), _rms_only(ob), _rms_only(yc)], axis=1) * gmix


def _ssm_prep(ar, ai, ldt, brT, biT):
    dt = jnp.exp(ldt)
    mag = jnp.exp(ar * dt)
    abr = mag * jnp.cos(ai * dt)
    abi = mag * jnp.sin(ai * dt)
    den = ar * ar + ai * ai
    cr = ((abr - 1.0) * ar + abi * ai) / den
    ci = (abi * ar - (abr - 1.0) * ai) / den
    return abr, abi, cr * brT - ci * biT, cr * biT + ci * brT


ATT_FWD_HEADS = 2


def _att_tile(S):
    return _pick(S, (512, 256, 128))


def _nt(a, b):
    return lax.dot_general(a, b, (((1,), (1,)), ((), ())), preferred_element_type=f32)


def _diag_keep(T):
    krow = lax.broadcasted_iota(jnp.int32, (T, T), 0)
    qcol = lax.broadcasted_iota(jnp.int32, (T, T), 1)
    return qcol >= krow


def grid_call(body, exch, *, grid, in_specs, out_specs, out_shape, scratch_shapes, name, args):
    params = _params(("arbitrary", "arbitrary"))
    if exch is None:
        return pl.pallas_call(body, grid=grid, in_specs=in_specs, out_specs=out_specs, out_shape=out_shape,
                              scratch_shapes=scratch_shapes, compiler_params=params, name=name)(*args)
    gather, xs = exch
    n, n_in, n_out, n_sc = len(xs), len(in_specs), len(out_specs), len(scratch_shapes)

    def riding(*refs):
        ins, x_refs = refs[:n_in], refs[n_in:n_in + n]
        outs, xo_refs = refs[n_in + n:n_in + n + n_out], refs[n_in + n + n_out:n_in + 2 * n + n_out]
        rest = refs[n_in + 2 * n + n_out:]
        scratch, sems = rest[:n_sc], rest[n_sc:]
        h, i = pl.program_id(0), pl.program_id(1)

        @pl.when((h == 0) & (i == 0))
        def _():
            _direct_exchange(gather, x_refs, xo_refs, *sems, start=True)

        body(*ins, *outs, *scratch)

        @pl.when((h == grid[0] - 1) & (i == grid[1] - 1))
        def _():
            _direct_exchange(gather, x_refs, xo_refs, *sems, start=False)

    return pl.pallas_call(
        riding, grid=grid, in_specs=list(in_specs) + [HBM_SPEC] * n, out_specs=list(out_specs) + [HBM_SPEC] * n,
        out_shape=list(out_shape) + _exchange_out_shapes(gather, xs),
        scratch_shapes=list(scratch_shapes) + _exchange_scratch(n), compiler_params=params,
        name=name + ("_gather" if gather else "_scatter"))(*args, *xs)


def flash_fwd(q, k, vT, exch=None):
    Hh, S, _ = q.shape
    T = _att_tile(S)
    HB = ATT_FWD_HEADS

    def body(q_ref, k_ref, vT_ref, o_ref, lse_ref, m_sc, l_sc, acc_sc):
        i = pl.program_id(1)
        m_sc[...] = jnp.full_like(m_sc, NEG_BIG)
        l_sc[...] = jnp.zeros_like(l_sc)
        acc_sc[...] = jnp.zeros_like(acc_sc)

        def block(j, diagonal):
            rows = pl.ds(pl.multiple_of(j * T, T), T)
            for hh in range(HB):
                sT = _nt(k_ref[hh, rows, :], q_ref[hh]) * ATT_SCALE
                if diagonal:
                    sT = jnp.where(_diag_keep(T), sT, NEG_BIG)
                m_old = m_sc[hh]
                m_new = jnp.maximum(m_old, jnp.max(sT, axis=0, keepdims=True))
                alpha = jnp.exp(m_old - m_new)
                pT = jnp.exp(sT - m_new)
                l_sc[hh] = alpha * l_sc[hh] + jnp.sum(pT, axis=0, keepdims=True)
                acc_sc[hh] = alpha * acc_sc[hh] + jnp.dot(vT_ref[hh, :, rows], pT.astype(bf16), preferred_element_type=f32)
                m_sc[hh] = m_new

        def loop_body(j, c):
            block(j, False)
            return c

        lax.fori_loop(0, i, loop_body, 0)
        block(i, True)
        o_ref[...] = acc_sc[...] / l_sc[...]
        lse_ref[...] = m_sc[...] + jnp.log(l_sc[...])

    return grid_call(
        body, exch, grid=(Hh // HB, S // T),
        in_specs=[pl.BlockSpec((HB, T, LANES), lambda h, i: (h, i, 0)), pl.BlockSpec((HB, S, LANES), lambda h, i: (h, 0, 0)),
                  pl.BlockSpec((HB, MLA_V, S), lambda h, i: (h, 0, 0))],
        out_specs=[pl.BlockSpec((HB, MLA_V, T), lambda h, i: (h, 0, i)), pl.BlockSpec((HB, 1, T), lambda h, i: (h, 0, i))],
        out_shape=[jax.ShapeDtypeStruct((Hh, MLA_V, S), f32), jax.ShapeDtypeStruct((Hh, 1, S), f32)],
        scratch_shapes=[pltpu.VMEM((HB, 1, T), f32), pltpu.VMEM((HB, 1, T), f32), pltpu.VMEM((HB, MLA_V, T), f32)],
        name="flash_fwd", args=(q, k, vT))


def flash_bwd(q, k, kT, v, do, lse, delta, exch=None):
    Hh, S, _ = q.shape
    T = _att_tile(S)

    def body(q_ref, do_ref, lse_ref, dl_ref, k_ref, kT_ref, v_ref, dq_ref, dk_ref, dv_ref, dq_sc):
        i = pl.program_id(1)

        @pl.when(i == 0)
        def _():
            dk_ref[...] = jnp.zeros_like(dk_ref)
            dv_ref[...] = jnp.zeros_like(dv_ref)

        qi, doi = q_ref[0], do_ref[0]
        lse_i, dl_i = lse_ref[0], dl_ref[0]
        dq_sc[...] = jnp.zeros_like(dq_sc)

        def block(j, diagonal):
            rows = pl.ds(pl.multiple_of(j * T, T), T)
            sT = _nt(k_ref[0, rows, :], qi) * ATT_SCALE
            pT = jnp.exp(sT - lse_i)
            if diagonal:
                pT = jnp.where(_diag_keep(T), pT, 0.0)
            dpT = _nt(v_ref[0, rows, :], doi)
            dsT = (pT * (dpT - dl_i) * ATT_SCALE).astype(bf16)
            dv_ref[0, rows, :] += jnp.dot(pT.astype(bf16), doi, preferred_element_type=f32)
            dk_ref[0, rows, :] += jnp.dot(dsT, qi, preferred_element_type=f32)
            dq_sc[...] += jnp.dot(kT_ref[0, :, rows], dsT, preferred_element_type=f32)

        def loop_body(j, c):
            block(j, False)
            return c

        lax.fori_loop(0, i, loop_body, 0)
        block(i, True)
        dq_ref[0] = dq_sc[...]

    tile = lambda w: pl.BlockSpec((1, T, w), lambda h, i: (h, i, 0))
    row = pl.BlockSpec((1, 1, T), lambda h, i: (h, 0, i))
    full = lambda w: pl.BlockSpec((1, S, w), lambda h, i: (h, 0, 0))
    return grid_call(
        body, exch, grid=(Hh, S // T),
        in_specs=[tile(LANES), tile(MLA_V), row, row, full(LANES), pl.BlockSpec((1, LANES, S), lambda h, i: (h, 0, 0)), full(MLA_V)],
        out_specs=[pl.BlockSpec((1, LANES, T), lambda h, i: (h, 0, i)), full(LANES), full(MLA_V)],
        out_shape=[jax.ShapeDtypeStruct((Hh, LANES, S), f32), jax.ShapeDtypeStruct((Hh, S, LANES), f32),
                   jax.ShapeDtypeStruct((Hh, S, MLA_V), f32)],
        scratch_shapes=[pltpu.VMEM((LANES, T), f32)],
        name="flash_bwd", args=(q, do, lse, delta, k, kT, v))


def _scan_tile(S):
    return _pick(S, (256, 128, 64, 32, 16, 8))


def scan_fwd(bu3, a16):
    S = bu3.shape[0]
    ts = _scan_tile(S)

    def body(bu_ref, a_ref, o_ref, h_sc):
        @pl.when(pl.program_id(0) == 0)
        def _():
            h_sc[...] = jnp.zeros_like(h_sc)

        ar, ai = a_ref[0:8, :], a_ref[8:16, :]

        def step(t, carry):
            hr, hi = carry
            nr = ar * hr - ai * hi + bu_ref[t, 0:8, :]
            ni = ar * hi + ai * hr + bu_ref[t, 8:16, :]
            o_ref[t, 0:8, :] = nr
            o_ref[t, 8:16, :] = ni
            return nr, ni

        hr, hi = lax.fori_loop(0, ts, step, (h_sc[0:8, :], h_sc[8:16, :]), unroll=8)
        h_sc[0:8, :] = hr
        h_sc[8:16, :] = hi

    blk = pl.BlockSpec((ts, 16, LANES), lambda i: (i, 0, 0))
    return pl.pallas_call(
        body, grid=(S // ts,), in_specs=[blk, pl.BlockSpec((16, LANES), lambda i: (0, 0))], out_specs=blk,
        out_shape=jax.ShapeDtypeStruct(bu3.shape, f32), scratch_shapes=[pltpu.VMEM((16, LANES), f32)],
        compiler_params=_params(("arbitrary",)), name="scan_fwd")(bu3, a16)


def scan_bwd(g3, h3, a16):
    S = g3.shape[0]
    ts = _scan_tile(S)
    nb = S // ts

    def body(g_ref, h_ref, a_ref, o_ref, da_ref, lam_sc, da_sc):
        @pl.when(pl.program_id(0) == 0)
        def _():
            lam_sc[...] = jnp.zeros_like(lam_sc)
            da_sc[...] = jnp.zeros_like(da_sc)

        ar, ai = a_ref[0:8, :], a_ref[8:16, :]

        def step(kk, carry):
            lr, li, dar, dai = carry
            t = ts - 1 - kk
            hr, hi = h_ref[t, 0:8, :], h_ref[t, 8:16, :]
            dar = dar + lr * hr + li * hi
            dai = dai + li * hr - lr * hi
            nlr = ar * lr + ai * li + g_ref[t, 0:8, :]
            nli = ar * li - ai * lr + g_ref[t, 8:16, :]
            o_ref[t, 0:8, :] = nlr
            o_ref[t, 8:16, :] = nli
            return nlr, nli, dar, dai

        lr, li, dar, dai = lax.fori_loop(
            0, ts, step, (lam_sc[0:8, :], lam_sc[8:16, :], da_sc[0:8, :], da_sc[8:16, :]), unroll=8)
        lam_sc[0:8, :] = lr
        lam_sc[8:16, :] = li
        da_sc[0:8, :] = dar
        da_sc[8:16, :] = dai
        da_ref[0:8, :] = dar
        da_ref[8:16, :] = dai

    blk = pl.BlockSpec((ts, 16, LANES), lambda i: (nb - 1 - i, 0, 0))
    small = pl.BlockSpec((16, LANES), lambda i: (0, 0))
    return pl.pallas_call(
        body, grid=(nb,), in_specs=[blk, blk, small], out_specs=[blk, small],
        out_shape=[jax.ShapeDtypeStruct(g3.shape, f32), jax.ShapeDtypeStruct((16, LANES), f32)],
        scratch_shapes=[pltpu.VMEM((16, LANES), f32), pltpu.VMEM((16, LANES), f32)],
        compiler_params=_params(("arbitrary",)), name="scan_bwd")(g3, h3, a16)


HBM_SPEC = pl.BlockSpec(memory_space=pltpu.HBM)


def _my_id():
    return 4 * lax.axis_index("x") + 2 * lax.axis_index("y") + lax.axis_index("c")


def all_gather(xs, name):
    n = len(xs)

    def body(*refs):
        x_refs, o_refs = refs[:n], refs[n:2 * n]
        send_sems, recv_sems, local_sems = refs[2 * n:]
        x, y, c = lax.axis_index("x"), lax.axis_index("y"), lax.axis_index("c")
        me, sibling = (x, y, c), (x, y, 1 - c)
        chips = [(1 - x, y), (x, 1 - y), (1 - x, 1 - y)]

        def slot(o, p):
            return o.at[4 * p[0] + 2 * p[1] + p[2]]

        def copy(a, k, block, to, src=None):
            o = o_refs[a]
            return pltpu.make_async_remote_copy(
                src_ref=slot(o, block) if src is None else src, dst_ref=slot(o, block),
                send_sem=send_sems.at[7 * a + k], recv_sem=recv_sems.at[7 * a + k], device_id=to, device_id_type=MESH)

        own, sends = [], []
        for a in range(n):
            mine = pltpu.make_async_copy(x_refs[a], slot(o_refs[a], me), local_sems.at[a])
            mine.start()
            own.append(mine)
            first = [copy(a, 0, me, sibling, src=x_refs[a])]
            first += [copy(a, 1 + j, me, (*chip, c), src=x_refs[a]) for j, chip in enumerate(chips)]
            for cp in first:
                cp.start()
            sends += first
        for a in range(n):
            for j, chip in enumerate(chips):
                copy(a, 1 + j, (*chip, c), me).wait_recv()
                fwd = copy(a, 4 + j, (*chip, c), sibling)
                fwd.start()
                sends.append(fwd)
        for a in range(n):
            copy(a, 0, sibling, me).wait_recv()
            for j, chip in enumerate(chips):
                copy(a, 4 + j, (*chip, 1 - c), me).wait_recv()
        for cp in sends:
            cp.wait_send()
        for cp in own:
            cp.wait()

    return pl.pallas_call(
        body, out_shape=[jax.ShapeDtypeStruct((N_DEV,) + v.shape, v.dtype) for v in xs],
        in_specs=[HBM_SPEC] * n, out_specs=[HBM_SPEC] * n,
        scratch_shapes=[pltpu.SemaphoreType.DMA((7 * n,)), pltpu.SemaphoreType.DMA((7 * n,)), pltpu.SemaphoreType.DMA((n,))],
        name=name)(*xs)


def _direct_exchange(gather, x_refs, o_refs, send_sems, recv_sems, local_sems, start):
    x, y, c = lax.axis_index("x"), lax.axis_index("y"), lax.axis_index("c")
    my = 4 * x + 2 * y + c
    for a, (x_ref, o_ref) in enumerate(zip(x_refs, o_refs)):
        mine = pltpu.make_async_copy(x_ref if gather else x_ref.at[my], o_ref.at[my], local_sems.at[a])
        if start:
            mine.start()
        else:
            mine.wait()
        for k in range(1, N_DEV):
            px = 1 - x if k & 4 else x
            py = 1 - y if k & 2 else y
            pc = 1 - c if k & 1 else c
            pid = 4 * px + 2 * py + pc
            src = x_ref if gather else x_ref.at[pid]
            sems = dict(send_sem=send_sems.at[7 * a + k - 1], recv_sem=recv_sems.at[7 * a + k - 1],
                        device_id=(px, py, pc), device_id_type=MESH)
            if start:
                pltpu.make_async_remote_copy(src_ref=src, dst_ref=o_ref.at[my], **sems).start()
            else:
                pltpu.make_async_remote_copy(src_ref=src, dst_ref=o_ref.at[my], **sems).wait_send()
                pltpu.make_async_remote_copy(src_ref=src, dst_ref=o_ref.at[pid], **sems).wait_recv()


def _exchange_scratch(n):
    return [pltpu.SemaphoreType.DMA((7 * n,)), pltpu.SemaphoreType.DMA((7 * n,)), pltpu.SemaphoreType.DMA((n,))]


def _exchange_out_shapes(gather, xs):
    return [jax.ShapeDtypeStruct(((N_DEV,) + v.shape) if gather else v.shape, v.dtype) for v in xs]


def all_to_all(xs, name):
    n = len(xs)

    def body(*refs):
        ex = (False, refs[:n], refs[n:2 * n], *refs[2 * n:])
        _direct_exchange(*ex, start=True)
        _direct_exchange(*ex, start=False)

    return pl.pallas_call(
        body, out_shape=_exchange_out_shapes(False, xs), in_specs=[HBM_SPEC] * n, out_specs=[HBM_SPEC] * n,
        scratch_shapes=_exchange_scratch(n), name=name)(*xs)


def sum_slots(g8, name):
    R, C = g8.shape[1:]

    def fn(*tiles):
        tot = tiles[0].astype(f32)
        for t in tiles[1:]:
            tot = tot + t.astype(f32)
        return tot

    return rowwise(fn, [(g8, 'lead', k) for k in range(N_DEV)], [], [(C, f32)], [], _pick(R, (256, 128, 64, 32, 16, 8)), name)[0]


PACK_W = 1024


def _pad_rows(flat, mult):
    n = flat.shape[0]
    tot = -(-n // (PACK_W * mult)) * PACK_W * mult
    return jnp.pad(flat, (0, tot - n)).reshape(tot // PACK_W, PACK_W)


BF16_TILE_ROWS = 16


def _pad_tile_rows(a, axis):
    pad = [(0, 0)] * a.ndim
    pad[axis] = (0, -a.shape[axis] % BF16_TILE_ROWS)
    return jnp.pad(a, pad)


def _shard_shape(name):
    r, c = BIG_SHAPE[name]
    return (r // N_DEV, c) if BIG[name] == 0 else (r, c // N_DEV)


def _pack_rows(name):
    r, c = _shard_shape(name)
    assert (r * c) % PACK_W == 0
    return r * c // PACK_W


TRANSPOSED = ('ffn1_w_gate', 'ffn1_w_up', 'ffn2_w_gate', 'ffn2_w_up')
assert all(BIG_SHAPE[n][0] == PACK_W and BIG[n] == 1 for n in TRANSPOSED)


def _shard_to_rows(name, shard):
    return (shard.T if name in TRANSPOSED else shard).reshape(_pack_rows(name), PACK_W)


def _rows_to_shard(name, rows):
    r, c = _shard_shape(name)
    return rows.reshape(c, r).T if name in TRANSPOSED else rows.reshape(r, c)


def _split_for_devices(name, full):
    r, c = BIG_SHAPE[name]
    if BIG[name] == 0 or name in TRANSPOSED:
        return full.reshape(N_DEV, _pack_rows(name), PACK_W)
    return full.reshape(r, N_DEV, c // N_DEV).transpose(1, 0, 2).reshape(N_DEV, _pack_rows(name), PACK_W)


def _join_from_devices(name, parts):
    r, c = BIG_SHAPE[name]
    if name in TRANSPOSED:
        return parts.reshape(c, r)
    if BIG[name] == 0:
        return parts.reshape(r, c)
    return parts.reshape(N_DEV, r, c // N_DEV).transpose(1, 0, 2).reshape(r, c)


def _row_tile(S, want):
    return _pick(S, tuple(t for t in (512, 256, 128, 64, 32, 16) if t <= want))


def resid_ln(x, f, g, b, scale, name):
    D = x.shape[1]

    def fn(x, f, g, b):
        z = ALPHA * x + scale * f
        xo = _ln(z, g, b)
        return z, xo, xo

    return rowwise(fn, [x, f], [g, b], [(D, f32), (D, f32), (D, bf16)], [], _row_tile(x.shape[0], 512), name)


def ln_bwd(z, g, b, dxo, scale, name):
    D = z.shape[1]

    def fn(z, dxo, g, b):
        _, vjp = jax.vjp(_ln, z, g, b)
        dz, dg, db = vjp(dxo)
        return dz, scale * dz, dg, db

    return rowwise(fn, [z, dxo], [g, b], [(D, f32), (D, bf16)], [(1, D), (1, D)], _row_tile(z.shape[0], 512), name)


FF_TILE = 256


def ffn_up_act(xb, wgT, wuT):
    M, K = xb.shape
    tm = _pick(M, (1024, 512, 256, 128, 64, 32, 16))

    def body(x_ref, wg_ref, wu_ref, ab_ref, h_ref):
        x = x_ref[...]
        a = _nt(x, wg_ref[...])
        b = _nt(x, wu_ref[...])
        ab_ref[0] = a.astype(bf16)
        ab_ref[1] = b.astype(bf16)
        h_ref[...] = _swiglu(a, b).astype(bf16)

    wspec = pl.BlockSpec((FF_TILE, K), lambda i, j: (j, 0))
    return pl.pallas_call(
        body, grid=(M // tm, D_FF // FF_TILE),
        in_specs=[pl.BlockSpec((tm, K), lambda i, j: (i, 0)), wspec, wspec],
        out_specs=[pl.BlockSpec((2, tm, FF_TILE), lambda i, j: (0, i, j)), pl.BlockSpec((tm, FF_TILE), lambda i, j: (i, j))],
        out_shape=[jax.ShapeDtypeStruct((2, M, D_FF), bf16), jax.ShapeDtypeStruct((M, D_FF), bf16)],
        compiler_params=_params(("parallel", "parallel")), name="ffn_up_act")(xb, wgT, wuT)


def ffn_down_dx_act(dzs, wd, ab):
    M, K = dzs.shape
    tm = _pick(M, (1024, 512, 256, 128, 64, 32, 16))

    def body(dz_ref, w_ref, ab_ref, dab_ref, h_ref):
        dh = _nt(dz_ref[...], w_ref[...])
        h, vjp = jax.vjp(_swiglu, ab_ref[0].astype(f32), ab_ref[1].astype(f32))
        da, db = vjp(dh)
        dab_ref[0] = da.astype(bf16)
        dab_ref[1] = db.astype(bf16)
        h_ref[...] = h.astype(bf16)

    pair = pl.BlockSpec((2, tm, FF_TILE), lambda i, j: (0, i, j))
    return pl.pallas_call(
        body, grid=(M // tm, D_FF // FF_TILE),
        in_specs=[pl.BlockSpec((tm, K), lambda i, j: (i, 0)), pl.BlockSpec((FF_TILE, K), lambda i, j: (j, 0)), pair],
        out_specs=[pair, pl.BlockSpec((tm, FF_TILE), lambda i, j: (i, j))],
        out_shape=[jax.ShapeDtypeStruct((2, M, D_FF), bf16), jax.ShapeDtypeStruct((M, D_FF), bf16)],
        compiler_params=_params(("parallel", "parallel")), name="ffn_down_dx_act")(dzs, wd, ab)


def ffn_fwd(x, xb, w, g, b):
    ab, h = ffn_up_act(xb, w['guT'][0], w['guT'][1])
    f = mm(h, w['d'], name="ffn_down")
    z, xo, xob = resid_ln(x, f, g, b, 0.5, "ffn_ln")
    return xo, xob, (xb, ab, z)


def ffn_bwd(dxo, res, w, g, b):
    xb, ab, z = res
    dz, dzs, dg, db = ln_bwd(z, g, b, dxo, 0.5, "ffn_ln_bwd")
    dab, h = ffn_down_dx_act(dzs, w['d'], ab)
    dwd = mm_tn(h, dzs, name="ffn_down_dw")
    dwgT = mm_tn(dab, xb, a_lead=0, name="ffn_gate_dw")
    dwuT = mm_tn(dab, xb, a_lead=1, name="ffn_up_dw")
    dx = mm(dab, w['guT'], add=dz, add_scale=ALPHA, grouped=True, name="ffn_up_dx")
    return dx, dwgT, dwuT, dwd, dg, db


def _heads_first(a, width):
    return a.reshape(a.shape[0], MLA_HEADS, width)


def mixer_fwd(x, xb, w, g, b, cos8, sin8, exch=None):
    S = x.shape[0]
    H = mm(xb, w['in'], name="mix_in")
    ya = rowwise(_gmlp, [(H, H_UG, 256), (H, H_VG, 256)], [w['gm_ng'], w['gm_ws'], w['gm_bsb']], [(GM_WIDTH, f32)], [],
                 GM_CHUNK, "gmlp")[0]
    cqn, ckvn = rowwise(_mla_prep, [(H, H_CQ, Q_LORA), (H, H_CKV, KV_LORA)], [w['qg'], w['kvg']],
                        [(Q_LORA, bf16), (KV_LORA, bf16)], [], _row_tile(S, 512), "mla_prep")
    qraw = mm(cqn, w['uq'], name="mla_uq")
    kv = mm(ckvn, w['ukv'], name="mla_ukv")
    q1, q2, k1, k2 = rowwise(_rope, [(qraw, 512, LANES), (qraw, 640, LANES), (H, H_K1, LANES), (H, H_K2, LANES), cos8, sin8],
                             [], [(LANES, f32)] * 4, [], _row_tile(S, 512), "rope")
    zpad = jnp.zeros((S, MLA_HEADS, LANES - MLA_NOPE - MLA_ROPE), f32)
    qp = jnp.concatenate([_heads_first(qraw[:, :512], 64), _heads_first(q1, ROPE_HALF), _heads_first(q2, ROPE_HALF), zpad], axis=2)
    k1b = jnp.broadcast_to(k1[:, None, :ROPE_HALF], (S, MLA_HEADS, ROPE_HALF))
    k2b = jnp.broadcast_to(k2[:, None, :ROPE_HALF], (S, MLA_HEADS, ROPE_HALF))
    kp = jnp.concatenate([_heads_first(kv[:, :512], 64), k1b, k2b, zpad], axis=2)
    qp = qp.transpose(1, 0, 2).astype(bf16)
    kp = kp.transpose(1, 0, 2).astype(bf16)
    v3 = _heads_first(kv[:, 512:], 64).astype(bf16)
    vp = v3.transpose(1, 0, 2)
    oT, lse, *received = flash_fwd(qp, kp, v3.transpose(1, 2, 0), exch)
    ob = oT.transpose(2, 0, 1).reshape(S, MLA_HEADS * MLA_V)
    bu3 = mm(H, w['ssm_wb'], a_col0=H_US, out_s3=True, name="ssm_bu")
    hs3 = scan_fwd(bu3, w['ssm_a16'])
    y1 = mm(hs3, w['ssm_wc'], a_s3=True, name="ssm_c")
    y = rowwise(_mix_post, [ya, ob, y1, (H, H_US, SSM_WIDTH)], [w['ssm_d'], w['glu_w'], w['glu_b'], w['gmix']],
                [(D_MODEL, bf16)], [], _row_tile(S, 256), "mix_post")[0]
    f = mm(y, w['out'], name="mix_out")
    z, xo, xob = resid_ln(x, f, g, b, 1.0, "mix_ln")
    return xo, xob, (xb, H, cqn, ckvn, qp, kp, vp, lse, hs3, ya, ob, y1, y, z), received


def mixer_bwd(dxo, res, w, g, b, cos8, sin8, exch=None):
    xb, H, cqn, ckvn, qp, kp, vp, lse, hs3, ya, ob, y1, y, z = res
    S = z.shape[0]
    gr = {}
    dz, dzs, gr['ln_g'], gr['ln_b'] = ln_bwd(z, g, b, dxo, 1.0, "mix_ln_bwd")
    gr['w_out'] = mm_tn(y, dzs, name="mix_out_dw")
    dy = mm(dzs, w['out'], b_nt=True, name="mix_out_dx")

    def post_bwd(ya, ob, y1, us, dy, dskip, gluw, glub, gmix):
        _, vjp = jax.vjp(_mix_post, ya, ob, y1, us, dskip, gluw, glub, gmix)
        dya, dob, dy1, dus, *dpars = vjp(dy)
        prod = dob * ob
        col = lax.broadcasted_iota(jnp.int32, (1, MLA_HEADS * MLA_V), 1)
        lane = lax.broadcasted_iota(jnp.int32, (1, LANES), 1)
        delta = jnp.zeros((prod.shape[0], LANES), f32)
        for h in range(MLA_HEADS):
            in_head = ((col >= MLA_V * h) & (col < MLA_V * (h + 1))).astype(f32)
            delta = jnp.where(lane == h, jnp.sum(prod * in_head, axis=-1, keepdims=True), delta)
        return (dya, dob, dy1, dus, delta, *dpars)

    dya, dob, dy1, dus_skip, delta, gr['ssm_d'], gr['ssm_glu_w'], gr['ssm_glu_b'], gr['mix_norm_g'] = rowwise(
        post_bwd, [ya, ob, y1, (H, H_US, SSM_WIDTH), dy], [w['ssm_d'], w['glu_w'], w['glu_b'], w['gmix']],
        [(GM_WIDTH, f32), (MLA_HEADS * MLA_V, f32), (SSM_WIDTH, f32), (SSM_WIDTH, f32), (LANES, f32)],
        [(1, SSM_WIDTH), (SSM_WIDTH, SSM_WIDTH), (1, SSM_WIDTH), (1, D_MODEL)], _row_tile(S, 256), "mix_post_bwd")

    gr['ssm_wc'] = mm_tn(hs3, dy1, a_s3=True, name="ssm_c_dw")
    dhs3 = mm(dy1, w['ssm_wc'], out_s3=True, b_nt=True, name="ssm_c_dx")
    dbu3, gr['ssm_a16'] = scan_bwd(dhs3, hs3, w['ssm_a16'])
    gr['ssm_wb'] = mm_tn(H, dbu3, a_col0=H_US, m_dim=SSM_WIDTH, b_s3=True, name="ssm_bu_dw")
    dus = mm(dbu3, w['ssm_wb'], add=dus_skip, add_scale=1.0, a_s3=True, b_nt=True, name="ssm_bu_dx")

    do = _heads_first(dob, MLA_V).transpose(1, 0, 2)
    delta = delta[:, :MLA_HEADS].T.reshape(MLA_HEADS, 1, S)
    dqT, dkp, dvp, *received = flash_bwd(qp, kp, kp.transpose(0, 2, 1), vp, do.astype(bf16), lse, delta, exch)
    dqp = dqT.transpose(2, 0, 1)
    dkp = dkp.transpose(1, 0, 2)
    dv = dvp.transpose(1, 0, 2).reshape(S, MLA_HEADS * MLA_V)
    lane_pad = ((0, 0), (0, LANES - ROPE_HALF))
    dq1r = dqp[:, :, 64:80].reshape(S, LANES)
    dq2r = dqp[:, :, 80:96].reshape(S, LANES)
    dk1r = jnp.pad(jnp.sum(dkp[:, :, 64:80], axis=1), lane_pad)
    dk2r = jnp.pad(jnp.sum(dkp[:, :, 80:96], axis=1), lane_pad)

    def rope_bwd(d1, d2, d3, d4, cos, sin):
        return d1 * cos + d2 * sin, d2 * cos - d1 * sin, d3 * cos + d4 * sin, d4 * cos - d3 * sin

    dq1, dq2, dk1, dk2 = rowwise(rope_bwd, [dq1r, dq2r, dk1r, dk2r, cos8, sin8], [], [(LANES, f32)] * 4, [],
                                 _row_tile(S, 512), "rope_bwd")
    dqraw = jnp.concatenate([dqp[:, :, :64].reshape(S, 512), dq1, dq2], axis=1).astype(bf16)
    dkv = jnp.concatenate([dkp[:, :, :64].reshape(S, 512), dv], axis=1).astype(bf16)
    gr['uq'] = mm_tn(cqn, dqraw, name="mla_uq_dw")
    dcqn = mm(dqraw, w['uq'], b_nt=True, name="mla_uq_dx")
    gr['ukv'] = mm_tn(ckvn, dkv, name="mla_ukv_dw")
    dckvn = mm(dkv, w['ukv'], b_nt=True, name="mla_ukv_dx")

    def prep_bwd(cq, ckv, d1, d2, qg, kvg):
        _, vjp = jax.vjp(_mla_prep, cq, ckv, qg, kvg)
        return vjp((d1, d2))

    dcq, dckv, gr['mla_q_norm_g'], gr['mla_kv_norm_g'] = rowwise(
        prep_bwd, [(H, H_CQ, Q_LORA), (H, H_CKV, KV_LORA), dcqn, dckvn], [w['qg'], w['kvg']],
        [(Q_LORA, f32), (KV_LORA, f32)], [(1, Q_LORA), (1, KV_LORA)], _row_tile(S, 256), "mla_prep_bwd")

    def gmlp_bwd(hu, hv, dya, ng, ws, bsb):
        _, vjp = jax.vjp(_gmlp, hu, hv, ng, ws, bsb)
        return vjp(dya)

    dhu, dhv, gr['gmlp_norm_g'], gr['gmlp_ws'], gr['gm_bsb'] = rowwise(
        gmlp_bwd, [(H, H_UG, 256), (H, H_VG, 256), dya], [w['gm_ng'], w['gm_ws'], w['gm_bsb']],
        [(GM_WIDTH, f32), (GM_WIDTH, f32)], [(1, GM_WIDTH), (GM_HEADS, GM_CHUNK, GM_CHUNK), (GM_CHUNK, GM_WIDTH)],
        GM_CHUNK, "gmlp_bwd")

    dH = jnp.concatenate([dhu, dhv, dcq, dus, dckv, dk1, dk2], axis=1).astype(bf16)
    gr['in'] = mm_tn(xb, dH, name="mix_in_dw")
    dx = mm(dH, w['in'], add=dz, add_scale=ALPHA, b_nt=True, name="mix_in_dx")
    return dx, gr, received


def _block_diag(blocks):
    G, a, b = blocks.shape
    eye = jnp.eye(G, dtype=blocks.dtype)
    return (eye[:, None, :, None] * blocks[:, :, None, :]).reshape(G * a, G * b)


def _diag_blocks(mat, G):
    a, b = mat.shape[0] // G, mat.shape[1] // G
    m4 = mat.reshape(G, a, G, b)
    eye = jnp.eye(G, dtype=mat.dtype)
    return jnp.sum(m4 * eye[:, None, :, None], axis=2)


def prep_layer(W, rep, l):
    w = {}
    for f in ('ffn1', 'ffn2'):
        w[f] = {'guT': jnp.stack([W[f + '_w_gate'], W[f + '_w_up']]), 'd': W[f + '_w_down']}
    wi = W['w_in']
    z112 = jnp.zeros((D_MODEL, LANES - ROPE_HALF), wi.dtype)
    w['in'] = jnp.concatenate([wi[:, :768], wi[:, 928:1184], wi[:, 768:896], wi[:, 896:912], z112, wi[:, 912:928], z112], axis=1)
    uq = W['mla_w_uq'].reshape(Q_LORA, MLA_HEADS, MLA_NOPE + MLA_ROPE)
    w['uq'] = jnp.concatenate([uq[:, :, :64].reshape(Q_LORA, 512), uq[:, :, 64:80].reshape(Q_LORA, LANES),
                               uq[:, :, 80:96].reshape(Q_LORA, LANES)], axis=1)
    ukv = W['mla_w_ukv'].reshape(KV_LORA, MLA_HEADS, MLA_NOPE + MLA_V)
    w['ukv'] = jnp.concatenate([ukv[:, :, :64].reshape(KV_LORA, 512), ukv[:, :, 64:].reshape(KV_LORA, 512)], axis=1)
    w['out'] = W['w_out']
    w['glu_w'] = W['ssm_glu_w']
    w['gm_ng'] = rep['gmlp_norm_g'][l].reshape(1, GM_WIDTH)
    w['gm_ws'] = rep['gmlp_ws'][l]
    w['gm_bsb'] = jnp.repeat(rep['gmlp_bs'][l].T, GM_HEAD_DIM, axis=1)
    w['qg'] = rep['mla_q_norm_g'][l].reshape(1, Q_LORA)
    w['kvg'] = rep['mla_kv_norm_g'][l].reshape(1, KV_LORA)
    w['ssm_d'] = rep['ssm_d'][l].reshape(1, SSM_WIDTH)
    w['glu_b'] = rep['ssm_glu_b'][l].reshape(1, SSM_WIDTH)
    w['gmix'] = rep['mix_norm_g'][l].reshape(1, D_MODEL)
    ar = rep['ssm_a_re'][l].reshape(1, N_STATE)
    ai = rep['ssm_a_im'][l].reshape(1, N_STATE)
    ldt = jnp.repeat(rep['ssm_log_dt'][l], SSM_STATE).reshape(1, N_STATE)
    brT = rep['ssm_b_re'][l].transpose(2, 0, 1).reshape(SSM_GROUP_CH, N_STATE)
    biT = rep['ssm_b_im'][l].transpose(2, 0, 1).reshape(SSM_GROUP_CH, N_STATE)
    w['ssm_prep_in'] = (ar, ai, ldt, brT, biT)
    abr, abi, bbrT, bbiT = whole(_ssm_prep, w['ssm_prep_in'], [(1, N_STATE)] * 2 + [(SSM_GROUP_CH, N_STATE)] * 2, "ssm_prep")
    w['ssm_a16'] = jnp.concatenate([abr.reshape(8, LANES), abi.reshape(8, LANES)], axis=0)

    def to_gcp(t):
        return t.reshape(SSM_GROUP_CH, SSM_GROUPS, SSM_STATE).transpose(1, 0, 2)

    w['ssm_wb'] = jnp.concatenate([_block_diag(to_gcp(bbrT)), _block_diag(to_gcp(bbiT))], axis=1).astype(bf16)
    cre = rep['ssm_c_re'][l].transpose(0, 2, 1)
    cim = rep['ssm_c_im'][l].transpose(0, 2, 1)
    w['ssm_wc'] = jnp.concatenate([_block_diag(cre), -_block_diag(cim)], axis=0).astype(bf16)
    return w


def unprep_grads(gr, w):
    out = {}
    for k in ('ln_g', 'ln_b', 'w_out', 'mla_q_norm_g', 'mla_kv_norm_g', 'ssm_glu_w', 'gmlp_ws'):
        out[k] = gr[k]
    out['gmlp_norm_g'] = gr['gmlp_norm_g'].reshape(GM_WIDTH)
    out['mla_q_norm_g'] = gr['mla_q_norm_g'].reshape(Q_LORA)
    out['mla_kv_norm_g'] = gr['mla_kv_norm_g'].reshape(KV_LORA)
    out['ssm_d'] = gr['ssm_d'].reshape(SSM_GROUPS, SSM_GROUP_CH)
    out['ssm_glu_b'] = gr['ssm_glu_b'].reshape(SSM_WIDTH)
    out['mix_norm_g'] = gr['mix_norm_g'].reshape(D_MODEL)
    out['gmlp_bs'] = gr['gm_bsb'].reshape(GM_CHUNK, GM_HEADS, GM_HEAD_DIM).sum(axis=-1).T
    d = gr['in']
    out['w_in'] = jnp.concatenate([d[:, :768], d[:, H_CKV:H_CKV + KV_LORA], d[:, H_K1:H_K1 + ROPE_HALF],
                                   d[:, H_K2:H_K2 + ROPE_HALF], d[:, H_US:H_US + SSM_WIDTH]], axis=1)
    d = gr['uq']
    out['mla_w_uq'] = jnp.concatenate([d[:, :512].reshape(Q_LORA, MLA_HEADS, 64), d[:, 512:640].reshape(Q_LORA, MLA_HEADS, ROPE_HALF),
                                       d[:, 640:768].reshape(Q_LORA, MLA_HEADS, ROPE_HALF)], axis=2).reshape(Q_LORA, 768)
    d = gr['ukv']
    out['mla_w_ukv'] = jnp.concatenate([d[:, :512].reshape(KV_LORA, MLA_HEADS, 64), d[:, 512:].reshape(KV_LORA, MLA_HEADS, 64)],
                                       axis=2).reshape(KV_LORA, 1024)
    dwc = gr['ssm_wc']
    out['ssm_c_re'] = _diag_blocks(dwc[:N_STATE], SSM_GROUPS).transpose(0, 2, 1)
    out['ssm_c_im'] = -_diag_blocks(dwc[N_STATE:], SSM_GROUPS).transpose(0, 2, 1)
    dwb = gr['ssm_wb']

    def from_blocks(m):
        return _diag_blocks(m, SSM_GROUPS).transpose(1, 0, 2).reshape(SSM_GROUP_CH, N_STATE)

    dbbrT, dbbiT = from_blocks(dwb[:, :N_STATE]), from_blocks(dwb[:, N_STATE:])
    da16 = gr['ssm_a16']
    dabr, dabi = da16[0:8].reshape(1, N_STATE), da16[8:16].reshape(1, N_STATE)

    def prep_bwd(ar, ai, ldt, brT, biT, d1, d2, d3, d4):
        _, vjp = jax.vjp(_ssm_prep, ar, ai, ldt, brT, biT)
        return vjp((d1, d2, d3, d4))

    dar, dai, dldt, dbrT, dbiT = whole(prep_bwd, w['ssm_prep_in'] + (dabr, dabi, dbbrT, dbbiT),
                                       [(1, N_STATE)] * 3 + [(SSM_GROUP_CH, N_STATE)] * 2, "ssm_prep_bwd")
    out['ssm_a_re'] = dar.reshape(SSM_GROUPS, SSM_STATE)
    out['ssm_a_im'] = dai.reshape(SSM_GROUPS, SSM_STATE)
    out['ssm_log_dt'] = dldt.reshape(SSM_GROUPS, SSM_STATE).sum(axis=-1)
    out['ssm_b_re'] = dbrT.reshape(SSM_GROUP_CH, SSM_GROUPS, SSM_STATE).transpose(1, 2, 0)
    out['ssm_b_im'] = dbiT.reshape(SSM_GROUP_CH, SSM_GROUPS, SSM_STATE).transpose(1, 2, 0)
    return out


def adamw(w, g, m, v, name):
    R, C = w.shape

    def fn(w, g, m, v):
        m = ADAM_B1 * m + (1.0 - ADAM_B1) * g
        v = ADAM_B2 * v + (1.0 - ADAM_B2) * jnp.square(g)
        m_hat = m / (1.0 - ADAM_B1 ** ADAM_STEP)
        v_hat = v / (1.0 - ADAM_B2 ** ADAM_STEP)
        delta = -ADAM_LR * (m_hat / (jnp.sqrt(v_hat) + ADAM_EPS) + ADAM_WD * w)
        return delta, m, v

    return rowwise(fn, [w, g, m, v], [], [(C, f32)] * 3, [], _pick(R, (256, 128, 64, 32, 16, 8)), name)


def kernel(x, positions, ln_g, ln_b, ffn1_w_gate, ffn1_w_up, ffn1_w_down, w_in, gmlp_norm_g, gmlp_ws, gmlp_bs, mla_q_norm_g, mla_w_uq, mla_kv_norm_g, mla_w_ukv, ssm_a_re, ssm_a_im, ssm_b_re, ssm_b_im, ssm_c_re, ssm_c_im, ssm_d, ssm_log_dt, ssm_glu_w, ssm_glu_b, mix_norm_g, w_out, ffn2_w_gate, ffn2_w_up, ffn2_w_down, loss_target, m_ln_g, m_ln_b, m_ffn1_w_gate, m_ffn1_w_up, m_ffn1_w_down, m_w_in, m_gmlp_norm_g, m_gmlp_ws, m_gmlp_bs, m_mla_q_norm_g, m_mla_w_uq, m_mla_kv_norm_g, m_mla_w_ukv, m_ssm_a_re, m_ssm_a_im, m_ssm_b_re, m_ssm_b_im, m_ssm_c_re, m_ssm_c_im, m_ssm_d, m_ssm_log_dt, m_ssm_glu_w, m_ssm_glu_b, m_mix_norm_g, m_w_out, m_ffn2_w_gate, m_ffn2_w_up, m_ffn2_w_down, v_ln_g, v_ln_b, v_ffn1_w_gate, v_ffn1_w_up, v_ffn1_w_down, v_w_in, v_gmlp_norm_g, v_gmlp_ws, v_gmlp_bs, v_mla_q_norm_g, v_mla_w_uq, v_mla_kv_norm_g, v_mla_w_ukv, v_ssm_a_re, v_ssm_a_im, v_ssm_b_re, v_ssm_b_im, v_ssm_c_re, v_ssm_c_im, v_ssm_d, v_ssm_log_dt, v_ssm_glu_w, v_ssm_glu_b, v_mix_norm_g, v_w_out, v_ffn2_w_gate, v_ffn2_w_up, v_ffn2_w_down):
    Wp = dict(zip(W_NAMES, (ln_g, ln_b, ffn1_w_gate, ffn1_w_up, ffn1_w_down, w_in, gmlp_norm_g, gmlp_ws, gmlp_bs, mla_q_norm_g, mla_w_uq, mla_kv_norm_g, mla_w_ukv, ssm_a_re, ssm_a_im, ssm_b_re, ssm_b_im, ssm_c_re, ssm_c_im, ssm_d, ssm_log_dt, ssm_glu_w, ssm_glu_b, mix_norm_g, w_out, ffn2_w_gate, ffn2_w_up, ffn2_w_down)))
    Mp = dict(zip(W_NAMES, (m_ln_g, m_ln_b, m_ffn1_w_gate, m_ffn1_w_up, m_ffn1_w_down, m_w_in, m_gmlp_norm_g, m_gmlp_ws, m_gmlp_bs, m_mla_q_norm_g, m_mla_w_uq, m_mla_kv_norm_g, m_mla_w_ukv, m_ssm_a_re, m_ssm_a_im, m_ssm_b_re, m_ssm_b_im, m_ssm_c_re, m_ssm_c_im, m_ssm_d, m_ssm_log_dt, m_ssm_glu_w, m_ssm_glu_b, m_mix_norm_g, m_w_out, m_ffn2_w_gate, m_ffn2_w_up, m_ffn2_w_down)))
    Vp = dict(zip(W_NAMES, (v_ln_g, v_ln_b, v_ffn1_w_gate, v_ffn1_w_up, v_ffn1_w_down, v_w_in, v_gmlp_norm_g, v_gmlp_ws, v_gmlp_bs, v_mla_q_norm_g, v_mla_w_uq, v_mla_kv_norm_g, v_mla_w_ukv, v_ssm_a_re, v_ssm_a_im, v_ssm_b_re, v_ssm_b_im, v_ssm_c_re, v_ssm_c_im, v_ssm_d, v_ssm_log_dt, v_ssm_glu_w, v_ssm_glu_b, v_mix_norm_g, v_w_out, v_ffn2_w_gate, v_ffn2_w_up, v_ffn2_w_down)))
    S = x.shape[1]
    my = _my_id()

    def shard_rows(l):
        return [_pad_tile_rows(_shard_to_rows(n, Wp[n][l].astype(bf16)), 0) for n in BIG]

    def joined(got):
        return {n: _join_from_devices(n, g[:, :_pack_rows(n)]) for n, g in zip(BIG, got)}

    ln_flat = jnp.concatenate([Wp[n].reshape(-1) for n in LN_NAMES])
    *got, ln_all = all_gather(shard_rows(0) + [_pad_rows(ln_flat, 8)], "gather_weights")
    ln_all = ln_all.reshape(N_DEV, -1)
    lnsz = DEPTH * 3 * (D_MODEL // N_DEV)
    ln_full = {}
    for t, n in enumerate(LN_NAMES):
        sh = ln_all[:, t * lnsz:(t + 1) * lnsz].reshape(N_DEV, DEPTH, 3, D_MODEL // N_DEV)
        ln_full[n] = sh.transpose(1, 2, 0, 3).reshape(DEPTH, 3, 1, D_MODEL)
    rep = {n: Wp[n] for n in REPL}

    inv_freq = 1.0 / (ROPE_BASE ** (jnp.arange(0, MLA_ROPE, 2, dtype=f32) / MLA_ROPE))
    ang = positions.astype(f32).reshape(S, 1) * inv_freq[None, :]
    cos8 = jnp.tile(jnp.cos(ang), (1, MLA_HEADS))
    sin8 = jnp.tile(jnp.sin(ang), (1, MLA_HEADS))

    xs = x.reshape(S, D_MODEL)
    xb = xs.astype(bf16)
    ws, saved = [], []
    for l in range(DEPTH):
        w = prep_layer(joined(got), rep, l)
        lg, lb = ln_full['ln_g'][l], ln_full['ln_b'][l]
        xs, xb, r1 = ffn_fwd(xs, xb, w['ffn1'], lg[0], lb[0])
        xs, xb, r2, got = mixer_fwd(xs, xb, w, lg[1], lb[1], cos8, sin8, (True, shard_rows(l + 1)) if l + 1 < DEPTH else None)
        xs, xb, r3 = ffn_fwd(xs, xb, w['ffn2'], lg[2], lb[2])
        ws.append(w)
        saved.append((r1, r2, r3))

    def loss_fn(y, t):
        d = y - t
        part = jnp.sum(jnp.mean(jnp.square(d), axis=-1, keepdims=True), axis=0, keepdims=True)
        return d * (1.0 / D_MODEL), 0.5 * part

    dx, loss_part = rowwise(loss_fn, [xs, loss_target.reshape(S, D_MODEL)], [], [(D_MODEL, f32)], [(1, 1)],
                            _row_tile(S, 512), "loss")
    loss = lax.psum(loss_part[0, 0], ("x", "y", "c"))

    def grad_pack(n, g):
        return _pad_tile_rows(_split_for_devices(n, g.astype(bf16)), 1)

    ffn2_names = [n for n in BIG if n.startswith('ffn2')]
    rest_names = [n for n in BIG if n not in ffn2_names]
    grads, arrived = [None] * DEPTH, [{} for _ in range(DEPTH)]
    for l in reversed(range(DEPTH)):
        w = ws[l]
        lg, lb = ln_full['ln_g'][l], ln_full['ln_b'][l]
        r1, r2, r3 = saved[l]
        dx, g2g, g2u, g2d, dg2, db2 = ffn_bwd(dx, r3, w['ffn2'], lg[2], lb[2])
        riders = [(l, n, g) for n, g in zip(ffn2_names, (g2g, g2u, g2d))]
        if l + 1 < DEPTH:
            riders += [(l + 1, n, grads[l + 1][n]) for n in rest_names]
        dx, gm, got = mixer_bwd(dx, r2, w, lg[1], lb[1], cos8, sin8, (False, [grad_pack(n, g) for _, n, g in riders]))
        for (layer, n, _), arr in zip(riders, got):
            arrived[layer][n] = arr
        dx, g1g, g1u, g1d, dg0, db0 = ffn_bwd(dx, r1, w['ffn1'], lg[0], lb[0])
        g = unprep_grads(gm, w)
        g.update({'ffn1_w_gate': g1g, 'ffn1_w_up': g1u, 'ffn1_w_down': g1d,
                  'ffn2_w_gate': g2g, 'ffn2_w_up': g2u, 'ffn2_w_down': g2d})
        g['ln_g'] = jnp.concatenate([dg0, g['ln_g'], dg2], axis=0)
        g['ln_b'] = jnp.concatenate([db0, g['ln_b'], db2], axis=0)
        grads[l] = g
    grad_x = dx.reshape(1, S, D_MODEL)

    arrived[0].update(zip(rest_names, all_to_all([grad_pack(n, grads[0][n]) for n in rest_names], "scatter_grads")))
    G = {n: jnp.stack([_rows_to_shard(n, sum_slots(arrived[l][n], "sum_" + n)[:_pack_rows(n)]) for l in range(DEPTH)])
         for n in BIG}
    small_names = LN_NAMES + REPL
    spack = jnp.concatenate([jnp.stack([grads[l][n] for l in range(DEPTH)]).reshape(-1) for n in small_names])
    n_small = spack.shape[0]
    gsmall = sum_slots(all_gather([_pad_rows(spack, 8)], "gather_small")[0], "sum_small").reshape(-1)

    off = 0
    for n in small_names:
        shp = (DEPTH, 3, D_MODEL) if n in LN_NAMES else Wp[n].shape
        sz = math.prod(shp)
        G[n] = gsmall[off:off + sz].reshape(shp)
        off += sz
    for n in LN_NAMES:
        G[n] = lax.dynamic_slice_in_dim(G[n], my * (D_MODEL // N_DEV), D_MODEL // N_DEV, axis=2)

    delta, new_m, new_v = {}, {}, {}
    for n in BIG:
        shp = Wp[n].shape
        two = (shp[0] * shp[1], shp[2])
        d_, m_, v_ = adamw(Wp[n].reshape(two), G[n].reshape(two), Mp[n].reshape(two), Vp[n].reshape(two), "adamw_" + n)
        delta[n], new_m[n], new_v[n] = d_.reshape(shp), m_.reshape(shp), v_.reshape(shp)

    def pack_small(src):
        return _pad_rows(jnp.concatenate([src[n].reshape(-1) for n in small_names]), 8)

    d_, m_, v_ = adamw(pack_small(Wp), pack_small(G), pack_small(Mp), pack_small(Vp), "adamw_small")
    d_, m_, v_ = d_.reshape(-1), m_.reshape(-1), v_.reshape(-1)
    off = 0
    for n in small_names:
        shp = Wp[n].shape
        sz = math.prod(shp)
        delta[n], new_m[n], new_v[n] = (t[off:off + sz].reshape(shp) for t in (d_, m_, v_))
        off += sz

    return (loss, grad_x, *[G[n] for n in W_NAMES], *[delta[n] for n in W_NAMES],
            *[new_m[n] for n in W_NAMES], *[new_v[n] for n in W_NAMES])
```

```python
import functools
import math

import jax
import jax.numpy as jnp
from jax import lax
from jax.experimental import pallas as pl
from jax.experimental.pallas import tpu as pltpu

f32 = jnp.float32
bf16 = jnp.bfloat16

D_MODEL = 1024
DEPTH = 4
D_FF = 2816
GM_HEADS, GM_HEAD_DIM, GM_WIDTH, GM_CHUNK = 4, 64, 256, 128
MLA_HEADS, MLA_NOPE, MLA_ROPE, MLA_V = 8, 64, 32, 64
ROPE_HALF = MLA_ROPE // 2
Q_LORA, KV_LORA = 256, 128
ROPE_BASE = 10000.0
SSM_GROUPS, SSM_GROUP_CH, SSM_WIDTH, SSM_STATE = 16, 16, 256, 64
N_STATE = SSM_GROUPS * SSM_STATE
ALPHA = (2 * DEPTH) ** 0.25
LN_EPS = 1e-5
RMS_EPS = 1e-6
NEG_BIG = -1e30
ATT_SCALE = (MLA_NOPE + MLA_ROPE) ** -0.5
ADAM_LR, ADAM_B1, ADAM_B2, ADAM_EPS, ADAM_WD, ADAM_STEP = 0.001, 0.9, 0.999, 1e-08, 0.01, 10

N_DEV = 8
LANES = 128
VMEM_LIMIT = 48 * 1024 * 1024
MM_TILE_BUDGET = 32 * 1024 * 1024
MESH = pl.DeviceIdType.MESH

H_UG, H_VG, H_CQ, H_US, H_CKV, H_K1, H_K2, H_COLS = 0, 256, 512, 768, 1024, 1152, 1280, 1408

W_NAMES = ['ln_g', 'ln_b', 'ffn1_w_gate', 'ffn1_w_up', 'ffn1_w_down', 'w_in', 'gmlp_norm_g', 'gmlp_ws', 'gmlp_bs',
           'mla_q_norm_g', 'mla_w_uq', 'mla_kv_norm_g', 'mla_w_ukv', 'ssm_a_re', 'ssm_a_im', 'ssm_b_re', 'ssm_b_im',
           'ssm_c_re', 'ssm_c_im', 'ssm_d', 'ssm_log_dt', 'ssm_glu_w', 'ssm_glu_b', 'mix_norm_g', 'w_out',
           'ffn2_w_gate', 'ffn2_w_up', 'ffn2_w_down']
BIG = {'ffn1_w_gate': 1, 'ffn1_w_up': 1, 'ffn1_w_down': 0, 'w_in': 1, 'mla_w_uq': 1, 'mla_w_ukv': 1,
       'ssm_glu_w': 0, 'w_out': 0, 'ffn2_w_gate': 1, 'ffn2_w_up': 1, 'ffn2_w_down': 0}
BIG_SHAPE = {'ffn1_w_gate': (D_MODEL, D_FF), 'ffn1_w_up': (D_MODEL, D_FF), 'ffn1_w_down': (D_FF, D_MODEL),
             'w_in': (D_MODEL, 1184), 'mla_w_uq': (Q_LORA, 768), 'mla_w_ukv': (KV_LORA, 1024),
             'ssm_glu_w': (SSM_WIDTH, SSM_WIDTH), 'w_out': (D_MODEL, D_MODEL),
             'ffn2_w_gate': (D_MODEL, D_FF), 'ffn2_w_up': (D_MODEL, D_FF), 'ffn2_w_down': (D_FF, D_MODEL)}
LN_NAMES = ['ln_g', 'ln_b']
REPL = [n for n in W_NAMES if n not in BIG and n not in LN_NAMES]


def _pick(n, cands):
    for c in cands:
        if n % c == 0:
            return c
    return n


def _params(sem):
    return pltpu.CompilerParams(dimension_semantics=sem, vmem_limit_bytes=VMEM_LIMIT)


S3_ROWS = 2 * N_STATE // LANES


def _from_s3(ref):
    return jnp.concatenate([ref[:, c, :] for c in range(S3_ROWS)], axis=1)


def _to_s3(ref, val):
    for c in range(S3_ROWS):
        ref[:, c, :] = val[:, c * LANES:(c + 1) * LANES].astype(ref.dtype)


def mm(a, b, *, out_dtype=f32, add=None, add_scale=1.0, a_col0=0, a_s3=False, out_s3=False, grouped=False, b_nt=False, name):
    G = a.shape[0] if grouped else 1
    M = a.shape[1] if grouped else a.shape[0]
    K, N = (b.shape[-1], b.shape[-2]) if b_nt else b.shape[-2:]
    tn = N if out_s3 else _pick(N, (512, 384, 256) if N <= 1536 else (512, 384, 256, 128))
    tk = K if (K <= 2 * D_FF or a_s3) else _pick(K, (1024, 512, 256, 128))
    nk = G * (K // tk)
    assert not grouped or tk == K

    def tile_bytes(tm):
        return 2 * (tm * tk * a.dtype.itemsize + tk * tn * b.dtype.itemsize + tm * tn * 4 * (2 if add is not None else 1))

    tm = next(t for t in (1024, 512, 256, 128, 64, 32, 16, 8) if M % t == 0 and (tile_bytes(t) <= MM_TILE_BUDGET or t == 8))
    assert a_s3 or grouped or (a_col0 % tk == 0 and a_col0 + K <= a.shape[1])
    kb0 = a_col0 // tk
    has_add = add is not None

    def body(*refs):
        if has_add:
            a_ref, b_ref, add_ref, o_ref, acc = refs
        else:
            a_ref, b_ref, o_ref, acc = refs
        k = pl.program_id(2)
        a_val = _from_s3(a_ref) if a_s3 else a_ref[...]
        part = (_nt if b_nt else functools.partial(jnp.dot, preferred_element_type=f32))(a_val.astype(bf16), b_ref[...].astype(bf16))

        def finish(total):
            if has_add:
                total = total + add_scale * add_ref[...]
            if out_s3:
                _to_s3(o_ref, total)
            else:
                o_ref[...] = total.astype(o_ref.dtype)

        if nk == 1:
            finish(part)
        else:
            @pl.when(k == 0)
            def _():
                acc[...] = part

            @pl.when(k > 0)
            def _():
                acc[...] += part

            @pl.when(k == nk - 1)
            def _():
                finish(acc[...])

    if a_s3:
        a_spec = pl.BlockSpec((tm, S3_ROWS, LANES), lambda i, j, k: (i, 0, 0))
    elif grouped:
        a_spec = pl.BlockSpec((None, tm, tk), lambda i, j, k: (k, i, 0))
    else:
        a_spec = pl.BlockSpec((tm, tk), lambda i, j, k: (i, kb0 + k))
    if b_nt:
        b_spec = pl.BlockSpec((None, tn, tk), lambda i, j, k: (k, j, 0)) if grouped else pl.BlockSpec((tn, tk), lambda i, j, k: (j, k))
    else:
        b_spec = pl.BlockSpec((None, tk, tn), lambda i, j, k: (k, 0, j)) if grouped else pl.BlockSpec((tk, tn), lambda i, j, k: (k, j))
    in_specs, ops = [a_spec, b_spec], [a, b]
    if has_add:
        in_specs.append(pl.BlockSpec((tm, tn), lambda i, j, k: (i, j)))
        ops.append(add)
    if out_s3:
        out_spec = pl.BlockSpec((tm, S3_ROWS, LANES), lambda i, j, k: (i, 0, 0))
        out_shape = jax.ShapeDtypeStruct((M, S3_ROWS, LANES), out_dtype)
    else:
        out_spec = pl.BlockSpec((tm, tn), lambda i, j, k: (i, j))
        out_shape = jax.ShapeDtypeStruct((M, N), out_dtype)
    return pl.pallas_call(
        body, grid=(M // tm, N // tn, nk), in_specs=in_specs, out_specs=out_spec, out_shape=out_shape,
        scratch_shapes=[pltpu.VMEM((tm, tn) if nk > 1 else (8, LANES), f32)],
        compiler_params=_params(("parallel", "parallel", "arbitrary")), name=name)(*ops)


def mm_tn(a, b, *, a_col0=0, m_dim=None, a_s3=False, b_s3=False, a_lead=None, name):
    K = a.shape[-2] if a_lead is not None else a.shape[0]
    M = 2 * N_STATE if a_s3 else (a.shape[-1] if m_dim is None else m_dim)
    N = 2 * N_STATE if b_s3 else b.shape[-1]
    tm = M if a_s3 else _pick(M, (H_COLS, 1024, 768, 512, 384, 256, 128))
    tn = N if b_s3 else _pick(N, (H_COLS, 1024, 768, 512, 384, 256))

    def tile_bytes(tk):
        return 2 * tk * (tm * a.dtype.itemsize + tn * b.dtype.itemsize) + 3 * tm * tn * 4

    tk = next(t for t in (2048, 1024, 512, 256, 128, 64, 32, 16) if K % t == 0 and (tile_bytes(t) <= MM_TILE_BUDGET or t == 16))
    nk = K // tk
    assert a_col0 % tm == 0
    mb0 = a_col0 // tm

    def body(a_ref, b_ref, o_ref, acc):
        k = pl.program_id(2)
        a_val = _from_s3(a_ref) if a_s3 else a_ref[...]
        b_val = _from_s3(b_ref) if b_s3 else b_ref[...]
        part = lax.dot_general(a_val.astype(bf16), b_val.astype(bf16), (((0,), (0,)), ((), ())), preferred_element_type=f32)

        @pl.when(k == 0)
        def _():
            acc[...] = part

        @pl.when(k > 0)
        def _():
            acc[...] += part

        @pl.when(k == nk - 1)
        def _():
            o_ref[...] = acc[...]

    s3_spec = pl.BlockSpec((tk, S3_ROWS, LANES), lambda i, j, k: (k, 0, 0))
    if a_s3:
        a_spec = s3_spec
    elif a_lead is not None:
        a_spec = pl.BlockSpec((None, tk, tm), lambda i, j, k: (a_lead, k, i))
    else:
        a_spec = pl.BlockSpec((tk, tm), lambda i, j, k: (k, mb0 + i))
    return pl.pallas_call(
        body, grid=(M // tm, N // tn, nk),
        in_specs=[a_spec, s3_spec if b_s3 else pl.BlockSpec((tk, tn), lambda i, j, k: (k, j))],
        out_specs=pl.BlockSpec((tm, tn), lambda i, j, k: (i, j)),
        out_shape=jax.ShapeDtypeStruct((M, N), f32),
        scratch_shapes=[pltpu.VMEM((tm, tn), f32)],
        compiler_params=_params(("parallel", "parallel", "arbitrary")), name=name)(a, b)


def rowwise(fn, rows, pars, out_rows, out_accs, tm, name):
    first = rows[0]
    if isinstance(first, tuple):
        R = first[0].shape[1] if first[1] == 'lead' else first[0].shape[0]
    else:
        R = first.shape[0]
    assert R % tm == 0, (R, tm, name)
    n_rows, n_pars, n_or, n_oa = len(rows), len(pars), len(out_rows), len(out_accs)

    in_specs, ops = [], []
    for r in rows:
        if isinstance(r, tuple) and r[1] == 'lead':
            arr, _, kk = r
            in_specs.append(pl.BlockSpec((None, tm, arr.shape[2]), lambda i, kk=kk: (kk, i, 0)))
        elif isinstance(r, tuple):
            arr, c0, w = r
            assert c0 % w == 0
            in_specs.append(pl.BlockSpec((tm, w), lambda i, cb=c0 // w: (i, cb)))
        else:
            arr = r
            in_specs.append(pl.BlockSpec((tm, arr.shape[1]), lambda i: (i, 0)))
        ops.append(arr)
    for p in pars:
        in_specs.append(pl.BlockSpec(p.shape, lambda i, nd=p.ndim: (0,) * nd))
        ops.append(p)
    out_specs = [pl.BlockSpec((tm, w), lambda i: (i, 0)) for (w, _) in out_rows]
    out_specs += [pl.BlockSpec(s, lambda i, nd=len(s): (0,) * nd) for s in out_accs]
    out_shape = [jax.ShapeDtypeStruct((R, w), dt) for (w, dt) in out_rows]
    out_shape += [jax.ShapeDtypeStruct(s, f32) for s in out_accs]

    def body(*refs):
        ins = [r[...] for r in refs[:n_rows + n_pars]]
        o_refs = refs[n_rows + n_pars:]
        res = fn(*ins)
        if not isinstance(res, (tuple, list)):
            res = (res,)
        assert len(res) == n_or + n_oa, (len(res), n_or, n_oa, name)
        for o, v in zip(o_refs[:n_or], res[:n_or]):
            o[...] = v.astype(o.dtype)
        if n_oa:
            i = pl.program_id(0)

            @pl.when(i == 0)
            def _():
                for o, v in zip(o_refs[n_or:], res[n_or:]):
                    o[...] = v.astype(f32)

            @pl.when(i > 0)
            def _():
                for o, v in zip(o_refs[n_or:], res[n_or:]):
                    o[...] += v.astype(f32)

    return pl.pallas_call(
        body, grid=(R // tm,), in_specs=in_specs, out_specs=out_specs, out_shape=out_shape,
        compiler_params=_params(("arbitrary",)), name=name)(*ops)


def whole(fn, ins, out_shapes, name):
    def body(*refs):
        res = fn(*[r[...] for r in refs[:len(ins)]])
        for o, v in zip(refs[len(ins):], res):
            o[...] = v

    return pl.pallas_call(body, out_shape=[jax.ShapeDtypeStruct(s, f32) for s in out_shapes], name=name)(*ins)


@jax.custom_vjp
def _bdot(a, b):
    return jnp.dot(a.astype(bf16), b.astype(bf16), preferred_element_type=f32)


def _bdot_fwd(a, b):
    return _bdot(a, b), (a, b)


def _bdot_bwd(res, g):
    a, b = res
    gb = g.astype(bf16)
    da = lax.dot_general(gb, b.astype(bf16), (((1,), (1,)), ((), ())), preferred_element_type=f32)
    db = lax.dot_general(a.astype(bf16), gb, (((0,), (0,)), ((), ())), preferred_element_type=f32)
    return da, db


_bdot.defvjp(_bdot_fwd, _bdot_bwd)


def _ln(z, g, b):
    mu = jnp.mean(z, axis=-1, keepdims=True)
    var = jnp.mean(jnp.square(z - mu), axis=-1, keepdims=True)
    return (z - mu) * lax.rsqrt(var + LN_EPS) * g + b


def _rms_only(x):
    return x * lax.rsqrt(jnp.mean(jnp.square(x), axis=-1, keepdims=True) + RMS_EPS)


def _swiglu(a, b):
    return jax.nn.silu(a) * b


def _gmlp(hu, hv, ng, ws, bsb):
    u = jax.nn.gelu(hu)
    v = jax.nn.gelu(hv)
    lane = lax.broadcasted_iota(jnp.int32, (1, GM_WIDTH), 1)
    masks = [((lane >= GM_HEAD_DIM * h) & (lane < GM_HEAD_DIM * (h + 1))).astype(f32) for h in range(GM_HEADS)]
    mu = jnp.zeros_like(v)
    for m in masks:
        mu = mu + m * (jnp.sum(v * m, axis=-1, keepdims=True) / GM_HEAD_DIM)
    d = v - mu
    var = jnp.zeros_like(v)
    for m in masks:
        var = var + m * (jnp.sum(d * d * m, axis=-1, keepdims=True) / GM_HEAD_DIM)
    vn = d * lax.rsqrt(var + LN_EPS) * ng
    r = lax.broadcasted_iota(jnp.int32, (GM_CHUNK, GM_CHUNK), 0)
    c = lax.broadcasted_iota(jnp.int32, (GM_CHUNK, GM_CHUNK), 1)
    tril = (c <= r).astype(f32)
    z = bsb
    for h, m in enumerate(masks):
        z = z + _bdot(ws[h] * tril, vn * m)
    return u * z


def _mla_prep(cq, ckv, qg, kvg):
    return _rms_only(cq) * qg, _rms_only(ckv) * kvg


def _rope(q1, q2, k1, k2, cos, sin):
    return q1 * cos - q2 * sin, q2 * cos + q1 * sin, k1 * cos - k2 * sin, k2 * cos + k1 * sin


def _mix_post(ya, ob, y1, us, dskip, gluw, glub, gmix):
    y = jax.nn.gelu(y1 + dskip * us)
    yc = y * jax.nn.sigmoid(_bdot(y, gluw) + glub)
    return jnp.concatenate([_rms_only(ya), _rms_only(ob), _rms_only(yc)], axis=1) * gmix


def _ssm_prep(ar, ai, ldt, brT, biT):
    dt = jnp.exp(ldt)
    mag = jnp.exp(ar * dt)
    abr = mag * jnp.cos(ai * dt)
    abi = mag * jnp.sin(ai * dt)
    den = ar * ar + ai * ai
    cr = ((abr - 1.0) * ar + abi * ai) / den
    ci = (abi * ar - (abr - 1.0) * ai) / den
    return abr, abi, cr * brT - ci * biT, cr * biT + ci * brT


ATT_FWD_HEADS = 2


def _att_tile(S):
    return _pick(S, (512, 256, 128))


def _nt(a, b):
    return lax.dot_general(a, b, (((1,), (1,)), ((), ())), preferred_element_type=f32)


def _diag_keep(T):
    krow = lax.broadcasted_iota(jnp.int32, (T, T), 0)
    qcol = lax.broadcasted_iota(jnp.int32, (T, T), 1)
    return qcol >= krow


def grid_call(body, exch, *, grid, in_specs, out_specs, out_shape, scratch_shapes, name, args):
    params = _params(("arbitrary", "arbitrary"))
    if exch is None:
        return pl.pallas_call(body, grid=grid, in_specs=in_specs, out_specs=out_specs, out_shape=out_shape,
                              scratch_shapes=scratch_shapes, compiler_params=params, name=name)(*args)
    gather, xs = exch
    n, n_in, n_out, n_sc = len(xs), len(in_specs), len(out_specs), len(scratch_shapes)

    def riding(*refs):
        ins, x_refs = refs[:n_in], refs[n_in:n_in + n]
        outs, xo_refs = refs[n_in + n:n_in + n + n_out], refs[n_in + n + n_out:n_in + 2 * n + n_out]
        rest = refs[n_in + 2 * n + n_out:]
        scratch, sems = rest[:n_sc], rest[n_sc:]
        h, i = pl.program_id(0), pl.program_id(1)

        @pl.when((h == 0) & (i == 0))
        def _():
            _direct_exchange(gather, x_refs, xo_refs, *sems, start=True)

        body(*ins, *outs, *scratch)

        @pl.when((h == grid[0] - 1) & (i == grid[1] - 1))
        def _():
            _direct_exchange(gather, x_refs, xo_refs, *sems, start=False)

    return pl.pallas_call(
        riding, grid=grid, in_specs=list(in_specs) + [HBM_SPEC] * n, out_specs=list(out_specs) + [HBM_SPEC] * n,
        out_shape=list(out_shape) + _exchange_out_shapes(gather, xs),
        scratch_shapes=list(scratch_shapes) + _exchange_scratch(n), compiler_params=params,
        name=name + ("_gather" if gather else "_scatter"))(*args, *xs)


def flash_fwd(q, k, vT, exch=None):
    Hh, S, _ = q.shape
    T = _att_tile(S)
    HB = ATT_FWD_HEADS

    def body(q_ref, k_ref, vT_ref, o_ref, lse_ref, m_sc, l_sc, acc_sc):
        i = pl.program_id(1)
        m_sc[...] = jnp.full_like(m_sc, NEG_BIG)
        l_sc[...] = jnp.zeros_like(l_sc)
        acc_sc[...] = jnp.zeros_like(acc_sc)

        def block(j, diagonal):
            rows = pl.ds(pl.multiple_of(j * T, T), T)
            for hh in range(HB):
                sT = _nt(k_ref[hh, rows, :], q_ref[hh]) * ATT_SCALE
                if diagonal:
                    sT = jnp.where(_diag_keep(T), sT, NEG_BIG)
                m_old = m_sc[hh]
                m_new = jnp.maximum(m_old, jnp.max(sT, axis=0, keepdims=True))
                alpha = jnp.exp(m_old - m_new)
                pT = jnp.exp(sT - m_new)
                l_sc[hh] = alpha * l_sc[hh] + jnp.sum(pT, axis=0, keepdims=True)
                acc_sc[hh] = alpha * acc_sc[hh] + jnp.dot(vT_ref[hh, :, rows], pT.astype(bf16), preferred_element_type=f32)
                m_sc[hh] = m_new

        def loop_body(j, c):
            block(j, False)
            return c

        lax.fori_loop(0, i, loop_body, 0)
        block(i, True)
        o_ref[...] = acc_sc[...] / l_sc[...]
        lse_ref[...] = m_sc[...] + jnp.log(l_sc[...])

    return grid_call(
        body, exch, grid=(Hh // HB, S // T),
        in_specs=[pl.BlockSpec((HB, T, LANES), lambda h, i: (h, i, 0)), pl.BlockSpec((HB, S, LANES), lambda h, i: (h, 0, 0)),
                  pl.BlockSpec((HB, MLA_V, S), lambda h, i: (h, 0, 0))],
        out_specs=[pl.BlockSpec((HB, MLA_V, T), lambda h, i: (h, 0, i)), pl.BlockSpec((HB, 1, T), lambda h, i: (h, 0, i))],
        out_shape=[jax.ShapeDtypeStruct((Hh, MLA_V, S), f32), jax.ShapeDtypeStruct((Hh, 1, S), f32)],
        scratch_shapes=[pltpu.VMEM((HB, 1, T), f32), pltpu.VMEM((HB, 1, T), f32), pltpu.VMEM((HB, MLA_V, T), f32)],
        name="flash_fwd", args=(q, k, vT))


def flash_bwd(q, k, kT, v, do, lse, delta, exch=None):
    Hh, S, _ = q.shape
    T = _att_tile(S)

    def body(q_ref, do_ref, lse_ref, dl_ref, k_ref, kT_ref, v_ref, dq_ref, dk_ref, dv_ref, dq_sc):
        i = pl.program_id(1)

        @pl.when(i == 0)
        def _():
            dk_ref[...] = jnp.zeros_like(dk_ref)
            dv_ref[...] = jnp.zeros_like(dv_ref)

        qi, doi = q_ref[0], do_ref[0]
        lse_i, dl_i = lse_ref[0], dl_ref[0]
        dq_sc[...] = jnp.zeros_like(dq_sc)

        def block(j, diagonal):
            rows = pl.ds(pl.multiple_of(j * T, T), T)
            sT = _nt(k_ref[0, rows, :], qi) * ATT_SCALE
            pT = jnp.exp(sT - lse_i)
            if diagonal:
                pT = jnp.where(_diag_keep(T), pT, 0.0)
            dpT = _nt(v_ref[0, rows, :], doi)
            dsT = (pT * (dpT - dl_i) * ATT_SCALE).astype(bf16)
            dv_ref[0, rows, :] += jnp.dot(pT.astype(bf16), doi, preferred_element_type=f32)
            dk_ref[0, rows, :] += jnp.dot(dsT, qi, preferred_element_type=f32)
            dq_sc[...] += jnp.dot(kT_ref[0, :, rows], dsT, preferred_element_type=f32)

        def loop_body(j, c):
            block(j, False)
            return c

        lax.fori_loop(0, i, loop_body, 0)
        block(i, True)
        dq_ref[0] = dq_sc[...]

    tile = lambda w: pl.BlockSpec((1, T, w), lambda h, i: (h, i, 0))
    row = pl.BlockSpec((1, 1, T), lambda h, i: (h, 0, i))
    full = lambda w: pl.BlockSpec((1, S, w), lambda h, i: (h, 0, 0))
    return grid_call(
        body, exch, grid=(Hh, S // T),
        in_specs=[tile(LANES), tile(MLA_V), row, row, full(LANES), pl.BlockSpec((1, LANES, S), lambda h, i: (h, 0, 0)), full(MLA_V)],
        out_specs=[pl.BlockSpec((1, LANES, T), lambda h, i: (h, 0, i)), full(LANES), full(MLA_V)],
        out_shape=[jax.ShapeDtypeStruct((Hh, LANES, S), f32), jax.ShapeDtypeStruct((Hh, S, LANES), f32),
                   jax.ShapeDtypeStruct((Hh, S, MLA_V), f32)],
        scratch_shapes=[pltpu.VMEM((LANES, T), f32)],
        name="flash_bwd", args=(q, do, lse, delta, k, kT, v))


def _scan_tile(S):
    return _pick(S, (256, 128, 64, 32, 16, 8))


def scan_fwd(bu3, a16):
    S = bu3.shape[0]
    ts = _scan_tile(S)

    def body(bu_ref, a_ref, o_ref, h_sc):
        @pl.when(pl.program_id(0) == 0)
        def _():
            h_sc[...] = jnp.zeros_like(h_sc)

        ar, ai = a_ref[0:8, :], a_ref[8:16, :]
        a2r, a2i = ar * ar - ai * ai, 2.0 * ar * ai

        def pair(p, carry):
            hr, hi = carry
            t = 2 * p
            x0r, x0i = bu_ref[t, 0:8, :], bu_ref[t, 8:16, :]
            ur = ar * x0r - ai * x0i + bu_ref[t + 1, 0:8, :]
            ui = ar * x0i + ai * x0r + bu_ref[t + 1, 8:16, :]
            o_ref[t, 0:8, :] = ar * hr - ai * hi + x0r
            o_ref[t, 8:16, :] = ar * hi + ai * hr + x0i
            nr = a2r * hr - a2i * hi + ur
            ni = a2r * hi + a2i * hr + ui
            o_ref[t + 1, 0:8, :] = nr
            o_ref[t + 1, 8:16, :] = ni
            return nr, ni

        hr, hi = lax.fori_loop(0, ts // 2, pair, (h_sc[0:8, :], h_sc[8:16, :]), unroll=4)
        h_sc[0:8, :] = hr
        h_sc[8:16, :] = hi

    blk = pl.BlockSpec((ts, 16, LANES), lambda i: (i, 0, 0))
    return pl.pallas_call(
        body, grid=(S // ts,), in_specs=[blk, pl.BlockSpec((16, LANES), lambda i: (0, 0))], out_specs=blk,
        out_shape=jax.ShapeDtypeStruct(bu3.shape, f32), scratch_shapes=[pltpu.VMEM((16, LANES), f32)],
        compiler_params=_params(("arbitrary",)), name="scan_fwd")(bu3, a16)


def scan_bwd(g3, h3, a16):
    S = g3.shape[0]
    ts = _scan_tile(S)
    nb = S // ts

    def body(g_ref, h_ref, a_ref, o_ref, da_ref, lam_sc, da_sc):
        @pl.when(pl.program_id(0) == 0)
        def _():
            lam_sc[...] = jnp.zeros_like(lam_sc)
            da_sc[...] = jnp.zeros_like(da_sc)

        ar, ai = a_ref[0:8, :], a_ref[8:16, :]
        a2r, a2i = ar * ar - ai * ai, 2.0 * ar * ai

        def pair(p, carry):
            lr, li, dar, dai = carry
            t = ts - 1 - 2 * p
            h1r, h1i = h_ref[t, 0:8, :], h_ref[t, 8:16, :]
            h0r, h0i = h_ref[t - 1, 0:8, :], h_ref[t - 1, 8:16, :]
            g1r, g1i = g_ref[t, 0:8, :], g_ref[t, 8:16, :]
            l1r = ar * lr + ai * li + g1r
            l1i = ar * li - ai * lr + g1i
            vr = ar * g1r + ai * g1i + g_ref[t - 1, 0:8, :]
            vi = ar * g1i - ai * g1r + g_ref[t - 1, 8:16, :]
            l0r = a2r * lr + a2i * li + vr
            l0i = a2r * li - a2i * lr + vi
            dar = dar + (lr * h1r + li * h1i) + (l1r * h0r + l1i * h0i)
            dai = dai + (li * h1r - lr * h1i) + (l1i * h0r - l1r * h0i)
            o_ref[t, 0:8, :] = l1r
            o_ref[t, 8:16, :] = l1i
            o_ref[t - 1, 0:8, :] = l0r
            o_ref[t - 1, 8:16, :] = l0i
            return l0r, l0i, dar, dai

        lr, li, dar, dai = lax.fori_loop(
            0, ts // 2, pair, (lam_sc[0:8, :], lam_sc[8:16, :], da_sc[0:8, :], da_sc[8:16, :]), unroll=4)
        lam_sc[0:8, :] = lr
        lam_sc[8:16, :] = li
        da_sc[0:8, :] = dar
        da_sc[8:16, :] = dai
        da_ref[0:8, :] = dar
        da_ref[8:16, :] = dai

    blk = pl.BlockSpec((ts, 16, LANES), lambda i: (nb - 1 - i, 0, 0))
    small = pl.BlockSpec((16, LANES), lambda i: (0, 0))
    return pl.pallas_call(
        body, grid=(nb,), in_specs=[blk, blk, small], out_specs=[blk, small],
        out_shape=[jax.ShapeDtypeStruct(g3.shape, f32), jax.ShapeDtypeStruct((16, LANES), f32)],
        scratch_shapes=[pltpu.VMEM((16, LANES), f32), pltpu.VMEM((16, LANES), f32)],
        compiler_params=_params(("arbitrary",)), name="scan_bwd")(g3, h3, a16)


HBM_SPEC = pl.BlockSpec(memory_space=pltpu.HBM)


def _my_id():
    return 4 * lax.axis_index("x") + 2 * lax.axis_index("y") + lax.axis_index("c")


def all_gather(xs, name):
    n = len(xs)

    def body(*refs):
        x_refs, o_refs = refs[:n], refs[n:2 * n]
        send_sems, recv_sems, local_sems = refs[2 * n:]
        x, y, c = lax.axis_index("x"), lax.axis_index("y"), lax.axis_index("c")
        me, sibling = (x, y, c), (x, y, 1 - c)
        chips = [(1 - x, y), (x, 1 - y), (1 - x, 1 - y)]

        def slot(o, p):
            return o.at[4 * p[0] + 2 * p[1] + p[2]]

        def copy(a, k, block, to, src=None):
            o = o_refs[a]
            return pltpu.make_async_remote_copy(
                src_ref=slot(o, block) if src is None else src, dst_ref=slot(o, block),
                send_sem=send_sems.at[7 * a + k], recv_sem=recv_sems.at[7 * a + k], device_id=to, device_id_type=MESH)

        own, sends = [], []
        for a in range(n):
            mine = pltpu.make_async_copy(x_refs[a], slot(o_refs[a], me), local_sems.at[a])
            mine.start()
            own.append(mine)
            first = [copy(a, 0, me, sibling, src=x_refs[a])]
            first += [copy(a, 1 + j, me, (*chip, c), src=x_refs[a]) for j, chip in enumerate(chips)]
            for cp in first:
                cp.start()
            sends += first
        for a in range(n):
            for j, chip in enumerate(chips):
                copy(a, 1 + j, (*chip, c), me).wait_recv()
                fwd = copy(a, 4 + j, (*chip, c), sibling)
                fwd.start()
                sends.append(fwd)
        for a in range(n):
            copy(a, 0, sibling, me).wait_recv()
            for j, chip in enumerate(chips):
                copy(a, 4 + j, (*chip, 1 - c), me).wait_recv()
        for cp in sends:
            cp.wait_send()
        for cp in own:
            cp.wait()

    return pl.pallas_call(
        body, out_shape=[jax.ShapeDtypeStruct((N_DEV,) + v.shape, v.dtype) for v in xs],
        in_specs=[HBM_SPEC] * n, out_specs=[HBM_SPEC] * n,
        scratch_shapes=[pltpu.SemaphoreType.DMA((7 * n,)), pltpu.SemaphoreType.DMA((7 * n,)), pltpu.SemaphoreType.DMA((n,))],
        name=name)(*xs)


def _direct_exchange(gather, x_refs, o_refs, send_sems, recv_sems, local_sems, start):
    x, y, c = lax.axis_index("x"), lax.axis_index("y"), lax.axis_index("c")
    my = 4 * x + 2 * y + c
    for a, (x_ref, o_ref) in enumerate(zip(x_refs, o_refs)):
        mine = pltpu.make_async_copy(x_ref if gather else x_ref.at[my], o_ref.at[my], local_sems.at[a])
        if start:
            mine.start()
        else:
            mine.wait()
        for k in range(1, N_DEV):
            px = 1 - x if k & 4 else x
            py = 1 - y if k & 2 else y
            pc = 1 - c if k & 1 else c
            pid = 4 * px + 2 * py + pc
            src = x_ref if gather else x_ref.at[pid]
            sems = dict(send_sem=send_sems.at[7 * a + k - 1], recv_sem=recv_sems.at[7 * a + k - 1],
                        device_id=(px, py, pc), device_id_type=MESH)
            if start:
                pltpu.make_async_remote_copy(src_ref=src, dst_ref=o_ref.at[my], **sems).start()
            else:
                pltpu.make_async_remote_copy(src_ref=src, dst_ref=o_ref.at[my], **sems).wait_send()
                pltpu.make_async_remote_copy(src_ref=src, dst_ref=o_ref.at[pid], **sems).wait_recv()


def _exchange_scratch(n):
    return [pltpu.SemaphoreType.DMA((7 * n,)), pltpu.SemaphoreType.DMA((7 * n,)), pltpu.SemaphoreType.DMA((n,))]


def _exchange_out_shapes(gather, xs):
    return [jax.ShapeDtypeStruct(((N_DEV,) + v.shape) if gather else v.shape, v.dtype) for v in xs]


def all_to_all(xs, name):
    n = len(xs)

    def body(*refs):
        ex = (False, refs[:n], refs[n:2 * n], *refs[2 * n:])
        _direct_exchange(*ex, start=True)
        _direct_exchange(*ex, start=False)

    return pl.pallas_call(
        body, out_shape=_exchange_out_shapes(False, xs), in_specs=[HBM_SPEC] * n, out_specs=[HBM_SPEC] * n,
        scratch_shapes=_exchange_scratch(n), name=name)(*xs)


def sum_slots(g8, name):
    R, C = g8.shape[1:]

    def fn(*tiles):
        tot = tiles[0].astype(f32)
        for t in tiles[1:]:
            tot = tot + t.astype(f32)
        return tot

    return rowwise(fn, [(g8, 'lead', k) for k in range(N_DEV)], [], [(C, f32)], [], _pick(R, (256, 128, 64, 32, 16, 8)), name)[0]


PACK_W = 1024


def _pad_rows(flat, mult):
    n = flat.shape[0]
    tot = -(-n // (PACK_W * mult)) * PACK_W * mult
    return jnp.pad(flat, (0, tot - n)).reshape(tot // PACK_W, PACK_W)


BF16_TILE_ROWS = 16


def _pad_tile_rows(a, axis):
    pad = [(0, 0)] * a.ndim
    pad[axis] = (0, -a.shape[axis] % BF16_TILE_ROWS)
    return jnp.pad(a, pad)


def _shard_shape(name):
    r, c = BIG_SHAPE[name]
    return (r // N_DEV, c) if BIG[name] == 0 else (r, c // N_DEV)


def _pack_rows(name):
    r, c = _shard_shape(name)
    assert (r * c) % PACK_W == 0
    return r * c // PACK_W


TRANSPOSED = ('ffn1_w_gate', 'ffn1_w_up', 'ffn2_w_gate', 'ffn2_w_up')
assert all(BIG_SHAPE[n][0] == PACK_W and BIG[n] == 1 for n in TRANSPOSED)


def _shard_to_rows(name, shard):
    return (shard.T if name in TRANSPOSED else shard).reshape(_pack_rows(name), PACK_W)


def _rows_to_shard(name, rows):
    r, c = _shard_shape(name)
    return rows.reshape(c, r).T if name in TRANSPOSED else rows.reshape(r, c)


def _split_for_devices(name, full):
    r, c = BIG_SHAPE[name]
    if BIG[name] == 0 or name in TRANSPOSED:
        return full.reshape(N_DEV, _pack_rows(name), PACK_W)
    return full.reshape(r, N_DEV, c // N_DEV).transpose(1, 0, 2).reshape(N_DEV, _pack_rows(name), PACK_W)


def _join_from_devices(name, parts):
    r, c = BIG_SHAPE[name]
    if name in TRANSPOSED:
        return parts.reshape(c, r)
    if BIG[name] == 0:
        return parts.reshape(r, c)
    return parts.reshape(N_DEV, r, c // N_DEV).transpose(1, 0, 2).reshape(r, c)


def _row_tile(S, want):
    return _pick(S, tuple(t for t in (512, 256, 128, 64, 32, 16) if t <= want))


def resid_ln(x, f, g, b, scale, name):
    D = x.shape[1]

    def fn(x, f, g, b):
        z = ALPHA * x + scale * f
        xo = _ln(z, g, b)
        return z, xo, xo

    return rowwise(fn, [x, f], [g, b], [(D, f32), (D, f32), (D, bf16)], [], _row_tile(x.shape[0], 512), name)


def ln_bwd(z, g, b, dxo, scale, name):
    D = z.shape[1]

    def fn(z, dxo, g, b):
        _, vjp = jax.vjp(_ln, z, g, b)
        dz, dg, db = vjp(dxo)
        return dz, scale * dz, dg, db

    return rowwise(fn, [z, dxo], [g, b], [(D, f32), (D, bf16)], [(1, D), (1, D)], _row_tile(z.shape[0], 512), name)


FF_TILE = 256


def ffn_up_act(xb, wgT, wuT):
    M, K = xb.shape
    tm = _pick(M, (1024, 512, 256, 128, 64, 32, 16))

    def body(x_ref, wg_ref, wu_ref, ab_ref, h_ref):
        x = x_ref[...]
        a = _nt(x, wg_ref[...])
        b = _nt(x, wu_ref[...])
        ab_ref[0] = a.astype(bf16)
        ab_ref[1] = b.astype(bf16)
        h_ref[...] = _swiglu(a, b).astype(bf16)

    wspec = pl.BlockSpec((FF_TILE, K), lambda i, j: (j, 0))
    return pl.pallas_call(
        body, grid=(M // tm, D_FF // FF_TILE),
        in_specs=[pl.BlockSpec((tm, K), lambda i, j: (i, 0)), wspec, wspec],
        out_specs=[pl.BlockSpec((2, tm, FF_TILE), lambda i, j: (0, i, j)), pl.BlockSpec((tm, FF_TILE), lambda i, j: (i, j))],
        out_shape=[jax.ShapeDtypeStruct((2, M, D_FF), bf16), jax.ShapeDtypeStruct((M, D_FF), bf16)],
        compiler_params=_params(("parallel", "parallel")), name="ffn_up_act")(xb, wgT, wuT)


def ffn_down_dx_act(dzs, wd, ab):
    M, K = dzs.shape
    tm = _pick(M, (1024, 512, 256, 128, 64, 32, 16))

    def body(dz_ref, w_ref, ab_ref, dab_ref, h_ref):
        dh = _nt(dz_ref[...], w_ref[...])
        h, vjp = jax.vjp(_swiglu, ab_ref[0].astype(f32), ab_ref[1].astype(f32))
        da, db = vjp(dh)
        dab_ref[0] = da.astype(bf16)
        dab_ref[1] = db.astype(bf16)
        h_ref[...] = h.astype(bf16)

    pair = pl.BlockSpec((2, tm, FF_TILE), lambda i, j: (0, i, j))
    return pl.pallas_call(
        body, grid=(M // tm, D_FF // FF_TILE),
        in_specs=[pl.BlockSpec((tm, K), lambda i, j: (i, 0)), pl.BlockSpec((FF_TILE, K), lambda i, j: (j, 0)), pair],
        out_specs=[pair, pl.BlockSpec((tm, FF_TILE), lambda i, j: (i, j))],
        out_shape=[jax.ShapeDtypeStruct((2, M, D_FF), bf16), jax.ShapeDtypeStruct((M, D_FF), bf16)],
        compiler_params=_params(("parallel", "parallel")), name="ffn_down_dx_act")(dzs, wd, ab)


def ffn_fwd(x, xb, w, g, b):
    ab, h = ffn_up_act(xb, w['guT'][0], w['guT'][1])
    f = mm(h, w['d'], name="ffn_down")
    z, xo, xob = resid_ln(x, f, g, b, 0.5, "ffn_ln")
    return xo, xob, (xb, ab, z)


def ffn_bwd(dxo, res, w, g, b):
    xb, ab, z = res
    dz, dzs, dg, db = ln_bwd(z, g, b, dxo, 0.5, "ffn_ln_bwd")
    dab, h = ffn_down_dx_act(dzs, w['d'], ab)
    dwd = mm_tn(h, dzs, name="ffn_down_dw")
    dwgT = mm_tn(dab, xb, a_lead=0, name="ffn_gate_dw")
    dwuT = mm_tn(dab, xb, a_lead=1, name="ffn_up_dw")
    dx = mm(dab, w['guT'], add=dz, add_scale=ALPHA, grouped=True, name="ffn_up_dx")
    return dx, dwgT, dwuT, dwd, dg, db


def _heads_first(a, width):
    return a.reshape(a.shape[0], MLA_HEADS, width)


def mixer_fwd(x, xb, w, g, b, cos8, sin8, exch=None):
    S = x.shape[0]
    H = mm(xb, w['in'], name="mix_in")
    ya = rowwise(_gmlp, [(H, H_UG, 256), (H, H_VG, 256)], [w['gm_ng'], w['gm_ws'], w['gm_bsb']], [(GM_WIDTH, f32)], [],
                 GM_CHUNK, "gmlp")[0]
    cqn, ckvn = rowwise(_mla_prep, [(H, H_CQ, Q_LORA), (H, H_CKV, KV_LORA)], [w['qg'], w['kvg']],
                        [(Q_LORA, bf16), (KV_LORA, bf16)], [], _row_tile(S, 512), "mla_prep")
    qraw = mm(cqn, w['uq'], name="mla_uq")
    kv = mm(ckvn, w['ukv'], name="mla_ukv")
    q1, q2, k1, k2 = rowwise(_rope, [(qraw, 512, LANES), (qraw, 640, LANES), (H, H_K1, LANES), (H, H_K2, LANES), cos8, sin8],
                             [], [(LANES, f32)] * 4, [], _row_tile(S, 512), "rope")
    zpad = jnp.zeros((S, MLA_HEADS, LANES - MLA_NOPE - MLA_ROPE), f32)
    qp = jnp.concatenate([_heads_first(qraw[:, :512], 64), _heads_first(q1, ROPE_HALF), _heads_first(q2, ROPE_HALF), zpad], axis=2)
    k1b = jnp.broadcast_to(k1[:, None, :ROPE_HALF], (S, MLA_HEADS, ROPE_HALF))
    k2b = jnp.broadcast_to(k2[:, None, :ROPE_HALF], (S, MLA_HEADS, ROPE_HALF))
    kp = jnp.concatenate([_heads_first(kv[:, :512], 64), k1b, k2b, zpad], axis=2)
    qp = qp.transpose(1, 0, 2).astype(bf16)
    kp = kp.transpose(1, 0, 2).astype(bf16)
    v3 = _heads_first(kv[:, 512:], 64).astype(bf16)
    vp = v3.transpose(1, 0, 2)
    oT, lse, *received = flash_fwd(qp, kp, v3.transpose(1, 2, 0), exch)
    ob = oT.transpose(2, 0, 1).reshape(S, MLA_HEADS * MLA_V)
    bu3 = mm(H, w['ssm_wb'], a_col0=H_US, out_s3=True, name="ssm_bu")
    hs3 = scan_fwd(bu3, w['ssm_a16'])
    y1 = mm(hs3, w['ssm_wc'], a_s3=True, name="ssm_c")
    y = rowwise(_mix_post, [ya, ob, y1, (H, H_US, SSM_WIDTH)], [w['ssm_d'], w['glu_w'], w['glu_b'], w['gmix']],
                [(D_MODEL, bf16)], [], _row_tile(S, 256), "mix_post")[0]
    f = mm(y, w['out'], name="mix_out")
    z, xo, xob = resid_ln(x, f, g, b, 1.0, "mix_ln")
    return xo, xob, (xb, H, cqn, ckvn, qp, kp, vp, lse, hs3, ya, ob, y1, y, z), received


def mixer_bwd(dxo, res, w, g, b, cos8, sin8, exch=None):
    xb, H, cqn, ckvn, qp, kp, vp, lse, hs3, ya, ob, y1, y, z = res
    S = z.shape[0]
    gr = {}
    dz, dzs, gr['ln_g'], gr['ln_b'] = ln_bwd(z, g, b, dxo, 1.0, "mix_ln_bwd")
    gr['w_out'] = mm_tn(y, dzs, name="mix_out_dw")
    dy = mm(dzs, w['out'], b_nt=True, name="mix_out_dx")

    def post_bwd(ya, ob, y1, us, dy, dskip, gluw, glub, gmix):
        _, vjp = jax.vjp(_mix_post, ya, ob, y1, us, dskip, gluw, glub, gmix)
        dya, dob, dy1, dus, *dpars = vjp(dy)
        prod = dob * ob
        col = lax.broadcasted_iota(jnp.int32, (1, MLA_HEADS * MLA_V), 1)
        lane = lax.broadcasted_iota(jnp.int32, (1, LANES), 1)
        delta = jnp.zeros((prod.shape[0], LANES), f32)
        for h in range(MLA_HEADS):
            in_head = ((col >= MLA_V * h) & (col < MLA_V * (h + 1))).astype(f32)
            delta = jnp.where(lane == h, jnp.sum(prod * in_head, axis=-1, keepdims=True), delta)
        return (dya, dob, dy1, dus, delta, *dpars)

    dya, dob, dy1, dus_skip, delta, gr['ssm_d'], gr['ssm_glu_w'], gr['ssm_glu_b'], gr['mix_norm_g'] = rowwise(
        post_bwd, [ya, ob, y1, (H, H_US, SSM_WIDTH), dy], [w['ssm_d'], w['glu_w'], w['glu_b'], w['gmix']],
        [(GM_WIDTH, f32), (MLA_HEADS * MLA_V, f32), (SSM_WIDTH, f32), (SSM_WIDTH, f32), (LANES, f32)],
        [(1, SSM_WIDTH), (SSM_WIDTH, SSM_WIDTH), (1, SSM_WIDTH), (1, D_MODEL)], _row_tile(S, 256), "mix_post_bwd")

    gr['ssm_wc'] = mm_tn(hs3, dy1, a_s3=True, name="ssm_c_dw")
    dhs3 = mm(dy1, w['ssm_wc'], out_s3=True, b_nt=True, name="ssm_c_dx")
    dbu3, gr['ssm_a16'] = scan_bwd(dhs3, hs3, w['ssm_a16'])
    gr['ssm_wb'] = mm_tn(H, dbu3, a_col0=H_US, m_dim=SSM_WIDTH, b_s3=True, name="ssm_bu_dw")
    dus = mm(dbu3, w['ssm_wb'], add=dus_skip, add_scale=1.0, a_s3=True, b_nt=True, name="ssm_bu_dx")

    do = _heads_first(dob, MLA_V).transpose(1, 0, 2)
    delta = delta[:, :MLA_HEADS].T.reshape(MLA_HEADS, 1, S)
    dqT, dkp, dvp, *received = flash_bwd(qp, kp, kp.transpose(0, 2, 1), vp, do.astype(bf16), lse, delta, exch)
    dqp = dqT.transpose(2, 0, 1)
    dkp = dkp.transpose(1, 0, 2)
    dv = dvp.transpose(1, 0, 2).reshape(S, MLA_HEADS * MLA_V)
    lane_pad = ((0, 0), (0, LANES - ROPE_HALF))
    dq1r = dqp[:, :, 64:80].reshape(S, LANES)
    dq2r = dqp[:, :, 80:96].reshape(S, LANES)
    dk1r = jnp.pad(jnp.sum(dkp[:, :, 64:80], axis=1), lane_pad)
    dk2r = jnp.pad(jnp.sum(dkp[:, :, 80:96], axis=1), lane_pad)

    def rope_bwd(d1, d2, d3, d4, cos, sin):
        return d1 * cos + d2 * sin, d2 * cos - d1 * sin, d3 * cos + d4 * sin, d4 * cos - d3 * sin

    dq1, dq2, dk1, dk2 = rowwise(rope_bwd, [dq1r, dq2r, dk1r, dk2r, cos8, sin8], [], [(LANES, f32)] * 4, [],
                                 _row_tile(S, 512), "rope_bwd")
    dqraw = jnp.concatenate([dqp[:, :, :64].reshape(S, 512), dq1, dq2], axis=1).astype(bf16)
    dkv = jnp.concatenate([dkp[:, :, :64].reshape(S, 512), dv], axis=1).astype(bf16)
    gr['uq'] = mm_tn(cqn, dqraw, name="mla_uq_dw")
    dcqn = mm(dqraw, w['uq'], b_nt=True, name="mla_uq_dx")
    gr['ukv'] = mm_tn(ckvn, dkv, name="mla_ukv_dw")
    dckvn = mm(dkv, w['ukv'], b_nt=True, name="mla_ukv_dx")

    def prep_bwd(cq, ckv, d1, d2, qg, kvg):
        _, vjp = jax.vjp(_mla_prep, cq, ckv, qg, kvg)
        return vjp((d1, d2))

    dcq, dckv, gr['mla_q_norm_g'], gr['mla_kv_norm_g'] = rowwise(
        prep_bwd, [(H, H_CQ, Q_LORA), (H, H_CKV, KV_LORA), dcqn, dckvn], [w['qg'], w['kvg']],
        [(Q_LORA, f32), (KV_LORA, f32)], [(1, Q_LORA), (1, KV_LORA)], _row_tile(S, 256), "mla_prep_bwd")

    def gmlp_bwd(hu, hv, dya, ng, ws, bsb):
        _, vjp = jax.vjp(_gmlp, hu, hv, ng, ws, bsb)
        return vjp(dya)

    dhu, dhv, gr['gmlp_norm_g'], gr['gmlp_ws'], gr['gm_bsb'] = rowwise(
        gmlp_bwd, [(H, H_UG, 256), (H, H_VG, 256), dya], [w['gm_ng'], w['gm_ws'], w['gm_bsb']],
        [(GM_WIDTH, f32), (GM_WIDTH, f32)], [(1, GM_WIDTH), (GM_HEADS, GM_CHUNK, GM_CHUNK), (GM_CHUNK, GM_WIDTH)],
        GM_CHUNK, "gmlp_bwd")

    dH = jnp.concatenate([dhu, dhv, dcq, dus, dckv, dk1, dk2], axis=1).astype(bf16)
    gr['in'] = mm_tn(xb, dH, name="mix_in_dw")
    dx = mm(dH, w['in'], add=dz, add_scale=ALPHA, b_nt=True, name="mix_in_dx")
    return dx, gr, received


def _block_diag(blocks):
    G, a, b = blocks.shape
    eye = jnp.eye(G, dtype=blocks.dtype)
    return (eye[:, None, :, None] * blocks[:, :, None, :]).reshape(G * a, G * b)


def _diag_blocks(mat, G):
    a, b = mat.shape[0] // G, mat.shape[1] // G
    m4 = mat.reshape(G, a, G, b)
    eye = jnp.eye(G, dtype=mat.dtype)
    return jnp.sum(m4 * eye[:, None, :, None], axis=2)


def prep_layer(W, rep, l):
    w = {}
    for f in ('ffn1', 'ffn2'):
        w[f] = {'guT': jnp.stack([W[f + '_w_gate'], W[f + '_w_up']]), 'd': W[f + '_w_down']}
    wi = W['w_in']
    z112 = jnp.zeros((D_MODEL, LANES - ROPE_HALF), wi.dtype)
    w['in'] = jnp.concatenate([wi[:, :768], wi[:, 928:1184], wi[:, 768:896], wi[:, 896:912], z112, wi[:, 912:928], z112], axis=1)
    uq = W['mla_w_uq'].reshape(Q_LORA, MLA_HEADS, MLA_NOPE + MLA_ROPE)
    w['uq'] = jnp.concatenate([uq[:, :, :64].reshape(Q_LORA, 512), uq[:, :, 64:80].reshape(Q_LORA, LANES),
                               uq[:, :, 80:96].reshape(Q_LORA, LANES)], axis=1)
    ukv = W['mla_w_ukv'].reshape(KV_LORA, MLA_HEADS, MLA_NOPE + MLA_V)
    w['ukv'] = jnp.concatenate([ukv[:, :, :64].reshape(KV_LORA, 512), ukv[:, :, 64:].reshape(KV_LORA, 512)], axis=1)
    w['out'] = W['w_out']
    w['glu_w'] = W['ssm_glu_w']
    w['gm_ng'] = rep['gmlp_norm_g'][l].reshape(1, GM_WIDTH)
    w['gm_ws'] = rep['gmlp_ws'][l]
    w['gm_bsb'] = jnp.repeat(rep['gmlp_bs'][l].T, GM_HEAD_DIM, axis=1)
    w['qg'] = rep['mla_q_norm_g'][l].reshape(1, Q_LORA)
    w['kvg'] = rep['mla_kv_norm_g'][l].reshape(1, KV_LORA)
    w['ssm_d'] = rep['ssm_d'][l].reshape(1, SSM_WIDTH)
    w['glu_b'] = rep['ssm_glu_b'][l].reshape(1, SSM_WIDTH)
    w['gmix'] = rep['mix_norm_g'][l].reshape(1, D_MODEL)
    ar = rep['ssm_a_re'][l].reshape(1, N_STATE)
    ai = rep['ssm_a_im'][l].reshape(1, N_STATE)
    ldt = jnp.repeat(rep['ssm_log_dt'][l], SSM_STATE).reshape(1, N_STATE)
    brT = rep['ssm_b_re'][l].transpose(2, 0, 1).reshape(SSM_GROUP_CH, N_STATE)
    biT = rep['ssm_b_im'][l].transpose(2, 0, 1).reshape(SSM_GROUP_CH, N_STATE)
    w['ssm_prep_in'] = (ar, ai, ldt, brT, biT)
    abr, abi, bbrT, bbiT = whole(_ssm_prep, w['ssm_prep_in'], [(1, N_STATE)] * 2 + [(SSM_GROUP_CH, N_STATE)] * 2, "ssm_prep")
    w['ssm_a16'] = jnp.concatenate([abr.reshape(8, LANES), abi.reshape(8, LANES)], axis=0)

    def to_gcp(t):
        return t.reshape(SSM_GROUP_CH, SSM_GROUPS, SSM_STATE).transpose(1, 0, 2)

    w['ssm_wb'] = jnp.concatenate([_block_diag(to_gcp(bbrT)), _block_diag(to_gcp(bbiT))], axis=1).astype(bf16)
    cre = rep['ssm_c_re'][l].transpose(0, 2, 1)
    cim = rep['ssm_c_im'][l].transpose(0, 2, 1)
    w['ssm_wc'] = jnp.concatenate([_block_diag(cre), -_block_diag(cim)], axis=0).astype(bf16)
    return w


def unprep_grads(gr, w):
    out = {}
    for k in ('ln_g', 'ln_b', 'w_out', 'mla_q_norm_g', 'mla_kv_norm_g', 'ssm_glu_w', 'gmlp_ws'):
        out[k] = gr[k]
    out['gmlp_norm_g'] = gr['gmlp_norm_g'].reshape(GM_WIDTH)
    out['mla_q_norm_g'] = gr['mla_q_norm_g'].reshape(Q_LORA)
    out['mla_kv_norm_g'] = gr['mla_kv_norm_g'].reshape(KV_LORA)
    out['ssm_d'] = gr['ssm_d'].reshape(SSM_GROUPS, SSM_GROUP_CH)
    out['ssm_glu_b'] = gr['ssm_glu_b'].reshape(SSM_WIDTH)
    out['mix_norm_g'] = gr['mix_norm_g'].reshape(D_MODEL)
    out['gmlp_bs'] = gr['gm_bsb'].reshape(GM_CHUNK, GM_HEADS, GM_HEAD_DIM).sum(axis=-1).T
    d = gr['in']
    out['w_in'] = jnp.concatenate([d[:, :768], d[:, H_CKV:H_CKV + KV_LORA], d[:, H_K1:H_K1 + ROPE_HALF],
                                   d[:, H_K2:H_K2 + ROPE_HALF], d[:, H_US:H_US + SSM_WIDTH]], axis=1)
    d = gr['uq']
    out['mla_w_uq'] = jnp.concatenate([d[:, :512].reshape(Q_LORA, MLA_HEADS, 64), d[:, 512:640].reshape(Q_LORA, MLA_HEADS, ROPE_HALF),
                                       d[:, 640:768].reshape(Q_LORA, MLA_HEADS, ROPE_HALF)], axis=2).reshape(Q_LORA, 768)
    d = gr['ukv']
    out['mla_w_ukv'] = jnp.concatenate([d[:, :512].reshape(KV_LORA, MLA_HEADS, 64), d[:, 512:].reshape(KV_LORA, MLA_HEADS, 64)],
                                       axis=2).reshape(KV_LORA, 1024)
    dwc = gr['ssm_wc']
    out['ssm_c_re'] = _diag_blocks(dwc[:N_STATE], SSM_GROUPS).transpose(0, 2, 1)
    out['ssm_c_im'] = -_diag_blocks(dwc[N_STATE:], SSM_GROUPS).transpose(0, 2, 1)
    dwb = gr['ssm_wb']

    def from_blocks(m):
        return _diag_blocks(m, SSM_GROUPS).transpose(1, 0, 2).reshape(SSM_GROUP_CH, N_STATE)

    dbbrT, dbbiT = from_blocks(dwb[:, :N_STATE]), from_blocks(dwb[:, N_STATE:])
    da16 = gr['ssm_a16']
    dabr, dabi = da16[0:8].reshape(1, N_STATE), da16[8:16].reshape(1, N_STATE)

    def prep_bwd(ar, ai, ldt, brT, biT, d1, d2, d3, d4):
        _, vjp = jax.vjp(_ssm_prep, ar, ai, ldt, brT, biT)
        return vjp((d1, d2, d3, d4))

    dar, dai, dldt, dbrT, dbiT = whole(prep_bwd, w['ssm_prep_in'] + (dabr, dabi, dbbrT, dbbiT),
                                       [(1, N_STATE)] * 3 + [(SSM_GROUP_CH, N_STATE)] * 2, "ssm_prep_bwd")
    out['ssm_a_re'] = dar.reshape(SSM_GROUPS, SSM_STATE)
    out['ssm_a_im'] = dai.reshape(SSM_GROUPS, SSM_STATE)
    out['ssm_log_dt'] = dldt.reshape(SSM_GROUPS, SSM_STATE).sum(axis=-1)
    out['ssm_b_re'] = dbrT.reshape(SSM_GROUP_CH, SSM_GROUPS, SSM_STATE).transpose(1, 2, 0)
    out['ssm_b_im'] = dbiT.reshape(SSM_GROUP_CH, SSM_GROUPS, SSM_STATE).transpose(1, 2, 0)
    return out


def adamw(w, g, m, v, name):
    R, C = w.shape

    def fn(w, g, m, v):
        m = ADAM_B1 * m + (1.0 - ADAM_B1) * g
        v = ADAM_B2 * v + (1.0 - ADAM_B2) * jnp.square(g)
        m_hat = m / (1.0 - ADAM_B1 ** ADAM_STEP)
        v_hat = v / (1.0 - ADAM_B2 ** ADAM_STEP)
        delta = -ADAM_LR * (m_hat / (jnp.sqrt(v_hat) + ADAM_EPS) + ADAM_WD * w)
        return delta, m, v

    return rowwise(fn, [w, g, m, v], [], [(C, f32)] * 3, [], _pick(R, (256, 128, 64, 32, 16, 8)), name)


def kernel(x, positions, ln_g, ln_b, ffn1_w_gate, ffn1_w_up, ffn1_w_down, w_in, gmlp_norm_g, gmlp_ws, gmlp_bs, mla_q_norm_g, mla_w_uq, mla_kv_norm_g, mla_w_ukv, ssm_a_re, ssm_a_im, ssm_b_re, ssm_b_im, ssm_c_re, ssm_c_im, ssm_d, ssm_log_dt, ssm_glu_w, ssm_glu_b, mix_norm_g, w_out, ffn2_w_gate, ffn2_w_up, ffn2_w_down, loss_target, m_ln_g, m_ln_b, m_ffn1_w_gate, m_ffn1_w_up, m_ffn1_w_down, m_w_in, m_gmlp_norm_g, m_gmlp_ws, m_gmlp_bs, m_mla_q_norm_g, m_mla_w_uq, m_mla_kv_norm_g, m_mla_w_ukv, m_ssm_a_re, m_ssm_a_im, m_ssm_b_re, m_ssm_b_im, m_ssm_c_re, m_ssm_c_im, m_ssm_d, m_ssm_log_dt, m_ssm_glu_w, m_ssm_glu_b, m_mix_norm_g, m_w_out, m_ffn2_w_gate, m_ffn2_w_up, m_ffn2_w_down, v_ln_g, v_ln_b, v_ffn1_w_gate, v_ffn1_w_up, v_ffn1_w_down, v_w_in, v_gmlp_norm_g, v_gmlp_ws, v_gmlp_bs, v_mla_q_norm_g, v_mla_w_uq, v_mla_kv_norm_g, v_mla_w_ukv, v_ssm_a_re, v_ssm_a_im, v_ssm_b_re, v_ssm_b_im, v_ssm_c_re, v_ssm_c_im, v_ssm_d, v_ssm_log_dt, v_ssm_glu_w, v_ssm_glu_b, v_mix_norm_g, v_w_out, v_ffn2_w_gate, v_ffn2_w_up, v_ffn2_w_down):
    Wp = dict(zip(W_NAMES, (ln_g, ln_b, ffn1_w_gate, ffn1_w_up, ffn1_w_down, w_in, gmlp_norm_g, gmlp_ws, gmlp_bs, mla_q_norm_g, mla_w_uq, mla_kv_norm_g, mla_w_ukv, ssm_a_re, ssm_a_im, ssm_b_re, ssm_b_im, ssm_c_re, ssm_c_im, ssm_d, ssm_log_dt, ssm_glu_w, ssm_glu_b, mix_norm_g, w_out, ffn2_w_gate, ffn2_w_up, ffn2_w_down)))
    Mp = dict(zip(W_NAMES, (m_ln_g, m_ln_b, m_ffn1_w_gate, m_ffn1_w_up, m_ffn1_w_down, m_w_in, m_gmlp_norm_g, m_gmlp_ws, m_gmlp_bs, m_mla_q_norm_g, m_mla_w_uq, m_mla_kv_norm_g, m_mla_w_ukv, m_ssm_a_re, m_ssm_a_im, m_ssm_b_re, m_ssm_b_im, m_ssm_c_re, m_ssm_c_im, m_ssm_d, m_ssm_log_dt, m_ssm_glu_w, m_ssm_glu_b, m_mix_norm_g, m_w_out, m_ffn2_w_gate, m_ffn2_w_up, m_ffn2_w_down)))
    Vp = dict(zip(W_NAMES, (v_ln_g, v_ln_b, v_ffn1_w_gate, v_ffn1_w_up, v_ffn1_w_down, v_w_in, v_gmlp_norm_g, v_gmlp_ws, v_gmlp_bs, v_mla_q_norm_g, v_mla_w_uq, v_mla_kv_norm_g, v_mla_w_ukv, v_ssm_a_re, v_ssm_a_im, v_ssm_b_re, v_ssm_b_im, v_ssm_c_re, v_ssm_c_im, v_ssm_d, v_ssm_log_dt, v_ssm_glu_w, v_ssm_glu_b, v_mix_norm_g, v_w_out, v_ffn2_w_gate, v_ffn2_w_up, v_ffn2_w_down)))
    S = x.shape[1]
    my = _my_id()

    def shard_rows(l):
        return [_pad_tile_rows(_shard_to_rows(n, Wp[n][l].astype(bf16)), 0) for n in BIG]

    def joined(got):
        return {n: _join_from_devices(n, g[:, :_pack_rows(n)]) for n, g in zip(BIG, got)}

    ln_flat = jnp.concatenate([Wp[n].reshape(-1) for n in LN_NAMES])
    *got, ln_all = all_gather(shard_rows(0) + [_pad_rows(ln_flat, 8)], "gather_weights")
    ln_all = ln_all.reshape(N_DEV, -1)
    lnsz = DEPTH * 3 * (D_MODEL // N_DEV)
    ln_full = {}
    for t, n in enumerate(LN_NAMES):
        sh = ln_all[:, t * lnsz:(t + 1) * lnsz].reshape(N_DEV, DEPTH, 3, D_MODEL // N_DEV)
        ln_full[n] = sh.transpose(1, 2, 0, 3).reshape(DEPTH, 3, 1, D_MODEL)
    rep = {n: Wp[n] for n in REPL}

    inv_freq = 1.0 / (ROPE_BASE ** (jnp.arange(0, MLA_ROPE, 2, dtype=f32) / MLA_ROPE))
    ang = positions.astype(f32).reshape(S, 1) * inv_freq[None, :]
    cos8 = jnp.tile(jnp.cos(ang), (1, MLA_HEADS))
    sin8 = jnp.tile(jnp.sin(ang), (1, MLA_HEADS))

    xs = x.reshape(S, D_MODEL)
    xb = xs.astype(bf16)
    ws, saved = [], []
    for l in range(DEPTH):
        w = prep_layer(joined(got), rep, l)
        lg, lb = ln_full['ln_g'][l], ln_full['ln_b'][l]
        xs, xb, r1 = ffn_fwd(xs, xb, w['ffn1'], lg[0], lb[0])
        xs, xb, r2, got = mixer_fwd(xs, xb, w, lg[1], lb[1], cos8, sin8, (True, shard_rows(l + 1)) if l + 1 < DEPTH else None)
        xs, xb, r3 = ffn_fwd(xs, xb, w['ffn2'], lg[2], lb[2])
        ws.append(w)
        saved.append((r1, r2, r3))

    def loss_fn(y, t):
        d = y - t
        part = jnp.sum(jnp.mean(jnp.square(d), axis=-1, keepdims=True), axis=0, keepdims=True)
        return d * (1.0 / D_MODEL), 0.5 * part

    dx, loss_part = rowwise(loss_fn, [xs, loss_target.reshape(S, D_MODEL)], [], [(D_MODEL, f32)], [(1, 1)],
                            _row_tile(S, 512), "loss")
    loss = lax.psum(loss_part[0, 0], ("x", "y", "c"))

    def grad_pack(n, g):
        return _pad_tile_rows(_split_for_devices(n, g.astype(bf16)), 1)

    ffn2_names = [n for n in BIG if n.startswith('ffn2')]
    rest_names = [n for n in BIG if n not in ffn2_names]
    grads, arrived = [None] * DEPTH, [{} for _ in range(DEPTH)]
    for l in reversed(range(DEPTH)):
        w = ws[l]
        lg, lb = ln_full['ln_g'][l], ln_full['ln_b'][l]
        r1, r2, r3 = saved[l]
        dx, g2g, g2u, g2d, dg2, db2 = ffn_bwd(dx, r3, w['ffn2'], lg[2], lb[2])
        riders = [(l, n, g) for n, g in zip(ffn2_names, (g2g, g2u, g2d))]
        if l + 1 < DEPTH:
            riders += [(l + 1, n, grads[l + 1][n]) for n in rest_names]
        dx, gm, got = mixer_bwd(dx, r2, w, lg[1], lb[1], cos8, sin8, (False, [grad_pack(n, g) for _, n, g in riders]))
        for (layer, n, _), arr in zip(riders, got):
            arrived[layer][n] = arr
        dx, g1g, g1u, g1d, dg0, db0 = ffn_bwd(dx, r1, w['ffn1'], lg[0], lb[0])
        g = unprep_grads(gm, w)
        g.update({'ffn1_w_gate': g1g, 'ffn1_w_up': g1u, 'ffn1_w_down': g1d,
                  'ffn2_w_gate': g2g, 'ffn2_w_up': g2u, 'ffn2_w_down': g2d})
        g['ln_g'] = jnp.concatenate([dg0, g['ln_g'], dg2], axis=0)
        g['ln_b'] = jnp.concatenate([db0, g['ln_b'], db2], axis=0)
        grads[l] = g
    grad_x = dx.reshape(1, S, D_MODEL)

    arrived[0].update(zip(rest_names, all_to_all([grad_pack(n, grads[0][n]) for n in rest_names], "scatter_grads")))
    G = {n: jnp.stack([_rows_to_shard(n, sum_slots(arrived[l][n], "sum_" + n)[:_pack_rows(n)]) for l in range(DEPTH)])
         for n in BIG}
    small_names = LN_NAMES + REPL
    spack = jnp.concatenate([jnp.stack([grads[l][n] for l in range(DEPTH)]).reshape(-1) for n in small_names])
    n_small = spack.shape[0]
    gsmall = sum_slots(all_gather([_pad_rows(spack, 8)], "gather_small")[0], "sum_small").reshape(-1)

    off = 0
    for n in small_names:
        shp = (DEPTH, 3, D_MODEL) if n in LN_NAMES else Wp[n].shape
        sz = math.prod(shp)
        G[n] = gsmall[off:off + sz].reshape(shp)
        off += sz
    for n in LN_NAMES:
        G[n] = lax.dynamic_slice_in_dim(G[n], my * (D_MODEL // N_DEV), D_MODEL // N_DEV, axis=2)

    delta, new_m, new_v = {}, {}, {}
    for n in BIG:
        shp = Wp[n].shape
        two = (shp[0] * shp[1], shp[2])
        d_, m_, v_ = adamw(Wp[n].reshape(two), G[n].reshape(two), Mp[n].reshape(two), Vp[n].reshape(two), "adamw_" + n)
        delta[n], new_m[n], new_v[n] = d_.reshape(shp), m_.reshape(shp), v_.reshape(shp)

    def pack_small(src):
        return _pad_rows(jnp.concatenate([src[n].reshape(-1) for n in small_names]), 8)

    d_, m_, v_ = adamw(pack_small(Wp), pack_small(G), pack_small(Mp), pack_small(Vp), "adamw_small")
    d_, m_, v_ = d_.reshape(-1), m_.reshape(-1), v_.reshape(-1)
    off = 0
    for n in small_names:
        shp = Wp[n].shape
        sz = math.prod(shp)
        delta[n], new_m[n], new_v[n] = (t[off:off + sz].reshape(shp) for t in (d_, m_, v_))
        off += sz

    return (loss, grad_x, *[G[n] for n in W_NAMES], *[delta[n] for n in W_NAMES],
            *[new_m[n] for n in W_NAMES], *[new_v[n] for n in W_NAMES])
```

```python
import functools
import math

import jax
import jax.numpy as jnp
from jax import lax
from jax.experimental import pallas as pl
from jax.experimental.pallas import tpu as pltpu

f32 = jnp.float32
bf16 = jnp.bfloat16

D_MODEL = 1024
DEPTH = 4
D_FF = 2816
GM_HEADS, GM_HEAD_DIM, GM_WIDTH, GM_CHUNK = 4, 64, 256, 128
MLA_HEADS, MLA_NOPE, MLA_ROPE, MLA_V = 8, 64, 32, 64
ROPE_HALF = MLA_ROPE // 2
Q_LORA, KV_LORA = 256, 128
ROPE_BASE = 10000.0
SSM_GROUPS, SSM_GROUP_CH, SSM_WIDTH, SSM_STATE = 16, 16, 256, 64
N_STATE = SSM_GROUPS * SSM_STATE
ALPHA = (2 * DEPTH) ** 0.25
LN_EPS = 1e-5
RMS_EPS = 1e-6
NEG_BIG = -1e30
ATT_SCALE = (MLA_NOPE + MLA_ROPE) ** -0.5
ADAM_LR, ADAM_B1, ADAM_B2, ADAM_EPS, ADAM_WD, ADAM_STEP = 0.001, 0.9, 0.999, 1e-08, 0.01, 10

N_DEV = 8
LANES = 128
VMEM_LIMIT = 48 * 1024 * 1024
MM_TILE_BUDGET = 32 * 1024 * 1024
MESH = pl.DeviceIdType.MESH

H_UG, H_VG, H_CQ, H_US, H_CKV, H_K1, H_K2, H_COLS = 0, 256, 512, 768, 1024, 1152, 1280, 1408

W_NAMES = ['ln_g', 'ln_b', 'ffn1_w_gate', 'ffn1_w_up', 'ffn1_w_down', 'w_in', 'gmlp_norm_g', 'gmlp_ws', 'gmlp_bs',
           'mla_q_norm_g', 'mla_w_uq', 'mla_kv_norm_g', 'mla_w_ukv', 'ssm_a_re', 'ssm_a_im', 'ssm_b_re', 'ssm_b_im',
           'ssm_c_re', 'ssm_c_im', 'ssm_d', 'ssm_log_dt', 'ssm_glu_w', 'ssm_glu_b', 'mix_norm_g', 'w_out',
           'ffn2_w_gate', 'ffn2_w_up', 'ffn2_w_down']
BIG = {'ffn1_w_gate': 1, 'ffn1_w_up': 1, 'ffn1_w_down': 0, 'w_in': 1, 'mla_w_uq': 1, 'mla_w_ukv': 1,
       'ssm_glu_w': 0, 'w_out': 0, 'ffn2_w_gate': 1, 'ffn2_w_up': 1, 'ffn2_w_down': 0}
BIG_SHAPE = {'ffn1_w_gate': (D_MODEL, D_FF), 'ffn1_w_up': (D_MODEL, D_FF), 'ffn1_w_down': (D_FF, D_MODEL),
             'w_in': (D_MODEL, 1184), 'mla_w_uq': (Q_LORA, 768), 'mla_w_ukv': (KV_LORA, 1024),
             'ssm_glu_w': (SSM_WIDTH, SSM_WIDTH), 'w_out': (D_MODEL, D_MODEL),
             'ffn2_w_gate': (D_MODEL, D_FF), 'ffn2_w_up': (D_MODEL, D_FF), 'ffn2_w_down': (D_FF, D_MODEL)}
LN_NAMES = ['ln_g', 'ln_b']
REPL = [n for n in W_NAMES if n not in BIG and n not in LN_NAMES]


def _pick(n, cands):
    for c in cands:
        if n % c == 0:
            return c
    return n


def _params(sem):
    return pltpu.CompilerParams(dimension_semantics=sem, vmem_limit_bytes=VMEM_LIMIT)


S3_ROWS = 2 * N_STATE // LANES


def _from_s3(ref):
    return jnp.concatenate([ref[:, c, :] for c in range(S3_ROWS)], axis=1)


def _to_s3(ref, val):
    for c in range(S3_ROWS):
        ref[:, c, :] = val[:, c * LANES:(c + 1) * LANES].astype(ref.dtype)


def mm(a, b, *, out_dtype=f32, add=None, add_scale=1.0, a_col0=0, a_s3=False, out_s3=False, grouped=False, b_nt=False, name):
    G = a.shape[0] if grouped else 1
    M = a.shape[1] if grouped else a.shape[0]
    K, N = (b.shape[-1], b.shape[-2]) if b_nt else b.shape[-2:]
    tn = N if out_s3 else _pick(N, (512, 384, 256) if N <= 1536 else (512, 384, 256, 128))
    tk = K if (K <= 2 * D_FF or a_s3) else _pick(K, (1024, 512, 256, 128))
    nk = G * (K // tk)
    assert not grouped or tk == K

    def tile_bytes(tm):
        return 2 * (tm * tk * a.dtype.itemsize + tk * tn * b.dtype.itemsize + tm * tn * 4 * (2 if add is not None else 1))

    tm = next(t for t in (1024, 512, 256, 128, 64, 32, 16, 8) if M % t == 0 and (tile_bytes(t) <= MM_TILE_BUDGET or t == 8))
    assert a_s3 or grouped or (a_col0 % tk == 0 and a_col0 + K <= a.shape[1])
    kb0 = a_col0 // tk
    has_add = add is not None

    def body(*refs):
        if has_add:
            a_ref, b_ref, add_ref, o_ref, acc = refs
        else:
            a_ref, b_ref, o_ref, acc = refs
        k = pl.program_id(2)
        a_val = _from_s3(a_ref) if a_s3 else a_ref[...]
        part = (_nt if b_nt else functools.partial(jnp.dot, preferred_element_type=f32))(a_val.astype(bf16), b_ref[...].astype(bf16))

        def finish(total):
            if has_add:
                total = total + add_scale * add_ref[...]
            if out_s3:
                _to_s3(o_ref, total)
            else:
                o_ref[...] = total.astype(o_ref.dtype)

        if nk == 1:
            finish(part)
        else:
            @pl.when(k == 0)
            def _():
                acc[...] = part

            @pl.when(k > 0)
            def _():
                acc[...] += part

            @pl.when(k == nk - 1)
            def _():
                finish(acc[...])

    if a_s3:
        a_spec = pl.BlockSpec((tm, S3_ROWS, LANES), lambda i, j, k: (i, 0, 0))
    elif grouped:
        a_spec = pl.BlockSpec((None, tm, tk), lambda i, j, k: (k, i, 0))
    else:
        a_spec = pl.BlockSpec((tm, tk), lambda i, j, k: (i, kb0 + k))
    if b_nt:
        b_spec = pl.BlockSpec((None, tn, tk), lambda i, j, k: (k, j, 0)) if grouped else pl.BlockSpec((tn, tk), lambda i, j, k: (j, k))
    else:
        b_spec = pl.BlockSpec((None, tk, tn), lambda i, j, k: (k, 0, j)) if grouped else pl.BlockSpec((tk, tn), lambda i, j, k: (k, j))
    in_specs, ops = [a_spec, b_spec], [a, b]
    if has_add:
        in_specs.append(pl.BlockSpec((tm, tn), lambda i, j, k: (i, j)))
        ops.append(add)
    if out_s3:
        out_spec = pl.BlockSpec((tm, S3_ROWS, LANES), lambda i, j, k: (i, 0, 0))
        out_shape = jax.ShapeDtypeStruct((M, S3_ROWS, LANES), out_dtype)
    else:
        out_spec = pl.BlockSpec((tm, tn), lambda i, j, k: (i, j))
        out_shape = jax.ShapeDtypeStruct((M, N), out_dtype)
    return pl.pallas_call(
        body, grid=(M // tm, N // tn, nk), in_specs=in_specs, out_specs=out_spec, out_shape=out_shape,
        scratch_shapes=[pltpu.VMEM((tm, tn) if nk > 1 else (8, LANES), f32)],
        compiler_params=_params(("parallel", "parallel", "arbitrary")), name=name)(*ops)


def mm_tn(a, b, *, a_col0=0, m_dim=None, a_s3=False, b_s3=False, a_lead=None, name):
    K = a.shape[-2] if a_lead is not None else a.shape[0]
    M = 2 * N_STATE if a_s3 else (a.shape[-1] if m_dim is None else m_dim)
    N = 2 * N_STATE if b_s3 else b.shape[-1]
    tm = M if a_s3 else _pick(M, (H_COLS, 1024, 768, 512, 384, 256, 128))
    tn = N if b_s3 else _pick(N, (H_COLS, 1024, 768, 512, 384, 256))

    def tile_bytes(tk):
        return 2 * tk * (tm * a.dtype.itemsize + tn * b.dtype.itemsize) + 3 * tm * tn * 4

    tk = next(t for t in (2048, 1024, 512, 256, 128, 64, 32, 16) if K % t == 0 and (tile_bytes(t) <= MM_TILE_BUDGET or t == 16))
    nk = K // tk
    assert a_col0 % tm == 0
    mb0 = a_col0 // tm

    def body(a_ref, b_ref, o_ref, acc):
        k = pl.program_id(2)
        a_val = _from_s3(a_ref) if a_s3 else a_ref[...]
        b_val = _from_s3(b_ref) if b_s3 else b_ref[...]
        part = lax.dot_general(a_val.astype(bf16), b_val.astype(bf16), (((0,), (0,)), ((), ())), preferred_element_type=f32)

        @pl.when(k == 0)
        def _():
            acc[...] = part

        @pl.when(k > 0)
        def _():
            acc[...] += part

        @pl.when(k == nk - 1)
        def _():
            o_ref[...] = acc[...]

    s3_spec = pl.BlockSpec((tk, S3_ROWS, LANES), lambda i, j, k: (k, 0, 0))
    if a_s3:
        a_spec = s3_spec
    elif a_lead is not None:
        a_spec = pl.BlockSpec((None, tk, tm), lambda i, j, k: (a_lead, k, i))
    else:
        a_spec = pl.BlockSpec((tk, tm), lambda i, j, k: (k, mb0 + i))
    return pl.pallas_call(
        body, grid=(M // tm, N // tn, nk),
        in_specs=[a_spec, s3_spec if b_s3 else pl.BlockSpec((tk, tn), lambda i, j, k: (k, j))],
        out_specs=pl.BlockSpec((tm, tn), lambda i, j, k: (i, j)),
        out_shape=jax.ShapeDtypeStruct((M, N), f32),
        scratch_shapes=[pltpu.VMEM((tm, tn), f32)],
        compiler_params=_params(("parallel", "parallel", "arbitrary")), name=name)(a, b)


def rowwise(fn, rows, pars, out_rows, out_accs, tm, name):
    first = rows[0]
    if isinstance(first, tuple):
        R = first[0].shape[1] if first[1] == 'lead' else first[0].shape[0]
    else:
        R = first.shape[0]
    assert R % tm == 0, (R, tm, name)
    n_rows, n_pars, n_or, n_oa = len(rows), len(pars), len(out_rows), len(out_accs)

    in_specs, ops = [], []
    for r in rows:
        if isinstance(r, tuple) and r[1] == 'lead':
            arr, _, kk = r
            in_specs.append(pl.BlockSpec((None, tm, arr.shape[2]), lambda i, kk=kk: (kk, i, 0)))
        elif isinstance(r, tuple):
            arr, c0, w = r
            assert c0 % w == 0
            in_specs.append(pl.BlockSpec((tm, w), lambda i, cb=c0 // w: (i, cb)))
        else:
            arr = r
            in_specs.append(pl.BlockSpec((tm, arr.shape[1]), lambda i: (i, 0)))
        ops.append(arr)
    for p in pars:
        in_specs.append(pl.BlockSpec(p.shape, lambda i, nd=p.ndim: (0,) * nd))
        ops.append(p)
    out_specs = [pl.BlockSpec((tm, w), lambda i: (i, 0)) for (w, _) in out_rows]
    out_specs += [pl.BlockSpec(s, lambda i, nd=len(s): (0,) * nd) for s in out_accs]
    out_shape = [jax.ShapeDtypeStruct((R, w), dt) for (w, dt) in out_rows]
    out_shape += [jax.ShapeDtypeStruct(s, f32) for s in out_accs]

    def body(*refs):
        ins = [r[...] for r in refs[:n_rows + n_pars]]
        o_refs = refs[n_rows + n_pars:]
        res = fn(*ins)
        if not isinstance(res, (tuple, list)):
            res = (res,)
        assert len(res) == n_or + n_oa, (len(res), n_or, n_oa, name)
        for o, v in zip(o_refs[:n_or], res[:n_or]):
            o[...] = v.astype(o.dtype)
        if n_oa:
            i = pl.program_id(0)

            @pl.when(i == 0)
            def _():
                for o, v in zip(o_refs[n_or:], res[n_or:]):
                    o[...] = v.astype(f32)

            @pl.when(i > 0)
            def _():
                for o, v in zip(o_refs[n_or:], res[n_or:]):
                    o[...] += v.astype(f32)

    return pl.pallas_call(
        body, grid=(R // tm,), in_specs=in_specs, out_specs=out_specs, out_shape=out_shape,
        compiler_params=_params(("arbitrary",)), name=name)(*ops)


def whole(fn, ins, out_shapes, name):
    def body(*refs):
        res = fn(*[r[...] for r in refs[:len(ins)]])
        for o, v in zip(refs[len(ins):], res):
            o[...] = v

    return pl.pallas_call(body, out_shape=[jax.ShapeDtypeStruct(s, f32) for s in out_shapes], name=name)(*ins)


@jax.custom_vjp
def _bdot(a, b):
    return jnp.dot(a.astype(bf16), b.astype(bf16), preferred_element_type=f32)


def _bdot_fwd(a, b):
    return _bdot(a, b), (a, b)


def _bdot_bwd(res, g):
    a, b = res
    gb = g.astype(bf16)
    da = lax.dot_general(gb, b.astype(bf16), (((1,), (1,)), ((), ())), preferred_element_type=f32)
    db = lax.dot_general(a.astype(bf16), gb, (((0,), (0,)), ((), ())), preferred_element_type=f32)
    return da, db


_bdot.defvjp(_bdot_fwd, _bdot_bwd)


def _ln(z, g, b):
    mu = jnp.mean(z, axis=-1, keepdims=True)
    var = jnp.mean(jnp.square(z - mu), axis=-1, keepdims=True)
    return (z - mu) * lax.rsqrt(var + LN_EPS) * g + b


def _rms_only(x):
    return x * lax.rsqrt(jnp.mean(jnp.square(x), axis=-1, keepdims=True) + RMS_EPS)


def _swiglu(a, b):
    return jax.nn.silu(a) * b


def _gmlp(hu, hv, ng, ws, bsb):
    u = jax.nn.gelu(hu)
    v = jax.nn.gelu(hv)
    lane = lax.broadcasted_iota(jnp.int32, (1, GM_WIDTH), 1)
    masks = [((lane >= GM_HEAD_DIM * h) & (lane < GM_HEAD_DIM * (h + 1))).astype(f32) for h in range(GM_HEADS)]
    mu = jnp.zeros_like(v)
    for m in masks:
        mu = mu + m * (jnp.sum(v * m, axis=-1, keepdims=True) / GM_HEAD_DIM)
    d = v - mu
    var = jnp.zeros_like(v)
    for m in masks:
        var = var + m * (jnp.sum(d * d * m, axis=-1, keepdims=True) / GM_HEAD_DIM)
    vn = d * lax.rsqrt(var + LN_EPS) * ng
    r = lax.broadcasted_iota(jnp.int32, (GM_CHUNK, GM_CHUNK), 0)
    c = lax.broadcasted_iota(jnp.int32, (GM_CHUNK, GM_CHUNK), 1)
    tril = (c <= r).astype(f32)
    z = bsb
    for h, m in enumerate(masks):
        z = z + _bdot(ws[h] * tril, vn * m)
    return u * z


def _mla_prep(cq, ckv, qg, kvg):
    return _rms_only(cq) * qg, _rms_only(ckv) * kvg


def _rope(q1, q2, k1, k2, cos, sin):
    return q1 * cos - q2 * sin, q2 * cos + q1 * sin, k1 * cos - k2 * sin, k2 * cos + k1 * sin


def _mix_post(ya, ob, y1, us, dskip, gluw, glub, gmix):
    y = jax.nn.gelu(y1 + dskip * us)
    yc = y * jax.nn.sigmoid(_bdot(y, gluw) + glub)
    return jnp.concatenate([_rms_only(ya), _rms_only(ob), _rms_only(yc)], axis=1) * gmix


def _ssm_prep(ar, ai, ldt, brT, biT):
    dt = jnp.exp(ldt)
    mag = jnp.exp(ar * dt)
    abr = mag * jnp.cos(ai * dt)
    abi = mag * jnp.sin(ai * dt)
    den = ar * ar + ai * ai
    cr = ((abr - 1.0) * ar + abi * ai) / den
    ci = (abi * ar - (abr - 1.0) * ai) / den
    return abr, abi, cr * brT - ci * biT, cr * biT + ci * brT


ATT_FWD_HEADS = 2


def _att_tile(S):
    return _pick(S, (512, 256, 128))


def _nt(a, b):
    return lax.dot_general(a, b, (((1,), (1,)), ((), ())), preferred_element_type=f32)


def _diag_keep(T):
    krow = lax.broadcasted_iota(jnp.int32, (T, T), 0)
    qcol = lax.broadcasted_iota(jnp.int32, (T, T), 1)
    return qcol >= krow


def grid_call(body, exch, *, grid, in_specs, out_specs, out_shape, scratch_shapes, name, args):
    params = _params(("arbitrary", "arbitrary"))
    if exch is None:
        return pl.pallas_call(body, grid=grid, in_specs=in_specs, out_specs=out_specs, out_shape=out_shape,
                              scratch_shapes=scratch_shapes, compiler_params=params, name=name)(*args)
    gather, xs = exch
    n, n_in, n_out, n_sc = len(xs), len(in_specs), len(out_specs), len(scratch_shapes)

    def riding(*refs):
        ins, x_refs = refs[:n_in], refs[n_in:n_in + n]
        outs, xo_refs = refs[n_in + n:n_in + n + n_out], refs[n_in + n + n_out:n_in + 2 * n + n_out]
        rest = refs[n_in + 2 * n + n_out:]
        scratch, sems = rest[:n_sc], rest[n_sc:]
        h, i = pl.program_id(0), pl.program_id(1)

        @pl.when((h == 0) & (i == 0))
        def _():
            _direct_exchange(gather, x_refs, xo_refs, *sems, start=True)

        body(*ins, *outs, *scratch)

        @pl.when((h == grid[0] - 1) & (i == grid[1] - 1))
        def _():
            _direct_exchange(gather, x_refs, xo_refs, *sems, start=False)

    return pl.pallas_call(
        riding, grid=grid, in_specs=list(in_specs) + [HBM_SPEC] * n, out_specs=list(out_specs) + [HBM_SPEC] * n,
        out_shape=list(out_shape) + _exchange_out_shapes(gather, xs),
        scratch_shapes=list(scratch_shapes) + _exchange_scratch(n), compiler_params=params,
        name=name + ("_gather" if gather else "_scatter"))(*args, *xs)


def flash_fwd(q, k, vT, exch=None):
    Hh, S, _ = q.shape
    T = _att_tile(S)
    HB = ATT_FWD_HEADS

    def body(q_ref, k_ref, vT_ref, o_ref, lse_ref, m_sc, l_sc, acc_sc):
        i = pl.program_id(1)
        m_sc[...] = jnp.full_like(m_sc, NEG_BIG)
        l_sc[...] = jnp.zeros_like(l_sc)
        acc_sc[...] = jnp.zeros_like(acc_sc)

        def block(j, diagonal):
            rows = pl.ds(pl.multiple_of(j * T, T), T)
            for hh in range(HB):
                sT = _nt(k_ref[hh, rows, :], q_ref[hh]) * ATT_SCALE
                if diagonal:
                    sT = jnp.where(_diag_keep(T), sT, NEG_BIG)
                m_old = m_sc[hh]
                m_new = jnp.maximum(m_old, jnp.max(sT, axis=0, keepdims=True))
                alpha = jnp.exp(m_old - m_new)
                pT = jnp.exp(sT - m_new)
                l_sc[hh] = alpha * l_sc[hh] + jnp.sum(pT, axis=0, keepdims=True)
                acc_sc[hh] = alpha * acc_sc[hh] + jnp.dot(vT_ref[hh, :, rows], pT.astype(bf16), preferred_element_type=f32)
                m_sc[hh] = m_new

        def loop_body(j, c):
            block(j, False)
            return c

        lax.fori_loop(0, i, loop_body, 0)
        block(i, True)
        o_ref[...] = acc_sc[...] / l_sc[...]
        lse_ref[...] = m_sc[...] + jnp.log(l_sc[...])

    return grid_call(
        body, exch, grid=(Hh // HB, S // T),
        in_specs=[pl.BlockSpec((HB, T, LANES), lambda h, i: (h, i, 0)), pl.BlockSpec((HB, S, LANES), lambda h, i: (h, 0, 0)),
                  pl.BlockSpec((HB, MLA_V, S), lambda h, i: (h, 0, 0))],
        out_specs=[pl.BlockSpec((HB, MLA_V, T), lambda h, i: (h, 0, i)), pl.BlockSpec((HB, 1, T), lambda h, i: (h, 0, i))],
        out_shape=[jax.ShapeDtypeStruct((Hh, MLA_V, S), f32), jax.ShapeDtypeStruct((Hh, 1, S), f32)],
        scratch_shapes=[pltpu.VMEM((HB, 1, T), f32), pltpu.VMEM((HB, 1, T), f32), pltpu.VMEM((HB, MLA_V, T), f32)],
        name="flash_fwd", args=(q, k, vT))


def flash_bwd(q, k, kT, v, do, lse, delta, exch=None):
    Hh, S, _ = q.shape
    T = _att_tile(S)

    def body(q_ref, do_ref, lse_ref, dl_ref, k_ref, kT_ref, v_ref, dq_ref, dk_ref, dv_ref, dq_sc):
        i = pl.program_id(1)

        @pl.when(i == 0)
        def _():
            dk_ref[...] = jnp.zeros_like(dk_ref)
            dv_ref[...] = jnp.zeros_like(dv_ref)

        qi, doi = q_ref[0], do_ref[0]
        lse_i, dl_i = lse_ref[0], dl_ref[0]
        dq_sc[...] = jnp.zeros_like(dq_sc)

        def block(j, diagonal):
            rows = pl.ds(pl.multiple_of(j * T, T), T)
            sT = _nt(k_ref[0, rows, :], qi) * ATT_SCALE
            pT = jnp.exp(sT - lse_i)
            if diagonal:
                pT = jnp.where(_diag_keep(T), pT, 0.0)
            dpT = _nt(v_ref[0, rows, :], doi)
            dsT = (pT * (dpT - dl_i) * ATT_SCALE).astype(bf16)
            dv_ref[0, rows, :] += jnp.dot(pT.astype(bf16), doi, preferred_element_type=f32)
            dk_ref[0, rows, :] += jnp.dot(dsT, qi, preferred_element_type=f32)
            dq_sc[...] += jnp.dot(kT_ref[0, :, rows], dsT, preferred_element_type=f32)

        def loop_body(j, c):
            block(j, False)
            return c

        lax.fori_loop(0, i, loop_body, 0)
        block(i, True)
        dq_ref[0] = dq_sc[...]

    tile = lambda w: pl.BlockSpec((1, T, w), lambda h, i: (h, i, 0))
    row = pl.BlockSpec((1, 1, T), lambda h, i: (h, 0, i))
    full = lambda w: pl.BlockSpec((1, S, w), lambda h, i: (h, 0, 0))
    return grid_call(
        body, exch, grid=(Hh, S // T),
        in_specs=[tile(LANES), tile(MLA_V), row, row, full(LANES), pl.BlockSpec((1, LANES, S), lambda h, i: (h, 0, 0)), full(MLA_V)],
        out_specs=[pl.BlockSpec((1, LANES, T), lambda h, i: (h, 0, i)), full(LANES), full(MLA_V)],
        out_shape=[jax.ShapeDtypeStruct((Hh, LANES, S), f32), jax.ShapeDtypeStruct((Hh, S, LANES), f32),
                   jax.ShapeDtypeStruct((Hh, S, MLA_V), f32)],
        scratch_shapes=[pltpu.VMEM((LANES, T), f32)],
        name="flash_bwd", args=(q, do, lse, delta, k, kT, v))


def _scan_tile(S):
    return _pick(S, (256, 128, 64, 32, 16, 8))


def scan_fwd(bu3, a16):
    S = bu3.shape[0]
    ts = _scan_tile(S)

    def body(bu_ref, a_ref, o_ref, h_sc):
        @pl.when(pl.program_id(0) == 0)
        def _():
            h_sc[...] = jnp.zeros_like(h_sc)

        ar, ai = a_ref[0:8, :], a_ref[8:16, :]
        a2r, a2i = ar * ar - ai * ai, 2.0 * ar * ai

        def pair(p, carry):
            hr, hi = carry
            t = 2 * p
            x0r, x0i = bu_ref[t, 0:8, :], bu_ref[t, 8:16, :]
            ur = ar * x0r - ai * x0i + bu_ref[t + 1, 0:8, :]
            ui = ar * x0i + ai * x0r + bu_ref[t + 1, 8:16, :]
            o_ref[t, 0:8, :] = ar * hr - ai * hi + x0r
            o_ref[t, 8:16, :] = ar * hi + ai * hr + x0i
            nr = a2r * hr - a2i * hi + ur
            ni = a2r * hi + a2i * hr + ui
            o_ref[t + 1, 0:8, :] = nr
            o_ref[t + 1, 8:16, :] = ni
            return nr, ni

        hr, hi = lax.fori_loop(0, ts // 2, pair, (h_sc[0:8, :], h_sc[8:16, :]), unroll=4)
        h_sc[0:8, :] = hr
        h_sc[8:16, :] = hi

    blk = pl.BlockSpec((ts, 16, LANES), lambda i: (i, 0, 0))
    return pl.pallas_call(
        body, grid=(S // ts,), in_specs=[blk, pl.BlockSpec((16, LANES), lambda i: (0, 0))], out_specs=blk,
        out_shape=jax.ShapeDtypeStruct(bu3.shape, f32), scratch_shapes=[pltpu.VMEM((16, LANES), f32)],
        compiler_params=_params(("arbitrary",)), name="scan_fwd")(bu3, a16)


def scan_bwd(g3, h3, a16):
    S = g3.shape[0]
    ts = _scan_tile(S)
    nb = S // ts

    def body(g_ref, h_ref, a_ref, o_ref, da_ref, lam_sc, da_sc):
        @pl.when(pl.program_id(0) == 0)
        def _():
            lam_sc[...] = jnp.zeros_like(lam_sc)
            da_sc[...] = jnp.zeros_like(da_sc)

        ar, ai = a_ref[0:8, :], a_ref[8:16, :]
        a2r, a2i = ar * ar - ai * ai, 2.0 * ar * ai

        def pair(p, carry):
            lr, li, dar, dai = carry
            t = ts - 1 - 2 * p
            h1r, h1i = h_ref[t, 0:8, :], h_ref[t, 8:16, :]
            h0r, h0i = h_ref[t - 1, 0:8, :], h_ref[t - 1, 8:16, :]
            g1r, g1i = g_ref[t, 0:8, :], g_ref[t, 8:16, :]
            l1r = ar * lr + ai * li + g1r
            l1i = ar * li - ai * lr + g1i
            vr = ar * g1r + ai * g1i + g_ref[t - 1, 0:8, :]
            vi = ar * g1i - ai * g1r + g_ref[t - 1, 8:16, :]
            l0r = a2r * lr + a2i * li + vr
            l0i = a2r * li - a2i * lr + vi
            dar = dar + (lr * h1r + li * h1i) + (l1r * h0r + l1i * h0i)
            dai = dai + (li * h1r - lr * h1i) + (l1i * h0r - l1r * h0i)
            o_ref[t, 0:8, :] = l1r
            o_ref[t, 8:16, :] = l1i
            o_ref[t - 1, 0:8, :] = l0r
            o_ref[t - 1, 8:16, :] = l0i
            return l0r, l0i, dar, dai

        lr, li, dar, dai = lax.fori_loop(
            0, ts // 2, pair, (lam_sc[0:8, :], lam_sc[8:16, :], da_sc[0:8, :], da_sc[8:16, :]), unroll=4)
        lam_sc[0:8, :] = lr
        lam_sc[8:16, :] = li
        da_sc[0:8, :] = dar
        da_sc[8:16, :] = dai
        da_ref[0:8, :] = dar
        da_ref[8:16, :] = dai

    blk = pl.BlockSpec((ts, 16, LANES), lambda i: (nb - 1 - i, 0, 0))
    small = pl.BlockSpec((16, LANES), lambda i: (0, 0))
    return pl.pallas_call(
        body, grid=(nb,), in_specs=[blk, blk, small], out_specs=[blk, small],
        out_shape=[jax.ShapeDtypeStruct(g3.shape, f32), jax.ShapeDtypeStruct((16, LANES), f32)],
        scratch_shapes=[pltpu.VMEM((16, LANES), f32), pltpu.VMEM((16, LANES), f32)],
        compiler_params=_params(("arbitrary",)), name="scan_bwd")(g3, h3, a16)


HBM_SPEC = pl.BlockSpec(memory_space=pltpu.HBM)


def _my_id():
    return 4 * lax.axis_index("x") + 2 * lax.axis_index("y") + lax.axis_index("c")


def all_gather(xs, name):
    n = len(xs)

    def body(*refs):
        x_refs, o_refs = refs[:n], refs[n:2 * n]
        send_sems, recv_sems, local_sems = refs[2 * n:]
        x, y, c = lax.axis_index("x"), lax.axis_index("y"), lax.axis_index("c")
        me, sibling = (x, y, c), (x, y, 1 - c)
        chips = [(1 - x, y), (x, 1 - y), (1 - x, 1 - y)]

        def slot(o, p):
            return o.at[4 * p[0] + 2 * p[1] + p[2]]

        def copy(a, k, block, to, src=None):
            o = o_refs[a]
            return pltpu.make_async_remote_copy(
                src_ref=slot(o, block) if src is None else src, dst_ref=slot(o, block),
                send_sem=send_sems.at[7 * a + k], recv_sem=recv_sems.at[7 * a + k], device_id=to, device_id_type=MESH)

        own, sends = [], []
        for a in range(n):
            mine = pltpu.make_async_copy(x_refs[a], slot(o_refs[a], me), local_sems.at[a])
            mine.start()
            own.append(mine)
            first = [copy(a, 0, me, sibling, src=x_refs[a])]
            first += [copy(a, 1 + j, me, (*chip, c), src=x_refs[a]) for j, chip in enumerate(chips)]
            for cp in first:
                cp.start()
            sends += first
        for a in range(n):
            for j, chip in enumerate(chips):
                copy(a, 1 + j, (*chip, c), me).wait_recv()
                fwd = copy(a, 4 + j, (*chip, c), sibling)
                fwd.start()
                sends.append(fwd)
        for a in range(n):
            copy(a, 0, sibling, me).wait_recv()
            for j, chip in enumerate(chips):
                copy(a, 4 + j, (*chip, 1 - c), me).wait_recv()
        for cp in sends:
            cp.wait_send()
        for cp in own:
            cp.wait()

    return pl.pallas_call(
        body, out_shape=[jax.ShapeDtypeStruct((N_DEV,) + v.shape, v.dtype) for v in xs],
        in_specs=[HBM_SPEC] * n, out_specs=[HBM_SPEC] * n,
        scratch_shapes=[pltpu.SemaphoreType.DMA((7 * n,)), pltpu.SemaphoreType.DMA((7 * n,)), pltpu.SemaphoreType.DMA((n,))],
        name=name)(*xs)


def _direct_exchange(gather, x_refs, o_refs, send_sems, recv_sems, local_sems, start):
    x, y, c = lax.axis_index("x"), lax.axis_index("y"), lax.axis_index("c")
    my = 4 * x + 2 * y + c
    for a, (x_ref, o_ref) in enumerate(zip(x_refs, o_refs)):
        mine = pltpu.make_async_copy(x_ref if gather else x_ref.at[my], o_ref.at[my], local_sems.at[a])
        if start:
            mine.start()
        else:
            mine.wait()
        for k in range(1, N_DEV):
            px = 1 - x if k & 4 else x
            py = 1 - y if k & 2 else y
            pc = 1 - c if k & 1 else c
            pid = 4 * px + 2 * py + pc
            src = x_ref if gather else x_ref.at[pid]
            sems = dict(send_sem=send_sems.at[7 * a + k - 1], recv_sem=recv_sems.at[7 * a + k - 1],
                        device_id=(px, py, pc), device_id_type=MESH)
            if start:
                pltpu.make_async_remote_copy(src_ref=src, dst_ref=o_ref.at[my], **sems).start()
            else:
                pltpu.make_async_remote_copy(src_ref=src, dst_ref=o_ref.at[my], **sems).wait_send()
                pltpu.make_async_remote_copy(src_ref=src, dst_ref=o_ref.at[pid], **sems).wait_recv()


def _exchange_scratch(n):
    return [pltpu.SemaphoreType.DMA((7 * n,)), pltpu.SemaphoreType.DMA((7 * n,)), pltpu.SemaphoreType.DMA((n,))]


def _exchange_out_shapes(gather, xs):
    return [jax.ShapeDtypeStruct(((N_DEV,) + v.shape) if gather else v.shape, v.dtype) for v in xs]


def all_to_all(xs, name):
    n = len(xs)

    def body(*refs):
        ex = (False, refs[:n], refs[n:2 * n], *refs[2 * n:])
        _direct_exchange(*ex, start=True)
        _direct_exchange(*ex, start=False)

    return pl.pallas_call(
        body, out_shape=_exchange_out_shapes(False, xs), in_specs=[HBM_SPEC] * n, out_specs=[HBM_SPEC] * n,
        scratch_shapes=_exchange_scratch(n), name=name)(*xs)


def sum_slots(g8, name):
    R, C = g8.shape[1:]

    def fn(*tiles):
        tot = tiles[0].astype(f32)
        for t in tiles[1:]:
            tot = tot + t.astype(f32)
        return tot

    return rowwise(fn, [(g8, 'lead', k) for k in range(N_DEV)], [], [(C, f32)], [], _pick(R, (256, 128, 64, 32, 16, 8)), name)[0]


PACK_W = 1024


def _pad_rows(flat, mult):
    n = flat.shape[0]
    tot = -(-n // (PACK_W * mult)) * PACK_W * mult
    return jnp.pad(flat, (0, tot - n)).reshape(tot // PACK_W, PACK_W)


BF16_TILE_ROWS = 16


def _pad_tile_rows(a, axis):
    pad = [(0, 0)] * a.ndim
    pad[axis] = (0, -a.shape[axis] % BF16_TILE_ROWS)
    return jnp.pad(a, pad)


def _shard_shape(name):
    r, c = BIG_SHAPE[name]
    return (r // N_DEV, c) if BIG[name] == 0 else (r, c // N_DEV)


def _pack_rows(name):
    r, c = _shard_shape(name)
    assert (r * c) % PACK_W == 0
    return r * c // PACK_W


TRANSPOSED = ('ffn1_w_gate', 'ffn1_w_up', 'ffn2_w_gate', 'ffn2_w_up')
assert all(BIG_SHAPE[n][0] == PACK_W and BIG[n] == 1 for n in TRANSPOSED)


def _shard_to_rows(name, shard):
    return (shard.T if name in TRANSPOSED else shard).reshape(_pack_rows(name), PACK_W)


def _rows_to_shard(name, rows):
    r, c = _shard_shape(name)
    return rows.reshape(c, r).T if name in TRANSPOSED else rows.reshape(r, c)


def _split_for_devices(name, full):
    r, c = BIG_SHAPE[name]
    if BIG[name] == 0 or name in TRANSPOSED:
        return full.reshape(N_DEV, _pack_rows(name), PACK_W)
    return full.reshape(r, N_DEV, c // N_DEV).transpose(1, 0, 2).reshape(N_DEV, _pack_rows(name), PACK_W)


def _join_from_devices(name, parts):
    r, c = BIG_SHAPE[name]
    if name in TRANSPOSED:
        return parts.reshape(c, r)
    if BIG[name] == 0:
        return parts.reshape(r, c)
    return parts.reshape(N_DEV, r, c // N_DEV).transpose(1, 0, 2).reshape(r, c)


def _row_tile(S, want):
    return _pick(S, tuple(t for t in (512, 256, 128, 64, 32, 16) if t <= want))


def resid_ln(x, f, g, b, scale, name):
    D = x.shape[1]

    def fn(x, f, g, b):
        z = ALPHA * x + scale * f
        xo = _ln(z, g, b)
        return z, xo, xo

    return rowwise(fn, [x, f], [g, b], [(D, f32), (D, f32), (D, bf16)], [], _row_tile(x.shape[0], 512), name)


def ln_bwd(z, g, b, dxo, scale, name):
    D = z.shape[1]

    def fn(z, dxo, g, b):
        _, vjp = jax.vjp(_ln, z, g, b)
        dz, dg, db = vjp(dxo)
        return dz, scale * dz, dg, db

    return rowwise(fn, [z, dxo], [g, b], [(D, f32), (D, bf16)], [(1, D), (1, D)], _row_tile(z.shape[0], 512), name)


FF_TILE = 256


def ffn_up_act(xb, guT):
    M, K = xb.shape
    tm = _pick(M, (1024, 512, 256, 128, 64, 32, 16))

    def body(x_ref, wg_ref, wu_ref, ab_ref, h_ref):
        x = x_ref[...]
        a = _nt(x, wg_ref[...])
        b = _nt(x, wu_ref[...])
        ab_ref[0] = a.astype(bf16)
        ab_ref[1] = b.astype(bf16)
        h_ref[...] = _swiglu(a, b).astype(bf16)

    wspecs = [pl.BlockSpec((None, FF_TILE, K), lambda i, j, which=which: (which, j, 0)) for which in (0, 1)]
    return pl.pallas_call(
        body, grid=(M // tm, D_FF // FF_TILE),
        in_specs=[pl.BlockSpec((tm, K), lambda i, j: (i, 0)), *wspecs],
        out_specs=[pl.BlockSpec((2, tm, FF_TILE), lambda i, j: (0, i, j)), pl.BlockSpec((tm, FF_TILE), lambda i, j: (i, j))],
        out_shape=[jax.ShapeDtypeStruct((2, M, D_FF), bf16), jax.ShapeDtypeStruct((M, D_FF), bf16)],
        compiler_params=_params(("parallel", "parallel")), name="ffn_up_act")(xb, guT, guT)


def ffn_down_dx_act(dzs, wd, ab):
    M, K = dzs.shape
    tm = _pick(M, (1024, 512, 256, 128, 64, 32, 16))

    def body(dz_ref, w_ref, ab_ref, dab_ref, h_ref):
        dh = _nt(dz_ref[...], w_ref[...])
        h, vjp = jax.vjp(_swiglu, ab_ref[0].astype(f32), ab_ref[1].astype(f32))
        da, db = vjp(dh)
        dab_ref[0] = da.astype(bf16)
        dab_ref[1] = db.astype(bf16)
        h_ref[...] = h.astype(bf16)

    pair = pl.BlockSpec((2, tm, FF_TILE), lambda i, j: (0, i, j))
    return pl.pallas_call(
        body, grid=(M // tm, D_FF // FF_TILE),
        in_specs=[pl.BlockSpec((tm, K), lambda i, j: (i, 0)), pl.BlockSpec((FF_TILE, K), lambda i, j: (j, 0)), pair],
        out_specs=[pair, pl.BlockSpec((tm, FF_TILE), lambda i, j: (i, j))],
        out_shape=[jax.ShapeDtypeStruct((2, M, D_FF), bf16), jax.ShapeDtypeStruct((M, D_FF), bf16)],
        compiler_params=_params(("parallel", "parallel")), name="ffn_down_dx_act")(dzs, wd, ab)


def ffn_fwd(x, xb, w, g, b):
    ab, h = ffn_up_act(xb, w['guT'])
    f = mm(h, w['d'], name="ffn_down")
    z, xo, xob = resid_ln(x, f, g, b, 0.5, "ffn_ln")
    return xo, xob, (xb, ab, z)


def ffn_bwd(dxo, res, w, g, b):
    xb, ab, z = res
    dz, dzs, dg, db = ln_bwd(z, g, b, dxo, 0.5, "ffn_ln_bwd")
    dab, h = ffn_down_dx_act(dzs, w['d'], ab)
    dwd = mm_tn(h, dzs, name="ffn_down_dw")
    dwgT = mm_tn(dab, xb, a_lead=0, name="ffn_gate_dw")
    dwuT = mm_tn(dab, xb, a_lead=1, name="ffn_up_dw")
    dx = mm(dab, w['guT'], add=dz, add_scale=ALPHA, grouped=True, name="ffn_up_dx")
    return dx, dwgT, dwuT, dwd, dg, db


def _heads_first(a, width):
    return a.reshape(a.shape[0], MLA_HEADS, width)


def mixer_fwd(x, xb, w, g, b, cos8, sin8, exch=None):
    S = x.shape[0]
    H = mm(xb, w['in'], name="mix_in")
    ya = rowwise(_gmlp, [(H, H_UG, 256), (H, H_VG, 256)], [w['gm_ng'], w['gm_ws'], w['gm_bsb']], [(GM_WIDTH, f32)], [],
                 GM_CHUNK, "gmlp")[0]
    cqn, ckvn = rowwise(_mla_prep, [(H, H_CQ, Q_LORA), (H, H_CKV, KV_LORA)], [w['qg'], w['kvg']],
                        [(Q_LORA, bf16), (KV_LORA, bf16)], [], _row_tile(S, 512), "mla_prep")
    qraw = mm(cqn, w['uq'], name="mla_uq")
    kv = mm(ckvn, w['ukv'], name="mla_ukv")
    q1, q2, k1, k2 = rowwise(_rope, [(qraw, 512, LANES), (qraw, 640, LANES), (H, H_K1, LANES), (H, H_K2, LANES), cos8, sin8],
                             [], [(LANES, f32)] * 4, [], _row_tile(S, 512), "rope")
    zpad = jnp.zeros((S, MLA_HEADS, LANES - MLA_NOPE - MLA_ROPE), f32)
    qp = jnp.concatenate([_heads_first(qraw[:, :512], 64), _heads_first(q1, ROPE_HALF), _heads_first(q2, ROPE_HALF), zpad], axis=2)
    k1b = jnp.broadcast_to(k1[:, None, :ROPE_HALF], (S, MLA_HEADS, ROPE_HALF))
    k2b = jnp.broadcast_to(k2[:, None, :ROPE_HALF], (S, MLA_HEADS, ROPE_HALF))
    kp = jnp.concatenate([_heads_first(kv[:, :512], 64), k1b, k2b, zpad], axis=2)
    qp = qp.transpose(1, 0, 2).astype(bf16)
    kp = kp.transpose(1, 0, 2).astype(bf16)
    v3 = _heads_first(kv[:, 512:], 64).astype(bf16)
    vp = v3.transpose(1, 0, 2)
    oT, lse, *received = flash_fwd(qp, kp, v3.transpose(1, 2, 0), exch)
    ob = oT.transpose(2, 0, 1).reshape(S, MLA_HEADS * MLA_V)
    bu3 = mm(H, w['ssm_wb'], a_col0=H_US, out_s3=True, name="ssm_bu")
    hs3 = scan_fwd(bu3, w['ssm_a16'])
    y1 = mm(hs3, w['ssm_wc'], a_s3=True, name="ssm_c")
    y = rowwise(_mix_post, [ya, ob, y1, (H, H_US, SSM_WIDTH)], [w['ssm_d'], w['glu_w'], w['glu_b'], w['gmix']],
                [(D_MODEL, bf16)], [], _row_tile(S, 256), "mix_post")[0]
    f = mm(y, w['out'], name="mix_out")
    z, xo, xob = resid_ln(x, f, g, b, 1.0, "mix_ln")
    return xo, xob, (xb, H, cqn, ckvn, qp, kp, vp, lse, hs3, ya, ob, y1, y, z), received


def mixer_bwd(dxo, res, w, g, b, cos8, sin8, exch=None):
    xb, H, cqn, ckvn, qp, kp, vp, lse, hs3, ya, ob, y1, y, z = res
    S = z.shape[0]
    gr = {}
    dz, dzs, gr['ln_g'], gr['ln_b'] = ln_bwd(z, g, b, dxo, 1.0, "mix_ln_bwd")
    gr['w_out'] = mm_tn(y, dzs, name="mix_out_dw")
    dy = mm(dzs, w['out'], b_nt=True, name="mix_out_dx")

    def post_bwd(ya, ob, y1, us, dy, dskip, gluw, glub, gmix):
        _, vjp = jax.vjp(_mix_post, ya, ob, y1, us, dskip, gluw, glub, gmix)
        dya, dob, dy1, dus, *dpars = vjp(dy)
        prod = dob * ob
        col = lax.broadcasted_iota(jnp.int32, (1, MLA_HEADS * MLA_V), 1)
        lane = lax.broadcasted_iota(jnp.int32, (1, LANES), 1)
        delta = jnp.zeros((prod.shape[0], LANES), f32)
        for h in range(MLA_HEADS):
            in_head = ((col >= MLA_V * h) & (col < MLA_V * (h + 1))).astype(f32)
            delta = jnp.where(lane == h, jnp.sum(prod * in_head, axis=-1, keepdims=True), delta)
        return (dya, dob, dy1, dus, delta, *dpars)

    dya, dob, dy1, dus_skip, delta, gr['ssm_d'], gr['ssm_glu_w'], gr['ssm_glu_b'], gr['mix_norm_g'] = rowwise(
        post_bwd, [ya, ob, y1, (H, H_US, SSM_WIDTH), dy], [w['ssm_d'], w['glu_w'], w['glu_b'], w['gmix']],
        [(GM_WIDTH, f32), (MLA_HEADS * MLA_V, f32), (SSM_WIDTH, f32), (SSM_WIDTH, f32), (LANES, f32)],
        [(1, SSM_WIDTH), (SSM_WIDTH, SSM_WIDTH), (1, SSM_WIDTH), (1, D_MODEL)], _row_tile(S, 256), "mix_post_bwd")

    gr['ssm_wc'] = mm_tn(hs3, dy1, a_s3=True, name="ssm_c_dw")
    dhs3 = mm(dy1, w['ssm_wc'], out_s3=True, b_nt=True, name="ssm_c_dx")
    dbu3, gr['ssm_a16'] = scan_bwd(dhs3, hs3, w['ssm_a16'])
    gr['ssm_wb'] = mm_tn(H, dbu3, a_col0=H_US, m_dim=SSM_WIDTH, b_s3=True, name="ssm_bu_dw")
    dus = mm(dbu3, w['ssm_wb'], add=dus_skip, add_scale=1.0, a_s3=True, b_nt=True, name="ssm_bu_dx")

    do = _heads_first(dob, MLA_V).transpose(1, 0, 2)
    delta = delta[:, :MLA_HEADS].T.reshape(MLA_HEADS, 1, S)
    dqT, dkp, dvp, *received = flash_bwd(qp, kp, kp.transpose(0, 2, 1), vp, do.astype(bf16), lse, delta, exch)
    dqp = dqT.transpose(2, 0, 1)
    dkp = dkp.transpose(1, 0, 2)
    dv = dvp.transpose(1, 0, 2).reshape(S, MLA_HEADS * MLA_V)
    lane_pad = ((0, 0), (0, LANES - ROPE_HALF))
    dq1r = dqp[:, :, 64:80].reshape(S, LANES)
    dq2r = dqp[:, :, 80:96].reshape(S, LANES)
    dk1r = jnp.pad(jnp.sum(dkp[:, :, 64:80], axis=1), lane_pad)
    dk2r = jnp.pad(jnp.sum(dkp[:, :, 80:96], axis=1), lane_pad)

    def rope_bwd(d1, d2, d3, d4, cos, sin):
        return d1 * cos + d2 * sin, d2 * cos - d1 * sin, d3 * cos + d4 * sin, d4 * cos - d3 * sin

    dq1, dq2, dk1, dk2 = rowwise(rope_bwd, [dq1r, dq2r, dk1r, dk2r, cos8, sin8], [], [(LANES, f32)] * 4, [],
                                 _row_tile(S, 512), "rope_bwd")
    dqraw = jnp.concatenate([dqp[:, :, :64].reshape(S, 512), dq1, dq2], axis=1).astype(bf16)
    dkv = jnp.concatenate([dkp[:, :, :64].reshape(S, 512), dv], axis=1).astype(bf16)
    gr['uq'] = mm_tn(cqn, dqraw, name="mla_uq_dw")
    dcqn = mm(dqraw, w['uq'], b_nt=True, name="mla_uq_dx")
    gr['ukv'] = mm_tn(ckvn, dkv, name="mla_ukv_dw")
    dckvn = mm(dkv, w['ukv'], b_nt=True, name="mla_ukv_dx")

    def prep_bwd(cq, ckv, d1, d2, qg, kvg):
        _, vjp = jax.vjp(_mla_prep, cq, ckv, qg, kvg)
        return vjp((d1, d2))

    dcq, dckv, gr['mla_q_norm_g'], gr['mla_kv_norm_g'] = rowwise(
        prep_bwd, [(H, H_CQ, Q_LORA), (H, H_CKV, KV_LORA), dcqn, dckvn], [w['qg'], w['kvg']],
        [(Q_LORA, f32), (KV_LORA, f32)], [(1, Q_LORA), (1, KV_LORA)], _row_tile(S, 256), "mla_prep_bwd")

    def gmlp_bwd(hu, hv, dya, ng, ws, bsb):
        _, vjp = jax.vjp(_gmlp, hu, hv, ng, ws, bsb)
        return vjp(dya)

    dhu, dhv, gr['gmlp_norm_g'], gr['gmlp_ws'], gr['gm_bsb'] = rowwise(
        gmlp_bwd, [(H, H_UG, 256), (H, H_VG, 256), dya], [w['gm_ng'], w['gm_ws'], w['gm_bsb']],
        [(GM_WIDTH, f32), (GM_WIDTH, f32)], [(1, GM_WIDTH), (GM_HEADS, GM_CHUNK, GM_CHUNK), (GM_CHUNK, GM_WIDTH)],
        GM_CHUNK, "gmlp_bwd")

    dH = jnp.concatenate([dhu, dhv, dcq, dus, dckv, dk1, dk2], axis=1).astype(bf16)
    gr['in'] = mm_tn(xb, dH, name="mix_in_dw")
    dx = mm(dH, w['in'], add=dz, add_scale=ALPHA, b_nt=True, name="mix_in_dx")
    return dx, gr, received


def _block_diag(blocks):
    G, a, b = blocks.shape
    eye = jnp.eye(G, dtype=blocks.dtype)
    return (eye[:, None, :, None] * blocks[:, :, None, :]).reshape(G * a, G * b)


def _diag_blocks(mat, G):
    a, b = mat.shape[0] // G, mat.shape[1] // G
    m4 = mat.reshape(G, a, G, b)
    eye = jnp.eye(G, dtype=mat.dtype)
    return jnp.sum(m4 * eye[:, None, :, None], axis=2)


def prep_layer(W, rep, l):
    w = {}
    for f in ('ffn1', 'ffn2'):
        w[f] = {'guT': jnp.stack([W[f + '_w_gate'], W[f + '_w_up']]), 'd': W[f + '_w_down']}
    wi = W['w_in']
    z112 = jnp.zeros((D_MODEL, LANES - ROPE_HALF), wi.dtype)
    w['in'] = jnp.concatenate([wi[:, :768], wi[:, 928:1184], wi[:, 768:896], wi[:, 896:912], z112, wi[:, 912:928], z112], axis=1)
    uq = W['mla_w_uq'].reshape(Q_LORA, MLA_HEADS, MLA_NOPE + MLA_ROPE)
    w['uq'] = jnp.concatenate([uq[:, :, :64].reshape(Q_LORA, 512), uq[:, :, 64:80].reshape(Q_LORA, LANES),
                               uq[:, :, 80:96].reshape(Q_LORA, LANES)], axis=1)
    ukv = W['mla_w_ukv'].reshape(KV_LORA, MLA_HEADS, MLA_NOPE + MLA_V)
    w['ukv'] = jnp.concatenate([ukv[:, :, :64].reshape(KV_LORA, 512), ukv[:, :, 64:].reshape(KV_LORA, 512)], axis=1)
    w['out'] = W['w_out']
    w['glu_w'] = W['ssm_glu_w']
    w['gm_ng'] = rep['gmlp_norm_g'][l].reshape(1, GM_WIDTH)
    w['gm_ws'] = rep['gmlp_ws'][l]
    w['gm_bsb'] = jnp.repeat(rep['gmlp_bs'][l].T, GM_HEAD_DIM, axis=1)
    w['qg'] = rep['mla_q_norm_g'][l].reshape(1, Q_LORA)
    w['kvg'] = rep['mla_kv_norm_g'][l].reshape(1, KV_LORA)
    w['ssm_d'] = rep['ssm_d'][l].reshape(1, SSM_WIDTH)
    w['glu_b'] = rep['ssm_glu_b'][l].reshape(1, SSM_WIDTH)
    w['gmix'] = rep['mix_norm_g'][l].reshape(1, D_MODEL)
    ar = rep['ssm_a_re'][l].reshape(1, N_STATE)
    ai = rep['ssm_a_im'][l].reshape(1, N_STATE)
    ldt = jnp.repeat(rep['ssm_log_dt'][l], SSM_STATE).reshape(1, N_STATE)
    brT = rep['ssm_b_re'][l].transpose(2, 0, 1).reshape(SSM_GROUP_CH, N_STATE)
    biT = rep['ssm_b_im'][l].transpose(2, 0, 1).reshape(SSM_GROUP_CH, N_STATE)
    w['ssm_prep_in'] = (ar, ai, ldt, brT, biT)
    abr, abi, bbrT, bbiT = whole(_ssm_prep, w['ssm_prep_in'], [(1, N_STATE)] * 2 + [(SSM_GROUP_CH, N_STATE)] * 2, "ssm_prep")
    w['ssm_a16'] = jnp.concatenate([abr.reshape(8, LANES), abi.reshape(8, LANES)], axis=0)

    def to_gcp(t):
        return t.reshape(SSM_GROUP_CH, SSM_GROUPS, SSM_STATE).transpose(1, 0, 2)

    w['ssm_wb'] = jnp.concatenate([_block_diag(to_gcp(bbrT)), _block_diag(to_gcp(bbiT))], axis=1).astype(bf16)
    cre = rep['ssm_c_re'][l].transpose(0, 2, 1)
    cim = rep['ssm_c_im'][l].transpose(0, 2, 1)
    w['ssm_wc'] = jnp.concatenate([_block_diag(cre), -_block_diag(cim)], axis=0).astype(bf16)
    return w


def unprep_grads(gr, w):
    out = {}
    for k in ('ln_g', 'ln_b', 'w_out', 'mla_q_norm_g', 'mla_kv_norm_g', 'ssm_glu_w', 'gmlp_ws'):
        out[k] = gr[k]
    out['gmlp_norm_g'] = gr['gmlp_norm_g'].reshape(GM_WIDTH)
    out['mla_q_norm_g'] = gr['mla_q_norm_g'].reshape(Q_LORA)
    out['mla_kv_norm_g'] = gr['mla_kv_norm_g'].reshape(KV_LORA)
    out['ssm_d'] = gr['ssm_d'].reshape(SSM_GROUPS, SSM_GROUP_CH)
    out['ssm_glu_b'] = gr['ssm_glu_b'].reshape(SSM_WIDTH)
    out['mix_norm_g'] = gr['mix_norm_g'].reshape(D_MODEL)
    out['gmlp_bs'] = gr['gm_bsb'].reshape(GM_CHUNK, GM_HEADS, GM_HEAD_DIM).sum(axis=-1).T
    d = gr['in']
    out['w_in'] = jnp.concatenate([d[:, :768], d[:, H_CKV:H_CKV + KV_LORA], d[:, H_K1:H_K1 + ROPE_HALF],
                                   d[:, H_K2:H_K2 + ROPE_HALF], d[:, H_US:H_US + SSM_WIDTH]], axis=1)
    d = gr['uq']
    out['mla_w_uq'] = jnp.concatenate([d[:, :512].reshape(Q_LORA, MLA_HEADS, 64), d[:, 512:640].reshape(Q_LORA, MLA_HEADS, ROPE_HALF),
                                       d[:, 640:768].reshape(Q_LORA, MLA_HEADS, ROPE_HALF)], axis=2).reshape(Q_LORA, 768)
    d = gr['ukv']
    out['mla_w_ukv'] = jnp.concatenate([d[:, :512].reshape(KV_LORA, MLA_HEADS, 64), d[:, 512:].reshape(KV_LORA, MLA_HEADS, 64)],
                                       axis=2).reshape(KV_LORA, 1024)
    dwc = gr['ssm_wc']
    out['ssm_c_re'] = _diag_blocks(dwc[:N_STATE], SSM_GROUPS).transpose(0, 2, 1)
    out['ssm_c_im'] = -_diag_blocks(dwc[N_STATE:], SSM_GROUPS).transpose(0, 2, 1)
    dwb = gr['ssm_wb']

    def from_blocks(m):
        return _diag_blocks(m, SSM_GROUPS).transpose(1, 0, 2).reshape(SSM_GROUP_CH, N_STATE)

    dbbrT, dbbiT = from_blocks(dwb[:, :N_STATE]), from_blocks(dwb[:, N_STATE:])
    da16 = gr['ssm_a16']
    dabr, dabi = da16[0:8].reshape(1, N_STATE), da16[8:16].reshape(1, N_STATE)

    def prep_bwd(ar, ai, ldt, brT, biT, d1, d2, d3, d4):
        _, vjp = jax.vjp(_ssm_prep, ar, ai, ldt, brT, biT)
        return vjp((d1, d2, d3, d4))

    dar, dai, dldt, dbrT, dbiT = whole(prep_bwd, w['ssm_prep_in'] + (dabr, dabi, dbbrT, dbbiT),
                                       [(1, N_STATE)] * 3 + [(SSM_GROUP_CH, N_STATE)] * 2, "ssm_prep_bwd")
    out['ssm_a_re'] = dar.reshape(SSM_GROUPS, SSM_STATE)
    out['ssm_a_im'] = dai.reshape(SSM_GROUPS, SSM_STATE)
    out['ssm_log_dt'] = dldt.reshape(SSM_GROUPS, SSM_STATE).sum(axis=-1)
    out['ssm_b_re'] = dbrT.reshape(SSM_GROUP_CH, SSM_GROUPS, SSM_STATE).transpose(1, 2, 0)
    out['ssm_b_im'] = dbiT.reshape(SSM_GROUP_CH, SSM_GROUPS, SSM_STATE).transpose(1, 2, 0)
    return out


def adamw(w, g, m, v, name):
    R, C = w.shape

    def fn(w, g, m, v):
        m = ADAM_B1 * m + (1.0 - ADAM_B1) * g
        v = ADAM_B2 * v + (1.0 - ADAM_B2) * jnp.square(g)
        m_hat = m / (1.0 - ADAM_B1 ** ADAM_STEP)
        v_hat = v / (1.0 - ADAM_B2 ** ADAM_STEP)
        delta = -ADAM_LR * (m_hat / (jnp.sqrt(v_hat) + ADAM_EPS) + ADAM_WD * w)
        return delta, m, v

    return rowwise(fn, [w, g, m, v], [], [(C, f32)] * 3, [], _pick(R, (256, 128, 64, 32, 16, 8)), name)


def kernel(x, positions, ln_g, ln_b, ffn1_w_gate, ffn1_w_up, ffn1_w_down, w_in, gmlp_norm_g, gmlp_ws, gmlp_bs, mla_q_norm_g, mla_w_uq, mla_kv_norm_g, mla_w_ukv, ssm_a_re, ssm_a_im, ssm_b_re, ssm_b_im, ssm_c_re, ssm_c_im, ssm_d, ssm_log_dt, ssm_glu_w, ssm_glu_b, mix_norm_g, w_out, ffn2_w_gate, ffn2_w_up, ffn2_w_down, loss_target, m_ln_g, m_ln_b, m_ffn1_w_gate, m_ffn1_w_up, m_ffn1_w_down, m_w_in, m_gmlp_norm_g, m_gmlp_ws, m_gmlp_bs, m_mla_q_norm_g, m_mla_w_uq, m_mla_kv_norm_g, m_mla_w_ukv, m_ssm_a_re, m_ssm_a_im, m_ssm_b_re, m_ssm_b_im, m_ssm_c_re, m_ssm_c_im, m_ssm_d, m_ssm_log_dt, m_ssm_glu_w, m_ssm_glu_b, m_mix_norm_g, m_w_out, m_ffn2_w_gate, m_ffn2_w_up, m_ffn2_w_down, v_ln_g, v_ln_b, v_ffn1_w_gate, v_ffn1_w_up, v_ffn1_w_down, v_w_in, v_gmlp_norm_g, v_gmlp_ws, v_gmlp_bs, v_mla_q_norm_g, v_mla_w_uq, v_mla_kv_norm_g, v_mla_w_ukv, v_ssm_a_re, v_ssm_a_im, v_ssm_b_re, v_ssm_b_im, v_ssm_c_re, v_ssm_c_im, v_ssm_d, v_ssm_log_dt, v_ssm_glu_w, v_ssm_glu_b, v_mix_norm_g, v_w_out, v_ffn2_w_gate, v_ffn2_w_up, v_ffn2_w_down):
    Wp = dict(zip(W_NAMES, (ln_g, ln_b, ffn1_w_gate, ffn1_w_up, ffn1_w_down, w_in, gmlp_norm_g, gmlp_ws, gmlp_bs, mla_q_norm_g, mla_w_uq, mla_kv_norm_g, mla_w_ukv, ssm_a_re, ssm_a_im, ssm_b_re, ssm_b_im, ssm_c_re, ssm_c_im, ssm_d, ssm_log_dt, ssm_glu_w, ssm_glu_b, mix_norm_g, w_out, ffn2_w_gate, ffn2_w_up, ffn2_w_down)))
    Mp = dict(zip(W_NAMES, (m_ln_g, m_ln_b, m_ffn1_w_gate, m_ffn1_w_up, m_ffn1_w_down, m_w_in, m_gmlp_norm_g, m_gmlp_ws, m_gmlp_bs, m_mla_q_norm_g, m_mla_w_uq, m_mla_kv_norm_g, m_mla_w_ukv, m_ssm_a_re, m_ssm_a_im, m_ssm_b_re, m_ssm_b_im, m_ssm_c_re, m_ssm_c_im, m_ssm_d, m_ssm_log_dt, m_ssm_glu_w, m_ssm_glu_b, m_mix_norm_g, m_w_out, m_ffn2_w_gate, m_ffn2_w_up, m_ffn2_w_down)))
    Vp = dict(zip(W_NAMES, (v_ln_g, v_ln_b, v_ffn1_w_gate, v_ffn1_w_up, v_ffn1_w_down, v_w_in, v_gmlp_norm_g, v_gmlp_ws, v_gmlp_bs, v_mla_q_norm_g, v_mla_w_uq, v_mla_kv_norm_g, v_mla_w_ukv, v_ssm_a_re, v_ssm_a_im, v_ssm_b_re, v_ssm_b_im, v_ssm_c_re, v_ssm_c_im, v_ssm_d, v_ssm_log_dt, v_ssm_glu_w, v_ssm_glu_b, v_mix_norm_g, v_w_out, v_ffn2_w_gate, v_ffn2_w_up, v_ffn2_w_down)))
    S = x.shape[1]
    my = _my_id()

    def shard_rows(l):
        return [_pad_tile_rows(_shard_to_rows(n, Wp[n][l].astype(bf16)), 0) for n in BIG]

    def joined(got):
        return {n: _join_from_devices(n, g[:, :_pack_rows(n)]) for n, g in zip(BIG, got)}

    ln_flat = jnp.concatenate([Wp[n].reshape(-1) for n in LN_NAMES])
    *got, ln_all = all_gather(shard_rows(0) + [_pad_rows(ln_flat, 8)], "gather_weights")
    ln_all = ln_all.reshape(N_DEV, -1)
    lnsz = DEPTH * 3 * (D_MODEL // N_DEV)
    ln_full = {}
    for t, n in enumerate(LN_NAMES):
        sh = ln_all[:, t * lnsz:(t + 1) * lnsz].reshape(N_DEV, DEPTH, 3, D_MODEL // N_DEV)
        ln_full[n] = sh.transpose(1, 2, 0, 3).reshape(DEPTH, 3, 1, D_MODEL)
    rep = {n: Wp[n] for n in REPL}

    inv_freq = 1.0 / (ROPE_BASE ** (jnp.arange(0, MLA_ROPE, 2, dtype=f32) / MLA_ROPE))
    ang = positions.astype(f32).reshape(S, 1) * inv_freq[None, :]
    cos8 = jnp.tile(jnp.cos(ang), (1, MLA_HEADS))
    sin8 = jnp.tile(jnp.sin(ang), (1, MLA_HEADS))

    xs = x.reshape(S, D_MODEL)
    xb = xs.astype(bf16)
    ws, saved = [], []
    for l in range(DEPTH):
        w = prep_layer(joined(got), rep, l)
        lg, lb = ln_full['ln_g'][l], ln_full['ln_b'][l]
        xs, xb, r1 = ffn_fwd(xs, xb, w['ffn1'], lg[0], lb[0])
        xs, xb, r2, got = mixer_fwd(xs, xb, w, lg[1], lb[1], cos8, sin8, (True, shard_rows(l + 1)) if l + 1 < DEPTH else None)
        xs, xb, r3 = ffn_fwd(xs, xb, w['ffn2'], lg[2], lb[2])
        ws.append(w)
        saved.append((r1, r2, r3))

    def loss_fn(y, t):
        d = y - t
        part = jnp.sum(jnp.mean(jnp.square(d), axis=-1, keepdims=True), axis=0, keepdims=True)
        return d * (1.0 / D_MODEL), 0.5 * part

    dx, loss_part = rowwise(loss_fn, [xs, loss_target.reshape(S, D_MODEL)], [], [(D_MODEL, f32)], [(1, 1)],
                            _row_tile(S, 512), "loss")
    loss = lax.psum(loss_part[0, 0], ("x", "y", "c"))

    def grad_pack(n, g):
        return _pad_tile_rows(_split_for_devices(n, g.astype(bf16)), 1)

    ffn2_names = [n for n in BIG if n.startswith('ffn2')]
    rest_names = [n for n in BIG if n not in ffn2_names]
    grads, arrived = [None] * DEPTH, [{} for _ in range(DEPTH)]
    for l in reversed(range(DEPTH)):
        w = ws[l]
        lg, lb = ln_full['ln_g'][l], ln_full['ln_b'][l]
        r1, r2, r3 = saved[l]
        dx, g2g, g2u, g2d, dg2, db2 = ffn_bwd(dx, r3, w['ffn2'], lg[2], lb[2])
        riders = [(l, n, g) for n, g in zip(ffn2_names, (g2g, g2u, g2d))]
        if l + 1 < DEPTH:
            riders += [(l + 1, n, grads[l + 1][n]) for n in rest_names]
        dx, gm, got = mixer_bwd(dx, r2, w, lg[1], lb[1], cos8, sin8, (False, [grad_pack(n, g) for _, n, g in riders]))
        for (layer, n, _), arr in zip(riders, got):
            arrived[layer][n] = arr
        dx, g1g, g1u, g1d, dg0, db0 = ffn_bwd(dx, r1, w['ffn1'], lg[0], lb[0])
        g = unprep_grads(gm, w)
        g.update({'ffn1_w_gate': g1g, 'ffn1_w_up': g1u, 'ffn1_w_down': g1d,
                  'ffn2_w_gate': g2g, 'ffn2_w_up': g2u, 'ffn2_w_down': g2d})
        g['ln_g'] = jnp.concatenate([dg0, g['ln_g'], dg2], axis=0)
        g['ln_b'] = jnp.concatenate([db0, g['ln_b'], db2], axis=0)
        grads[l] = g
    grad_x = dx.reshape(1, S, D_MODEL)

    arrived[0].update(zip(rest_names, all_to_all([grad_pack(n, grads[0][n]) for n in rest_names], "scatter_grads")))
    G = {n: jnp.stack([_rows_to_shard(n, sum_slots(arrived[l][n], "sum_" + n)[:_pack_rows(n)]) for l in range(DEPTH)])
         for n in BIG}
    small_names = LN_NAMES + REPL
    spack = jnp.concatenate([jnp.stack([grads[l][n] for l in range(DEPTH)]).reshape(-1) for n in small_names])
    n_small = spack.shape[0]
    gsmall = sum_slots(all_gather([_pad_rows(spack, 8)], "gather_small")[0], "sum_small").reshape(-1)

    off = 0
    for n in small_names:
        shp = (DEPTH, 3, D_MODEL) if n in LN_NAMES else Wp[n].shape
        sz = math.prod(shp)
        G[n] = gsmall[off:off + sz].reshape(shp)
        off += sz
    for n in LN_NAMES:
        G[n] = lax.dynamic_slice_in_dim(G[n], my * (D_MODEL // N_DEV), D_MODEL // N_DEV, axis=2)

    delta, new_m, new_v = {}, {}, {}
    for n in BIG:
        shp = Wp[n].shape
        two = (shp[0] * shp[1], shp[2])
        d_, m_, v_ = adamw(Wp[n].reshape(two), G[n].reshape(two), Mp[n].reshape(two), Vp[n].reshape(two), "adamw_" + n)
        delta[n], new_m[n], new_v[n] = d_.reshape(shp), m_.reshape(shp), v_.reshape(shp)

    def pack_small(src):
        return _pad_rows(jnp.concatenate([src[n].reshape(-1) for n in small_names]), 8)

    d_, m_, v_ = adamw(pack_small(Wp), pack_small(G), pack_small(Mp), pack_small(Vp), "adamw_small")
    d_, m_, v_ = d_.reshape(-1), m_.reshape(-1), v_.reshape(-1)
    off = 0
    for n in small_names:
        shp = Wp[n].shape
        sz = math.prod(shp)
        delta[n], new_m[n], new_v[n] = (t[off:off + sz].reshape(shp) for t in (d_, m_, v_))
        off += sz

    return (loss, grad_x, *[G[n] for n in W_NAMES], *[delta[n] for n in W_NAMES],
            *[new_m[n] for n in W_NAMES], *[new_v[n] for n in W_NAMES])
```

```python
import functools
import math

import jax
import jax.numpy as jnp
from jax import lax
from jax.experimental import pallas as pl
from jax.experimental.pallas import tpu as pltpu

f32 = jnp.float32
bf16 = jnp.bfloat16

D_MODEL = 1024
DEPTH = 4
D_FF = 2816
GM_HEADS, GM_HEAD_DIM, GM_WIDTH, GM_CHUNK = 4, 64, 256, 128
MLA_HEADS, MLA_NOPE, MLA_ROPE, MLA_V = 8, 64, 32, 64
ROPE_HALF = MLA_ROPE // 2
Q_LORA, KV_LORA = 256, 128
ROPE_BASE = 10000.0
SSM_GROUPS, SSM_GROUP_CH, SSM_WIDTH, SSM_STATE = 16, 16, 256, 64
N_STATE = SSM_GROUPS * SSM_STATE
ALPHA = (2 * DEPTH) ** 0.25
LN_EPS = 1e-5
RMS_EPS = 1e-6
NEG_BIG = -1e30
ATT_SCALE = (MLA_NOPE + MLA_ROPE) ** -0.5
ADAM_LR, ADAM_B1, ADAM_B2, ADAM_EPS, ADAM_WD, ADAM_STEP = 0.001, 0.9, 0.999, 1e-08, 0.01, 10

N_DEV = 8
LANES = 128
VMEM_LIMIT = 48 * 1024 * 1024
MM_TILE_BUDGET = 32 * 1024 * 1024
MESH = pl.DeviceIdType.MESH

H_UG, H_VG, H_CQ, H_US, H_CKV, H_K1, H_K2, H_COLS = 0, 256, 512, 768, 1024, 1152, 1280, 1408

W_NAMES = ['ln_g', 'ln_b', 'ffn1_w_gate', 'ffn1_w_up', 'ffn1_w_down', 'w_in', 'gmlp_norm_g', 'gmlp_ws', 'gmlp_bs',
           'mla_q_norm_g', 'mla_w_uq', 'mla_kv_norm_g', 'mla_w_ukv', 'ssm_a_re', 'ssm_a_im', 'ssm_b_re', 'ssm_b_im',
           'ssm_c_re', 'ssm_c_im', 'ssm_d', 'ssm_log_dt', 'ssm_glu_w', 'ssm_glu_b', 'mix_norm_g', 'w_out',
           'ffn2_w_gate', 'ffn2_w_up', 'ffn2_w_down']
BIG = {'ffn1_w_gate': 1, 'ffn1_w_up': 1, 'ffn1_w_down': 0, 'w_in': 1, 'mla_w_uq': 1, 'mla_w_ukv': 1,
       'ssm_glu_w': 0, 'w_out': 0, 'ffn2_w_gate': 1, 'ffn2_w_up': 1, 'ffn2_w_down': 0}
BIG_SHAPE = {'ffn1_w_gate': (D_MODEL, D_FF), 'ffn1_w_up': (D_MODEL, D_FF), 'ffn1_w_down': (D_FF, D_MODEL),
             'w_in': (D_MODEL, 1184), 'mla_w_uq': (Q_LORA, 768), 'mla_w_ukv': (KV_LORA, 1024),
             'ssm_glu_w': (SSM_WIDTH, SSM_WIDTH), 'w_out': (D_MODEL, D_MODEL),
             'ffn2_w_gate': (D_MODEL, D_FF), 'ffn2_w_up': (D_MODEL, D_FF), 'ffn2_w_down': (D_FF, D_MODEL)}
LN_NAMES = ['ln_g', 'ln_b']
REPL = [n for n in W_NAMES if n not in BIG and n not in LN_NAMES]


def _pick(n, cands):
    for c in cands:
        if n % c == 0:
            return c
    return n


def _params(sem):
    return pltpu.CompilerParams(dimension_semantics=sem, vmem_limit_bytes=VMEM_LIMIT)


S3_ROWS = 2 * N_STATE // LANES


def _from_s3(ref):
    return jnp.concatenate([ref[:, c, :] for c in range(S3_ROWS)], axis=1)


def _to_s3(ref, val):
    for c in range(S3_ROWS):
        ref[:, c, :] = val[:, c * LANES:(c + 1) * LANES].astype(ref.dtype)


def mm(a, b, *, out_dtype=f32, add=None, add_scale=1.0, a_col0=0, a_s3=False, out_s3=False, grouped=False, b_nt=False, name):
    G = a.shape[0] if grouped else 1
    M = a.shape[1] if grouped else a.shape[0]
    K, N = (b.shape[-1], b.shape[-2]) if b_nt else b.shape[-2:]
    tn = N if out_s3 else _pick(N, (512, 384, 256) if N <= 1536 else (512, 384, 256, 128))
    tk = K if (K <= 2 * D_FF or a_s3) else _pick(K, (1024, 512, 256, 128))
    nk = G * (K // tk)
    assert not grouped or tk == K

    def tile_bytes(tm):
        return 2 * (tm * tk * a.dtype.itemsize + tk * tn * b.dtype.itemsize + tm * tn * 4 * (2 if add is not None else 1))

    tm = next(t for t in (1024, 512, 256, 128, 64, 32, 16, 8) if M % t == 0 and (tile_bytes(t) <= MM_TILE_BUDGET or t == 8))
    assert a_s3 or grouped or (a_col0 % tk == 0 and a_col0 + K <= a.shape[1])
    kb0 = a_col0 // tk
    has_add = add is not None

    def body(*refs):
        if has_add:
            a_ref, b_ref, add_ref, o_ref, acc = refs
        else:
            a_ref, b_ref, o_ref, acc = refs
        k = pl.program_id(2)
        a_val = _from_s3(a_ref) if a_s3 else a_ref[...]
        part = (_nt if b_nt else functools.partial(jnp.dot, preferred_element_type=f32))(a_val.astype(bf16), b_ref[...].astype(bf16))

        def finish(total):
            if has_add:
                total = total + add_scale * add_ref[...]
            if out_s3:
                _to_s3(o_ref, total)
            else:
                o_ref[...] = total.astype(o_ref.dtype)

        if nk == 1:
            finish(part)
        else:
            @pl.when(k == 0)
            def _():
                acc[...] = part

            @pl.when(k > 0)
            def _():
                acc[...] += part

            @pl.when(k == nk - 1)
            def _():
                finish(acc[...])

    if a_s3:
        a_spec = pl.BlockSpec((tm, S3_ROWS, LANES), lambda i, j, k: (i, 0, 0))
    elif grouped:
        a_spec = pl.BlockSpec((None, tm, tk), lambda i, j, k: (k, i, 0))
    else:
        a_spec = pl.BlockSpec((tm, tk), lambda i, j, k: (i, kb0 + k))
    if b_nt:
        b_spec = pl.BlockSpec((None, tn, tk), lambda i, j, k: (k, j, 0)) if grouped else pl.BlockSpec((tn, tk), lambda i, j, k: (j, k))
    else:
        b_spec = pl.BlockSpec((None, tk, tn), lambda i, j, k: (k, 0, j)) if grouped else pl.BlockSpec((tk, tn), lambda i, j, k: (k, j))
    in_specs, ops = [a_spec, b_spec], [a, b]
    if has_add:
        in_specs.append(pl.BlockSpec((tm, tn), lambda i, j, k: (i, j)))
        ops.append(add)
    if out_s3:
        out_spec = pl.BlockSpec((tm, S3_ROWS, LANES), lambda i, j, k: (i, 0, 0))
        out_shape = jax.ShapeDtypeStruct((M, S3_ROWS, LANES), out_dtype)
    else:
        out_spec = pl.BlockSpec((tm, tn), lambda i, j, k: (i, j))
        out_shape = jax.ShapeDtypeStruct((M, N), out_dtype)
    return pl.pallas_call(
        body, grid=(M // tm, N // tn, nk), in_specs=in_specs, out_specs=out_spec, out_shape=out_shape,
        scratch_shapes=[pltpu.VMEM((tm, tn) if nk > 1 else (8, LANES), f32)],
        compiler_params=_params(("parallel", "parallel", "arbitrary")), name=name)(*ops)


def mm_tn(a, b, *, a_col0=0, m_dim=None, a_s3=False, b_s3=False, a_lead=None, name):
    K = a.shape[-2] if a_lead is not None else a.shape[0]
    M = 2 * N_STATE if a_s3 else (a.shape[-1] if m_dim is None else m_dim)
    N = 2 * N_STATE if b_s3 else b.shape[-1]
    tm = M if a_s3 else _pick(M, (H_COLS, 1024, 768, 512, 384, 256, 128))
    tn = N if b_s3 else _pick(N, (H_COLS, 1024, 768, 512, 384, 256))

    def tile_bytes(tk):
        return 2 * tk * (tm * a.dtype.itemsize + tn * b.dtype.itemsize) + 3 * tm * tn * 4

    tk = next(t for t in (2048, 1024, 512, 256, 128, 64, 32, 16) if K % t == 0 and (tile_bytes(t) <= MM_TILE_BUDGET or t == 16))
    nk = K // tk
    assert a_col0 % tm == 0
    mb0 = a_col0 // tm

    def body(a_ref, b_ref, o_ref, acc):
        k = pl.program_id(2)
        a_val = _from_s3(a_ref) if a_s3 else a_ref[...]
        b_val = _from_s3(b_ref) if b_s3 else b_ref[...]
        part = lax.dot_general(a_val.astype(bf16), b_val.astype(bf16), (((0,), (0,)), ((), ())), preferred_element_type=f32)

        @pl.when(k == 0)
        def _():
            acc[...] = part

        @pl.when(k > 0)
        def _():
            acc[...] += part

        @pl.when(k == nk - 1)
        def _():
            o_ref[...] = acc[...]

    s3_spec = pl.BlockSpec((tk, S3_ROWS, LANES), lambda i, j, k: (k, 0, 0))
    if a_s3:
        a_spec = s3_spec
    elif a_lead is not None:
        a_spec = pl.BlockSpec((None, tk, tm), lambda i, j, k: (a_lead, k, i))
    else:
        a_spec = pl.BlockSpec((tk, tm), lambda i, j, k: (k, mb0 + i))
    return pl.pallas_call(
        body, grid=(M // tm, N // tn, nk),
        in_specs=[a_spec, s3_spec if b_s3 else pl.BlockSpec((tk, tn), lambda i, j, k: (k, j))],
        out_specs=pl.BlockSpec((tm, tn), lambda i, j, k: (i, j)),
        out_shape=jax.ShapeDtypeStruct((M, N), f32),
        scratch_shapes=[pltpu.VMEM((tm, tn), f32)],
        compiler_params=_params(("parallel", "parallel", "arbitrary")), name=name)(a, b)


def rowwise(fn, rows, pars, out_rows, out_accs, tm, name):
    first = rows[0]
    if isinstance(first, tuple):
        R = first[0].shape[1] if first[1] == 'lead' else first[0].shape[0]
    else:
        R = first.shape[0]
    assert R % tm == 0, (R, tm, name)
    n_rows, n_pars, n_or, n_oa = len(rows), len(pars), len(out_rows), len(out_accs)

    in_specs, ops = [], []
    for r in rows:
        if isinstance(r, tuple) and r[1] == 'lead':
            arr, _, kk = r
            in_specs.append(pl.BlockSpec((None, tm, arr.shape[2]), lambda i, kk=kk: (kk, i, 0)))
        elif isinstance(r, tuple):
            arr, c0, w = r
            assert c0 % w == 0
            in_specs.append(pl.BlockSpec((tm, w), lambda i, cb=c0 // w: (i, cb)))
        else:
            arr = r
            in_specs.append(pl.BlockSpec((tm, arr.shape[1]), lambda i: (i, 0)))
        ops.append(arr)
    for p in pars:
        in_specs.append(pl.BlockSpec(p.shape, lambda i, nd=p.ndim: (0,) * nd))
        ops.append(p)
    out_specs = [pl.BlockSpec((tm, w), lambda i: (i, 0)) for (w, _) in out_rows]
    out_specs += [pl.BlockSpec(s, lambda i, nd=len(s): (0,) * nd) for s in out_accs]
    out_shape = [jax.ShapeDtypeStruct((R, w), dt) for (w, dt) in out_rows]
    out_shape += [jax.ShapeDtypeStruct(s, f32) for s in out_accs]

    def body(*refs):
        ins = [r[...] for r in refs[:n_rows + n_pars]]
        o_refs = refs[n_rows + n_pars:]
        res = fn(*ins)
        if not isinstance(res, (tuple, list)):
            res = (res,)
        assert len(res) == n_or + n_oa, (len(res), n_or, n_oa, name)
        for o, v in zip(o_refs[:n_or], res[:n_or]):
            o[...] = v.astype(o.dtype)
        if n_oa:
            i = pl.program_id(0)

            @pl.when(i == 0)
            def _():
                for o, v in zip(o_refs[n_or:], res[n_or:]):
                    o[...] = v.astype(f32)

            @pl.when(i > 0)
            def _():
                for o, v in zip(o_refs[n_or:], res[n_or:]):
                    o[...] += v.astype(f32)

    return pl.pallas_call(
        body, grid=(R // tm,), in_specs=in_specs, out_specs=out_specs, out_shape=out_shape,
        compiler_params=_params(("arbitrary",)), name=name)(*ops)


def whole(fn, ins, out_shapes, name):
    def body(*refs):
        res = fn(*[r[...] for r in refs[:len(ins)]])
        for o, v in zip(refs[len(ins):], res):
            o[...] = v

    return pl.pallas_call(body, out_shape=[jax.ShapeDtypeStruct(s, f32) for s in out_shapes], name=name)(*ins)


@jax.custom_vjp
def _bdot(a, b):
    return jnp.dot(a.astype(bf16), b.astype(bf16), preferred_element_type=f32)


def _bdot_fwd(a, b):
    return _bdot(a, b), (a, b)


def _bdot_bwd(res, g):
    a, b = res
    gb = g.astype(bf16)
    da = lax.dot_general(gb, b.astype(bf16), (((1,), (1,)), ((), ())), preferred_element_type=f32)
    db = lax.dot_general(a.astype(bf16), gb, (((0,), (0,)), ((), ())), preferred_element_type=f32)
    return da, db


_bdot.defvjp(_bdot_fwd, _bdot_bwd)


def _ln(z, g, b):
    mu = jnp.mean(z, axis=-1, keepdims=True)
    var = jnp.mean(jnp.square(z - mu), axis=-1, keepdims=True)
    return (z - mu) * lax.rsqrt(var + LN_EPS) * g + b


def _rms_only(x):
    return x * lax.rsqrt(jnp.mean(jnp.square(x), axis=-1, keepdims=True) + RMS_EPS)


def _swiglu(a, b):
    return jax.nn.silu(a) * b


def _gmlp(hu, hv, ng, ws, bsb):
    u = jax.nn.gelu(hu)
    v = jax.nn.gelu(hv)
    lane = lax.broadcasted_iota(jnp.int32, (1, GM_WIDTH), 1)
    masks = [((lane >= GM_HEAD_DIM * h) & (lane < GM_HEAD_DIM * (h + 1))).astype(f32) for h in range(GM_HEADS)]
    mu = jnp.zeros_like(v)
    for m in masks:
        mu = mu + m * (jnp.sum(v * m, axis=-1, keepdims=True) / GM_HEAD_DIM)
    d = v - mu
    var = jnp.zeros_like(v)
    for m in masks:
        var = var + m * (jnp.sum(d * d * m, axis=-1, keepdims=True) / GM_HEAD_DIM)
    vn = d * lax.rsqrt(var + LN_EPS) * ng
    r = lax.broadcasted_iota(jnp.int32, (GM_CHUNK, GM_CHUNK), 0)
    c = lax.broadcasted_iota(jnp.int32, (GM_CHUNK, GM_CHUNK), 1)
    tril = (c <= r).astype(f32)
    z = bsb
    for h, m in enumerate(masks):
        z = z + _bdot(ws[h] * tril, vn * m)
    return u * z


def _mla_prep(cq, ckv, qg, kvg):
    return _rms_only(cq) * qg, _rms_only(ckv) * kvg


def _rope(q1, q2, k1, k2, cos, sin):
    return q1 * cos - q2 * sin, q2 * cos + q1 * sin, k1 * cos - k2 * sin, k2 * cos + k1 * sin


def _mix_post(ya, ob, y1, us, dskip, gluw, glub, gmix):
    y = jax.nn.gelu(y1 + dskip * us)
    yc = y * jax.nn.sigmoid(_bdot(y, gluw) + glub)
    return jnp.concatenate([_rms_only(ya), _rms_only(ob), _rms_only(yc)], axis=1) * gmix


def _ssm_prep(ar, ai, ldt, brT, biT):
    dt = jnp.exp(ldt)
    mag = jnp.exp(ar * dt)
    abr = mag * jnp.cos(ai * dt)
    abi = mag * jnp.sin(ai * dt)
    den = ar * ar + ai * ai
    cr = ((abr - 1.0) * ar + abi * ai) / den
    ci = (abi * ar - (abr - 1.0) * ai) / den
    return abr, abi, cr * brT - ci * biT, cr * biT + ci * brT


ATT_FWD_HEADS = 4


def _att_tile(S):
    return _pick(S, (512, 256, 128))


def _nt(a, b):
    return lax.dot_general(a, b, (((1,), (1,)), ((), ())), preferred_element_type=f32)


def _diag_keep(T):
    krow = lax.broadcasted_iota(jnp.int32, (T, T), 0)
    qcol = lax.broadcasted_iota(jnp.int32, (T, T), 1)
    return qcol >= krow


def grid_call(body, exch, *, grid, in_specs, out_specs, out_shape, scratch_shapes, name, args):
    params = _params(("arbitrary", "arbitrary"))
    if exch is None:
        return pl.pallas_call(body, grid=grid, in_specs=in_specs, out_specs=out_specs, out_shape=out_shape,
                              scratch_shapes=scratch_shapes, compiler_params=params, name=name)(*args)
    gather, xs = exch
    n, n_in, n_out, n_sc = len(xs), len(in_specs), len(out_specs), len(scratch_shapes)

    def riding(*refs):
        ins, x_refs = refs[:n_in], refs[n_in:n_in + n]
        outs, xo_refs = refs[n_in + n:n_in + n + n_out], refs[n_in + n + n_out:n_in + 2 * n + n_out]
        rest = refs[n_in + 2 * n + n_out:]
        scratch, sems = rest[:n_sc], rest[n_sc:]
        h, i = pl.program_id(0), pl.program_id(1)

        @pl.when((h == 0) & (i == 0))
        def _():
            _direct_exchange(gather, x_refs, xo_refs, *sems, start=True)

        body(*ins, *outs, *scratch)

        @pl.when((h == grid[0] - 1) & (i == grid[1] - 1))
        def _():
            _direct_exchange(gather, x_refs, xo_refs, *sems, start=False)

    return pl.pallas_call(
        riding, grid=grid, in_specs=list(in_specs) + [HBM_SPEC] * n, out_specs=list(out_specs) + [HBM_SPEC] * n,
        out_shape=list(out_shape) + _exchange_out_shapes(gather, xs),
        scratch_shapes=list(scratch_shapes) + _exchange_scratch(n), compiler_params=params,
        name=name + ("_gather" if gather else "_scatter"))(*args, *xs)


def flash_fwd(q, k, vT, exch=None):
    Hh, S, _ = q.shape
    T = _att_tile(S)
    HB = ATT_FWD_HEADS

    def body(q_ref, k_ref, vT_ref, o_ref, lse_ref, m_sc, l_sc, acc_sc):
        i = pl.program_id(1)
        m_sc[...] = jnp.full_like(m_sc, NEG_BIG)
        l_sc[...] = jnp.zeros_like(l_sc)
        acc_sc[...] = jnp.zeros_like(acc_sc)

        def block(j, diagonal):
            rows = pl.ds(pl.multiple_of(j * T, T), T)
            for hh in range(HB):
                sT = _nt(k_ref[hh, rows, :], q_ref[hh]) * ATT_SCALE
                if diagonal:
                    sT = jnp.where(_diag_keep(T), sT, NEG_BIG)
                m_old = m_sc[hh]
                m_new = jnp.maximum(m_old, jnp.max(sT, axis=0, keepdims=True))
                alpha = jnp.exp(m_old - m_new)
                pT = jnp.exp(sT - m_new)
                l_sc[hh] = alpha * l_sc[hh] + jnp.sum(pT, axis=0, keepdims=True)
                acc_sc[hh] = alpha * acc_sc[hh] + jnp.dot(vT_ref[hh, :, rows], pT.astype(bf16), preferred_element_type=f32)
                m_sc[hh] = m_new

        def loop_body(j, c):
            block(j, False)
            return c

        lax.fori_loop(0, i, loop_body, 0)
        block(i, True)
        o_ref[...] = acc_sc[...] / l_sc[...]
        lse_ref[...] = m_sc[...] + jnp.log(l_sc[...])

    return grid_call(
        body, exch, grid=(Hh // HB, S // T),
        in_specs=[pl.BlockSpec((HB, T, LANES), lambda h, i: (h, i, 0)), pl.BlockSpec((HB, S, LANES), lambda h, i: (h, 0, 0)),
                  pl.BlockSpec((HB, MLA_V, S), lambda h, i: (h, 0, 0))],
        out_specs=[pl.BlockSpec((HB, MLA_V, T), lambda h, i: (h, 0, i)), pl.BlockSpec((HB, 1, T), lambda h, i: (h, 0, i))],
        out_shape=[jax.ShapeDtypeStruct((Hh, MLA_V, S), f32), jax.ShapeDtypeStruct((Hh, 1, S), f32)],
        scratch_shapes=[pltpu.VMEM((HB, 1, T), f32), pltpu.VMEM((HB, 1, T), f32), pltpu.VMEM((HB, MLA_V, T), f32)],
        name="flash_fwd", args=(q, k, vT))


def flash_bwd(q, k, kT, v, do, lse, delta, exch=None):
    Hh, S, _ = q.shape
    T = _att_tile(S)

    def body(q_ref, do_ref, lse_ref, dl_ref, k_ref, kT_ref, v_ref, dq_ref, dk_ref, dv_ref, dq_sc):
        i = pl.program_id(1)

        @pl.when(i == 0)
        def _():
            dk_ref[...] = jnp.zeros_like(dk_ref)
            dv_ref[...] = jnp.zeros_like(dv_ref)

        qi, doi = q_ref[0], do_ref[0]
        lse_i, dl_i = lse_ref[0], dl_ref[0]
        dq_sc[...] = jnp.zeros_like(dq_sc)

        def block(j, diagonal):
            rows = pl.ds(pl.multiple_of(j * T, T), T)
            sT = _nt(k_ref[0, rows, :], qi) * ATT_SCALE
            pT = jnp.exp(sT - lse_i)
            if diagonal:
                pT = jnp.where(_diag_keep(T), pT, 0.0)
            dpT = _nt(v_ref[0, rows, :], doi)
            dsT = (pT * (dpT - dl_i) * ATT_SCALE).astype(bf16)
            dv_ref[0, rows, :] += jnp.dot(pT.astype(bf16), doi, preferred_element_type=f32)
            dk_ref[0, rows, :] += jnp.dot(dsT, qi, preferred_element_type=f32)
            dq_sc[...] += jnp.dot(kT_ref[0, :, rows], dsT, preferred_element_type=f32)

        def loop_body(j, c):
            block(j, False)
            return c

        lax.fori_loop(0, i, loop_body, 0)
        block(i, True)
        dq_ref[0] = dq_sc[...]

    tile = lambda w: pl.BlockSpec((1, T, w), lambda h, i: (h, i, 0))
    row = pl.BlockSpec((1, 1, T), lambda h, i: (h, 0, i))
    full = lambda w: pl.BlockSpec((1, S, w), lambda h, i: (h, 0, 0))
    return grid_call(
        body, exch, grid=(Hh, S // T),
        in_specs=[tile(LANES), tile(MLA_V), row, row, full(LANES), pl.BlockSpec((1, LANES, S), lambda h, i: (h, 0, 0)), full(MLA_V)],
        out_specs=[pl.BlockSpec((1, LANES, T), lambda h, i: (h, 0, i)), full(LANES), full(MLA_V)],
        out_shape=[jax.ShapeDtypeStruct((Hh, LANES, S), f32), jax.ShapeDtypeStruct((Hh, S, LANES), f32),
                   jax.ShapeDtypeStruct((Hh, S, MLA_V), f32)],
        scratch_shapes=[pltpu.VMEM((LANES, T), f32)],
        name="flash_bwd", args=(q, do, lse, delta, k, kT, v))


def _scan_tile(S):
    return _pick(S, (256, 128, 64, 32, 16, 8))


def scan_fwd(bu3, a16):
    S = bu3.shape[0]
    ts = _scan_tile(S)

    def body(bu_ref, a_ref, o_ref, h_sc):
        @pl.when(pl.program_id(0) == 0)
        def _():
            h_sc[...] = jnp.zeros_like(h_sc)

        ar, ai = a_ref[0:8, :], a_ref[8:16, :]
        a2r, a2i = ar * ar - ai * ai, 2.0 * ar * ai

        def pair(p, carry):
            hr, hi = carry
            t = 2 * p
            x0r, x0i = bu_ref[t, 0:8, :], bu_ref[t, 8:16, :]
            ur = ar * x0r - ai * x0i + bu_ref[t + 1, 0:8, :]
            ui = ar * x0i + ai * x0r + bu_ref[t + 1, 8:16, :]
            o_ref[t, 0:8, :] = ar * hr - ai * hi + x0r
            o_ref[t, 8:16, :] = ar * hi + ai * hr + x0i
            nr = a2r * hr - a2i * hi + ur
            ni = a2r * hi + a2i * hr + ui
            o_ref[t + 1, 0:8, :] = nr
            o_ref[t + 1, 8:16, :] = ni
            return nr, ni

        hr, hi = lax.fori_loop(0, ts // 2, pair, (h_sc[0:8, :], h_sc[8:16, :]), unroll=4)
        h_sc[0:8, :] = hr
        h_sc[8:16, :] = hi

    blk = pl.BlockSpec((ts, 16, LANES), lambda i: (i, 0, 0))
    return pl.pallas_call(
        body, grid=(S // ts,), in_specs=[blk, pl.BlockSpec((16, LANES), lambda i: (0, 0))], out_specs=blk,
        out_shape=jax.ShapeDtypeStruct(bu3.shape, f32), scratch_shapes=[pltpu.VMEM((16, LANES), f32)],
        compiler_params=_params(("arbitrary",)), name="scan_fwd")(bu3, a16)


def scan_bwd(g3, h3, a16):
    S = g3.shape[0]
    ts = _scan_tile(S)
    nb = S // ts

    def body(g_ref, h_ref, a_ref, o_ref, da_ref, lam_sc, da_sc):
        @pl.when(pl.program_id(0) == 0)
        def _():
            lam_sc[...] = jnp.zeros_like(lam_sc)
            da_sc[...] = jnp.zeros_like(da_sc)

        ar, ai = a_ref[0:8, :], a_ref[8:16, :]
        a2r, a2i = ar * ar - ai * ai, 2.0 * ar * ai

        def pair(p, carry):
            lr, li, dar, dai = carry
            t = ts - 1 - 2 * p
            h1r, h1i = h_ref[t, 0:8, :], h_ref[t, 8:16, :]
            h0r, h0i = h_ref[t - 1, 0:8, :], h_ref[t - 1, 8:16, :]
            g1r, g1i = g_ref[t, 0:8, :], g_ref[t, 8:16, :]
            l1r = ar * lr + ai * li + g1r
            l1i = ar * li - ai * lr + g1i
            vr = ar * g1r + ai * g1i + g_ref[t - 1, 0:8, :]
            vi = ar * g1i - ai * g1r + g_ref[t - 1, 8:16, :]
            l0r = a2r * lr + a2i * li + vr
            l0i = a2r * li - a2i * lr + vi
            dar = dar + (lr * h1r + li * h1i) + (l1r * h0r + l1i * h0i)
            dai = dai + (li * h1r - lr * h1i) + (l1i * h0r - l1r * h0i)
            o_ref[t, 0:8, :] = l1r
            o_ref[t, 8:16, :] = l1i
            o_ref[t - 1, 0:8, :] = l0r
            o_ref[t - 1, 8:16, :] = l0i
            return l0r, l0i, dar, dai

        lr, li, dar, dai = lax.fori_loop(
            0, ts // 2, pair, (lam_sc[0:8, :], lam_sc[8:16, :], da_sc[0:8, :], da_sc[8:16, :]), unroll=4)
        lam_sc[0:8, :] = lr
        lam_sc[8:16, :] = li
        da_sc[0:8, :] = dar
        da_sc[8:16, :] = dai
        da_ref[0:8, :] = dar
        da_ref[8:16, :] = dai

    blk = pl.BlockSpec((ts, 16, LANES), lambda i: (nb - 1 - i, 0, 0))
    small = pl.BlockSpec((16, LANES), lambda i: (0, 0))
    return pl.pallas_call(
        body, grid=(nb,), in_specs=[blk, blk, small], out_specs=[blk, small],
        out_shape=[jax.ShapeDtypeStruct(g3.shape, f32), jax.ShapeDtypeStruct((16, LANES), f32)],
        scratch_shapes=[pltpu.VMEM((16, LANES), f32), pltpu.VMEM((16, LANES), f32)],
        compiler_params=_params(("arbitrary",)), name="scan_bwd")(g3, h3, a16)


HBM_SPEC = pl.BlockSpec(memory_space=pltpu.HBM)


def _my_id():
    return 4 * lax.axis_index("x") + 2 * lax.axis_index("y") + lax.axis_index("c")


def all_gather(xs, name):
    n = len(xs)

    def body(*refs):
        x_refs, o_refs = refs[:n], refs[n:2 * n]
        send_sems, recv_sems, local_sems = refs[2 * n:]
        x, y, c = lax.axis_index("x"), lax.axis_index("y"), lax.axis_index("c")
        me, sibling = (x, y, c), (x, y, 1 - c)
        chips = [(1 - x, y), (x, 1 - y), (1 - x, 1 - y)]

        def slot(o, p):
            return o.at[4 * p[0] + 2 * p[1] + p[2]]

        def copy(a, k, block, to, src=None):
            o = o_refs[a]
            return pltpu.make_async_remote_copy(
                src_ref=slot(o, block) if src is None else src, dst_ref=slot(o, block),
                send_sem=send_sems.at[7 * a + k], recv_sem=recv_sems.at[7 * a + k], device_id=to, device_id_type=MESH)

        own, sends = [], []
        for a in range(n):
            mine = pltpu.make_async_copy(x_refs[a], slot(o_refs[a], me), local_sems.at[a])
            mine.start()
            own.append(mine)
            first = [copy(a, 0, me, sibling, src=x_refs[a])]
            first += [copy(a, 1 + j, me, (*chip, c), src=x_refs[a]) for j, chip in enumerate(chips)]
            for cp in first:
                cp.start()
            sends += first
        for a in range(n):
            for j, chip in enumerate(chips):
                copy(a, 1 + j, (*chip, c), me).wait_recv()
                fwd = copy(a, 4 + j, (*chip, c), sibling)
                fwd.start()
                sends.append(fwd)
        for a in range(n):
            copy(a, 0, sibling, me).wait_recv()
            for j, chip in enumerate(chips):
                copy(a, 4 + j, (*chip, 1 - c), me).wait_recv()
        for cp in sends:
            cp.wait_send()
        for cp in own:
            cp.wait()

    return pl.pallas_call(
        body, out_shape=[jax.ShapeDtypeStruct((N_DEV,) + v.shape, v.dtype) for v in xs],
        in_specs=[HBM_SPEC] * n, out_specs=[HBM_SPEC] * n,
        scratch_shapes=[pltpu.SemaphoreType.DMA((7 * n,)), pltpu.SemaphoreType.DMA((7 * n,)), pltpu.SemaphoreType.DMA((n,))],
        name=name)(*xs)


def _direct_exchange(gather, x_refs, o_refs, send_sems, recv_sems, local_sems, start):
    x, y, c = lax.axis_index("x"), lax.axis_index("y"), lax.axis_index("c")
    my = 4 * x + 2 * y + c
    for a, (x_ref, o_ref) in enumerate(zip(x_refs, o_refs)):
        mine = pltpu.make_async_copy(x_ref if gather else x_ref.at[my], o_ref.at[my], local_sems.at[a])
        if start:
            mine.start()
        else:
            mine.wait()
        for k in range(1, N_DEV):
            px = 1 - x if k & 4 else x
            py = 1 - y if k & 2 else y
            pc = 1 - c if k & 1 else c
            pid = 4 * px + 2 * py + pc
            src = x_ref if gather else x_ref.at[pid]
            sems = dict(send_sem=send_sems.at[7 * a + k - 1], recv_sem=recv_sems.at[7 * a + k - 1],
                        device_id=(px, py, pc), device_id_type=MESH)
            if start:
                pltpu.make_async_remote_copy(src_ref=src, dst_ref=o_ref.at[my], **sems).start()
            else:
                pltpu.make_async_remote_copy(src_ref=src, dst_ref=o_ref.at[my], **sems).wait_send()
                pltpu.make_async_remote_copy(src_ref=src, dst_ref=o_ref.at[pid], **sems).wait_recv()


def _exchange_scratch(n):
    return [pltpu.SemaphoreType.DMA((7 * n,)), pltpu.SemaphoreType.DMA((7 * n,)), pltpu.SemaphoreType.DMA((n,))]


def _exchange_out_shapes(gather, xs):
    return [jax.ShapeDtypeStruct(((N_DEV,) + v.shape) if gather else v.shape, v.dtype) for v in xs]


def all_to_all(xs, name):
    n = len(xs)

    def body(*refs):
        ex = (False, refs[:n], refs[n:2 * n], *refs[2 * n:])
        _direct_exchange(*ex, start=True)
        _direct_exchange(*ex, start=False)

    return pl.pallas_call(
        body, out_shape=_exchange_out_shapes(False, xs), in_specs=[HBM_SPEC] * n, out_specs=[HBM_SPEC] * n,
        scratch_shapes=_exchange_scratch(n), name=name)(*xs)


def sum_slots(g8, name):
    R, C = g8.shape[1:]

    def fn(*tiles):
        tot = tiles[0].astype(f32)
        for t in tiles[1:]:
            tot = tot + t.astype(f32)
        return tot

    return rowwise(fn, [(g8, 'lead', k) for k in range(N_DEV)], [], [(C, f32)], [], _pick(R, (256, 128, 64, 32, 16, 8)), name)[0]


PACK_W = 1024


def _pad_rows(flat, mult):
    n = flat.shape[0]
    tot = -(-n // (PACK_W * mult)) * PACK_W * mult
    return jnp.pad(flat, (0, tot - n)).reshape(tot // PACK_W, PACK_W)


BF16_TILE_ROWS = 16


def _pad_tile_rows(a, axis):
    pad = [(0, 0)] * a.ndim
    pad[axis] = (0, -a.shape[axis] % BF16_TILE_ROWS)
    return jnp.pad(a, pad)


def _shard_shape(name):
    r, c = BIG_SHAPE[name]
    return (r // N_DEV, c) if BIG[name] == 0 else (r, c // N_DEV)


def _pack_rows(name):
    r, c = _shard_shape(name)
    assert (r * c) % PACK_W == 0
    return r * c // PACK_W


TRANSPOSED = ('ffn1_w_gate', 'ffn1_w_up', 'ffn2_w_gate', 'ffn2_w_up')
assert all(BIG_SHAPE[n][0] == PACK_W and BIG[n] == 1 for n in TRANSPOSED)


def _shard_to_rows(name, shard):
    return (shard.T if name in TRANSPOSED else shard).reshape(_pack_rows(name), PACK_W)


def _rows_to_shard(name, rows):
    r, c = _shard_shape(name)
    return rows.reshape(c, r).T if name in TRANSPOSED else rows.reshape(r, c)


def _split_for_devices(name, full):
    r, c = BIG_SHAPE[name]
    if BIG[name] == 0 or name in TRANSPOSED:
        return full.reshape(N_DEV, _pack_rows(name), PACK_W)
    return full.reshape(r, N_DEV, c // N_DEV).transpose(1, 0, 2).reshape(N_DEV, _pack_rows(name), PACK_W)


def _join_from_devices(name, parts):
    r, c = BIG_SHAPE[name]
    if name in TRANSPOSED:
        return parts.reshape(c, r)
    if BIG[name] == 0:
        return parts.reshape(r, c)
    return parts.reshape(N_DEV, r, c // N_DEV).transpose(1, 0, 2).reshape(r, c)


def _row_tile(S, want):
    return _pick(S, tuple(t for t in (512, 256, 128, 64, 32, 16) if t <= want))


def resid_ln(x, f, g, b, scale, name):
    D = x.shape[1]

    def fn(x, f, g, b):
        z = ALPHA * x + scale * f
        xo = _ln(z, g, b)
        return z, xo, xo

    return rowwise(fn, [x, f], [g, b], [(D, f32), (D, f32), (D, bf16)], [], _row_tile(x.shape[0], 512), name)


def ln_bwd(z, g, b, dxo, scale, name):
    D = z.shape[1]

    def fn(z, dxo, g, b):
        _, vjp = jax.vjp(_ln, z, g, b)
        dz, dg, db = vjp(dxo)
        return dz, scale * dz, dg, db

    return rowwise(fn, [z, dxo], [g, b], [(D, f32), (D, bf16)], [(1, D), (1, D)], _row_tile(z.shape[0], 512), name)


FF_TILE = 256


def ffn_up_act(xb, guT):
    M, K = xb.shape
    tm = _pick(M, (1024, 512, 256, 128, 64, 32, 16))

    def body(x_ref, wg_ref, wu_ref, ab_ref, h_ref):
        x = x_ref[...]
        a = _nt(x, wg_ref[...])
        b = _nt(x, wu_ref[...])
        ab_ref[0] = a.astype(bf16)
        ab_ref[1] = b.astype(bf16)
        h_ref[...] = _swiglu(a, b).astype(bf16)

    wspecs = [pl.BlockSpec((None, FF_TILE, K), lambda i, j, which=which: (which, j, 0)) for which in (0, 1)]
    return pl.pallas_call(
        body, grid=(M // tm, D_FF // FF_TILE),
        in_specs=[pl.BlockSpec((tm, K), lambda i, j: (i, 0)), *wspecs],
        out_specs=[pl.BlockSpec((2, tm, FF_TILE), lambda i, j: (0, i, j)), pl.BlockSpec((tm, FF_TILE), lambda i, j: (i, j))],
        out_shape=[jax.ShapeDtypeStruct((2, M, D_FF), bf16), jax.ShapeDtypeStruct((M, D_FF), bf16)],
        compiler_params=_params(("parallel", "parallel")), name="ffn_up_act")(xb, guT, guT)


def ffn_down_dx_act(dzs, wd, ab):
    M, K = dzs.shape
    tm = _pick(M, (1024, 512, 256, 128, 64, 32, 16))

    def body(dz_ref, w_ref, ab_ref, dab_ref, h_ref):
        dh = _nt(dz_ref[...], w_ref[...])
        h, vjp = jax.vjp(_swiglu, ab_ref[0].astype(f32), ab_ref[1].astype(f32))
        da, db = vjp(dh)
        dab_ref[0] = da.astype(bf16)
        dab_ref[1] = db.astype(bf16)
        h_ref[...] = h.astype(bf16)

    pair = pl.BlockSpec((2, tm, FF_TILE), lambda i, j: (0, i, j))
    return pl.pallas_call(
        body, grid=(M // tm, D_FF // FF_TILE),
        in_specs=[pl.BlockSpec((tm, K), lambda i, j: (i, 0)), pl.BlockSpec((FF_TILE, K), lambda i, j: (j, 0)), pair],
        out_specs=[pair, pl.BlockSpec((tm, FF_TILE), lambda i, j: (i, j))],
        out_shape=[jax.ShapeDtypeStruct((2, M, D_FF), bf16), jax.ShapeDtypeStruct((M, D_FF), bf16)],
        compiler_params=_params(("parallel", "parallel")), name="ffn_down_dx_act")(dzs, wd, ab)


def ffn_fwd(x, xb, w, g, b):
    ab, h = ffn_up_act(xb, w['guT'])
    f = mm(h, w['d'], name="ffn_down")
    z, xo, xob = resid_ln(x, f, g, b, 0.5, "ffn_ln")
    return xo, xob, (xb, ab, z)


def ffn_bwd(dxo, res, w, g, b):
    xb, ab, z = res
    dz, dzs, dg, db = ln_bwd(z, g, b, dxo, 0.5, "ffn_ln_bwd")
    dab, h = ffn_down_dx_act(dzs, w['d'], ab)
    dwd = mm_tn(h, dzs, name="ffn_down_dw")
    dwgT = mm_tn(dab, xb, a_lead=0, name="ffn_gate_dw")
    dwuT = mm_tn(dab, xb, a_lead=1, name="ffn_up_dw")
    dx = mm(dab, w['guT'], add=dz, add_scale=ALPHA, grouped=True, name="ffn_up_dx")
    return dx, dwgT, dwuT, dwd, dg, db


def _heads_first(a, width):
    return a.reshape(a.shape[0], MLA_HEADS, width)


def mixer_fwd(x, xb, w, g, b, cos8, sin8, exch=None):
    S = x.shape[0]
    H = mm(xb, w['in'], name="mix_in")
    ya = rowwise(_gmlp, [(H, H_UG, 256), (H, H_VG, 256)], [w['gm_ng'], w['gm_ws'], w['gm_bsb']], [(GM_WIDTH, f32)], [],
                 GM_CHUNK, "gmlp")[0]
    cqn, ckvn = rowwise(_mla_prep, [(H, H_CQ, Q_LORA), (H, H_CKV, KV_LORA)], [w['qg'], w['kvg']],
                        [(Q_LORA, bf16), (KV_LORA, bf16)], [], _row_tile(S, 512), "mla_prep")
    qraw = mm(cqn, w['uq'], name="mla_uq")
    kv = mm(ckvn, w['ukv'], name="mla_ukv")
    q1, q2, k1, k2 = rowwise(_rope, [(qraw, 512, LANES), (qraw, 640, LANES), (H, H_K1, LANES), (H, H_K2, LANES), cos8, sin8],
                             [], [(LANES, f32)] * 4, [], _row_tile(S, 512), "rope")
    zpad = jnp.zeros((S, MLA_HEADS, LANES - MLA_NOPE - MLA_ROPE), f32)
    qp = jnp.concatenate([_heads_first(qraw[:, :512], 64), _heads_first(q1, ROPE_HALF), _heads_first(q2, ROPE_HALF), zpad], axis=2)
    k1b = jnp.broadcast_to(k1[:, None, :ROPE_HALF], (S, MLA_HEADS, ROPE_HALF))
    k2b = jnp.broadcast_to(k2[:, None, :ROPE_HALF], (S, MLA_HEADS, ROPE_HALF))
    kp = jnp.concatenate([_heads_first(kv[:, :512], 64), k1b, k2b, zpad], axis=2)
    qp = qp.transpose(1, 0, 2).astype(bf16)
    kp = kp.transpose(1, 0, 2).astype(bf16)
    v3 = _heads_first(kv[:, 512:], 64).astype(bf16)
    vp = v3.transpose(1, 0, 2)
    oT, lse, *received = flash_fwd(qp, kp, v3.transpose(1, 2, 0), exch)
    ob = oT.transpose(2, 0, 1).reshape(S, MLA_HEADS * MLA_V)
    bu3 = mm(H, w['ssm_wb'], a_col0=H_US, out_s3=True, name="ssm_bu")
    hs3 = scan_fwd(bu3, w['ssm_a16'])
    y1 = mm(hs3, w['ssm_wc'], a_s3=True, name="ssm_c")
    y = rowwise(_mix_post, [ya, ob, y1, (H, H_US, SSM_WIDTH)], [w['ssm_d'], w['glu_w'], w['glu_b'], w['gmix']],
                [(D_MODEL, bf16)], [], _row_tile(S, 256), "mix_post")[0]
    f = mm(y, w['out'], name="mix_out")
    z, xo, xob = resid_ln(x, f, g, b, 1.0, "mix_ln")
    return xo, xob, (xb, H, cqn, ckvn, qp, kp, vp, lse, hs3, ya, ob, y1, y, z), received


def mixer_bwd(dxo, res, w, g, b, cos8, sin8, exch=None):
    xb, H, cqn, ckvn, qp, kp, vp, lse, hs3, ya, ob, y1, y, z = res
    S = z.shape[0]
    gr = {}
    dz, dzs, gr['ln_g'], gr['ln_b'] = ln_bwd(z, g, b, dxo, 1.0, "mix_ln_bwd")
    gr['w_out'] = mm_tn(y, dzs, name="mix_out_dw")
    dy = mm(dzs, w['out'], b_nt=True, name="mix_out_dx")

    def post_bwd(ya, ob, y1, us, dy, dskip, gluw, glub, gmix):
        _, vjp = jax.vjp(_mix_post, ya, ob, y1, us, dskip, gluw, glub, gmix)
        dya, dob, dy1, dus, *dpars = vjp(dy)
        prod = dob * ob
        col = lax.broadcasted_iota(jnp.int32, (1, MLA_HEADS * MLA_V), 1)
        lane = lax.broadcasted_iota(jnp.int32, (1, LANES), 1)
        delta = jnp.zeros((prod.shape[0], LANES), f32)
        for h in range(MLA_HEADS):
            in_head = ((col >= MLA_V * h) & (col < MLA_V * (h + 1))).astype(f32)
            delta = jnp.where(lane == h, jnp.sum(prod * in_head, axis=-1, keepdims=True), delta)
        return (dya, dob, dy1, dus, delta, *dpars)

    dya, dob, dy1, dus_skip, delta, gr['ssm_d'], gr['ssm_glu_w'], gr['ssm_glu_b'], gr['mix_norm_g'] = rowwise(
        post_bwd, [ya, ob, y1, (H, H_US, SSM_WIDTH), dy], [w['ssm_d'], w['glu_w'], w['glu_b'], w['gmix']],
        [(GM_WIDTH, f32), (MLA_HEADS * MLA_V, f32), (SSM_WIDTH, f32), (SSM_WIDTH, f32), (LANES, f32)],
        [(1, SSM_WIDTH), (SSM_WIDTH, SSM_WIDTH), (1, SSM_WIDTH), (1, D_MODEL)], _row_tile(S, 256), "mix_post_bwd")

    gr['ssm_wc'] = mm_tn(hs3, dy1, a_s3=True, name="ssm_c_dw")
    dhs3 = mm(dy1, w['ssm_wc'], out_s3=True, b_nt=True, name="ssm_c_dx")
    dbu3, gr['ssm_a16'] = scan_bwd(dhs3, hs3, w['ssm_a16'])
    gr['ssm_wb'] = mm_tn(H, dbu3, a_col0=H_US, m_dim=SSM_WIDTH, b_s3=True, name="ssm_bu_dw")
    dus = mm(dbu3, w['ssm_wb'], add=dus_skip, add_scale=1.0, a_s3=True, b_nt=True, name="ssm_bu_dx")

    do = _heads_first(dob, MLA_V).transpose(1, 0, 2)
    delta = delta[:, :MLA_HEADS].T.reshape(MLA_HEADS, 1, S)
    dqT, dkp, dvp, *received = flash_bwd(qp, kp, kp.transpose(0, 2, 1), vp, do.astype(bf16), lse, delta, exch)
    dqp = dqT.transpose(2, 0, 1)
    dkp = dkp.transpose(1, 0, 2)
    dv = dvp.transpose(1, 0, 2).reshape(S, MLA_HEADS * MLA_V)
    lane_pad = ((0, 0), (0, LANES - ROPE_HALF))
    dq1r = dqp[:, :, 64:80].reshape(S, LANES)
    dq2r = dqp[:, :, 80:96].reshape(S, LANES)
    dk1r = jnp.pad(jnp.sum(dkp[:, :, 64:80], axis=1), lane_pad)
    dk2r = jnp.pad(jnp.sum(dkp[:, :, 80:96], axis=1), lane_pad)

    def rope_bwd(d1, d2, d3, d4, cos, sin):
        return d1 * cos + d2 * sin, d2 * cos - d1 * sin, d3 * cos + d4 * sin, d4 * cos - d3 * sin

    dq1, dq2, dk1, dk2 = rowwise(rope_bwd, [dq1r, dq2r, dk1r, dk2r, cos8, sin8], [], [(LANES, f32)] * 4, [],
                                 _row_tile(S, 512), "rope_bwd")
    dqraw = jnp.concatenate([dqp[:, :, :64].reshape(S, 512), dq1, dq2], axis=1).astype(bf16)
    dkv = jnp.concatenate([dkp[:, :, :64].reshape(S, 512), dv], axis=1).astype(bf16)
    gr['uq'] = mm_tn(cqn, dqraw, name="mla_uq_dw")
    dcqn = mm(dqraw, w['uq'], b_nt=True, name="mla_uq_dx")
    gr['ukv'] = mm_tn(ckvn, dkv, name="mla_ukv_dw")
    dckvn = mm(dkv, w['ukv'], b_nt=True, name="mla_ukv_dx")

    def prep_bwd(cq, ckv, d1, d2, qg, kvg):
        _, vjp = jax.vjp(_mla_prep, cq, ckv, qg, kvg)
        return vjp((d1, d2))

    dcq, dckv, gr['mla_q_norm_g'], gr['mla_kv_norm_g'] = rowwise(
        prep_bwd, [(H, H_CQ, Q_LORA), (H, H_CKV, KV_LORA), dcqn, dckvn], [w['qg'], w['kvg']],
        [(Q_LORA, f32), (KV_LORA, f32)], [(1, Q_LORA), (1, KV_LORA)], _row_tile(S, 256), "mla_prep_bwd")

    def gmlp_bwd(hu, hv, dya, ng, ws, bsb):
        _, vjp = jax.vjp(_gmlp, hu, hv, ng, ws, bsb)
        return vjp(dya)

    dhu, dhv, gr['gmlp_norm_g'], gr['gmlp_ws'], gr['gm_bsb'] = rowwise(
        gmlp_bwd, [(H, H_UG, 256), (H, H_VG, 256), dya], [w['gm_ng'], w['gm_ws'], w['gm_bsb']],
        [(GM_WIDTH, f32), (GM_WIDTH, f32)], [(1, GM_WIDTH), (GM_HEADS, GM_CHUNK, GM_CHUNK), (GM_CHUNK, GM_WIDTH)],
        GM_CHUNK, "gmlp_bwd")

    dH = jnp.concatenate([dhu, dhv, dcq, dus, dckv, dk1, dk2], axis=1).astype(bf16)
    gr['in'] = mm_tn(xb, dH, name="mix_in_dw")
    dx = mm(dH, w['in'], add=dz, add_scale=ALPHA, b_nt=True, name="mix_in_dx")
    return dx, gr, received


def _block_diag(blocks):
    G, a, b = blocks.shape
    eye = jnp.eye(G, dtype=blocks.dtype)
    return (eye[:, None, :, None] * blocks[:, :, None, :]).reshape(G * a, G * b)


def _diag_blocks(mat, G):
    a, b = mat.shape[0] // G, mat.shape[1] // G
    m4 = mat.reshape(G, a, G, b)
    eye = jnp.eye(G, dtype=mat.dtype)
    return jnp.sum(m4 * eye[:, None, :, None], axis=2)


def prep_layer(W, rep, l):
    w = {}
    for f in ('ffn1', 'ffn2'):
        w[f] = {'guT': jnp.stack([W[f + '_w_gate'], W[f + '_w_up']]), 'd': W[f + '_w_down']}
    wi = W['w_in']
    z112 = jnp.zeros((D_MODEL, LANES - ROPE_HALF), wi.dtype)
    w['in'] = jnp.concatenate([wi[:, :768], wi[:, 928:1184], wi[:, 768:896], wi[:, 896:912], z112, wi[:, 912:928], z112], axis=1)
    uq = W['mla_w_uq'].reshape(Q_LORA, MLA_HEADS, MLA_NOPE + MLA_ROPE)
    w['uq'] = jnp.concatenate([uq[:, :, :64].reshape(Q_LORA, 512), uq[:, :, 64:80].reshape(Q_LORA, LANES),
                               uq[:, :, 80:96].reshape(Q_LORA, LANES)], axis=1)
    ukv = W['mla_w_ukv'].reshape(KV_LORA, MLA_HEADS, MLA_NOPE + MLA_V)
    w['ukv'] = jnp.concatenate([ukv[:, :, :64].reshape(KV_LORA, 512), ukv[:, :, 64:].reshape(KV_LORA, 512)], axis=1)
    w['out'] = W['w_out']
    w['glu_w'] = W['ssm_glu_w']
    w['gm_ng'] = rep['gmlp_norm_g'][l].reshape(1, GM_WIDTH)
    w['gm_ws'] = rep['gmlp_ws'][l]
    w['gm_bsb'] = jnp.repeat(rep['gmlp_bs'][l].T, GM_HEAD_DIM, axis=1)
    w['qg'] = rep['mla_q_norm_g'][l].reshape(1, Q_LORA)
    w['kvg'] = rep['mla_kv_norm_g'][l].reshape(1, KV_LORA)
    w['ssm_d'] = rep['ssm_d'][l].reshape(1, SSM_WIDTH)
    w['glu_b'] = rep['ssm_glu_b'][l].reshape(1, SSM_WIDTH)
    w['gmix'] = rep['mix_norm_g'][l].reshape(1, D_MODEL)
    ar = rep['ssm_a_re'][l].reshape(1, N_STATE)
    ai = rep['ssm_a_im'][l].reshape(1, N_STATE)
    ldt = jnp.repeat(rep['ssm_log_dt'][l], SSM_STATE).reshape(1, N_STATE)
    brT = rep['ssm_b_re'][l].transpose(2, 0, 1).reshape(SSM_GROUP_CH, N_STATE)
    biT = rep['ssm_b_im'][l].transpose(2, 0, 1).reshape(SSM_GROUP_CH, N_STATE)
    w['ssm_prep_in'] = (ar, ai, ldt, brT, biT)
    abr, abi, bbrT, bbiT = whole(_ssm_prep, w['ssm_prep_in'], [(1, N_STATE)] * 2 + [(SSM_GROUP_CH, N_STATE)] * 2, "ssm_prep")
    w['ssm_a16'] = jnp.concatenate([abr.reshape(8, LANES), abi.reshape(8, LANES)], axis=0)

    def to_gcp(t):
        return t.reshape(SSM_GROUP_CH, SSM_GROUPS, SSM_STATE).transpose(1, 0, 2)

    w['ssm_wb'] = jnp.concatenate([_block_diag(to_gcp(bbrT)), _block_diag(to_gcp(bbiT))], axis=1).astype(bf16)
    cre = rep['ssm_c_re'][l].transpose(0, 2, 1)
    cim = rep['ssm_c_im'][l].transpose(0, 2, 1)
    w['ssm_wc'] = jnp.concatenate([_block_diag(cre), -_block_diag(cim)], axis=0).astype(bf16)
    return w


def unprep_grads(gr, w):
    out = {}
    for k in ('ln_g', 'ln_b', 'w_out', 'mla_q_norm_g', 'mla_kv_norm_g', 'ssm_glu_w', 'gmlp_ws'):
        out[k] = gr[k]
    out['gmlp_norm_g'] = gr['gmlp_norm_g'].reshape(GM_WIDTH)
    out['mla_q_norm_g'] = gr['mla_q_norm_g'].reshape(Q_LORA)
    out['mla_kv_norm_g'] = gr['mla_kv_norm_g'].reshape(KV_LORA)
    out['ssm_d'] = gr['ssm_d'].reshape(SSM_GROUPS, SSM_GROUP_CH)
    out['ssm_glu_b'] = gr['ssm_glu_b'].reshape(SSM_WIDTH)
    out['mix_norm_g'] = gr['mix_norm_g'].reshape(D_MODEL)
    out['gmlp_bs'] = gr['gm_bsb'].reshape(GM_CHUNK, GM_HEADS, GM_HEAD_DIM).sum(axis=-1).T
    d = gr['in']
    out['w_in'] = jnp.concatenate([d[:, :768], d[:, H_CKV:H_CKV + KV_LORA], d[:, H_K1:H_K1 + ROPE_HALF],
                                   d[:, H_K2:H_K2 + ROPE_HALF], d[:, H_US:H_US + SSM_WIDTH]], axis=1)
    d = gr['uq']
    out['mla_w_uq'] = jnp.concatenate([d[:, :512].reshape(Q_LORA, MLA_HEADS, 64), d[:, 512:640].reshape(Q_LORA, MLA_HEADS, ROPE_HALF),
                                       d[:, 640:768].reshape(Q_LORA, MLA_HEADS, ROPE_HALF)], axis=2).reshape(Q_LORA, 768)
    d = gr['ukv']
    out['mla_w_ukv'] = jnp.concatenate([d[:, :512].reshape(KV_LORA, MLA_HEADS, 64), d[:, 512:].reshape(KV_LORA, MLA_HEADS, 64)],
                                       axis=2).reshape(KV_LORA, 1024)
    dwc = gr['ssm_wc']
    out['ssm_c_re'] = _diag_blocks(dwc[:N_STATE], SSM_GROUPS).transpose(0, 2, 1)
    out['ssm_c_im'] = -_diag_blocks(dwc[N_STATE:], SSM_GROUPS).transpose(0, 2, 1)
    dwb = gr['ssm_wb']

    def from_blocks(m):
        return _diag_blocks(m, SSM_GROUPS).transpose(1, 0, 2).reshape(SSM_GROUP_CH, N_STATE)

    dbbrT, dbbiT = from_blocks(dwb[:, :N_STATE]), from_blocks(dwb[:, N_STATE:])
    da16 = gr['ssm_a16']
    dabr, dabi = da16[0:8].reshape(1, N_STATE), da16[8:16].reshape(1, N_STATE)

    def prep_bwd(ar, ai, ldt, brT, biT, d1, d2, d3, d4):
        _, vjp = jax.vjp(_ssm_prep, ar, ai, ldt, brT, biT)
        return vjp((d1, d2, d3, d4))

    dar, dai, dldt, dbrT, dbiT = whole(prep_bwd, w['ssm_prep_in'] + (dabr, dabi, dbbrT, dbbiT),
                                       [(1, N_STATE)] * 3 + [(SSM_GROUP_CH, N_STATE)] * 2, "ssm_prep_bwd")
    out['ssm_a_re'] = dar.reshape(SSM_GROUPS, SSM_STATE)
    out['ssm_a_im'] = dai.reshape(SSM_GROUPS, SSM_STATE)
    out['ssm_log_dt'] = dldt.reshape(SSM_GROUPS, SSM_STATE).sum(axis=-1)
    out['ssm_b_re'] = dbrT.reshape(SSM_GROUP_CH, SSM_GROUPS, SSM_STATE).transpose(1, 2, 0)
    out['ssm_b_im'] = dbiT.reshape(SSM_GROUP_CH, SSM_GROUPS, SSM_STATE).transpose(1, 2, 0)
    return out


def adamw(w, g, m, v, name):
    R, C = w.shape

    def fn(w, g, m, v):
        m = ADAM_B1 * m + (1.0 - ADAM_B1) * g
        v = ADAM_B2 * v + (1.0 - ADAM_B2) * jnp.square(g)
        m_hat = m / (1.0 - ADAM_B1 ** ADAM_STEP)
        v_hat = v / (1.0 - ADAM_B2 ** ADAM_STEP)
        delta = -ADAM_LR * (m_hat / (jnp.sqrt(v_hat) + ADAM_EPS) + ADAM_WD * w)
        return delta, m, v

    return rowwise(fn, [w, g, m, v], [], [(C, f32)] * 3, [], _pick(R, (256, 128, 64, 32, 16, 8)), name)


def kernel(x, positions, ln_g, ln_b, ffn1_w_gate, ffn1_w_up, ffn1_w_down, w_in, gmlp_norm_g, gmlp_ws, gmlp_bs, mla_q_norm_g, mla_w_uq, mla_kv_norm_g, mla_w_ukv, ssm_a_re, ssm_a_im, ssm_b_re, ssm_b_im, ssm_c_re, ssm_c_im, ssm_d, ssm_log_dt, ssm_glu_w, ssm_glu_b, mix_norm_g, w_out, ffn2_w_gate, ffn2_w_up, ffn2_w_down, loss_target, m_ln_g, m_ln_b, m_ffn1_w_gate, m_ffn1_w_up, m_ffn1_w_down, m_w_in, m_gmlp_norm_g, m_gmlp_ws, m_gmlp_bs, m_mla_q_norm_g, m_mla_w_uq, m_mla_kv_norm_g, m_mla_w_ukv, m_ssm_a_re, m_ssm_a_im, m_ssm_b_re, m_ssm_b_im, m_ssm_c_re, m_ssm_c_im, m_ssm_d, m_ssm_log_dt, m_ssm_glu_w, m_ssm_glu_b, m_mix_norm_g, m_w_out, m_ffn2_w_gate, m_ffn2_w_up, m_ffn2_w_down, v_ln_g, v_ln_b, v_ffn1_w_gate, v_ffn1_w_up, v_ffn1_w_down, v_w_in, v_gmlp_norm_g, v_gmlp_ws, v_gmlp_bs, v_mla_q_norm_g, v_mla_w_uq, v_mla_kv_norm_g, v_mla_w_ukv, v_ssm_a_re, v_ssm_a_im, v_ssm_b_re, v_ssm_b_im, v_ssm_c_re, v_ssm_c_im, v_ssm_d, v_ssm_log_dt, v_ssm_glu_w, v_ssm_glu_b, v_mix_norm_g, v_w_out, v_ffn2_w_gate, v_ffn2_w_up, v_ffn2_w_down):
    Wp = dict(zip(W_NAMES, (ln_g, ln_b, ffn1_w_gate, ffn1_w_up, ffn1_w_down, w_in, gmlp_norm_g, gmlp_ws, gmlp_bs, mla_q_norm_g, mla_w_uq, mla_kv_norm_g, mla_w_ukv, ssm_a_re, ssm_a_im, ssm_b_re, ssm_b_im, ssm_c_re, ssm_c_im, ssm_d, ssm_log_dt, ssm_glu_w, ssm_glu_b, mix_norm_g, w_out, ffn2_w_gate, ffn2_w_up, ffn2_w_down)))
    Mp = dict(zip(W_NAMES, (m_ln_g, m_ln_b, m_ffn1_w_gate, m_ffn1_w_up, m_ffn1_w_down, m_w_in, m_gmlp_norm_g, m_gmlp_ws, m_gmlp_bs, m_mla_q_norm_g, m_mla_w_uq, m_mla_kv_norm_g, m_mla_w_ukv, m_ssm_a_re, m_ssm_a_im, m_ssm_b_re, m_ssm_b_im, m_ssm_c_re, m_ssm_c_im, m_ssm_d, m_ssm_log_dt, m_ssm_glu_w, m_ssm_glu_b, m_mix_norm_g, m_w_out, m_ffn2_w_gate, m_ffn2_w_up, m_ffn2_w_down)))
    Vp = dict(zip(W_NAMES, (v_ln_g, v_ln_b, v_ffn1_w_gate, v_ffn1_w_up, v_ffn1_w_down, v_w_in, v_gmlp_norm_g, v_gmlp_ws, v_gmlp_bs, v_mla_q_norm_g, v_mla_w_uq, v_mla_kv_norm_g, v_mla_w_ukv, v_ssm_a_re, v_ssm_a_im, v_ssm_b_re, v_ssm_b_im, v_ssm_c_re, v_ssm_c_im, v_ssm_d, v_ssm_log_dt, v_ssm_glu_w, v_ssm_glu_b, v_mix_norm_g, v_w_out, v_ffn2_w_gate, v_ffn2_w_up, v_ffn2_w_down)))
    S = x.shape[1]
    my = _my_id()

    def shard_rows(l):
        return [_pad_tile_rows(_shard_to_rows(n, Wp[n][l].astype(bf16)), 0) for n in BIG]

    def joined(got):
        return {n: _join_from_devices(n, g[:, :_pack_rows(n)]) for n, g in zip(BIG, got)}

    ln_flat = jnp.concatenate([Wp[n].reshape(-1) for n in LN_NAMES])
    *got, ln_all = all_gather(shard_rows(0) + [_pad_rows(ln_flat, 8)], "gather_weights")
    ln_all = ln_all.reshape(N_DEV, -1)
    lnsz = DEPTH * 3 * (D_MODEL // N_DEV)
    ln_full = {}
    for t, n in enumerate(LN_NAMES):
        sh = ln_all[:, t * lnsz:(t + 1) * lnsz].reshape(N_DEV, DEPTH, 3, D_MODEL // N_DEV)
        ln_full[n] = sh.transpose(1, 2, 0, 3).reshape(DEPTH, 3, 1, D_MODEL)
    rep = {n: Wp[n] for n in REPL}

    inv_freq = 1.0 / (ROPE_BASE ** (jnp.arange(0, MLA_ROPE, 2, dtype=f32) / MLA_ROPE))
    ang = positions.astype(f32).reshape(S, 1) * inv_freq[None, :]
    cos8 = jnp.tile(jnp.cos(ang), (1, MLA_HEADS))
    sin8 = jnp.tile(jnp.sin(ang), (1, MLA_HEADS))

    xs = x.reshape(S, D_MODEL)
    xb = xs.astype(bf16)
    ws, saved = [], []
    for l in range(DEPTH):
        w = prep_layer(joined(got), rep, l)
        lg, lb = ln_full['ln_g'][l], ln_full['ln_b'][l]
        xs, xb, r1 = ffn_fwd(xs, xb, w['ffn1'], lg[0], lb[0])
        xs, xb, r2, got = mixer_fwd(xs, xb, w, lg[1], lb[1], cos8, sin8, (True, shard_rows(l + 1)) if l + 1 < DEPTH else None)
        xs, xb, r3 = ffn_fwd(xs, xb, w['ffn2'], lg[2], lb[2])
        ws.append(w)
        saved.append((r1, r2, r3))

    def loss_fn(y, t):
        d = y - t
        part = jnp.sum(jnp.mean(jnp.square(d), axis=-1, keepdims=True), axis=0, keepdims=True)
        return d * (1.0 / D_MODEL), 0.5 * part

    dx, loss_part = rowwise(loss_fn, [xs, loss_target.reshape(S, D_MODEL)], [], [(D_MODEL, f32)], [(1, 1)],
                            _row_tile(S, 512), "loss")
    loss = lax.psum(loss_part[0, 0], ("x", "y", "c"))

    def grad_pack(n, g):
        return _pad_tile_rows(_split_for_devices(n, g.astype(bf16)), 1)

    ffn2_names = [n for n in BIG if n.startswith('ffn2')]
    rest_names = [n for n in BIG if n not in ffn2_names]
    grads, arrived = [None] * DEPTH, [{} for _ in range(DEPTH)]
    for l in reversed(range(DEPTH)):
        w = ws[l]
        lg, lb = ln_full['ln_g'][l], ln_full['ln_b'][l]
        r1, r2, r3 = saved[l]
        dx, g2g, g2u, g2d, dg2, db2 = ffn_bwd(dx, r3, w['ffn2'], lg[2], lb[2])
        riders = [(l, n, g) for n, g in zip(ffn2_names, (g2g, g2u, g2d))]
        if l + 1 < DEPTH:
            riders += [(l + 1, n, grads[l + 1][n]) for n in rest_names]
        dx, gm, got = mixer_bwd(dx, r2, w, lg[1], lb[1], cos8, sin8, (False, [grad_pack(n, g) for _, n, g in riders]))
        for (layer, n, _), arr in zip(riders, got):
            arrived[layer][n] = arr
        dx, g1g, g1u, g1d, dg0, db0 = ffn_bwd(dx, r1, w['ffn1'], lg[0], lb[0])
        g = unprep_grads(gm, w)
        g.update({'ffn1_w_gate': g1g, 'ffn1_w_up': g1u, 'ffn1_w_down': g1d,
                  'ffn2_w_gate': g2g, 'ffn2_w_up': g2u, 'ffn2_w_down': g2d})
        g['ln_g'] = jnp.concatenate([dg0, g['ln_g'], dg2], axis=0)
        g['ln_b'] = jnp.concatenate([db0, g['ln_b'], db2], axis=0)
        grads[l] = g
    grad_x = dx.reshape(1, S, D_MODEL)

    arrived[0].update(zip(rest_names, all_to_all([grad_pack(n, grads[0][n]) for n in rest_names], "scatter_grads")))
    G = {n: jnp.stack([_rows_to_shard(n, sum_slots(arrived[l][n], "sum_" + n)[:_pack_rows(n)]) for l in range(DEPTH)])
         for n in BIG}
    small_names = LN_NAMES + REPL
    spack = jnp.concatenate([jnp.stack([grads[l][n] for l in range(DEPTH)]).reshape(-1) for n in small_names])
    n_small = spack.shape[0]
    gsmall = sum_slots(all_gather([_pad_rows(spack, 8)], "gather_small")[0], "sum_small").reshape(-1)

    off = 0
    for n in small_names:
        shp = (DEPTH, 3, D_MODEL) if n in LN_NAMES else Wp[n].shape
        sz = math.prod(shp)
        G[n] = gsmall[off:off + sz].reshape(shp)
        off += sz
    for n in LN_NAMES:
        G[n] = lax.dynamic_slice_in_dim(G[n], my * (D_MODEL // N_DEV), D_MODEL // N_DEV, axis=2)

    delta, new_m, new_v = {}, {}, {}
    for n in BIG:
        shp = Wp[n].shape
        two = (shp[0] * shp[1], shp[2])
        d_, m_, v_ = adamw(Wp[n].reshape(two), G[n].reshape(two), Mp[n].reshape(two), Vp[n].reshape(two), "adamw_" + n)
        delta[n], new_m[n], new_v[n] = d_.reshape(shp), m_.reshape(shp), v_.reshape(shp)

    def pack_small(src):
        return _pad_rows(jnp.concatenate([src[n].reshape(-1) for n in small_names]), 8)

    d_, m_, v_ = adamw(pack_small(Wp), pack_small(G), pack_small(Mp), pack_small(Vp), "adamw_small")
    d_, m_, v_ = d_.reshape(-1), m_.reshape(-1), v_.reshape(-1)
    off = 0
    for n in small_names:
        shp = Wp[n].shape
        sz = math.prod(shp)
        delta[n], new_m[n], new_v[n] = (t[off:off + sz].reshape(shp) for t in (d_, m_, v_))
        off += sz

    return (loss, grad_x, *[G[n] for n in W_NAMES], *[delta[n] for n in W_NAMES],
            *[new_m[n] for n in W_NAMES], *[new_v[n] for n in W_NAMES])
```

```python
import functools
import math

import jax
import jax.numpy as jnp
from jax import lax
from jax.experimental import pallas as pl
from jax.experimental.pallas import tpu as pltpu

f32 = jnp.float32
bf16 = jnp.bfloat16

D_MODEL = 1024
DEPTH = 4
D_FF = 2816
GM_HEADS, GM_HEAD_DIM, GM_WIDTH, GM_CHUNK = 4, 64, 256, 128
MLA_HEADS, MLA_NOPE, MLA_ROPE, MLA_V = 8, 64, 32, 64
ROPE_HALF = MLA_ROPE // 2
Q_LORA, KV_LORA = 256, 128
ROPE_BASE = 10000.0
SSM_GROUPS, SSM_GROUP_CH, SSM_WIDTH, SSM_STATE = 16, 16, 256, 64
N_STATE = SSM_GROUPS * SSM_STATE
ALPHA = (2 * DEPTH) ** 0.25
LN_EPS = 1e-5
RMS_EPS = 1e-6
NEG_BIG = -1e30
ATT_SCALE = (MLA_NOPE + MLA_ROPE) ** -0.5
ADAM_LR, ADAM_B1, ADAM_B2, ADAM_EPS, ADAM_WD, ADAM_STEP = 0.001, 0.9, 0.999, 1e-08, 0.01, 10

N_DEV = 8
LANES = 128
VMEM_LIMIT = 48 * 1024 * 1024
MM_TILE_BUDGET = 32 * 1024 * 1024
MESH = pl.DeviceIdType.MESH

H_UG, H_VG, H_CQ, H_US, H_CKV, H_K1, H_K2, H_COLS = 0, 256, 512, 768, 1024, 1152, 1280, 1408

W_NAMES = ['ln_g', 'ln_b', 'ffn1_w_gate', 'ffn1_w_up', 'ffn1_w_down', 'w_in', 'gmlp_norm_g', 'gmlp_ws', 'gmlp_bs',
           'mla_q_norm_g', 'mla_w_uq', 'mla_kv_norm_g', 'mla_w_ukv', 'ssm_a_re', 'ssm_a_im', 'ssm_b_re', 'ssm_b_im',
           'ssm_c_re', 'ssm_c_im', 'ssm_d', 'ssm_log_dt', 'ssm_glu_w', 'ssm_glu_b', 'mix_norm_g', 'w_out',
           'ffn2_w_gate', 'ffn2_w_up', 'ffn2_w_down']
BIG = {'ffn1_w_gate': 1, 'ffn1_w_up': 1, 'ffn1_w_down': 0, 'w_in': 1, 'mla_w_uq': 1, 'mla_w_ukv': 1,
       'ssm_glu_w': 0, 'w_out': 0, 'ffn2_w_gate': 1, 'ffn2_w_up': 1, 'ffn2_w_down': 0}
BIG_SHAPE = {'ffn1_w_gate': (D_MODEL, D_FF), 'ffn1_w_up': (D_MODEL, D_FF), 'ffn1_w_down': (D_FF, D_MODEL),
             'w_in': (D_MODEL, 1184), 'mla_w_uq': (Q_LORA, 768), 'mla_w_ukv': (KV_LORA, 1024),
             'ssm_glu_w': (SSM_WIDTH, SSM_WIDTH), 'w_out': (D_MODEL, D_MODEL),
             'ffn2_w_gate': (D_MODEL, D_FF), 'ffn2_w_up': (D_MODEL, D_FF), 'ffn2_w_down': (D_FF, D_MODEL)}
LN_NAMES = ['ln_g', 'ln_b']
REPL = [n for n in W_NAMES if n not in BIG and n not in LN_NAMES]


def _pick(n, cands):
    for c in cands:
        if n % c == 0:
            return c
    return n


def _params(sem):
    return pltpu.CompilerParams(dimension_semantics=sem, vmem_limit_bytes=VMEM_LIMIT)


S3_ROWS = 2 * N_STATE // LANES


def _from_s3(ref):
    return jnp.concatenate([ref[:, c, :] for c in range(S3_ROWS)], axis=1)


def _to_s3(ref, val):
    for c in range(S3_ROWS):
        ref[:, c, :] = val[:, c * LANES:(c + 1) * LANES].astype(ref.dtype)


def mm(a, b, *, out_dtype=f32, add=None, add_scale=1.0, a_col0=0, a_s3=False, out_s3=False, grouped=False, b_nt=False, name):
    G = a.shape[0] if grouped else 1
    M = a.shape[1] if grouped else a.shape[0]
    K, N = (b.shape[-1], b.shape[-2]) if b_nt else b.shape[-2:]
    tn = N if out_s3 else _pick(N, (512, 384, 256) if N <= 1536 else (512, 384, 256, 128))
    tk = K if (K <= 2 * D_FF or a_s3) else _pick(K, (1024, 512, 256, 128))
    nk = G * (K // tk)
    assert not grouped or tk == K

    def tile_bytes(tm):
        return 2 * (tm * tk * a.dtype.itemsize + tk * tn * b.dtype.itemsize + tm * tn * 4 * (2 if add is not None else 1))

    tm = next(t for t in (1024, 512, 256, 128, 64, 32, 16, 8) if M % t == 0 and (tile_bytes(t) <= MM_TILE_BUDGET or t == 8))
    assert a_s3 or grouped or (a_col0 % tk == 0 and a_col0 + K <= a.shape[1])
    kb0 = a_col0 // tk
    has_add = add is not None

    def body(*refs):
        if has_add:
            a_ref, b_ref, add_ref, o_ref, acc = refs
        else:
            a_ref, b_ref, o_ref, acc = refs
        k = pl.program_id(2)
        a_val = _from_s3(a_ref) if a_s3 else a_ref[...]
        part = (_nt if b_nt else functools.partial(jnp.dot, preferred_element_type=f32))(a_val.astype(bf16), b_ref[...].astype(bf16))

        def finish(total):
            if has_add:
                total = total + add_scale * add_ref[...]
            if out_s3:
                _to_s3(o_ref, total)
            else:
                o_ref[...] = total.astype(o_ref.dtype)

        if nk == 1:
            finish(part)
        else:
            @pl.when(k == 0)
            def _():
                acc[...] = part

            @pl.when(k > 0)
            def _():
                acc[...] += part

            @pl.when(k == nk - 1)
            def _():
                finish(acc[...])

    if a_s3:
        a_spec = pl.BlockSpec((tm, S3_ROWS, LANES), lambda i, j, k: (i, 0, 0))
    elif grouped:
        a_spec = pl.BlockSpec((None, tm, tk), lambda i, j, k: (k, i, 0))
    else:
        a_spec = pl.BlockSpec((tm, tk), lambda i, j, k: (i, kb0 + k))
    if b_nt:
        b_spec = pl.BlockSpec((None, tn, tk), lambda i, j, k: (k, j, 0)) if grouped else pl.BlockSpec((tn, tk), lambda i, j, k: (j, k))
    else:
        b_spec = pl.BlockSpec((None, tk, tn), lambda i, j, k: (k, 0, j)) if grouped else pl.BlockSpec((tk, tn), lambda i, j, k: (k, j))
    in_specs, ops = [a_spec, b_spec], [a, b]
    if has_add:
        in_specs.append(pl.BlockSpec((tm, tn), lambda i, j, k: (i, j)))
        ops.append(add)
    if out_s3:
        out_spec = pl.BlockSpec((tm, S3_ROWS, LANES), lambda i, j, k: (i, 0, 0))
        out_shape = jax.ShapeDtypeStruct((M, S3_ROWS, LANES), out_dtype)
    else:
        out_spec = pl.BlockSpec((tm, tn), lambda i, j, k: (i, j))
        out_shape = jax.ShapeDtypeStruct((M, N), out_dtype)
    return pl.pallas_call(
        body, grid=(M // tm, N // tn, nk), in_specs=in_specs, out_specs=out_spec, out_shape=out_shape,
        scratch_shapes=[pltpu.VMEM((tm, tn) if nk > 1 else (8, LANES), f32)],
        compiler_params=_params(("parallel", "parallel", "arbitrary")), name=name)(*ops)


def mm_tn(a, b, *, a_col0=0, m_dim=None, a_s3=False, b_s3=False, a_lead=None, name):
    K = a.shape[-2] if a_lead is not None else a.shape[0]
    M = 2 * N_STATE if a_s3 else (a.shape[-1] if m_dim is None else m_dim)
    N = 2 * N_STATE if b_s3 else b.shape[-1]
    tm = M if a_s3 else _pick(M, (H_COLS, 1024, 768, 512, 384, 256, 128))
    tn = N if b_s3 else _pick(N, (H_COLS, 1024, 768, 512, 384, 256))

    def tile_bytes(tk):
        return 2 * tk * (tm * a.dtype.itemsize + tn * b.dtype.itemsize) + 3 * tm * tn * 4

    tk = next(t for t in (2048, 1024, 512, 256, 128, 64, 32, 16) if K % t == 0 and (tile_bytes(t) <= MM_TILE_BUDGET or t == 16))
    nk = K // tk
    assert a_col0 % tm == 0
    mb0 = a_col0 // tm

    def body(a_ref, b_ref, o_ref, acc):
        k = pl.program_id(2)
        a_val = _from_s3(a_ref) if a_s3 else a_ref[...]
        b_val = _from_s3(b_ref) if b_s3 else b_ref[...]
        part = lax.dot_general(a_val.astype(bf16), b_val.astype(bf16), (((0,), (0,)), ((), ())), preferred_element_type=f32)

        @pl.when(k == 0)
        def _():
            acc[...] = part

        @pl.when(k > 0)
        def _():
            acc[...] += part

        @pl.when(k == nk - 1)
        def _():
            o_ref[...] = acc[...]

    s3_spec = pl.BlockSpec((tk, S3_ROWS, LANES), lambda i, j, k: (k, 0, 0))
    if a_s3:
        a_spec = s3_spec
    elif a_lead is not None:
        a_spec = pl.BlockSpec((None, tk, tm), lambda i, j, k: (a_lead, k, i))
    else:
        a_spec = pl.BlockSpec((tk, tm), lambda i, j, k: (k, mb0 + i))
    return pl.pallas_call(
        body, grid=(M // tm, N // tn, nk),
        in_specs=[a_spec, s3_spec if b_s3 else pl.BlockSpec((tk, tn), lambda i, j, k: (k, j))],
        out_specs=pl.BlockSpec((tm, tn), lambda i, j, k: (i, j)),
        out_shape=jax.ShapeDtypeStruct((M, N), f32),
        scratch_shapes=[pltpu.VMEM((tm, tn), f32)],
        compiler_params=_params(("parallel", "parallel", "arbitrary")), name=name)(a, b)


def rowwise(fn, rows, pars, out_rows, out_accs, tm, name):
    first = rows[0]
    if isinstance(first, tuple):
        R = first[0].shape[1] if first[1] == 'lead' else first[0].shape[0]
    else:
        R = first.shape[0]
    assert R % tm == 0, (R, tm, name)
    n_rows, n_pars, n_or, n_oa = len(rows), len(pars), len(out_rows), len(out_accs)

    in_specs, ops = [], []
    for r in rows:
        if isinstance(r, tuple) and r[1] == 'lead':
            arr, _, kk = r
            in_specs.append(pl.BlockSpec((None, tm, arr.shape[2]), lambda i, kk=kk: (kk, i, 0)))
        elif isinstance(r, tuple):
            arr, c0, w = r
            assert c0 % w == 0
            in_specs.append(pl.BlockSpec((tm, w), lambda i, cb=c0 // w: (i, cb)))
        else:
            arr = r
            in_specs.append(pl.BlockSpec((tm, arr.shape[1]), lambda i: (i, 0)))
        ops.append(arr)
    for p in pars:
        in_specs.append(pl.BlockSpec(p.shape, lambda i, nd=p.ndim: (0,) * nd))
        ops.append(p)
    out_specs = [pl.BlockSpec((tm, w), lambda i: (i, 0)) for (w, _) in out_rows]
    out_specs += [pl.BlockSpec(s, lambda i, nd=len(s): (0,) * nd) for s in out_accs]
    out_shape = [jax.ShapeDtypeStruct((R, w), dt) for (w, dt) in out_rows]
    out_shape += [jax.ShapeDtypeStruct(s, f32) for s in out_accs]

    def body(*refs):
        ins = [r[...] for r in refs[:n_rows + n_pars]]
        o_refs = refs[n_rows + n_pars:]
        res = fn(*ins)
        if not isinstance(res, (tuple, list)):
            res = (res,)
        assert len(res) == n_or + n_oa, (len(res), n_or, n_oa, name)
        for o, v in zip(o_refs[:n_or], res[:n_or]):
            o[...] = v.astype(o.dtype)
        if n_oa:
            i = pl.program_id(0)

            @pl.when(i == 0)
            def _():
                for o, v in zip(o_refs[n_or:], res[n_or:]):
                    o[...] = v.astype(f32)

            @pl.when(i > 0)
            def _():
                for o, v in zip(o_refs[n_or:], res[n_or:]):
                    o[...] += v.astype(f32)

    return pl.pallas_call(
        body, grid=(R // tm,), in_specs=in_specs, out_specs=out_specs, out_shape=out_shape,
        compiler_params=_params(("arbitrary",)), name=name)(*ops)


def whole(fn, ins, out_shapes, name):
    def body(*refs):
        res = fn(*[r[...] for r in refs[:len(ins)]])
        for o, v in zip(refs[len(ins):], res):
            o[...] = v

    return pl.pallas_call(body, out_shape=[jax.ShapeDtypeStruct(s, f32) for s in out_shapes], name=name)(*ins)


@jax.custom_vjp
def _bdot(a, b):
    return jnp.dot(a.astype(bf16), b.astype(bf16), preferred_element_type=f32)


def _bdot_fwd(a, b):
    return _bdot(a, b), (a, b)


def _bdot_bwd(res, g):
    a, b = res
    gb = g.astype(bf16)
    da = lax.dot_general(gb, b.astype(bf16), (((1,), (1,)), ((), ())), preferred_element_type=f32)
    db = lax.dot_general(a.astype(bf16), gb, (((0,), (0,)), ((), ())), preferred_element_type=f32)
    return da, db


_bdot.defvjp(_bdot_fwd, _bdot_bwd)


def _ln(z, g, b):
    mu = jnp.mean(z, axis=-1, keepdims=True)
    var = jnp.mean(jnp.square(z - mu), axis=-1, keepdims=True)
    return (z - mu) * lax.rsqrt(var + LN_EPS) * g + b


def _rms_only(x):
    return x * lax.rsqrt(jnp.mean(jnp.square(x), axis=-1, keepdims=True) + RMS_EPS)


def _swiglu(a, b):
    return jax.nn.silu(a) * b


def _gmlp(hu, hv, ng, ws, bsb):
    u = jax.nn.gelu(hu)
    v = jax.nn.gelu(hv)
    lane = lax.broadcasted_iota(jnp.int32, (1, GM_WIDTH), 1)
    masks = [((lane >= GM_HEAD_DIM * h) & (lane < GM_HEAD_DIM * (h + 1))).astype(f32) for h in range(GM_HEADS)]
    mu = jnp.zeros_like(v)
    for m in masks:
        mu = mu + m * (jnp.sum(v * m, axis=-1, keepdims=True) / GM_HEAD_DIM)
    d = v - mu
    var = jnp.zeros_like(v)
    for m in masks:
        var = var + m * (jnp.sum(d * d * m, axis=-1, keepdims=True) / GM_HEAD_DIM)
    vn = d * lax.rsqrt(var + LN_EPS) * ng
    r = lax.broadcasted_iota(jnp.int32, (GM_CHUNK, GM_CHUNK), 0)
    c = lax.broadcasted_iota(jnp.int32, (GM_CHUNK, GM_CHUNK), 1)
    tril = (c <= r).astype(f32)
    z = bsb
    for h, m in enumerate(masks):
        z = z + _bdot(ws[h] * tril, vn * m)
    return u * z


def _mla_prep(cq, ckv, qg, kvg):
    return _rms_only(cq) * qg, _rms_only(ckv) * kvg


def _rope(q1, q2, k1, k2, cos, sin):
    return q1 * cos - q2 * sin, q2 * cos + q1 * sin, k1 * cos - k2 * sin, k2 * cos + k1 * sin


def _mix_post(ya, ob, y1, us, dskip, gluw, glub, gmix):
    y = jax.nn.gelu(y1 + dskip * us)
    yc = y * jax.nn.sigmoid(_bdot(y, gluw) + glub)
    return jnp.concatenate([_rms_only(ya), _rms_only(ob), _rms_only(yc)], axis=1) * gmix


def _ssm_prep(ar, ai, ldt, brT, biT):
    dt = jnp.exp(ldt)
    mag = jnp.exp(ar * dt)
    abr = mag * jnp.cos(ai * dt)
    abi = mag * jnp.sin(ai * dt)
    den = ar * ar + ai * ai
    cr = ((abr - 1.0) * ar + abi * ai) / den
    ci = (abi * ar - (abr - 1.0) * ai) / den
    return abr, abi, cr * brT - ci * biT, cr * biT + ci * brT


ATT_FWD_HEADS = 4


def _att_tile(S):
    return _pick(S, (512, 256, 128))


def _nt(a, b):
    return lax.dot_general(a, b, (((1,), (1,)), ((), ())), preferred_element_type=f32)


def _diag_keep(T):
    krow = lax.broadcasted_iota(jnp.int32, (T, T), 0)
    qcol = lax.broadcasted_iota(jnp.int32, (T, T), 1)
    return qcol >= krow


def grid_call(body, exch, *, grid, in_specs, out_specs, out_shape, scratch_shapes, name, args):
    params = _params(("arbitrary", "arbitrary"))
    if exch is None:
        return pl.pallas_call(body, grid=grid, in_specs=in_specs, out_specs=out_specs, out_shape=out_shape,
                              scratch_shapes=scratch_shapes, compiler_params=params, name=name)(*args)
    gather, xs = exch
    n, n_in, n_out, n_sc = len(xs), len(in_specs), len(out_specs), len(scratch_shapes)

    def riding(*refs):
        ins, x_refs = refs[:n_in], refs[n_in:n_in + n]
        outs, xo_refs = refs[n_in + n:n_in + n + n_out], refs[n_in + n + n_out:n_in + 2 * n + n_out]
        rest = refs[n_in + 2 * n + n_out:]
        scratch, sems = rest[:n_sc], rest[n_sc:]
        h, i = pl.program_id(0), pl.program_id(1)

        @pl.when((h == 0) & (i == 0))
        def _():
            _direct_exchange(gather, x_refs, xo_refs, *sems, start=True)

        body(*ins, *outs, *scratch)

        @pl.when((h == grid[0] - 1) & (i == grid[1] - 1))
        def _():
            _direct_exchange(gather, x_refs, xo_refs, *sems, start=False)

    return pl.pallas_call(
        riding, grid=grid, in_specs=list(in_specs) + [HBM_SPEC] * n, out_specs=list(out_specs) + [HBM_SPEC] * n,
        out_shape=list(out_shape) + _exchange_out_shapes(gather, xs),
        scratch_shapes=list(scratch_shapes) + _exchange_scratch(n), compiler_params=params,
        name=name + ("_gather" if gather else "_scatter"))(*args, *xs)


def flash_fwd(q, k, vT, exch=None):
    Hh, S, _ = q.shape
    T = _att_tile(S)
    HB = ATT_FWD_HEADS

    def body(q_ref, k_ref, vT_ref, o_ref, lse_ref, m_sc, l_sc, acc_sc):
        i = pl.program_id(1)
        m_sc[...] = jnp.full_like(m_sc, NEG_BIG)
        l_sc[...] = jnp.zeros_like(l_sc)
        acc_sc[...] = jnp.zeros_like(acc_sc)

        def block(j, diagonal):
            rows = pl.ds(pl.multiple_of(j * T, T), T)
            for hh in range(HB):
                sT = _nt(k_ref[hh, rows, :], q_ref[hh]) * ATT_SCALE
                if diagonal:
                    sT = jnp.where(_diag_keep(T), sT, NEG_BIG)
                m_old = m_sc[hh]
                m_new = jnp.maximum(m_old, jnp.max(sT, axis=0, keepdims=True))
                alpha = jnp.exp(m_old - m_new)
                pT = jnp.exp(sT - m_new)
                l_sc[hh] = alpha * l_sc[hh] + jnp.sum(pT, axis=0, keepdims=True)
                acc_sc[hh] = alpha * acc_sc[hh] + jnp.dot(vT_ref[hh, :, rows], pT.astype(bf16), preferred_element_type=f32)
                m_sc[hh] = m_new

        def loop_body(j, c):
            block(j, False)
            return c

        lax.fori_loop(0, i, loop_body, 0)
        block(i, True)
        o_ref[...] = acc_sc[...] / l_sc[...]
        lse_ref[...] = m_sc[...] + jnp.log(l_sc[...])

    return grid_call(
        body, exch, grid=(Hh // HB, S // T),
        in_specs=[pl.BlockSpec((HB, T, LANES), lambda h, i: (h, i, 0)), pl.BlockSpec((HB, S, LANES), lambda h, i: (h, 0, 0)),
                  pl.BlockSpec((HB, MLA_V, S), lambda h, i: (h, 0, 0))],
        out_specs=[pl.BlockSpec((HB, MLA_V, T), lambda h, i: (h, 0, i)), pl.BlockSpec((HB, 1, T), lambda h, i: (h, 0, i))],
        out_shape=[jax.ShapeDtypeStruct((Hh, MLA_V, S), f32), jax.ShapeDtypeStruct((Hh, 1, S), f32)],
        scratch_shapes=[pltpu.VMEM((HB, 1, T), f32), pltpu.VMEM((HB, 1, T), f32), pltpu.VMEM((HB, MLA_V, T), f32)],
        name="flash_fwd", args=(q, k, vT))


def flash_bwd(q, k, kT, v, do, lse, delta, exch=None):
    Hh, S, _ = q.shape
    T = _att_tile(S)

    def body(q_ref, do_ref, lse_ref, dl_ref, k_ref, kT_ref, v_ref, dq_ref, dk_ref, dv_ref, dq_sc):
        i = pl.program_id(1)

        @pl.when(i == 0)
        def _():
            dk_ref[...] = jnp.zeros_like(dk_ref)
            dv_ref[...] = jnp.zeros_like(dv_ref)

        qi, doi = q_ref[0], do_ref[0]
        lse_i, dl_i = lse_ref[0], dl_ref[0]
        dq_sc[...] = jnp.zeros_like(dq_sc)

        def block(j, diagonal):
            Th = T // 2
            for half in range(2):
                rows = pl.ds(pl.multiple_of(j * T + half * Th, Th), Th)
                sT = _nt(k_ref[0, rows, :], qi) * ATT_SCALE
                pT = jnp.exp(sT - lse_i)
                if diagonal:
                    krow = half * Th + lax.broadcasted_iota(jnp.int32, (Th, T), 0)
                    qcol = lax.broadcasted_iota(jnp.int32, (Th, T), 1)
                    pT = jnp.where(qcol >= krow, pT, 0.0)
                dpT = _nt(v_ref[0, rows, :], doi)
                dsT = (pT * (dpT - dl_i) * ATT_SCALE).astype(bf16)
                dv_ref[0, rows, :] += jnp.dot(pT.astype(bf16), doi, preferred_element_type=f32)
                dk_ref[0, rows, :] += jnp.dot(dsT, qi, preferred_element_type=f32)
                dq_sc[...] += jnp.dot(kT_ref[0, :, rows], dsT, preferred_element_type=f32)

        def loop_body(j, c):
            block(j, False)
            return c

        lax.fori_loop(0, i, loop_body, 0)
        block(i, True)
        dq_ref[0] = dq_sc[...]

    tile = lambda w: pl.BlockSpec((1, T, w), lambda h, i: (h, i, 0))
    row = pl.BlockSpec((1, 1, T), lambda h, i: (h, 0, i))
    full = lambda w: pl.BlockSpec((1, S, w), lambda h, i: (h, 0, 0))
    return grid_call(
        body, exch, grid=(Hh, S // T),
        in_specs=[tile(LANES), tile(MLA_V), row, row, full(LANES), pl.BlockSpec((1, LANES, S), lambda h, i: (h, 0, 0)), full(MLA_V)],
        out_specs=[pl.BlockSpec((1, LANES, T), lambda h, i: (h, 0, i)), full(LANES), full(MLA_V)],
        out_shape=[jax.ShapeDtypeStruct((Hh, LANES, S), f32), jax.ShapeDtypeStruct((Hh, S, LANES), f32),
                   jax.ShapeDtypeStruct((Hh, S, MLA_V), f32)],
        scratch_shapes=[pltpu.VMEM((LANES, T), f32)],
        name="flash_bwd", args=(q, do, lse, delta, k, kT, v))


def _scan_tile(S):
    return _pick(S, (256, 128, 64, 32, 16, 8))


def scan_fwd(bu3, a16):
    S = bu3.shape[0]
    ts = _scan_tile(S)

    def body(bu_ref, a_ref, o_ref, h_sc):
        @pl.when(pl.program_id(0) == 0)
        def _():
            h_sc[...] = jnp.zeros_like(h_sc)

        ar, ai = a_ref[0:8, :], a_ref[8:16, :]
        a2r, a2i = ar * ar - ai * ai, 2.0 * ar * ai

        def pair(p, carry):
            hr, hi = carry
            t = 2 * p
            x0r, x0i = bu_ref[t, 0:8, :], bu_ref[t, 8:16, :]
            ur = ar * x0r - ai * x0i + bu_ref[t + 1, 0:8, :]
            ui = ar * x0i + ai * x0r + bu_ref[t + 1, 8:16, :]
            o_ref[t, 0:8, :] = ar * hr - ai * hi + x0r
            o_ref[t, 8:16, :] = ar * hi + ai * hr + x0i
            nr = a2r * hr - a2i * hi + ur
            ni = a2r * hi + a2i * hr + ui
            o_ref[t + 1, 0:8, :] = nr
            o_ref[t + 1, 8:16, :] = ni
            return nr, ni

        hr, hi = lax.fori_loop(0, ts // 2, pair, (h_sc[0:8, :], h_sc[8:16, :]), unroll=4)
        h_sc[0:8, :] = hr
        h_sc[8:16, :] = hi

    blk = pl.BlockSpec((ts, 16, LANES), lambda i: (i, 0, 0))
    return pl.pallas_call(
        body, grid=(S // ts,), in_specs=[blk, pl.BlockSpec((16, LANES), lambda i: (0, 0))], out_specs=blk,
        out_shape=jax.ShapeDtypeStruct(bu3.shape, f32), scratch_shapes=[pltpu.VMEM((16, LANES), f32)],
        compiler_params=_params(("arbitrary",)), name="scan_fwd")(bu3, a16)


def scan_bwd(g3, h3, a16):
    S = g3.shape[0]
    ts = _scan_tile(S)
    nb = S // ts

    def body(g_ref, h_ref, a_ref, o_ref, da_ref, lam_sc, da_sc):
        @pl.when(pl.program_id(0) == 0)
        def _():
            lam_sc[...] = jnp.zeros_like(lam_sc)
            da_sc[...] = jnp.zeros_like(da_sc)

        ar, ai = a_ref[0:8, :], a_ref[8:16, :]
        a2r, a2i = ar * ar - ai * ai, 2.0 * ar * ai

        def pair(p, carry):
            lr, li, dar, dai = carry
            t = ts - 1 - 2 * p
            h1r, h1i = h_ref[t, 0:8, :], h_ref[t, 8:16, :]
            h0r, h0i = h_ref[t - 1, 0:8, :], h_ref[t - 1, 8:16, :]
            g1r, g1i = g_ref[t, 0:8, :], g_ref[t, 8:16, :]
            l1r = ar * lr + ai * li + g1r
            l1i = ar * li - ai * lr + g1i
            vr = ar * g1r + ai * g1i + g_ref[t - 1, 0:8, :]
            vi = ar * g1i - ai * g1r + g_ref[t - 1, 8:16, :]
            l0r = a2r * lr + a2i * li + vr
            l0i = a2r * li - a2i * lr + vi
            dar = dar + (lr * h1r + li * h1i) + (l1r * h0r + l1i * h0i)
            dai = dai + (li * h1r - lr * h1i) + (l1i * h0r - l1r * h0i)
            o_ref[t, 0:8, :] = l1r
            o_ref[t, 8:16, :] = l1i
            o_ref[t - 1, 0:8, :] = l0r
            o_ref[t - 1, 8:16, :] = l0i
            return l0r, l0i, dar, dai

        lr, li, dar, dai = lax.fori_loop(
            0, ts // 2, pair, (lam_sc[0:8, :], lam_sc[8:16, :], da_sc[0:8, :], da_sc[8:16, :]), unroll=4)
        lam_sc[0:8, :] = lr
        lam_sc[8:16, :] = li
        da_sc[0:8, :] = dar
        da_sc[8:16, :] = dai
        da_ref[0:8, :] = dar
        da_ref[8:16, :] = dai

    blk = pl.BlockSpec((ts, 16, LANES), lambda i: (nb - 1 - i, 0, 0))
    small = pl.BlockSpec((16, LANES), lambda i: (0, 0))
    return pl.pallas_call(
        body, grid=(nb,), in_specs=[blk, blk, small], out_specs=[blk, small],
        out_shape=[jax.ShapeDtypeStruct(g3.shape, f32), jax.ShapeDtypeStruct((16, LANES), f32)],
        scratch_shapes=[pltpu.VMEM((16, LANES), f32), pltpu.VMEM((16, LANES), f32)],
        compiler_params=_params(("arbitrary",)), name="scan_bwd")(g3, h3, a16)


HBM_SPEC = pl.BlockSpec(memory_space=pltpu.HBM)


def _my_id():
    return 4 * lax.axis_index("x") + 2 * lax.axis_index("y") + lax.axis_index("c")


def all_gather(xs, name):
    n = len(xs)

    def body(*refs):
        x_refs, o_refs = refs[:n], refs[n:2 * n]
        send_sems, recv_sems, local_sems = refs[2 * n:]
        x, y, c = lax.axis_index("x"), lax.axis_index("y"), lax.axis_index("c")
        me, sibling = (x, y, c), (x, y, 1 - c)
        chips = [(1 - x, y), (x, 1 - y), (1 - x, 1 - y)]

        def slot(o, p):
            return o.at[4 * p[0] + 2 * p[1] + p[2]]

        def copy(a, k, block, to, src=None):
            o = o_refs[a]
            return pltpu.make_async_remote_copy(
                src_ref=slot(o, block) if src is None else src, dst_ref=slot(o, block),
                send_sem=send_sems.at[7 * a + k], recv_sem=recv_sems.at[7 * a + k], device_id=to, device_id_type=MESH)

        own, sends = [], []
        for a in range(n):
            mine = pltpu.make_async_copy(x_refs[a], slot(o_refs[a], me), local_sems.at[a])
            mine.start()
            own.append(mine)
            first = [copy(a, 0, me, sibling, src=x_refs[a])]
            first += [copy(a, 1 + j, me, (*chip, c), src=x_refs[a]) for j, chip in enumerate(chips)]
            for cp in first:
                cp.start()
            sends += first
        for a in range(n):
            for j, chip in enumerate(chips):
                copy(a, 1 + j, (*chip, c), me).wait_recv()
                fwd = copy(a, 4 + j, (*chip, c), sibling)
                fwd.start()
                sends.append(fwd)
        for a in range(n):
            copy(a, 0, sibling, me).wait_recv()
            for j, chip in enumerate(chips):
                copy(a, 4 + j, (*chip, 1 - c), me).wait_recv()
        for cp in sends:
            cp.wait_send()
        for cp in own:
            cp.wait()

    return pl.pallas_call(
        body, out_shape=[jax.ShapeDtypeStruct((N_DEV,) + v.shape, v.dtype) for v in xs],
        in_specs=[HBM_SPEC] * n, out_specs=[HBM_SPEC] * n,
        scratch_shapes=[pltpu.SemaphoreType.DMA((7 * n,)), pltpu.SemaphoreType.DMA((7 * n,)), pltpu.SemaphoreType.DMA((n,))],
        name=name)(*xs)


def _direct_exchange(gather, x_refs, o_refs, send_sems, recv_sems, local_sems, start):
    x, y, c = lax.axis_index("x"), lax.axis_index("y"), lax.axis_index("c")
    my = 4 * x + 2 * y + c
    for a, (x_ref, o_ref) in enumerate(zip(x_refs, o_refs)):
        mine = pltpu.make_async_copy(x_ref if gather else x_ref.at[my], o_ref.at[my], local_sems.at[a])
        if start:
            mine.start()
        else:
            mine.wait()
        for k in range(1, N_DEV):
            px = 1 - x if k & 4 else x
            py = 1 - y if k & 2 else y
            pc = 1 - c if k & 1 else c
            pid = 4 * px + 2 * py + pc
            src = x_ref if gather else x_ref.at[pid]
            sems = dict(send_sem=send_sems.at[7 * a + k - 1], recv_sem=recv_sems.at[7 * a + k - 1],
                        device_id=(px, py, pc), device_id_type=MESH)
            if start:
                pltpu.make_async_remote_copy(src_ref=src, dst_ref=o_ref.at[my], **sems).start()
            else:
                pltpu.make_async_remote_copy(src_ref=src, dst_ref=o_ref.at[my], **sems).wait_send()
                pltpu.make_async_remote_copy(src_ref=src, dst_ref=o_ref.at[pid], **sems).wait_recv()


def _exchange_scratch(n):
    return [pltpu.SemaphoreType.DMA((7 * n,)), pltpu.SemaphoreType.DMA((7 * n,)), pltpu.SemaphoreType.DMA((n,))]


def _exchange_out_shapes(gather, xs):
    return [jax.ShapeDtypeStruct(((N_DEV,) + v.shape) if gather else v.shape, v.dtype) for v in xs]


def all_to_all(xs, name):
    n = len(xs)

    def body(*refs):
        ex = (False, refs[:n], refs[n:2 * n], *refs[2 * n:])
        _direct_exchange(*ex, start=True)
        _direct_exchange(*ex, start=False)

    return pl.pallas_call(
        body, out_shape=_exchange_out_shapes(False, xs), in_specs=[HBM_SPEC] * n, out_specs=[HBM_SPEC] * n,
        scratch_shapes=_exchange_scratch(n), name=name)(*xs)


def sum_slots(g8, name):
    R, C = g8.shape[1:]

    def fn(*tiles):
        tot = tiles[0].astype(f32)
        for t in tiles[1:]:
            tot = tot + t.astype(f32)
        return tot

    return rowwise(fn, [(g8, 'lead', k) for k in range(N_DEV)], [], [(C, f32)], [], _pick(R, (256, 128, 64, 32, 16, 8)), name)[0]


PACK_W = 1024


def _pad_rows(flat, mult):
    n = flat.shape[0]
    tot = -(-n // (PACK_W * mult)) * PACK_W * mult
    return jnp.pad(flat, (0, tot - n)).reshape(tot // PACK_W, PACK_W)


BF16_TILE_ROWS = 16


def _pad_tile_rows(a, axis):
    pad = [(0, 0)] * a.ndim
    pad[axis] = (0, -a.shape[axis] % BF16_TILE_ROWS)
    return jnp.pad(a, pad)


def _shard_shape(name):
    r, c = BIG_SHAPE[name]
    return (r // N_DEV, c) if BIG[name] == 0 else (r, c // N_DEV)


def _pack_rows(name):
    r, c = _shard_shape(name)
    assert (r * c) % PACK_W == 0
    return r * c // PACK_W


TRANSPOSED = ('ffn1_w_gate', 'ffn1_w_up', 'ffn2_w_gate', 'ffn2_w_up')
assert all(BIG_SHAPE[n][0] == PACK_W and BIG[n] == 1 for n in TRANSPOSED)


def _shard_to_rows(name, shard):
    return (shard.T if name in TRANSPOSED else shard).reshape(_pack_rows(name), PACK_W)


def _rows_to_shard(name, rows):
    r, c = _shard_shape(name)
    return rows.reshape(c, r).T if name in TRANSPOSED else rows.reshape(r, c)


def _split_for_devices(name, full):
    r, c = BIG_SHAPE[name]
    if BIG[name] == 0 or name in TRANSPOSED:
        return full.reshape(N_DEV, _pack_rows(name), PACK_W)
    return full.reshape(r, N_DEV, c // N_DEV).transpose(1, 0, 2).reshape(N_DEV, _pack_rows(name), PACK_W)


def _join_from_devices(name, parts):
    r, c = BIG_SHAPE[name]
    if name in TRANSPOSED:
        return parts.reshape(c, r)
    if BIG[name] == 0:
        return parts.reshape(r, c)
    return parts.reshape(N_DEV, r, c // N_DEV).transpose(1, 0, 2).reshape(r, c)


def _row_tile(S, want):
    return _pick(S, tuple(t for t in (512, 256, 128, 64, 32, 16) if t <= want))


def resid_ln(x, f, g, b, scale, name):
    D = x.shape[1]

    def fn(x, f, g, b):
        z = ALPHA * x + scale * f
        xo = _ln(z, g, b)
        return z, xo, xo

    return rowwise(fn, [x, f], [g, b], [(D, f32), (D, f32), (D, bf16)], [], _row_tile(x.shape[0], 512), name)


def ln_bwd(z, g, b, dxo, scale, name):
    D = z.shape[1]

    def fn(z, dxo, g, b):
        _, vjp = jax.vjp(_ln, z, g, b)
        dz, dg, db = vjp(dxo)
        return dz, scale * dz, dg, db

    return rowwise(fn, [z, dxo], [g, b], [(D, f32), (D, bf16)], [(1, D), (1, D)], _row_tile(z.shape[0], 512), name)


FF_TILE = 256


def ffn_up_act(xb, guT):
    M, K = xb.shape
    tm = _pick(M, (1024, 512, 256, 128, 64, 32, 16))

    def body(x_ref, wg_ref, wu_ref, ab_ref, h_ref):
        x = x_ref[...]
        a = _nt(x, wg_ref[...])
        b = _nt(x, wu_ref[...])
        ab_ref[0] = a.astype(bf16)
        ab_ref[1] = b.astype(bf16)
        h_ref[...] = _swiglu(a, b).astype(bf16)

    wspecs = [pl.BlockSpec((None, FF_TILE, K), lambda i, j, which=which: (which, j, 0)) for which in (0, 1)]
    return pl.pallas_call(
        body, grid=(M // tm, D_FF // FF_TILE),
        in_specs=[pl.BlockSpec((tm, K), lambda i, j: (i, 0)), *wspecs],
        out_specs=[pl.BlockSpec((2, tm, FF_TILE), lambda i, j: (0, i, j)), pl.BlockSpec((tm, FF_TILE), lambda i, j: (i, j))],
        out_shape=[jax.ShapeDtypeStruct((2, M, D_FF), bf16), jax.ShapeDtypeStruct((M, D_FF), bf16)],
        compiler_params=_params(("parallel", "parallel")), name="ffn_up_act")(xb, guT, guT)


def ffn_down_dx_act(dzs, wd, ab):
    M, K = dzs.shape
    tm = _pick(M, (1024, 512, 256, 128, 64, 32, 16))

    def body(dz_ref, w_ref, ab_ref, dab_ref, h_ref):
        dh = _nt(dz_ref[...], w_ref[...])
        h, vjp = jax.vjp(_swiglu, ab_ref[0].astype(f32), ab_ref[1].astype(f32))
        da, db = vjp(dh)
        dab_ref[0] = da.astype(bf16)
        dab_ref[1] = db.astype(bf16)
        h_ref[...] = h.astype(bf16)

    pair = pl.BlockSpec((2, tm, FF_TILE), lambda i, j: (0, i, j))
    return pl.pallas_call(
        body, grid=(M // tm, D_FF // FF_TILE),
        in_specs=[pl.BlockSpec((tm, K), lambda i, j: (i, 0)), pl.BlockSpec((FF_TILE, K), lambda i, j: (j, 0)), pair],
        out_specs=[pair, pl.BlockSpec((tm, FF_TILE), lambda i, j: (i, j))],
        out_shape=[jax.ShapeDtypeStruct((2, M, D_FF), bf16), jax.ShapeDtypeStruct((M, D_FF), bf16)],
        compiler_params=_params(("parallel", "parallel")), name="ffn_down_dx_act")(dzs, wd, ab)


def ffn_fwd(x, xb, w, g, b):
    ab, h = ffn_up_act(xb, w['guT'])
    f = mm(h, w['d'], name="ffn_down")
    z, xo, xob = resid_ln(x, f, g, b, 0.5, "ffn_ln")
    return xo, xob, (xb, ab, z)


def ffn_bwd(dxo, res, w, g, b):
    xb, ab, z = res
    dz, dzs, dg, db = ln_bwd(z, g, b, dxo, 0.5, "ffn_ln_bwd")
    dab, h = ffn_down_dx_act(dzs, w['d'], ab)
    dwd = mm_tn(h, dzs, name="ffn_down_dw")
    dwgT = mm_tn(dab, xb, a_lead=0, name="ffn_gate_dw")
    dwuT = mm_tn(dab, xb, a_lead=1, name="ffn_up_dw")
    dx = mm(dab, w['guT'], add=dz, add_scale=ALPHA, grouped=True, name="ffn_up_dx")
    return dx, dwgT, dwuT, dwd, dg, db


def _heads_first(a, width):
    return a.reshape(a.shape[0], MLA_HEADS, width)


def mixer_fwd(x, xb, w, g, b, cos8, sin8, exch=None):
    S = x.shape[0]
    H = mm(xb, w['in'], name="mix_in")
    ya = rowwise(_gmlp, [(H, H_UG, 256), (H, H_VG, 256)], [w['gm_ng'], w['gm_ws'], w['gm_bsb']], [(GM_WIDTH, f32)], [],
                 GM_CHUNK, "gmlp")[0]
    cqn, ckvn = rowwise(_mla_prep, [(H, H_CQ, Q_LORA), (H, H_CKV, KV_LORA)], [w['qg'], w['kvg']],
                        [(Q_LORA, bf16), (KV_LORA, bf16)], [], _row_tile(S, 512), "mla_prep")
    qraw = mm(cqn, w['uq'], name="mla_uq")
    kv = mm(ckvn, w['ukv'], name="mla_ukv")
    q1, q2, k1, k2 = rowwise(_rope, [(qraw, 512, LANES), (qraw, 640, LANES), (H, H_K1, LANES), (H, H_K2, LANES), cos8, sin8],
                             [], [(LANES, f32)] * 4, [], _row_tile(S, 512), "rope")
    zpad = jnp.zeros((S, MLA_HEADS, LANES - MLA_NOPE - MLA_ROPE), f32)
    qp = jnp.concatenate([_heads_first(qraw[:, :512], 64), _heads_first(q1, ROPE_HALF), _heads_first(q2, ROPE_HALF), zpad], axis=2)
    k1b = jnp.broadcast_to(k1[:, None, :ROPE_HALF], (S, MLA_HEADS, ROPE_HALF))
    k2b = jnp.broadcast_to(k2[:, None, :ROPE_HALF], (S, MLA_HEADS, ROPE_HALF))
    kp = jnp.concatenate([_heads_first(kv[:, :512], 64), k1b, k2b, zpad], axis=2)
    qp = qp.transpose(1, 0, 2).astype(bf16)
    kp = kp.transpose(1, 0, 2).astype(bf16)
    v3 = _heads_first(kv[:, 512:], 64).astype(bf16)
    vp = v3.transpose(1, 0, 2)
    oT, lse, *received = flash_fwd(qp, kp, v3.transpose(1, 2, 0), exch)
    ob = oT.transpose(2, 0, 1).reshape(S, MLA_HEADS * MLA_V)
    bu3 = mm(H, w['ssm_wb'], a_col0=H_US, out_s3=True, name="ssm_bu")
    hs3 = scan_fwd(bu3, w['ssm_a16'])
    y1 = mm(hs3, w['ssm_wc'], a_s3=True, name="ssm_c")
    y = rowwise(_mix_post, [ya, ob, y1, (H, H_US, SSM_WIDTH)], [w['ssm_d'], w['glu_w'], w['glu_b'], w['gmix']],
                [(D_MODEL, bf16)], [], _row_tile(S, 256), "mix_post")[0]
    f = mm(y, w['out'], name="mix_out")
    z, xo, xob = resid_ln(x, f, g, b, 1.0, "mix_ln")
    return xo, xob, (xb, H, cqn, ckvn, qp, kp, vp, lse, hs3, ya, ob, y1, y, z), received


def mixer_bwd(dxo, res, w, g, b, cos8, sin8, exch=None):
    xb, H, cqn, ckvn, qp, kp, vp, lse, hs3, ya, ob, y1, y, z = res
    S = z.shape[0]
    gr = {}
    dz, dzs, gr['ln_g'], gr['ln_b'] = ln_bwd(z, g, b, dxo, 1.0, "mix_ln_bwd")
    gr['w_out'] = mm_tn(y, dzs, name="mix_out_dw")
    dy = mm(dzs, w['out'], b_nt=True, name="mix_out_dx")

    def post_bwd(ya, ob, y1, us, dy, dskip, gluw, glub, gmix):
        _, vjp = jax.vjp(_mix_post, ya, ob, y1, us, dskip, gluw, glub, gmix)
        dya, dob, dy1, dus, *dpars = vjp(dy)
        prod = dob * ob
        col = lax.broadcasted_iota(jnp.int32, (1, MLA_HEADS * MLA_V), 1)
        lane = lax.broadcasted_iota(jnp.int32, (1, LANES), 1)
        delta = jnp.zeros((prod.shape[0], LANES), f32)
        for h in range(MLA_HEADS):
            in_head = ((col >= MLA_V * h) & (col < MLA_V * (h + 1))).astype(f32)
            delta = jnp.where(lane == h, jnp.sum(prod * in_head, axis=-1, keepdims=True), delta)
        return (dya, dob, dy1, dus, delta, *dpars)

    dya, dob, dy1, dus_skip, delta, gr['ssm_d'], gr['ssm_glu_w'], gr['ssm_glu_b'], gr['mix_norm_g'] = rowwise(
        post_bwd, [ya, ob, y1, (H, H_US, SSM_WIDTH), dy], [w['ssm_d'], w['glu_w'], w['glu_b'], w['gmix']],
        [(GM_WIDTH, f32), (MLA_HEADS * MLA_V, f32), (SSM_WIDTH, f32), (SSM_WIDTH, f32), (LANES, f32)],
        [(1, SSM_WIDTH), (SSM_WIDTH, SSM_WIDTH), (1, SSM_WIDTH), (1, D_MODEL)], _row_tile(S, 256), "mix_post_bwd")

    gr['ssm_wc'] = mm_tn(hs3, dy1, a_s3=True, name="ssm_c_dw")
    dhs3 = mm(dy1, w['ssm_wc'], out_s3=True, b_nt=True, name="ssm_c_dx")
    dbu3, gr['ssm_a16'] = scan_bwd(dhs3, hs3, w['ssm_a16'])
    gr['ssm_wb'] = mm_tn(H, dbu3, a_col0=H_US, m_dim=SSM_WIDTH, b_s3=True, name="ssm_bu_dw")
    dus = mm(dbu3, w['ssm_wb'], add=dus_skip, add_scale=1.0, a_s3=True, b_nt=True, name="ssm_bu_dx")

    do = _heads_first(dob, MLA_V).transpose(1, 0, 2)
    delta = delta[:, :MLA_HEADS].T.reshape(MLA_HEADS, 1, S)
    dqT, dkp, dvp, *received = flash_bwd(qp, kp, kp.transpose(0, 2, 1), vp, do.astype(bf16), lse, delta, exch)
    dqp = dqT.transpose(2, 0, 1)
    dkp = dkp.transpose(1, 0, 2)
    dv = dvp.transpose(1, 0, 2).reshape(S, MLA_HEADS * MLA_V)
    lane_pad = ((0, 0), (0, LANES - ROPE_HALF))
    dq1r = dqp[:, :, 64:80].reshape(S, LANES)
    dq2r = dqp[:, :, 80:96].reshape(S, LANES)
    dk1r = jnp.pad(jnp.sum(dkp[:, :, 64:80], axis=1), lane_pad)
    dk2r = jnp.pad(jnp.sum(dkp[:, :, 80:96], axis=1), lane_pad)

    def rope_bwd(d1, d2, d3, d4, cos, sin):
        return d1 * cos + d2 * sin, d2 * cos - d1 * sin, d3 * cos + d4 * sin, d4 * cos - d3 * sin

    dq1, dq2, dk1, dk2 = rowwise(rope_bwd, [dq1r, dq2r, dk1r, dk2r, cos8, sin8], [], [(LANES, f32)] * 4, [],
                                 _row_tile(S, 512), "rope_bwd")
    dqraw = jnp.concatenate([dqp[:, :, :64].reshape(S, 512), dq1, dq2], axis=1).astype(bf16)
    dkv = jnp.concatenate([dkp[:, :, :64].reshape(S, 512), dv], axis=1).astype(bf16)
    gr['uq'] = mm_tn(cqn, dqraw, name="mla_uq_dw")
    dcqn = mm(dqraw, w['uq'], b_nt=True, name="mla_uq_dx")
    gr['ukv'] = mm_tn(ckvn, dkv, name="mla_ukv_dw")
    dckvn = mm(dkv, w['ukv'], b_nt=True, name="mla_ukv_dx")

    def prep_bwd(cq, ckv, d1, d2, qg, kvg):
        _, vjp = jax.vjp(_mla_prep, cq, ckv, qg, kvg)
        return vjp((d1, d2))

    dcq, dckv, gr['mla_q_norm_g'], gr['mla_kv_norm_g'] = rowwise(
        prep_bwd, [(H, H_CQ, Q_LORA), (H, H_CKV, KV_LORA), dcqn, dckvn], [w['qg'], w['kvg']],
        [(Q_LORA, f32), (KV_LORA, f32)], [(1, Q_LORA), (1, KV_LORA)], _row_tile(S, 256), "mla_prep_bwd")

    def gmlp_bwd(hu, hv, dya, ng, ws, bsb):
        _, vjp = jax.vjp(_gmlp, hu, hv, ng, ws, bsb)
        return vjp(dya)

    dhu, dhv, gr['gmlp_norm_g'], gr['gmlp_ws'], gr['gm_bsb'] = rowwise(
        gmlp_bwd, [(H, H_UG, 256), (H, H_VG, 256), dya], [w['gm_ng'], w['gm_ws'], w['gm_bsb']],
        [(GM_WIDTH, f32), (GM_WIDTH, f32)], [(1, GM_WIDTH), (GM_HEADS, GM_CHUNK, GM_CHUNK), (GM_CHUNK, GM_WIDTH)],
        GM_CHUNK, "gmlp_bwd")

    dH = jnp.concatenate([dhu, dhv, dcq, dus, dckv, dk1, dk2], axis=1).astype(bf16)
    gr['in'] = mm_tn(xb, dH, name="mix_in_dw")
    dx = mm(dH, w['in'], add=dz, add_scale=ALPHA, b_nt=True, name="mix_in_dx")
    return dx, gr, received


def _block_diag(blocks):
    G, a, b = blocks.shape
    eye = jnp.eye(G, dtype=blocks.dtype)
    return (eye[:, None, :, None] * blocks[:, :, None, :]).reshape(G * a, G * b)


def _diag_blocks(mat, G):
    a, b = mat.shape[0] // G, mat.shape[1] // G
    m4 = mat.reshape(G, a, G, b)
    eye = jnp.eye(G, dtype=mat.dtype)
    return jnp.sum(m4 * eye[:, None, :, None], axis=2)


def prep_layer(W, rep, l):
    w = {}
    for f in ('ffn1', 'ffn2'):
        w[f] = {'guT': jnp.stack([W[f + '_w_gate'], W[f + '_w_up']]), 'd': W[f + '_w_down']}
    wi = W['w_in']
    z112 = jnp.zeros((D_MODEL, LANES - ROPE_HALF), wi.dtype)
    w['in'] = jnp.concatenate([wi[:, :768], wi[:, 928:1184], wi[:, 768:896], wi[:, 896:912], z112, wi[:, 912:928], z112], axis=1)
    uq = W['mla_w_uq'].reshape(Q_LORA, MLA_HEADS, MLA_NOPE + MLA_ROPE)
    w['uq'] = jnp.concatenate([uq[:, :, :64].reshape(Q_LORA, 512), uq[:, :, 64:80].reshape(Q_LORA, LANES),
                               uq[:, :, 80:96].reshape(Q_LORA, LANES)], axis=1)
    ukv = W['mla_w_ukv'].reshape(KV_LORA, MLA_HEADS, MLA_NOPE + MLA_V)
    w['ukv'] = jnp.concatenate([ukv[:, :, :64].reshape(KV_LORA, 512), ukv[:, :, 64:].reshape(KV_LORA, 512)], axis=1)
    w['out'] = W['w_out']
    w['glu_w'] = W['ssm_glu_w']
    w['gm_ng'] = rep['gmlp_norm_g'][l].reshape(1, GM_WIDTH)
    w['gm_ws'] = rep['gmlp_ws'][l]
    w['gm_bsb'] = jnp.repeat(rep['gmlp_bs'][l].T, GM_HEAD_DIM, axis=1)
    w['qg'] = rep['mla_q_norm_g'][l].reshape(1, Q_LORA)
    w['kvg'] = rep['mla_kv_norm_g'][l].reshape(1, KV_LORA)
    w['ssm_d'] = rep['ssm_d'][l].reshape(1, SSM_WIDTH)
    w['glu_b'] = rep['ssm_glu_b'][l].reshape(1, SSM_WIDTH)
    w['gmix'] = rep['mix_norm_g'][l].reshape(1, D_MODEL)
    ar = rep['ssm_a_re'][l].reshape(1, N_STATE)
    ai = rep['ssm_a_im'][l].reshape(1, N_STATE)
    ldt = jnp.repeat(rep['ssm_log_dt'][l], SSM_STATE).reshape(1, N_STATE)
    brT = rep['ssm_b_re'][l].transpose(2, 0, 1).reshape(SSM_GROUP_CH, N_STATE)
    biT = rep['ssm_b_im'][l].transpose(2, 0, 1).reshape(SSM_GROUP_CH, N_STATE)
    w['ssm_prep_in'] = (ar, ai, ldt, brT, biT)
    abr, abi, bbrT, bbiT = whole(_ssm_prep, w['ssm_prep_in'], [(1, N_STATE)] * 2 + [(SSM_GROUP_CH, N_STATE)] * 2, "ssm_prep")
    w['ssm_a16'] = jnp.concatenate([abr.reshape(8, LANES), abi.reshape(8, LANES)], axis=0)

    def to_gcp(t):
        return t.reshape(SSM_GROUP_CH, SSM_GROUPS, SSM_STATE).transpose(1, 0, 2)

    w['ssm_wb'] = jnp.concatenate([_block_diag(to_gcp(bbrT)), _block_diag(to_gcp(bbiT))], axis=1).astype(bf16)
    cre = rep['ssm_c_re'][l].transpose(0, 2, 1)
    cim = rep['ssm_c_im'][l].transpose(0, 2, 1)
    w['ssm_wc'] = jnp.concatenate([_block_diag(cre), -_block_diag(cim)], axis=0).astype(bf16)
    return w


def unprep_grads(gr, w):
    out = {}
    for k in ('ln_g', 'ln_b', 'w_out', 'mla_q_norm_g', 'mla_kv_norm_g', 'ssm_glu_w', 'gmlp_ws'):
        out[k] = gr[k]
    out['gmlp_norm_g'] = gr['gmlp_norm_g'].reshape(GM_WIDTH)
    out['mla_q_norm_g'] = gr['mla_q_norm_g'].reshape(Q_LORA)
    out['mla_kv_norm_g'] = gr['mla_kv_norm_g'].reshape(KV_LORA)
    out['ssm_d'] = gr['ssm_d'].reshape(SSM_GROUPS, SSM_GROUP_CH)
    out['ssm_glu_b'] = gr['ssm_glu_b'].reshape(SSM_WIDTH)
    out['mix_norm_g'] = gr['mix_norm_g'].reshape(D_MODEL)
    out['gmlp_bs'] = gr['gm_bsb'].reshape(GM_CHUNK, GM_HEADS, GM_HEAD_DIM).sum(axis=-1).T
    d = gr['in']
    out['w_in'] = jnp.concatenate([d[:, :768], d[:, H_CKV:H_CKV + KV_LORA], d[:, H_K1:H_K1 + ROPE_HALF],
                                   d[:, H_K2:H_K2 + ROPE_HALF], d[:, H_US:H_US + SSM_WIDTH]], axis=1)
    d = gr['uq']
    out['mla_w_uq'] = jnp.concatenate([d[:, :512].reshape(Q_LORA, MLA_HEADS, 64), d[:, 512:640].reshape(Q_LORA, MLA_HEADS, ROPE_HALF),
                                       d[:, 640:768].reshape(Q_LORA, MLA_HEADS, ROPE_HALF)], axis=2).reshape(Q_LORA, 768)
    d = gr['ukv']
    out['mla_w_ukv'] = jnp.concatenate([d[:, :512].reshape(KV_LORA, MLA_HEADS, 64), d[:, 512:].reshape(KV_LORA, MLA_HEADS, 64)],
                                       axis=2).reshape(KV_LORA, 1024)
    dwc = gr['ssm_wc']
    out['ssm_c_re'] = _diag_blocks(dwc[:N_STATE], SSM_GROUPS).transpose(0, 2, 1)
    out['ssm_c_im'] = -_diag_blocks(dwc[N_STATE:], SSM_GROUPS).transpose(0, 2, 1)
    dwb = gr['ssm_wb']

    def from_blocks(m):
        return _diag_blocks(m, SSM_GROUPS).transpose(1, 0, 2).reshape(SSM_GROUP_CH, N_STATE)

    dbbrT, dbbiT = from_blocks(dwb[:, :N_STATE]), from_blocks(dwb[:, N_STATE:])
    da16 = gr['ssm_a16']
    dabr, dabi = da16[0:8].reshape(1, N_STATE), da16[8:16].reshape(1, N_STATE)

    def prep_bwd(ar, ai, ldt, brT, biT, d1, d2, d3, d4):
        _, vjp = jax.vjp(_ssm_prep, ar, ai, ldt, brT, biT)
        return vjp((d1, d2, d3, d4))

    dar, dai, dldt, dbrT, dbiT = whole(prep_bwd, w['ssm_prep_in'] + (dabr, dabi, dbbrT, dbbiT),
                                       [(1, N_STATE)] * 3 + [(SSM_GROUP_CH, N_STATE)] * 2, "ssm_prep_bwd")
    out['ssm_a_re'] = dar.reshape(SSM_GROUPS, SSM_STATE)
    out['ssm_a_im'] = dai.reshape(SSM_GROUPS, SSM_STATE)
    out['ssm_log_dt'] = dldt.reshape(SSM_GROUPS, SSM_STATE).sum(axis=-1)
    out['ssm_b_re'] = dbrT.reshape(SSM_GROUP_CH, SSM_GROUPS, SSM_STATE).transpose(1, 2, 0)
    out['ssm_b_im'] = dbiT.reshape(SSM_GROUP_CH, SSM_GROUPS, SSM_STATE).transpose(1, 2, 0)
    return out


def adamw(w, g, m, v, name):
    R, C = w.shape

    def fn(w, g, m, v):
        m = ADAM_B1 * m + (1.0 - ADAM_B1) * g
        v = ADAM_B2 * v + (1.0 - ADAM_B2) * jnp.square(g)
        m_hat = m / (1.0 - ADAM_B1 ** ADAM_STEP)
        v_hat = v / (1.0 - ADAM_B2 ** ADAM_STEP)
        delta = -ADAM_LR * (m_hat / (jnp.sqrt(v_hat) + ADAM_EPS) + ADAM_WD * w)
        return delta, m, v

    return rowwise(fn, [w, g, m, v], [], [(C, f32)] * 3, [], _pick(R, (256, 128, 64, 32, 16, 8)), name)


def kernel(x, positions, ln_g, ln_b, ffn1_w_gate, ffn1_w_up, ffn1_w_down, w_in, gmlp_norm_g, gmlp_ws, gmlp_bs, mla_q_norm_g, mla_w_uq, mla_kv_norm_g, mla_w_ukv, ssm_a_re, ssm_a_im, ssm_b_re, ssm_b_im, ssm_c_re, ssm_c_im, ssm_d, ssm_log_dt, ssm_glu_w, ssm_glu_b, mix_norm_g, w_out, ffn2_w_gate, ffn2_w_up, ffn2_w_down, loss_target, m_ln_g, m_ln_b, m_ffn1_w_gate, m_ffn1_w_up, m_ffn1_w_down, m_w_in, m_gmlp_norm_g, m_gmlp_ws, m_gmlp_bs, m_mla_q_norm_g, m_mla_w_uq, m_mla_kv_norm_g, m_mla_w_ukv, m_ssm_a_re, m_ssm_a_im, m_ssm_b_re, m_ssm_b_im, m_ssm_c_re, m_ssm_c_im, m_ssm_d, m_ssm_log_dt, m_ssm_glu_w, m_ssm_glu_b, m_mix_norm_g, m_w_out, m_ffn2_w_gate, m_ffn2_w_up, m_ffn2_w_down, v_ln_g, v_ln_b, v_ffn1_w_gate, v_ffn1_w_up, v_ffn1_w_down, v_w_in, v_gmlp_norm_g, v_gmlp_ws, v_gmlp_bs, v_mla_q_norm_g, v_mla_w_uq, v_mla_kv_norm_g, v_mla_w_ukv, v_ssm_a_re, v_ssm_a_im, v_ssm_b_re, v_ssm_b_im, v_ssm_c_re, v_ssm_c_im, v_ssm_d, v_ssm_log_dt, v_ssm_glu_w, v_ssm_glu_b, v_mix_norm_g, v_w_out, v_ffn2_w_gate, v_ffn2_w_up, v_ffn2_w_down):
    Wp = dict(zip(W_NAMES, (ln_g, ln_b, ffn1_w_gate, ffn1_w_up, ffn1_w_down, w_in, gmlp_norm_g, gmlp_ws, gmlp_bs, mla_q_norm_g, mla_w_uq, mla_kv_norm_g, mla_w_ukv, ssm_a_re, ssm_a_im, ssm_b_re, ssm_b_im, ssm_c_re, ssm_c_im, ssm_d, ssm_log_dt, ssm_glu_w, ssm_glu_b, mix_norm_g, w_out, ffn2_w_gate, ffn2_w_up, ffn2_w_down)))
    Mp = dict(zip(W_NAMES, (m_ln_g, m_ln_b, m_ffn1_w_gate, m_ffn1_w_up, m_ffn1_w_down, m_w_in, m_gmlp_norm_g, m_gmlp_ws, m_gmlp_bs, m_mla_q_norm_g, m_mla_w_uq, m_mla_kv_norm_g, m_mla_w_ukv, m_ssm_a_re, m_ssm_a_im, m_ssm_b_re, m_ssm_b_im, m_ssm_c_re, m_ssm_c_im, m_ssm_d, m_ssm_log_dt, m_ssm_glu_w, m_ssm_glu_b, m_mix_norm_g, m_w_out, m_ffn2_w_gate, m_ffn2_w_up, m_ffn2_w_down)))
    Vp = dict(zip(W_NAMES, (v_ln_g, v_ln_b, v_ffn1_w_gate, v_ffn1_w_up, v_ffn1_w_down, v_w_in, v_gmlp_norm_g, v_gmlp_ws, v_gmlp_bs, v_mla_q_norm_g, v_mla_w_uq, v_mla_kv_norm_g, v_mla_w_ukv, v_ssm_a_re, v_ssm_a_im, v_ssm_b_re, v_ssm_b_im, v_ssm_c_re, v_ssm_c_im, v_ssm_d, v_ssm_log_dt, v_ssm_glu_w, v_ssm_glu_b, v_mix_norm_g, v_w_out, v_ffn2_w_gate, v_ffn2_w_up, v_ffn2_w_down)))
    S = x.shape[1]
    my = _my_id()

    def shard_rows(l):
        return [_pad_tile_rows(_shard_to_rows(n, Wp[n][l].astype(bf16)), 0) for n in BIG]

    def joined(got):
        return {n: _join_from_devices(n, g[:, :_pack_rows(n)]) for n, g in zip(BIG, got)}

    ln_flat = jnp.concatenate([Wp[n].reshape(-1) for n in LN_NAMES])
    *got, ln_all = all_gather(shard_rows(0) + [_pad_rows(ln_flat, 8)], "gather_weights")
    ln_all = ln_all.reshape(N_DEV, -1)
    lnsz = DEPTH * 3 * (D_MODEL // N_DEV)
    ln_full = {}
    for t, n in enumerate(LN_NAMES):
        sh = ln_all[:, t * lnsz:(t + 1) * lnsz].reshape(N_DEV, DEPTH, 3, D_MODEL // N_DEV)
        ln_full[n] = sh.transpose(1, 2, 0, 3).reshape(DEPTH, 3, 1, D_MODEL)
    rep = {n: Wp[n] for n in REPL}

    inv_freq = 1.0 / (ROPE_BASE ** (jnp.arange(0, MLA_ROPE, 2, dtype=f32) / MLA_ROPE))
    ang = positions.astype(f32).reshape(S, 1) * inv_freq[None, :]
    cos8 = jnp.tile(jnp.cos(ang), (1, MLA_HEADS))
    sin8 = jnp.tile(jnp.sin(ang), (1, MLA_HEADS))

    xs = x.reshape(S, D_MODEL)
    xb = xs.astype(bf16)
    ws, saved = [], []
    for l in range(DEPTH):
        w = prep_layer(joined(got), rep, l)
        lg, lb = ln_full['ln_g'][l], ln_full['ln_b'][l]
        xs, xb, r1 = ffn_fwd(xs, xb, w['ffn1'], lg[0], lb[0])
        xs, xb, r2, got = mixer_fwd(xs, xb, w, lg[1], lb[1], cos8, sin8, (True, shard_rows(l + 1)) if l + 1 < DEPTH else None)
        xs, xb, r3 = ffn_fwd(xs, xb, w['ffn2'], lg[2], lb[2])
        ws.append(w)
        saved.append((r1, r2, r3))

    def loss_fn(y, t):
        d = y - t
        part = jnp.sum(jnp.mean(jnp.square(d), axis=-1, keepdims=True), axis=0, keepdims=True)
        return d * (1.0 / D_MODEL), 0.5 * part

    dx, loss_part = rowwise(loss_fn, [xs, loss_target.reshape(S, D_MODEL)], [], [(D_MODEL, f32)], [(1, 1)],
                            _row_tile(S, 512), "loss")
    loss = lax.psum(loss_part[0, 0], ("x", "y", "c"))

    def grad_pack(n, g):
        return _pad_tile_rows(_split_for_devices(n, g.astype(bf16)), 1)

    ffn2_names = [n for n in BIG if n.startswith('ffn2')]
    rest_names = [n for n in BIG if n not in ffn2_names]
    grads, arrived = [None] * DEPTH, [{} for _ in range(DEPTH)]
    for l in reversed(range(DEPTH)):
        w = ws[l]
        lg, lb = ln_full['ln_g'][l], ln_full['ln_b'][l]
        r1, r2, r3 = saved[l]
        dx, g2g, g2u, g2d, dg2, db2 = ffn_bwd(dx, r3, w['ffn2'], lg[2], lb[2])
        riders = [(l, n, g) for n, g in zip(ffn2_names, (g2g, g2u, g2d))]
        if l + 1 < DEPTH:
            riders += [(l + 1, n, grads[l + 1][n]) for n in rest_names]
        dx, gm, got = mixer_bwd(dx, r2, w, lg[1], lb[1], cos8, sin8, (False, [grad_pack(n, g) for _, n, g in riders]))
        for (layer, n, _), arr in zip(riders, got):
            arrived[layer][n] = arr
        dx, g1g, g1u, g1d, dg0, db0 = ffn_bwd(dx, r1, w['ffn1'], lg[0], lb[0])
        g = unprep_grads(gm, w)
        g.update({'ffn1_w_gate': g1g, 'ffn1_w_up': g1u, 'ffn1_w_down': g1d,
                  'ffn2_w_gate': g2g, 'ffn2_w_up': g2u, 'ffn2_w_down': g2d})
        g['ln_g'] = jnp.concatenate([dg0, g['ln_g'], dg2], axis=0)
        g['ln_b'] = jnp.concatenate([db0, g['ln_b'], db2], axis=0)
        grads[l] = g
    grad_x = dx.reshape(1, S, D_MODEL)

    arrived[0].update(zip(rest_names, all_to_all([grad_pack(n, grads[0][n]) for n in rest_names], "scatter_grads")))
    G = {n: jnp.stack([_rows_to_shard(n, sum_slots(arrived[l][n], "sum_" + n)[:_pack_rows(n)]) for l in range(DEPTH)])
         for n in BIG}
    small_names = LN_NAMES + REPL
    spack = jnp.concatenate([jnp.stack([grads[l][n] for l in range(DEPTH)]).reshape(-1) for n in small_names])
    n_small = spack.shape[0]
    gsmall = sum_slots(all_gather([_pad_rows(spack, 8)], "gather_small")[0], "sum_small").reshape(-1)

    off = 0
    for n in small_names:
        shp = (DEPTH, 3, D_MODEL) if n in LN_NAMES else Wp[n].shape
        sz = math.prod(shp)
        G[n] = gsmall[off:off + sz].reshape(shp)
        off += sz
    for n in LN_NAMES:
        G[n] = lax.dynamic_slice_in_dim(G[n], my * (D_MODEL // N_DEV), D_MODEL // N_DEV, axis=2)

    delta, new_m, new_v = {}, {}, {}
    for n in BIG:
        shp = Wp[n].shape
        two = (shp[0] * shp[1], shp[2])
        d_, m_, v_ = adamw(Wp[n].reshape(two), G[n].reshape(two), Mp[n].reshape(two), Vp[n].reshape(two), "adamw_" + n)
        delta[n], new_m[n], new_v[n] = d_.reshape(shp), m_.reshape(shp), v_.reshape(shp)

    def pack_small(src):
        return _pad_rows(jnp.concatenate([src[n].reshape(-1) for n in small_names]), 8)

    d_, m_, v_ = adamw(pack_small(Wp), pack_small(G), pack_small(Mp), pack_small(Vp), "adamw_small")
    d_, m_, v_ = d_.reshape(-1), m_.reshape(-1), v_.reshape(-1)
    off = 0
    for n in small_names:
        shp = Wp[n].shape
        sz = math.prod(shp)
        delta[n], new_m[n], new_v[n] = (t[off:off + sz].reshape(shp) for t in (d_, m_, v_))
        off += sz

    return (loss, grad_x, *[G[n] for n in W_NAMES], *[delta[n] for n in W_NAMES],
            *[new_m[n] for n in W_NAMES], *[new_v[n] for n in W_NAMES])
```
